```python
import math
import jax, jax.numpy as jnp
from jax import lax
import numpy as np

D_MODEL = 1024
BATCH = 32
SEQ = 256
DEPTH = 4
DEC_BATCH = 8
DEC_SEQ = 4096
PAST_LEN = 512

GRID_W = 64
EPS = 1e-6
CHUNK = 64
SHORT_CONV = 3
N_EVEN = (DEPTH + 1) // 2
N_ODD = DEPTH // 2
HY_W = D_MODEL // 2
HY_ORDER = 2
HY_EMB = 33
HY_BANDS = (HY_EMB - 1) // 2
HY_FILTER_HIDDEN = 64
HY_TARGET = 1e-2
HY_FAST = 0.3
HY_SLOW = 1.5
M_HEADS = 4
M_DH = (D_MODEL // 2) // M_HEADS
M_W = M_HEADS * M_DH
EV_CONV = 3 * HY_W + 2 * M_W
EV_PROJ = EV_CONV + 2 * M_W + 4 * M_HEADS
EV_OUT_IN = HY_W + M_W
SSD_INNER = 2 * D_MODEL
SSD_HEADDIM = 64
SSD_HEADS = SSD_INNER // SSD_HEADDIM
SSD_GROUPS = 8
SSD_HPG = SSD_HEADS // SSD_GROUPS
SSD_STATE = 128
SSD_CONV_DIM = SSD_INNER + 2 * SSD_GROUPS * SSD_STATE
OD_PROJ = SSD_INNER + SSD_CONV_DIM + 2 * SSD_HEADS
FFN_HIDDEN = -(-8 * D_MODEL // (3 * 256)) * 256

kernel_name = "hybrid_hyena_mlstm_ssd_prefix_diffusion_step"


def _rmsnorm(x, g):
    xf = x.astype(jnp.float32)
    y = xf * lax.rsqrt(jnp.mean(xf * xf, axis=-1, keepdims=True) + EPS)
    return (y * g.astype(jnp.float32)).astype(x.dtype)


def _short_conv(u, w, b):
    L = u.shape[1]
    pad = SHORT_CONV // 2
    up = jnp.pad(u, ((0, 0), (pad, SHORT_CONV - 1 - pad), (0, 0)))
    y = b
    for j in range(SHORT_CONV):
        y = y + up[:, j:j + L] * w[j]
    return y


def _to_colmajor(x):
    B, L, C = x.shape
    rows = L // GRID_W
    return x.reshape(B, rows, GRID_W, C).swapaxes(1, 2).reshape(B, L, C)


def _from_colmajor(x):
    B, L, C = x.shape
    rows = L // GRID_W
    return x.reshape(B, GRID_W, rows, C).swapaxes(1, 2).reshape(B, L, C)


def _chunks(t):
    B, L = t.shape[:2]
    return t.reshape((B, L // CHUNK, CHUNK) + t.shape[2:]).swapaxes(0, 1)


def _unchunk(t):
    t = t.swapaxes(0, 1)
    return t.reshape((t.shape[0], t.shape[1] * t.shape[2]) + t.shape[3:])


def _hyena_spectra(L, w1, b1, w2, b2, w3, freq):
    f32 = jnp.float32
    t = jnp.linspace(0.0, 1.0, L, dtype=f32)[:, None]
    ang = (2.0 * math.pi / L) * jnp.arange(L, dtype=f32)[:, None] * \
        jnp.linspace(1e-4, HY_BANDS - 1, HY_BANDS, dtype=f32)[None]
    feats = jnp.concatenate([t, jnp.cos(ang), -jnp.sin(ang)], axis=-1)
    fr = freq.astype(f32)
    h = jnp.sin(fr * (feats @ w1.astype(f32) + b1.astype(f32)))
    h = jnp.sin(fr * (h @ w2.astype(f32) + b2.astype(f32)))
    h = (h @ w3.astype(f32)).reshape(L, 2, HY_ORDER, HY_W)
    deltas = jnp.abs(jnp.linspace(math.log(HY_TARGET) / HY_SLOW, math.log(HY_TARGET) / HY_FAST, HY_W, dtype=f32))
    h = h * jnp.exp(-t * deltas)[:, None, None, :]
    k = jnp.concatenate([h[:, 0], jnp.zeros((1, HY_ORDER, HY_W), f32), h[:0:-1, 1]], axis=0)
    k = k / jnp.sum(jnp.abs(k), axis=0, keepdims=True)
    return jnp.fft.rfft(k, axis=0)


def _hyena(v, x1, x2, w1, b1, w2, b2, w3, freq, bias):
    L = v.shape[1]
    K = _hyena_spectra(L, w1, b1, w2, b2, w3, freq)
    bias = bias.astype(jnp.float32)
    z = v.astype(jnp.float32)
    for o, gate in enumerate((x1, x2)):
        Z = jnp.fft.rfft(z, n=2 * L, axis=1)
        y = jnp.fft.irfft(Z * K[:, o], n=2 * L, axis=1)[:, :L]
        z = gate.astype(jnp.float32) * (y + z * bias[o])
    return z.astype(v.dtype)


def _mlstm_scan(q, k, v, ig, lf, C0, n0, m0):
    mask = jnp.tril(jnp.ones((CHUNK, CHUNK), bool))[None, :, :, None]

    def step(carry, inp):
        C, n, m = carry
        qc, kc, vc, ic, fc = inp
        b = jnp.cumsum(fc, axis=1)
        dlog = jnp.where(mask, b[:, :, None] - b[:, None, :] + ic[:, None, :], -jnp.inf)
        inter = b + m[:, None]
        mt = jnp.maximum(inter, jnp.max(dlog, axis=2))
        w = jnp.exp(dlog - mt[:, :, None])
        a_in = jnp.exp(inter - mt)
        s = jnp.einsum('blhd,bshd->blsh', qc, kc) * w
        num = jnp.einsum('blsh,bshd->blhd', s, vc) + a_in[..., None] * jnp.einsum('blhd,bhde->blhe', qc, C)
        den = jnp.sum(s, axis=2) + a_in * jnp.einsum('blhd,bhd->blh', qc, n)
        h = num / jnp.maximum(jnp.abs(den), jnp.exp(-mt))[..., None]
        m_new = mt[:, -1]
        w_end = jnp.exp(b[:, -1:] - b + ic - m_new[:, None])
        dec = jnp.exp(b[:, -1] + m - m_new)
        kw = kc * w_end[..., None]
        C_new = dec[..., None, None] * C + jnp.einsum('bshd,bshe->bhde', kw, vc)
        n_new = dec[..., None] * n + jnp.sum(kw, axis=1)
        return (C_new, n_new, m_new), h

    (C, n, m), hs = lax.scan(step, (C0, n0, m0),
                             (_chunks(q), _chunks(k), _chunks(v), _chunks(ig), _chunks(lf)))
    return _unchunk(hs), C, n, m


def _mlstm(q, k, v, o, gates, gate_b, norm_g, st):
    f32 = jnp.float32
    B, L, _ = q.shape
    q = q.astype(f32).reshape(B, L, M_HEADS, M_DH)
    k = k.astype(f32).reshape(B, L, M_HEADS, M_DH) * (M_DH ** -0.5)
    v = v.astype(f32).reshape(B, L, M_HEADS, M_DH)
    g = (gates.astype(f32) + gate_b.astype(f32)).reshape(B, L, 4, M_HEADS)
    i_f, lf_f = g[:, :, 0], jax.nn.log_sigmoid(g[:, :, 1])
    i_b, lf_b = g[:, :, 2], jax.nn.log_sigmoid(g[:, :, 3])
    C0, n0, m0 = (s.astype(f32) for s in st)
    hf, Cf, nf, mf = _mlstm_scan(q, k, v, i_f, lf_f, C0[:, 0], n0[:, 0], m0[:, 0])
    fl = lambda t: jnp.flip(t, axis=1)
    hb, Cb, nb, mb = _mlstm_scan(fl(q), fl(k), fl(v), fl(i_b), fl(lf_b), C0[:, 1], n0[:, 1], m0[:, 1])
    h = hf + fl(hb)
    h = h * lax.rsqrt(jnp.mean(h * h, axis=-1, keepdims=True) + EPS) * norm_g.astype(f32).reshape(M_HEADS, M_DH)
    h = jax.nn.sigmoid(o.astype(f32)) * h.reshape(B, L, M_W)
    new_st = (jnp.stack([Cf, Cb], 1), jnp.stack([nf, nb], 1), jnp.stack([mf, mb], 1))
    return h, new_st


def _ssd_scan(x, a, Bm, Cm, S0):
    mask = jnp.tril(jnp.ones((CHUNK, CHUNK), bool))[None, :, :, None, None]

    def step(S, inp):
        xc, ac, bc, cc = inp
        acs = jnp.cumsum(ac, axis=1)
        Lm = jnp.exp(jnp.where(mask, acs[:, :, None] - acs[:, None, :], -jnp.inf))
        cb = jnp.einsum('blgn,bsgn->blsg', cc, bc)
        y = jnp.einsum('blsgr,bsgrp->blgrp', cb[..., None] * Lm, xc) + \
            jnp.einsum('blgn,bgrpn->blgrp', cc, S) * jnp.exp(acs)[..., None]
        dec_end = jnp.exp(acs[:, -1:] - acs)
        S_new = jnp.exp(acs[:, -1])[..., None, None] * S + \
            jnp.einsum('blgn,blgrp->bgrpn', bc, xc * dec_end[..., None])
        return S_new, y

    S, ys = lax.scan(step, S0, (_chunks(x), _chunks(a), _chunks(Bm), _chunks(Cm)))
    return _unchunk(ys), S


def _ssd(z, xbc, dt_pre, dt_bias, A_log, Dskip, norm_g, st):
    f32 = jnp.float32
    B, L, _ = z.shape
    xbc = xbc.astype(f32)
    xs = xbc[..., :SSD_INNER].reshape(B, L, SSD_GROUPS, SSD_HPG, SSD_HEADDIM)
    GN = SSD_GROUPS * SSD_STATE
    Bm = xbc[..., SSD_INNER:SSD_INNER + GN].reshape(B, L, SSD_GROUPS, SSD_STATE)
    Cm = xbc[..., SSD_INNER + GN:].reshape(B, L, SSD_GROUPS, SSD_STATE)
    dt = jax.nn.softplus(dt_pre.astype(f32).reshape(B, L, 2, SSD_HEADS) + dt_bias.astype(f32))
    A = -jnp.exp(A_log.astype(f32))
    S0 = st.astype(f32)
    fl = lambda t: jnp.flip(t, axis=1)
    outs = []
    for d in range(2):
        dtd = dt[:, :, d].reshape(B, L, SSD_GROUPS, SSD_HPG)
        xd = xs * dtd[..., None]
        ad = dtd * A[d].reshape(SSD_GROUPS, SSD_HPG)
        s0 = S0[:, d].reshape(B, SSD_GROUPS, SSD_HPG, SSD_HEADDIM, SSD_STATE)
        if d == 0:
            y, S = _ssd_scan(xd, ad, Bm, Cm, s0)
        else:
            y, S = _ssd_scan(fl(xd), fl(ad), fl(Bm), fl(Cm), s0)
            y = fl(y)
        outs.append((y, S.reshape(B, SSD_HEADS, SSD_HEADDIM, SSD_STATE)))
    y = outs[0][0] + outs[1][0] + Dskip.astype(f32).reshape(SSD_GROUPS, SSD_HPG)[..., None] * xs
    y = y.reshape(B, L, SSD_INNER) * jax.nn.silu(z.astype(f32))
    y = y * lax.rsqrt(jnp.mean(y * y, axis=-1, keepdims=True) + EPS) * norm_g.astype(f32)
    return y, jnp.stack([outs[0][1], outs[1][1]], 1)


def _even_mixer(h, e, st, P):
    proj = h @ P['ev_in_w'][e]
    cv = _short_conv(proj[..., :EV_CONV], P['ev_conv_w'][e], P['ev_conv_b'][e])
    hv, hx1, hx2 = cv[..., :HY_W], cv[..., HY_W:2 * HY_W], cv[..., 2 * HY_W:3 * HY_W]
    o0 = 3 * HY_W
    mq = jax.nn.silu(cv[..., o0:o0 + M_W])
    mk = jax.nn.silu(cv[..., o0 + M_W:o0 + 2 * M_W])
    mv = proj[..., EV_CONV:EV_CONV + M_W]
    mo = proj[..., EV_CONV + M_W:EV_CONV + 2 * M_W]
    mg = proj[..., EV_CONV + 2 * M_W:]
    y_hy = _hyena(hv, hx1, hx2, P['hy_w1'][e], P['hy_b1'][e], P['hy_w2'][e], P['hy_b2'][e],
                  P['hy_w3'][e], P['hy_freq'][e], P['hy_bias'][e])
    y_m, new_st = _mlstm(mq, mk, mv, mo, mg, P['m_gate_b'][e], P['m_norm_g'][e], st)
    y = jnp.concatenate([y_hy.astype(h.dtype), y_m.astype(h.dtype)], axis=-1)
    return y @ P['ev_out_w'][e], new_st


def _odd_mixer(h, o, st, P):
    proj = h @ P['od_in_w'][o]
    z = proj[..., :SSD_INNER]
    xbc = jax.nn.silu(_short_conv(proj[..., SSD_INNER:SSD_INNER + SSD_CONV_DIM], P['od_conv_w'][o], P['od_conv_b'][o]))
    dt_pre = proj[..., SSD_INNER + SSD_CONV_DIM:]
    y, new_st = _ssd(z, xbc, dt_pre, P['ssd_dt_bias'][o], P['ssd_A_log'][o], P['ssd_D'][o], P['ssd_norm_g'][o], st)
    return y.astype(h.dtype) @ P['od_out_w'][o], new_st


def _trunk(x, cond, init_states, grid, P):
    states_out = []
    for l in range(DEPTH):
        mod = (jax.nn.silu(cond) @ P['mod_w'][l] + P['mod_b'][l])[:, None, :]
        sh1, sc1, g1, sh2, sc2, g2 = jnp.split(mod, 6, axis=-1)
        h = _rmsnorm(x, P['norm1_g'][l]) * (1 + sc1) + sh1
        colmajor = grid and (l // 2) % 2 == 1
        if colmajor:
            h = _to_colmajor(h)
        if l % 2 == 0:
            out, st = _even_mixer(h, l // 2, init_states[l], P)
        else:
            out, st = _odd_mixer(h, l // 2, init_states[l], P)
        if colmajor:
            out = _from_colmajor(out)
        x = x + g1 * out
        h = _rmsnorm(x, P['norm2_g'][l]) * (1 + sc2) + sh2
        ff = (jax.nn.silu(h @ P['ffn_w1'][l]) * (h @ P['ffn_w3'][l])) @ P['ffn_w2'][l]
        x = x + g2 * ff
        states_out.append(st)
    return _rmsnorm(x, P['final_g']), states_out


def setup_inputs(seed: int = 0) -> dict:
    key = jax.random.key(seed)
    ks = iter(jax.random.split(key, 64))
    nrm = lambda shape, s: s * jax.random.normal(next(ks), shape, jnp.float32)
    D = D_MODEL
    inp = {}
    inp['x_prompt'] = nrm((BATCH, SEQ, D), 1.0)
    inp['x_sample'] = nrm((DEC_BATCH, DEC_SEQ, D), 1.0)
    inp['state_mlstm_C'] = nrm((DEC_BATCH, N_EVEN, 2, M_HEADS, M_DH, M_DH), 0.1)
    inp['state_mlstm_n'] = nrm((DEC_BATCH, N_EVEN, 2, M_HEADS, M_DH), 0.1)
    inp['state_mlstm_m'] = nrm((DEC_BATCH, N_EVEN, 2, M_HEADS), 1.0)
    inp['state_ssd'] = nrm((DEC_BATCH, N_ODD, 2, SSD_HEADS, SSD_HEADDIM, SSD_STATE), 0.1)
    inp['c'] = nrm((DEC_BATCH, D), 1.0)
    inp['c_ctx'] = nrm((D,), 1.0)
    inp['mod_w'] = nrm((DEPTH, D, 6 * D), 0.5 * D ** -0.5)
    inp['mod_b'] = nrm((DEPTH, 6 * D), 0.02)
    inp['norm1_g'] = 1.0 + nrm((DEPTH, D), 0.02)
    inp['norm2_g'] = 1.0 + nrm((DEPTH, D), 0.02)
    inp['ffn_w1'] = nrm((DEPTH, D, FFN_HIDDEN), D ** -0.5)
    inp['ffn_w3'] = nrm((DEPTH, D, FFN_HIDDEN), D ** -0.5)
    inp['ffn_w2'] = nrm((DEPTH, FFN_HIDDEN, D), FFN_HIDDEN ** -0.5)
    inp['final_g'] = 1.0 + nrm((D,), 0.02)
    inp['ev_in_w'] = nrm((N_EVEN, D, EV_PROJ), D ** -0.5)
    inp['ev_conv_w'] = nrm((N_EVEN, SHORT_CONV, EV_CONV), SHORT_CONV ** -0.5)
    inp['ev_conv_b'] = nrm((N_EVEN, EV_CONV), 0.02)
    inp['hy_w1'] = nrm((N_EVEN, HY_EMB, HY_FILTER_HIDDEN), HY_EMB ** -0.5)
    inp['hy_b1'] = nrm((N_EVEN, HY_FILTER_HIDDEN), 0.1)
    inp['hy_w2'] = nrm((N_EVEN, HY_FILTER_HIDDEN, HY_FILTER_HIDDEN), HY_FILTER_HIDDEN ** -0.5)
    inp['hy_b2'] = nrm((N_EVEN, HY_FILTER_HIDDEN), 0.1)
    inp['hy_w3'] = nrm((N_EVEN, HY_FILTER_HIDDEN, 2 * HY_ORDER * HY_W), HY_FILTER_HIDDEN ** -0.5)
    inp['hy_freq'] = 1.0 + nrm((N_EVEN, HY_FILTER_HIDDEN), 0.02)
    inp['hy_bias'] = nrm((N_EVEN, HY_ORDER, HY_W), 0.5)
    ib = nrm((N_EVEN, 2, M_HEADS), 0.1)
    fb = jnp.linspace(3.0, 6.0, M_HEADS, dtype=jnp.float32) + nrm((N_EVEN, 2, M_HEADS), 0.1)
    inp['m_gate_b'] = jnp.stack([ib[:, 0], fb[:, 0], ib[:, 1], fb[:, 1]], axis=1).reshape(N_EVEN, 4 * M_HEADS)
    inp['m_norm_g'] = 1.0 + nrm((N_EVEN, M_W), 0.02)
    inp['ev_out_w'] = nrm((N_EVEN, EV_OUT_IN, D), EV_OUT_IN ** -0.5)
    inp['od_in_w'] = nrm((N_ODD, D, OD_PROJ), D ** -0.5)
    inp['od_conv_w'] = nrm((N_ODD, SHORT_CONV, SSD_CONV_DIM), SHORT_CONV ** -0.5)
    inp['od_conv_b'] = nrm((N_ODD, SSD_CONV_DIM), 0.02)
    u = jax.random.uniform(next(ks), (N_ODD, 2, SSD_HEADS), jnp.float32)
    dt0 = jnp.exp(u * (math.log(0.1) - math.log(0.001)) + math.log(0.001))
    inp['ssd_dt_bias'] = dt0 + jnp.log(-jnp.expm1(-dt0))
    inp['ssd_A_log'] = jnp.log(jax.random.uniform(next(ks), (N_ODD, 2, SSD_HEADS), jnp.float32, 1.0, 16.0))
    inp['ssd_D'] = 1.0 + nrm((N_ODD, SSD_HEADS), 0.1)
    inp['ssd_norm_g'] = 1.0 + nrm((N_ODD, SSD_INNER), 0.02)
    inp['od_out_w'] = nrm((N_ODD, SSD_INNER, D), SSD_INNER ** -0.5)
    return inp


def reference(x_prompt, x_sample, state_mlstm_C, state_mlstm_n, state_mlstm_m, state_ssd, c, c_ctx,
              mod_w, mod_b, norm1_g, norm2_g, ffn_w1, ffn_w3, ffn_w2, final_g,
              ev_in_w, ev_conv_w, ev_conv_b, hy_w1, hy_b1, hy_w2, hy_b2, hy_w3, hy_freq, hy_bias,
              m_gate_b, m_norm_g, ev_out_w,
              od_in_w, od_conv_w, od_conv_b, ssd_dt_bias, ssd_A_log, ssd_D, ssd_norm_g, od_out_w):
    P = dict(mod_w=mod_w, mod_b=mod_b, norm1_g=norm1_g, norm2_g=norm2_g, ffn_w1=ffn_w1, ffn_w3=ffn_w3,
             ffn_w2=ffn_w2, final_g=final_g, ev_in_w=ev_in_w, ev_conv_w=ev_conv_w, ev_conv_b=ev_conv_b,
             hy_w1=hy_w1, hy_b1=hy_b1, hy_w2=hy_w2, hy_b2=hy_b2, hy_w3=hy_w3, hy_freq=hy_freq, hy_bias=hy_bias,
             m_gate_b=m_gate_b, m_norm_g=m_norm_g, ev_out_w=ev_out_w, od_in_w=od_in_w, od_conv_w=od_conv_w,
             od_conv_b=od_conv_b, ssd_dt_bias=ssd_dt_bias, ssd_A_log=ssd_A_log, ssd_D=ssd_D,
             ssd_norm_g=ssd_norm_g, od_out_w=od_out_w)
    f32 = jnp.float32
    Bp = x_prompt.shape[0]
    ctx_init = []
    for l in range(DEPTH):
        if l % 2 == 0:
            ctx_init.append((jnp.zeros((Bp, 2, M_HEADS, M_DH, M_DH), f32),
                             jnp.zeros((Bp, 2, M_HEADS, M_DH), f32),
                             jnp.zeros((Bp, 2, M_HEADS), f32)))
        else:
            ctx_init.append(jnp.zeros((Bp, 2, SSD_HEADS, SSD_HEADDIM, SSD_STATE), f32))
    y_prompt, ctx_states = _trunk(x_prompt, c_ctx[None, :], ctx_init, False, P)
    lat_init = []
    for l in range(DEPTH):
        if l % 2 == 0:
            e = l // 2
            lat_init.append((state_mlstm_C[:, e], state_mlstm_n[:, e], state_mlstm_m[:, e]))
        else:
            lat_init.append(state_ssd[:, l // 2])
    y_sample, _ = _trunk(x_sample, c, lat_init, True, P)
    dt = x_prompt.dtype
    new_mlstm_C = jnp.stack([ctx_states[l][0] for l in range(0, DEPTH, 2)], axis=1).astype(dt)
    new_mlstm_n = jnp.stack([ctx_states[l][1] for l in range(0, DEPTH, 2)], axis=1).astype(dt)
    new_mlstm_m = jnp.stack([ctx_states[l][2] for l in range(0, DEPTH, 2)], axis=1).astype(dt)
    new_ssd = jnp.stack([ctx_states[l] for l in range(1, DEPTH, 2)], axis=1).astype(dt)
    return (y_prompt, y_sample, new_mlstm_C, new_mlstm_n, new_mlstm_m, new_ssd)
```

```python
import functools
import math

import jax
import jax.numpy as jnp
from jax import lax
from jax.experimental import pallas as pl
from jax.experimental.pallas import tpu as pltpu

F32 = jnp.float32
BF16 = jnp.bfloat16

D_MODEL = 1024
DEPTH = 4
GRID_W = 64
EPS = 1e-6
CHUNK = 64
HY_W = D_MODEL // 2
HY_ORDER = 2
HY_EMB = 33
HY_BANDS = (HY_EMB - 1) // 2
HY_HID = 64
HY_TARGET = 1e-2
HY_FAST = 0.3
HY_SLOW = 1.5
M_HEADS = 4
M_DH = 128
M_W = M_HEADS * M_DH
EV_CONV = 3 * HY_W + 2 * M_W
EV_PROJ = EV_CONV + 2 * M_W + 4 * M_HEADS
SSD_INNER = 2 * D_MODEL
SSD_HEADDIM = 64
SSD_HEADS = SSD_INNER // SSD_HEADDIM
SSD_GROUPS = 8
SSD_HPG = SSD_HEADS // SSD_GROUPS
SSD_STATE = 128
SSD_GW = SSD_HPG * SSD_HEADDIM
SSD_CONV_DIM = SSD_INNER + 2 * SSD_GROUPS * SSD_STATE
OD_PROJ = SSD_INNER + SSD_CONV_DIM + 2 * SSD_HEADS
FFN_HIDDEN = -(-8 * D_MODEL // (3 * 256)) * 256

V7X_VMEM_BYTES = 64 * 1024 * 1024
VMEM_LIMIT = V7X_VMEM_BYTES * 3 // 4
ROW_TILE = 256
HALO = 16
COL_CHUNK = 512
N2_SUB = 8


def _cparams(*sem):
    return pltpu.CompilerParams(dimension_semantics=sem, vmem_limit_bytes=VMEM_LIMIT)


def _dot(a, b):
    return jnp.dot(a, b, preferred_element_type=F32)


def _dot_nt(a, b):
    return lax.dot_general(a, b, (((1,), (1,)), ((), ())), preferred_element_type=F32)


def _dot_tn(a, b):
    return lax.dot_general(a, b, (((0,), (0,)), ((), ())), preferred_element_type=F32)


def _split2(x):
    h = x.astype(BF16)
    return h, (x - h.astype(F32)).astype(BF16)


def _split3(x):
    h = x.astype(BF16)
    r = x - h.astype(F32)
    m = r.astype(BF16)
    return h, m, (r - m.astype(F32)).astype(BF16)


def _dot3(th, tl, x):
    xh, xl = _split2(x)
    return _dot(th, xh) + _dot(th, xl) + _dot(tl, xh)


def _sigmoid(x):
    return 1.0 / (1.0 + jnp.exp(-x))


def _silu(x):
    return x * _sigmoid(x)


def _log_sigmoid(x):
    return jnp.minimum(x, 0.0) - jnp.log(1.0 + jnp.exp(-jnp.abs(x)))


def _softplus(x):
    return jnp.maximum(x, 0.0) + jnp.log(1.0 + jnp.exp(-jnp.abs(x)))


def _rms(x):
    return x * lax.rsqrt(jnp.mean(x * x, axis=-1, keepdims=True) + EPS)


def _mod_kernel(c_ref, w_ref, b_ref, o_ref):
    a = _silu(c_ref[...]).astype(BF16)
    o_ref[0] = _dot(a, w_ref[0].astype(BF16)) + b_ref[0]


def _modulation(cond, mod_w, mod_b):
    nc = cond.shape[0]
    tn = 1536
    return pl.pallas_call(
        _mod_kernel,
        grid=(DEPTH, 6 * D_MODEL // tn),
        in_specs=[pl.BlockSpec((nc, D_MODEL), lambda l, j: (0, 0)),
                  pl.BlockSpec((1, D_MODEL, tn), lambda l, j: (l, 0, j)),
                  pl.BlockSpec((1, 1, tn), lambda l, j: (l, 0, j))],
        out_specs=pl.BlockSpec((1, nc, tn), lambda l, j: (l, 0, j)),
        out_shape=jax.ShapeDtypeStruct((DEPTH, nc, 6 * D_MODEL), F32),
        compiler_params=_cparams("arbitrary", "arbitrary"),
        name="modulation",
    )(cond, mod_w, mod_b.reshape(DEPTH, 1, 6 * D_MODEL))


def _inproj_kernel(x_ref, xp_ref, xn_ref, mod_ref, g_ref, w_ref, wt_ref, cw_ref, cb_ref, *rest,
                   plan, tiles_per_seq, conv0):
    nseg = len(plan)
    outs = rest[:nseg]
    tail_ref, tailt_ref, hbuf = rest[nseg], rest[nseg + 1], rest[nseg + 2]
    tm = x_ref.shape[0]
    i = pl.program_id(0)
    j = i % tiles_per_seq
    pmask = (j != 0).astype(F32)
    nmask = (j != tiles_per_seq - 1).astype(F32)
    mod = mod_ref[0]
    sh, sc = mod[0:1], mod[1:2]
    g = g_ref[...]

    def normmod(xv):
        return (_rms(xv) * g * (1.0 + sc) + sh).astype(BF16)

    hbuf[0:tm] = normmod(x_ref[...])
    hbuf[tm:tm + HALO] = normmod(xp_ref[...])
    hbuf[tm + HALO:tm + 2 * HALO] = normmod(xn_ref[...])
    rowid = lax.broadcasted_iota(jnp.int32, (tm, COL_CHUNK), 0)

    for (c0, width, kind, scale), o_ref in zip(plan, outs):
        for cc in range(c0, c0 + width, COL_CHUNK):
            w = w_ref[0, :, cc:cc + COL_CHUNK]
            if kind == "raw":
                y = _dot(hbuf[0:tm], w)
            else:
                full = _dot(hbuf[...], w)
                acc = full[0:tm]
                ap = full[tm + HALO - 1:tm + HALO] * pmask
                an = full[tm + HALO:tm + HALO + 1] * nmask
                prev = jnp.where(rowid == 0, ap, pltpu.roll(acc, 1, 0))
                nxt = jnp.where(rowid == tm - 1, an, pltpu.roll(acc, tm - 1, 0))
                cw = cw_ref[0, :, cc - conv0:cc - conv0 + COL_CHUNK]
                cb = cb_ref[0, :, cc - conv0:cc - conv0 + COL_CHUNK]
                y = cb + cw[0:1] * prev + cw[1:2] * acc + cw[2:3] * nxt
                if kind == "conv_silu":
                    y = _silu(y)
                if scale != 1.0:
                    y = y * scale
            o_ref[:, cc - c0:cc - c0 + COL_CHUNK] = y.astype(o_ref.dtype)

    h = hbuf[0:tm]
    tw = tail_ref.shape[1]
    tail_ref[...] = _dot(h, w_ref[0, :, w_ref.shape[2] - tw:])
    tailt_ref[...] = _dot_nt(wt_ref[0], h)


def _inproj(x, mods_l, g_l, w_all, wt_all, cw_all, cb_all, layer_idx, *, plan, conv0, seq_len, per_batch_cond):
    M = x.shape[0]
    tm = ROW_TILE
    tps = seq_len // tm
    nt = M // tm
    hb = tm // HALO
    ntot = w_all.shape[2]
    tw = wt_all.shape[1]
    cdim = cw_all.shape[2]
    e = layer_idx
    cond_map = (lambda i: (1 + i // tps, 0, 0)) if per_batch_cond else (lambda i: (0, 0, 0))
    in_specs = [
        pl.BlockSpec((tm, D_MODEL), lambda i: (i, 0)),
        pl.BlockSpec((HALO, D_MODEL), lambda i: (jnp.maximum(i * hb - 1, 0), 0)),
        pl.BlockSpec((HALO, D_MODEL), lambda i: (jnp.minimum((i + 1) * hb, nt * hb - 1), 0)),
        pl.BlockSpec((1, 6, D_MODEL), cond_map),
        pl.BlockSpec((1, D_MODEL), lambda i: (0, 0)),
        pl.BlockSpec((1, D_MODEL, ntot), lambda i: (e, 0, 0), pipeline_mode=pl.Buffered(1)),
        pl.BlockSpec((1, tw, D_MODEL), lambda i: (e, 0, 0)),
        pl.BlockSpec((1, 3, cdim), lambda i: (e, 0, 0)),
        pl.BlockSpec((1, 1, cdim), lambda i: (e, 0, 0)),
    ]
    out_specs, out_shape = [], []
    for (c0, width, kind, scale, dt) in plan:
        out_specs.append(pl.BlockSpec((tm, width), lambda i: (i, 0)))
        out_shape.append(jax.ShapeDtypeStruct((M, width), dt))
    out_specs += [pl.BlockSpec((tm, tw), lambda i: (i, 0)), pl.BlockSpec((tw, tm), lambda i: (0, i))]
    out_shape += [jax.ShapeDtypeStruct((M, tw), F32), jax.ShapeDtypeStruct((tw, M), F32)]
    kplan = tuple(p[:4] for p in plan)
    return pl.pallas_call(
        functools.partial(_inproj_kernel, plan=kplan, tiles_per_seq=tps, conv0=conv0),
        grid=(nt,),
        in_specs=in_specs,
        out_specs=out_specs,
        out_shape=out_shape,
        scratch_shapes=[pltpu.VMEM((tm + 2 * HALO, D_MODEL), BF16)],
        compiler_params=_cparams("arbitrary"),
        name="inproj",
    )(x, x, x, mods_l, g_l, w_all, wt_all, cw_all, cb_all.reshape(cb_all.shape[0], 1, cdim))


def _outproj_even_kernel(yh_ref, ym_ref, x_ref, mod_ref, w_ref, o_ref):
    out = _dot(yh_ref[...], w_ref[0, 0:HY_W]) + _dot(ym_ref[...], w_ref[0, HY_W:])
    o_ref[...] = x_ref[...] + mod_ref[0][2:3] * out


def _outproj_odd_kernel(y_ref, z_ref, x_ref, mod_ref, ng_ref, w_ref, o_ref):
    u = y_ref[...] * _silu(z_ref[...])
    un = (_rms(u) * ng_ref[0]).astype(BF16)
    o_ref[...] = x_ref[...] + mod_ref[0][2:3] * _dot(un, w_ref[0])


def _cond_map(seq_len, per_batch_cond):
    tps = seq_len // ROW_TILE
    return (lambda i: (1 + i // tps, 0, 0)) if per_batch_cond else (lambda i: (0, 0, 0))


def _outproj_even(yh, ym, x, mods_l, w_all, e, *, seq_len, per_batch_cond):
    M = x.shape[0]
    tm = ROW_TILE
    return pl.pallas_call(
        _outproj_even_kernel,
        grid=(M // tm,),
        in_specs=[pl.BlockSpec((tm, HY_W), lambda i: (i, 0)),
                  pl.BlockSpec((tm, M_W), lambda i: (i, 0)),
                  pl.BlockSpec((tm, D_MODEL), lambda i: (i, 0)),
                  pl.BlockSpec((1, 6, D_MODEL), _cond_map(seq_len, per_batch_cond)),
                  pl.BlockSpec((1, HY_W + M_W, D_MODEL), lambda i: (e, 0, 0), pipeline_mode=pl.Buffered(1))],
        out_specs=pl.BlockSpec((tm, D_MODEL), lambda i: (i, 0)),
        out_shape=jax.ShapeDtypeStruct((M, D_MODEL), F32),
        compiler_params=_cparams("arbitrary"),
        name="outproj_even",
    )(yh, ym, x, mods_l, w_all)


def _outproj_odd(y, z, x, mods_l, ng_all, w_all, o, *, seq_len, per_batch_cond):
    M = x.shape[0]
    tm = ROW_TILE
    return pl.pallas_call(
        _outproj_odd_kernel,
        grid=(M // tm,),
        in_specs=[pl.BlockSpec((tm, SSD_INNER), lambda i: (i, 0)),
                  pl.BlockSpec((tm, SSD_INNER), lambda i: (i, 0)),
                  pl.BlockSpec((tm, D_MODEL), lambda i: (i, 0)),
                  pl.BlockSpec((1, 6, D_MODEL), _cond_map(seq_len, per_batch_cond)),
                  pl.BlockSpec((1, 1, SSD_INNER), lambda i: (o, 0, 0)),
                  pl.BlockSpec((1, SSD_INNER, D_MODEL), lambda i: (o, 0, 0), pipeline_mode=pl.Buffered(1))],
        out_specs=pl.BlockSpec((tm, D_MODEL), lambda i: (i, 0)),
        out_shape=jax.ShapeDtypeStruct((M, D_MODEL), F32),
        compiler_params=_cparams("arbitrary"),
        name="outproj_odd",
    )(y, z, x, mods_l, ng_all, w_all)


FFN_CHUNK = FFN_HIDDEN // 2


def _ffn_kernel(x_ref, mod_ref, g_ref, w1_ref, w3_ref, w2_ref, fg_ref, o_ref, *, final):
    x = x_ref[...]
    mod = mod_ref[0]
    h = (_rms(x) * g_ref[0] * (1.0 + mod[4:5]) + mod[3:4]).astype(BF16)
    ff = jnp.zeros(x.shape, F32)
    for c0 in range(0, FFN_HIDDEN, FFN_CHUNK):
        a = _dot(h, w1_ref[0, :, c0:c0 + FFN_CHUNK])
        b = _dot(h, w3_ref[0, :, c0:c0 + FFN_CHUNK])
        u = (_silu(a) * b).astype(BF16)
        ff = ff + _dot(u, w2_ref[0, c0:c0 + FFN_CHUNK, :])
    xo = x + mod[5:6] * ff
    if final:
        xo = _rms(xo) * fg_ref[...]
    o_ref[...] = xo


def _ffn(x, mods_l, g_all, w1, w3, w2, fg, l, *, seq_len, per_batch_cond, final):
    M = x.shape[0]
    tm = ROW_TILE
    wspec = lambda shape: pl.BlockSpec(shape, lambda i: (l, 0, 0), pipeline_mode=pl.Buffered(1))
    return pl.pallas_call(
        functools.partial(_ffn_kernel, final=final),
        grid=(M // tm,),
        in_specs=[pl.BlockSpec((tm, D_MODEL), lambda i: (i, 0)),
                  pl.BlockSpec((1, 6, D_MODEL), _cond_map(seq_len, per_batch_cond)),
                  pl.BlockSpec((1, 1, D_MODEL), lambda i: (l, 0, 0)),
                  wspec((1, D_MODEL, FFN_HIDDEN)), wspec((1, D_MODEL, FFN_HIDDEN)),
                  wspec((1, FFN_HIDDEN, D_MODEL)),
                  pl.BlockSpec((1, D_MODEL), lambda i: (0, 0))],
        out_specs=pl.BlockSpec((tm, D_MODEL), lambda i: (i, 0)),
        out_shape=jax.ShapeDtypeStruct((M, D_MODEL), F32),
        compiler_params=_cparams("arbitrary"),
        name="ffn",
    )(x, mods_l, g_all, w1, w3, w2, fg)


def _tri_masks():
    row = lax.broadcasted_iota(jnp.int32, (CHUNK, CHUNK), 0)
    col = lax.broadcasted_iota(jnp.int32, (CHUNK, CHUNK), 1)
    lower = col <= row
    upper = col >= row
    return lower, upper


def _cumsum_cols(t, x):
    a, b, c = _split3(x)
    return _dot(t, a) + _dot(t, b) + _dot(t, c)


def _cumsum_rows(x, t):
    a, b, c = _split3(x)
    return _dot(a, t) + _dot(b, t) + _dot(c, t)


def _mlstm_kernel(q_ref, k_ref, v_ref, o_ref, g_ref, gt_ref, gb_ref, gbt_ref, ng_ref, c0_ref, m0_ref,
                  y_ref, cst_ref, mst_ref, h_scr, c_scr, m_scr, *, nc, epi_rows):
    c_scr[...] = c0_ref[0]
    m_scr[...] = m0_ref[0]
    h_scr[...] = jnp.zeros(h_scr.shape, F32)
    lower, upper = _tri_masks()
    t_lower = jnp.where(lower, 1.0, 0.0).astype(BF16)
    t_upper = jnp.where(upper, 1.0, 0.0).astype(BF16)
    ones_col = jnp.where(lax.broadcasted_iota(jnp.int32, (CHUNK, M_DH), 1) == 0, 1.0, 0.0).astype(BF16)
    gb = gb_ref[...]
    gbt = gbt_ref[...]

    def chunk_dir(ci, d):
        r0 = pl.multiple_of(ci * CHUNK, CHUNK)
        rows = pl.ds(r0, CHUNK)
        gc = g_ref[0, ci] + gb
        gr = gt_ref[0, ci] + gbt
        t_col, t_row, mask, last = ((t_lower, t_upper, lower, CHUNK - 1) if d == 0
                                    else (t_upper, t_lower, upper, 0))
        b_col = _cumsum_cols(t_col, _log_sigmoid(gc))
        b_row = _cumsum_rows(_log_sigmoid(gr), t_row)
        for h in range(M_HEADS):
            fi = (2 * d + 1) * M_HEADS + h
            ii = (2 * d) * M_HEADS + h
            sidx = d * M_HEADS + h
            lanes = slice(h * M_DH, (h + 1) * M_DH)
            bc = b_col[:, fi:fi + 1]
            br = b_row[fi:fi + 1, :]
            i_row = gr[ii:ii + 1, :]
            i_col = gc[:, ii:ii + 1]
            m = m_scr[sidx][:, 0:1]
            dlog = jnp.where(mask, bc - br + i_row, -jnp.inf)
            inter = bc + m
            mt = jnp.maximum(inter, jnp.max(dlog, axis=1, keepdims=True))
            w = jnp.exp(dlog - mt)
            a_in = jnp.exp(inter - mt)
            qh = q_ref[0, rows, lanes]
            kh = k_ref[0, rows, lanes]
            vext = jnp.concatenate([v_ref[0, rows, lanes], ones_col], axis=1)
            s = _dot_nt(qh, kh) * w
            sv = _dot(s.astype(BF16), vext)
            cx = c_scr[sidx]
            qc = _dot(qh, cx.astype(BF16))
            num = sv[:, 0:M_DH] + a_in * qc[:, 0:M_DH]
            den = sv[:, M_DH:M_DH + 1] + a_in * qc[:, M_DH:M_DH + 1]
            hout = num / jnp.maximum(jnp.abs(den), jnp.exp(-mt))
            h_scr[rows, lanes] = h_scr[rows, lanes] + hout
            m_new = mt[last:last + 1, :]
            b_last = bc[last:last + 1, :]
            w_end = jnp.exp(b_last - bc + i_col - m_new)
            dec = jnp.exp(b_last + m - m_new)
            kw = (kh.astype(F32) * w_end).astype(BF16)
            c_scr[sidx] = dec * cx + _dot_tn(kw, vext)
            m_scr[sidx] = jnp.broadcast_to(m_new, (1, M_DH))

    def body(c, carry):
        chunk_dir(c, 0)
        chunk_dir(nc - 1 - c, 1)
        return carry

    lax.fori_loop(0, nc, body, 0)

    def epilogue(i, carry):
        rows = pl.ds(pl.multiple_of(i * epi_rows, epi_rows), epi_rows)
        for h in range(M_HEADS):
            lanes = slice(h * M_DH, (h + 1) * M_DH)
            hn = _rms(h_scr[rows, lanes]) * ng_ref[:, lanes]
            y_ref[0, rows, lanes] = (_sigmoid(o_ref[0, rows, lanes]) * hn).astype(y_ref.dtype)
        return carry

    lax.fori_loop(0, (nc * CHUNK) // epi_rows, epilogue, 0)
    cst_ref[0] = c_scr[...]
    mst_ref[0] = m_scr[...]


def _mlstm(q, k, v, o, g, gt, gate_b, norm_g, c0, m0):
    B, L, _ = q.shape
    nc = L // CHUNK
    ng = 4 * M_HEADS
    seq = lambda: pl.BlockSpec((1, L, M_W), lambda b: (b, 0, 0))
    seq1 = lambda: pl.BlockSpec((1, L, M_W), lambda b: (b, 0, 0), pipeline_mode=pl.Buffered(1))
    return pl.pallas_call(
        functools.partial(_mlstm_kernel, nc=nc, epi_rows=min(L, 256)),
        grid=(B,),
        in_specs=[seq1(), seq1(), seq1(), seq1(),
                  pl.BlockSpec((1, nc, CHUNK, ng), lambda b: (b, 0, 0, 0)),
                  pl.BlockSpec((1, nc, ng, CHUNK), lambda b: (b, 0, 0, 0)),
                  pl.BlockSpec((1, ng), lambda b: (0, 0)),
                  pl.BlockSpec((ng, 1), lambda b: (0, 0)),
                  pl.BlockSpec((1, M_W), lambda b: (0, 0)),
                  pl.BlockSpec((1, 2 * M_HEADS, M_DH, 2 * M_DH), lambda b: (b, 0, 0, 0)),
                  pl.BlockSpec((1, 2 * M_HEADS, 1, M_DH), lambda b: (b, 0, 0, 0))],
        out_specs=[seq(),
                   pl.BlockSpec((1, 2 * M_HEADS, M_DH, 2 * M_DH), lambda b: (b, 0, 0, 0)),
                   pl.BlockSpec((1, 2 * M_HEADS, 1, M_DH), lambda b: (b, 0, 0, 0))],
        out_shape=[jax.ShapeDtypeStruct((B, L, M_W), BF16),
                   jax.ShapeDtypeStruct((B, 2 * M_HEADS, M_DH, 2 * M_DH), F32),
                   jax.ShapeDtypeStruct((B, 2 * M_HEADS, 1, M_DH), F32)],
        scratch_shapes=[pltpu.VMEM((L, M_W), F32),
                        pltpu.VMEM((2 * M_HEADS, M_DH, 2 * M_DH), F32),
                        pltpu.VMEM((2 * M_HEADS, 1, M_DH), F32)],
        compiler_params=_cparams("arbitrary"),
        name="mlstm",
    )(q, k, v, o, g, gt, gate_b.reshape(1, ng), gate_b.reshape(ng, 1), norm_g.reshape(1, M_W), c0, m0)


def _ssd_kernel(xs_ref, b_ref, c_ref, dt_ref, dtt_ref, db_ref, dbt_ref, al_ref, alt_ref, dsk_ref, s0_ref,
                y_ref, sout_ref, s_scr, *, nc, pre_rows):
    s_scr[...] = s0_ref[0, :, 0]
    lower, upper = _tri_masks()
    t_lower = jnp.where(lower, 1.0, 0.0).astype(BF16)
    t_upper = jnp.where(upper, 1.0, 0.0).astype(BF16)
    db = db_ref[0]
    dbt = dbt_ref[0]
    a_row = -jnp.exp(al_ref[0])
    a_col = -jnp.exp(alt_ref[0])
    dsk = dsk_ref[0]

    def prefill(i, carry):
        rows = pl.ds(pl.multiple_of(i * pre_rows, pre_rows), pre_rows)
        y_ref[0, rows, :] = dsk * xs_ref[0, rows, :]
        return carry

    lax.fori_loop(0, (nc * CHUNK) // pre_rows, prefill, 0)

    def chunk_dir(ci, d):
        r0 = pl.multiple_of(ci * CHUNK, CHUNK)
        rows = pl.ds(r0, CHUNK)
        dtc = _softplus(dt_ref[0, 0, ci] + db)
        dtr = _softplus(dtt_ref[0, 0, ci] + dbt)
        t_col, t_row, mask, last = ((t_lower, t_upper, lower, CHUNK - 1) if d == 0
                                    else (t_upper, t_lower, upper, 0))
        acs_c = _cumsum_cols(t_col, dtc * a_row)
        acs_r = _cumsum_rows(dtr * a_col, t_row)
        xs = xs_ref[0, rows, :]
        bc = b_ref[0, rows, :]
        cc = c_ref[0, rows, :]
        cb = _dot_nt(cc, bc)
        s_all = s_scr[d]
        cs = _dot_nt(cc, s_all.astype(BF16))
        for r in range(SSD_HPG):
            j = d * SSD_HPG + r
            lanes = slice(r * SSD_HEADDIM, (r + 1) * SSD_HEADDIM)
            ac = acs_c[:, j:j + 1]
            lm = jnp.exp(jnp.where(mask, ac - acs_r[j:j + 1, :], -jnp.inf))
            xd = xs[:, lanes] * dtc[:, j:j + 1]
            y = _dot((cb * lm).astype(BF16), xd.astype(BF16)) + cs[:, lanes] * jnp.exp(ac)
            y_ref[0, rows, lanes] = y_ref[0, rows, lanes] + y
            a_last = ac[last:last + 1, :]
            xdec = (xd * jnp.exp(a_last - ac)).astype(BF16)
            srows = slice(r * SSD_HEADDIM, (r + 1) * SSD_HEADDIM)
            s_scr[d, srows, :] = jnp.exp(a_last) * s_all[srows, :] + _dot_tn(xdec, bc)

    def body(c, carry):
        chunk_dir(c, 0)
        chunk_dir(nc - 1 - c, 1)
        return carry

    lax.fori_loop(0, nc, body, 0)
    sout_ref[0, :, 0] = s_scr[...]


def _ssd(xs, bm, cm, dt, dtt, dt_bias, a_log, dskip, s0):
    B, L, _ = xs.shape
    nc = L // CHUNK
    G = SSD_GROUPS
    nd = 2 * SSD_HPG
    small = lambda shp: pl.BlockSpec((1,) + shp, lambda b, g: (g, 0, 0))
    return pl.pallas_call(
        functools.partial(_ssd_kernel, nc=nc, pre_rows=min(L, 512)),
        grid=(B, G),
        in_specs=[pl.BlockSpec((1, L, SSD_GW), lambda b, g: (b, 0, g)),
                  pl.BlockSpec((1, L, SSD_STATE), lambda b, g: (b, 0, g)),
                  pl.BlockSpec((1, L, SSD_STATE), lambda b, g: (b, 0, g)),
                  pl.BlockSpec((1, 1, nc, CHUNK, nd), lambda b, g: (b, g, 0, 0, 0)),
                  pl.BlockSpec((1, 1, nc, nd, CHUNK), lambda b, g: (b, g, 0, 0, 0)),
                  small((1, nd)), small((nd, 1)), small((1, nd)), small((nd, 1)), small((1, SSD_GW)),
                  pl.BlockSpec((1, 2, 1, SSD_GW, SSD_STATE), lambda b, g: (b, 0, g, 0, 0))],
        out_specs=[pl.BlockSpec((1, L, SSD_GW), lambda b, g: (b, 0, g)),
                   pl.BlockSpec((1, 2, 1, SSD_GW, SSD_STATE), lambda b, g: (b, 0, g, 0, 0))],
        out_shape=[jax.ShapeDtypeStruct((B, L, SSD_INNER), F32),
                   jax.ShapeDtypeStruct((B, 2, G, SSD_GW, SSD_STATE), F32)],
        scratch_shapes=[pltpu.VMEM((2, SSD_GW, SSD_STATE), F32)],
        compiler_params=_cparams("arbitrary", "arbitrary"),
        name="ssd",
    )(xs, bm, cm, dt, dtt, dt_bias.reshape(G, 1, nd), dt_bias.reshape(G, nd, 1),
      a_log.reshape(G, 1, nd), a_log.reshape(G, nd, 1), dskip.reshape(G, 1, SSD_GW), s0)


def _hy_factors(L):
    return (128, 64) if L == 4096 else (32, (2 * L) // 32)


def _hilo(t):
    return _split2(t.astype(F32))


def _cblock(cr, ci):
    return jnp.concatenate([jnp.concatenate([cr, -ci], axis=-1), jnp.concatenate([ci, cr], axis=-1)], axis=-2)


def _dft_tables(L):
    n1c, n2c = _hy_factors(L)
    N = n1c * n2c
    h1 = n1c // 2
    i32 = jnp.int32
    k1 = jnp.arange(n1c, dtype=i32)
    n2 = jnp.arange(n2c, dtype=i32)

    def stage1(n1_count):
        n1 = jnp.arange(n1_count, dtype=i32)
        idx = (k1[None, :, None] * (n2[:, None, None] + n2c * n1[None, None, :])) % N
        ang = idx.astype(F32) * (2.0 * math.pi / N)
        return jnp.cos(ang), jnp.sin(ang)

    c, s = stage1(h1)
    g_fwd = _cblock(c, -s)
    g_inv = _cblock(jnp.swapaxes(c, 1, 2), jnp.swapaxes(s, 1, 2)) / n1c
    cf, sf = stage1(n1c)
    g_real = jnp.concatenate([cf, -sf], axis=1)
    idx2 = (n2[:, None] * n2[None, :]) % n2c
    ang2 = idx2.astype(F32) * (2.0 * math.pi / n2c)
    c2, s2 = jnp.cos(ang2), jnp.sin(ang2)
    f2 = _cblock(c2, -s2)
    f2_inv = _cblock(c2, s2) / n2c
    return dict(g_fwd=_hilo(g_fwd), g_inv=_hilo(g_inv), g_real=_hilo(g_real), f2=_hilo(f2), f2_inv=_hilo(f2_inv))


def _hy_s1_kernel(x_ref, gh_ref, gl_ref, o_ref, *, nb, C):
    for j in range(nb):
        lanes = slice(j * C, (j + 1) * C)
        o_ref[0, :, lanes] = _dot3(gh_ref[j], gl_ref[j], x_ref[0, :, lanes])


def _hy_stage1(x, tab, C, nb):
    P, R, W = x.shape
    gh, gl = tab
    n2c, rows_out, _ = gh.shape
    tspec = pl.BlockSpec((nb, rows_out, R), lambda p, t: (t, 0, 0))
    return pl.pallas_call(
        functools.partial(_hy_s1_kernel, nb=nb, C=C),
        grid=(P, n2c // nb),
        in_specs=[pl.BlockSpec((1, R, nb * C), lambda p, t: (p, 0, t)), tspec, tspec],
        out_specs=pl.BlockSpec((1, rows_out, nb * C), lambda p, t: (p, 0, t)),
        out_shape=jax.ShapeDtypeStruct((P, rows_out, W), F32),
        compiler_params=_cparams("arbitrary", "arbitrary"),
        name="hyena_stage1",
    )(x, gh, gl)


def _hy_spec_kernel(a_ref, inv_ref, fh_ref, fl_ref, o_ref, *, kb):
    n2c = a_ref.shape[2]
    for kk in range(kb):
        rhs = jnp.concatenate([a_ref[0, kk], a_ref[1, kk]], axis=0)
        X = _dot3(fh_ref[...], fl_ref[...], rhs) * inv_ref[...]
        o_ref[0, kk] = X[0:n2c]
        o_ref[1, kk] = X[n2c:]


def _hy_filter_spectrum(a, inv_norm, f2, kb):
    _, n1c, n2c, W = a.shape
    fh, fl = f2
    blk = pl.BlockSpec((2, kb, n2c, W), lambda t: (0, t, 0, 0))
    fspec = pl.BlockSpec((2 * n2c, 2 * n2c), lambda t: (0, 0))
    return pl.pallas_call(
        functools.partial(_hy_spec_kernel, kb=kb),
        grid=(n1c // kb,),
        in_specs=[blk, pl.BlockSpec((1, W), lambda t: (0, 0)), fspec, fspec],
        out_specs=blk,
        out_shape=jax.ShapeDtypeStruct(a.shape, F32),
        compiler_params=_cparams("arbitrary"),
        name="hyena_filter_spectrum",
    )(a, inv_norm, fh, fl)


def _hy_mid_kernel(a_ref, k_ref, fh_ref, fl_ref, ih_ref, il_ref, o_ref, *, kb):
    n2c = a_ref.shape[2]
    C = a_ref.shape[3]
    res = []
    for kk in range(kb):
        rhs = jnp.concatenate([a_ref[0, kk], a_ref[1, kk]], axis=0)
        X = _dot3(fh_ref[...], fl_ref[...], rhs)
        xr, xi = X[0:n2c], X[n2c:]
        kr, ki = k_ref[0, kk], k_ref[1, kk]
        Y = jnp.concatenate([xr * kr - xi * ki, xr * ki + xi * kr], axis=0)
        res.append(_dot3(ih_ref[...], il_ref[...], Y))
    o_ref[...] = jnp.swapaxes(jnp.stack(res, axis=0), 0, 1).reshape(2, n2c, kb, C)


def _hy_mid(a, kspec, order, f2, f2_inv, kb):
    assert kb == N2_SUB
    P, _, n1c, n2c, C = a.shape
    fh, fl = f2
    ih, il = f2_inv
    fspec = pl.BlockSpec((2 * n2c, 2 * n2c), lambda t, p: (0, 0))
    return pl.pallas_call(
        functools.partial(_hy_mid_kernel, kb=kb),
        grid=(n1c // kb, P),
        in_specs=[pl.BlockSpec((None, 2, kb, n2c, C), lambda t, p: (p, 0, t, 0, 0)),
                  pl.BlockSpec((2, kb, n2c, C), lambda t, p: (0, t, 0, order)),
                  fspec, fspec, fspec, fspec],
        out_specs=pl.BlockSpec((None, 2, n2c, kb, C), lambda t, p: (p, 0, 0, t, 0)),
        out_shape=jax.ShapeDtypeStruct((P, 2, n2c, n1c, C), F32),
        compiler_params=_cparams("arbitrary", "arbitrary"),
        name="hyena_mid",
    )(a, kspec, fh, fl, ih, il)


def _hy_last_kernel(b_ref, z_ref, gate_ref, bias_ref, gih_ref, gil_ref, *rest, fuse, n1c, C):
    if fuse:
        gfh_ref, gfl_ref, zo_ref, ao_ref = rest
    else:
        (zo_ref,) = rest
    for j in range(N2_SUB):
        lanes = slice(j * C, (j + 1) * C)
        bv = jnp.concatenate([b_ref[0, j], b_ref[1, j]], axis=0)
        y = _dot3(gih_ref[j], gil_ref[j], bv)
        znew = gate_ref[0, :, lanes] * (y + z_ref[0, :, lanes] * bias_ref[...])
        zo_ref[0, :, lanes] = znew.astype(zo_ref.dtype)
        if fuse:
            ao_ref[0, :, lanes] = _dot3(gfh_ref[j], gfl_ref[j], znew)


def _hy_last(bf, z, gate, bias, g_inv, g_fwd, out_dtype):
    P, _, n2c, n1c, C = bf.shape
    nhi = n2c // N2_SUB
    R = z.shape[1]
    fuse = g_fwd is not None
    zspec = pl.BlockSpec((1, R, N2_SUB * C), lambda p, t: (p, 0, t))
    in_specs = [pl.BlockSpec((None, 2, N2_SUB, n1c, C), lambda p, t: (p, 0, t, 0, 0)),
                zspec, zspec, pl.BlockSpec((1, C), lambda p, t: (0, 0)),
                pl.BlockSpec((N2_SUB, R, 2 * n1c), lambda p, t: (t, 0, 0)),
                pl.BlockSpec((N2_SUB, R, 2 * n1c), lambda p, t: (t, 0, 0))]
    args = [bf, z, gate, bias, g_inv[0], g_inv[1]]
    out_specs = [zspec]
    out_shape = [jax.ShapeDtypeStruct(z.shape, out_dtype)]
    if fuse:
        in_specs += [pl.BlockSpec((N2_SUB, 2 * n1c, R), lambda p, t: (t, 0, 0))] * 2
        args += [g_fwd[0], g_fwd[1]]
        out_specs.append(pl.BlockSpec((1, 2 * n1c, N2_SUB * C), lambda p, t: (p, 0, t)))
        out_shape.append(jax.ShapeDtypeStruct((P, 2 * n1c, n2c * C), F32))
    return pl.pallas_call(
        functools.partial(_hy_last_kernel, fuse=fuse, n1c=n1c, C=C),
        grid=(P, nhi),
        in_specs=in_specs, out_specs=out_specs, out_shape=out_shape,
        compiler_params=_cparams("arbitrary", "arbitrary"),
        name="hyena_last",
    )(*args)


def _hy_filter_kernel(f_ref, w1_ref, b1_ref, w2_ref, b2_ref, w3_ref, fr_ref, dl_ref, hk_ref, s_ref):
    i = pl.program_id(0)
    f = f_ref[...]
    fr = fr_ref[...]
    h = jnp.sin(fr * (_dot(f.astype(BF16), w1_ref[...].astype(BF16)) + b1_ref[...]))
    h = jnp.sin(fr * (_dot(h.astype(BF16), w2_ref[...].astype(BF16)) + b2_ref[...]))
    hk = _dot(h.astype(BF16), w3_ref[...].astype(BF16)) * jnp.exp(-f[:, 0:1] * dl_ref[...])
    hk_ref[...] = hk
    a = jnp.abs(hk)
    tot = jnp.sum(a, axis=0, keepdims=True)

    @pl.when(i == 0)
    def _():
        back = lax.broadcasted_iota(jnp.int32, tot.shape, 1) >= (tot.shape[1] // 2)
        s_ref[...] = tot - jnp.where(back, a[0:1], 0.0)

    @pl.when(i != 0)
    def _():
        s_ref[...] = s_ref[...] + tot


def _hy_filter(L, w1, b1, w2, b2, w3, freq):
    f32 = F32
    t = jnp.linspace(0.0, 1.0, L, dtype=f32)[:, None]
    ang = (2.0 * math.pi / L) * jnp.arange(L, dtype=f32)[:, None] * \
        jnp.linspace(1e-4, HY_BANDS - 1, HY_BANDS, dtype=f32)[None]
    feats = jnp.concatenate([t, jnp.cos(ang), -jnp.sin(ang)], axis=-1)
    fpad = 128
    feats = jnp.pad(feats, ((0, 0), (0, fpad - HY_EMB)))
    w1p = jnp.pad(w1, ((0, fpad - HY_EMB), (0, 0)))
    deltas = jnp.abs(jnp.linspace(math.log(HY_TARGET) / HY_SLOW, math.log(HY_TARGET) / HY_FAST, HY_W, dtype=f32))
    W = 2 * HY_ORDER * HY_W
    dl = jnp.tile(deltas, 2 * HY_ORDER).reshape(1, W)
    tl = min(L, 512)
    full = lambda shp: pl.BlockSpec(shp, lambda i: (0, 0))
    return pl.pallas_call(
        _hy_filter_kernel,
        grid=(L // tl,),
        in_specs=[pl.BlockSpec((tl, fpad), lambda i: (i, 0)),
                  full((fpad, HY_HID)), full((1, HY_HID)), full((HY_HID, HY_HID)), full((1, HY_HID)),
                  full((HY_HID, W)), full((1, HY_HID)), full((1, W))],
        out_specs=[pl.BlockSpec((tl, W), lambda i: (i, 0)), full((1, W))],
        out_shape=[jax.ShapeDtypeStruct((L, W), F32), jax.ShapeDtypeStruct((1, W), F32)],
        compiler_params=_cparams("arbitrary"),
        name="hyena_filter",
    )(feats, w1p, b1.reshape(1, HY_HID), w2, b2.reshape(1, HY_HID), w3, freq.reshape(1, HY_HID), dl)


def _hy_spectrum(L, tabs, w1, b1, w2, b2, w3, freq):
    n1c, n2c = _hy_factors(L)
    OC = HY_ORDER * HY_W
    hk, s = _hy_filter(L, w1, b1, w2, b2, w3, freq)
    inv_norm = 1.0 / (s[:, :OC] + s[:, OC:])
    ktime = jnp.concatenate([hk[:, :OC], jnp.zeros((1, OC), F32), hk[:0:-1, OC:]], axis=0)
    a = _hy_stage1(ktime.reshape(1, n1c, n2c * OC), tabs["g_real"], OC, 1)
    a = a.reshape(2, n1c, n2c, OC)
    return _hy_filter_spectrum(a, inv_norm, tabs["f2"], 8)


def _hyena(v, x1, x2, kspec, bias, tabs, B, L):
    n1c, n2c = _hy_factors(L)
    C = HY_W
    P = B // 2
    R = n1c
    view = lambda t: t.reshape(P, R, n2c * C)
    nb = 8 if n2c % 8 == 0 else 1
    kb = 8
    z0, g1, g2 = view(v), view(x1), view(x2)
    a = _hy_stage1(z0, tabs["g_fwd"], C, nb)
    bf = _hy_mid(a.reshape(P, 2, n1c, n2c, C), kspec, 0, tabs["f2"], tabs["f2_inv"], kb)
    z1, a = _hy_last(bf, z0, g1, bias[0:1], tabs["g_inv"], tabs["g_fwd"], F32)
    bf = _hy_mid(a.reshape(P, 2, n1c, n2c, C), kspec, 1, tabs["f2"], tabs["f2_inv"], kb)
    (z2,) = _hy_last(bf, z1, g2, bias[1:2], tabs["g_inv"], None, BF16)
    return z2.reshape(B * L, C)


_EVEN_PLAN = (
    (0, HY_W, "conv", 1.0, F32),
    (HY_W, HY_W, "conv", 1.0, F32),
    (2 * HY_W, HY_W, "conv", 1.0, F32),
    (3 * HY_W, M_W, "conv_silu", 1.0, BF16),
    (3 * HY_W + M_W, M_W, "conv_silu", M_DH ** -0.5, BF16),
    (EV_CONV, M_W, "raw", 1.0, BF16),
    (EV_CONV + M_W, M_W, "raw", 1.0, F32),
)
_GN = SSD_GROUPS * SSD_STATE
_ODD_PLAN = (
    (0, SSD_INNER, "raw", 1.0, F32),
    (SSD_INNER, SSD_INNER, "conv_silu", 1.0, F32),
    (2 * SSD_INNER, _GN, "conv_silu", 1.0, BF16),
    (2 * SSD_INNER + _GN, _GN, "conv_silu", 1.0, BF16),
)


def _even_layer(x, mods_l, P, e, l, B, L, st, per_batch_cond, kspec, tabs):
    nc = L // CHUNK
    ng = 4 * M_HEADS
    hv, hx1, hx2, q, k, v, o, g, gt = _inproj(
        x, mods_l, P["norm1_g"][l:l + 1], P["ev_in_w"], P["ev_in_wt"], P["ev_conv_w"], P["ev_conv_b"], e,
        plan=_EVEN_PLAN, conv0=0, seq_len=L, per_batch_cond=per_batch_cond)
    y_hy = _hyena(hv, hx1, hx2, kspec, P["hy_bias"][e], tabs, B, L)
    seq = lambda t: t.reshape(B, L, M_W)
    g4 = g.reshape(B, nc, CHUNK, ng)
    gt4 = gt.reshape(ng, B, nc, CHUNK).transpose(1, 2, 0, 3)
    c0, m0 = st
    y_m, c_new, m_new = _mlstm(seq(q), seq(k), seq(v), seq(o), g4, gt4, P["m_gate_b"][e], P["m_norm_g"][e], c0, m0)
    x = _outproj_even(y_hy, y_m.reshape(B * L, M_W), x, mods_l, P["ev_out_w"], e,
                      seq_len=L, per_batch_cond=per_batch_cond)
    return x, (c_new, m_new)


def _odd_layer(x, mods_l, P, o, l, B, L, s0, per_batch_cond):
    nc = L // CHUNK
    G, R = SSD_GROUPS, SSD_HPG
    z, xs, bm, cm, dt, dtt = _inproj(
        x, mods_l, P["norm1_g"][l:l + 1], P["od_in_w"], P["od_in_wt"], P["od_conv_w"], P["od_conv_b"], o,
        plan=_ODD_PLAN, conv0=SSD_INNER, seq_len=L, per_batch_cond=per_batch_cond)
    dt5 = dt.reshape(B, nc, CHUNK, 2, G, R).transpose(0, 4, 1, 2, 3, 5).reshape(B, G, nc, CHUNK, 2 * R)
    dtt5 = dtt.reshape(2, G, R, B, nc, CHUNK).transpose(3, 1, 4, 0, 2, 5).reshape(B, G, nc, 2 * R, CHUNK)
    regroup = lambda p: p.reshape(2, G, R).transpose(1, 0, 2).reshape(G, 2 * R)
    dskip = jnp.repeat(P["ssd_D"][o], SSD_HEADDIM).reshape(G, SSD_GW)
    y, s_new = _ssd(xs.reshape(B, L, SSD_INNER), bm.reshape(B, L, _GN), cm.reshape(B, L, _GN), dt5, dtt5,
                    regroup(P["ssd_dt_bias"][o]), regroup(P["ssd_A_log"][o]), dskip, s0)
    x = _outproj_odd(y.reshape(B * L, SSD_INNER), z, x, mods_l, P["ssd_norm_g"], P["od_out_w"], o,
                     seq_len=L, per_batch_cond=per_batch_cond)
    return x, s_new


def _to_colmajor(x, B, L):
    rows = L // GRID_W
    return x.reshape(B, rows, GRID_W, D_MODEL).swapaxes(1, 2).reshape(B * L, D_MODEL)


def _from_colmajor(x, B, L):
    rows = L // GRID_W
    return x.reshape(B, GRID_W, rows, D_MODEL).swapaxes(1, 2).reshape(B * L, D_MODEL)


def _trunk(x, mods, P, B, L, init_states, per_batch_cond, grid, kspecs, tabs):
    states = []
    for l in range(DEPTH):
        if grid and l == 2:
            x = _to_colmajor(x, B, L)
        mods_l = mods[l]
        if l % 2 == 0:
            x, st = _even_layer(x, mods_l, P, l // 2, l, B, L, init_states[l], per_batch_cond, kspecs[l // 2], tabs)
        else:
            x, st = _odd_layer(x, mods_l, P, l // 2, l, B, L, init_states[l], per_batch_cond)
        states.append(st)
        x = _ffn(x, mods_l, P["norm2_g"].reshape(DEPTH, 1, D_MODEL), P["ffn_w1"], P["ffn_w3"], P["ffn_w2"],
                 P["final_g"].reshape(1, D_MODEL), l, seq_len=L, per_batch_cond=per_batch_cond,
                 final=(l == DEPTH - 1))
    if grid:
        x = _from_colmajor(x, B, L)
    return x, states


def _pack_mlstm_state(C, n, m):
    B = C.shape[0]
    ext = jnp.concatenate([C, n[..., None], jnp.zeros(C.shape[:-1] + (M_DH - 1,), F32)], axis=-1)
    m_rep = jnp.broadcast_to(m.reshape(B, 2 * M_HEADS, 1, 1), (B, 2 * M_HEADS, 1, M_DH))
    return ext.reshape(B, 2 * M_HEADS, M_DH, 2 * M_DH), m_rep


def kernel(x_prompt, x_sample, state_mlstm_C, state_mlstm_n, state_mlstm_m, state_ssd, c, c_ctx,
           mod_w, mod_b, norm1_g, norm2_g, ffn_w1, ffn_w3, ffn_w2, final_g,
           ev_in_w, ev_conv_w, ev_conv_b, hy_w1, hy_b1, hy_w2, hy_b2, hy_w3, hy_freq, hy_bias,
           m_gate_b, m_norm_g, ev_out_w,
           od_in_w, od_conv_w, od_conv_b, ssd_dt_bias, ssd_A_log, ssd_D, ssd_norm_g, od_out_w):
    Bp, Lp, _ = x_prompt.shape
    Bs, Ls, _ = x_sample.shape
    bf = lambda w: w.astype(BF16)
    P = dict(norm1_g=norm1_g, norm2_g=norm2_g, ffn_w1=bf(ffn_w1), ffn_w3=bf(ffn_w3), ffn_w2=bf(ffn_w2),
             final_g=final_g, ev_in_w=bf(ev_in_w), ev_in_wt=bf(ev_in_w[:, :, EV_PROJ - 4 * M_HEADS:]).swapaxes(1, 2),
             ev_conv_w=ev_conv_w, ev_conv_b=ev_conv_b, hy_bias=hy_bias, m_gate_b=m_gate_b, m_norm_g=m_norm_g,
             ev_out_w=bf(ev_out_w), od_in_w=bf(od_in_w),
             od_in_wt=bf(od_in_w[:, :, OD_PROJ - 2 * SSD_HEADS:]).swapaxes(1, 2),
             od_conv_w=od_conv_w, od_conv_b=od_conv_b, ssd_dt_bias=ssd_dt_bias, ssd_A_log=ssd_A_log,
             ssd_D=ssd_D, ssd_norm_g=ssd_norm_g.reshape(-1, 1, SSD_INNER), od_out_w=bf(od_out_w))

    ncond = 16
    cond = jnp.concatenate([c_ctx[None, :], c, jnp.zeros((ncond - 1 - Bs, D_MODEL), F32)], axis=0)
    mods = _modulation(cond, mod_w, mod_b).reshape(DEPTH, ncond, 6, D_MODEL)

    tabs = {L: _dft_tables(L) for L in (Lp, Ls)}
    kspecs = {L: [_hy_spectrum(L, tabs[L], hy_w1[e], hy_b1[e], hy_w2[e], hy_b2[e], hy_w3[e], hy_freq[e])
                  for e in range((DEPTH + 1) // 2)] for L in (Lp, Ls)}

    G = SSD_GROUPS
    zero_m = _pack_mlstm_state(jnp.zeros((Bp, 2, M_HEADS, M_DH, M_DH), F32), jnp.zeros((Bp, 2, M_HEADS, M_DH), F32),
                               jnp.zeros((Bp, 2, M_HEADS), F32))
    zero_s = jnp.zeros((Bp, 2, G, SSD_GW, SSD_STATE), F32)
    ctx_init = [zero_m if l % 2 == 0 else zero_s for l in range(DEPTH)]
    y_prompt, ctx_states = _trunk(x_prompt.reshape(Bp * Lp, D_MODEL), mods, P, Bp, Lp, ctx_init, False, False,
                                  kspecs[Lp], tabs[Lp])

    lat_init = []
    for l in range(DEPTH):
        if l % 2 == 0:
            e = l // 2
            lat_init.append(_pack_mlstm_state(state_mlstm_C[:, e], state_mlstm_n[:, e], state_mlstm_m[:, e]))
        else:
            lat_init.append(state_ssd[:, l // 2].reshape(Bs, 2, G, SSD_GW, SSD_STATE))
    y_sample, _ = _trunk(x_sample.reshape(Bs * Ls, D_MODEL), mods, P, Bs, Ls, lat_init, True, True,
                         kspecs[Ls], tabs[Ls])

    cs = [ctx_states[l][0].reshape(Bp, 2, M_HEADS, M_DH, 2 * M_DH) for l in range(0, DEPTH, 2)]
    ms = [ctx_states[l][1][:, :, 0, 0].reshape(Bp, 2, M_HEADS) for l in range(0, DEPTH, 2)]
    new_C = jnp.stack([t[..., :M_DH] for t in cs], axis=1)
    new_n = jnp.stack([t[..., M_DH] for t in cs], axis=1)
    new_m = jnp.stack(ms, axis=1)
    new_ssd = jnp.stack([ctx_states[l].reshape(Bp, 2, SSD_HEADS, SSD_HEADDIM, SSD_STATE)
                         for l in range(1, DEPTH, 2)], axis=1)
    return (y_prompt.reshape(Bp, Lp, D_MODEL), y_sample.reshape(Bs, Ls, D_MODEL), new_C, new_n, new_m, new_ssd)
```

```python
import functools
import math

import jax
import jax.numpy as jnp
from jax import lax
from jax.experimental import pallas as pl
from jax.experimental.pallas import tpu as pltpu

F32 = jnp.float32
BF16 = jnp.bfloat16

D_MODEL = 1024
DEPTH = 4
GRID_W = 64
EPS = 1e-6
HY_W = D_MODEL // 2
HY_ORDER = 2
HY_EMB = 33
HY_BANDS = (HY_EMB - 1) // 2
HY_HID = 64
HY_TARGET = 1e-2
HY_FAST = 0.3
HY_SLOW = 1.5
M_HEADS = 4
M_DH = 128
M_W = M_HEADS * M_DH
EV_CONV = 3 * HY_W + 2 * M_W
EV_PROJ = EV_CONV + 2 * M_W + 4 * M_HEADS
SSD_INNER = 2 * D_MODEL
SSD_HEADDIM = 64
SSD_HEADS = SSD_INNER // SSD_HEADDIM
SSD_GROUPS = 8
SSD_HPG = SSD_HEADS // SSD_GROUPS
SSD_STATE = 128
SSD_GW = SSD_HPG * SSD_HEADDIM
SSD_CONV_DIM = SSD_INNER + 2 * SSD_GROUPS * SSD_STATE
OD_PROJ = SSD_INNER + SSD_CONV_DIM + 2 * SSD_HEADS
FFN_HIDDEN = -(-8 * D_MODEL // (3 * 256)) * 256

V7X_VMEM_BYTES = 64 * 1024 * 1024
VMEM_LIMIT = V7X_VMEM_BYTES * 3 // 4
ROW_TILE = 256
HALO = 16
COL_CHUNK = 512
N2_SUB = 8
SCAN_T = 128


def _cparams(*sem):
    return pltpu.CompilerParams(dimension_semantics=sem, vmem_limit_bytes=VMEM_LIMIT)


def _dot(a, b):
    return jnp.dot(a, b, preferred_element_type=F32)


def _dot_nt(a, b):
    return lax.dot_general(a, b, (((1,), (1,)), ((), ())), preferred_element_type=F32)


def _split2(x):
    h = x.astype(BF16)
    return h, (x - h.astype(F32)).astype(BF16)


def _dot3(th, tl, x):
    xh, xl = _split2(x)
    return _dot(th, xh) + _dot(th, xl) + _dot(tl, xh)


def _dot2(x, t):
    xh, xl = _split2(x)
    return _dot(xh, t) + _dot(xl, t)


def _sigmoid(x):
    return 1.0 / (1.0 + jnp.exp(-x))


def _silu(x):
    return x * _sigmoid(x)


def _log_sigmoid(x):
    return jnp.minimum(x, 0.0) - jnp.log(1.0 + jnp.exp(-jnp.abs(x)))


def _softplus(x):
    return jnp.maximum(x, 0.0) + jnp.log(1.0 + jnp.exp(-jnp.abs(x)))


def _rms(x):
    return x * lax.rsqrt(jnp.mean(x * x, axis=-1, keepdims=True) + EPS)


def _mod_kernel(c_ref, w_ref, b_ref, o_ref):
    a = _silu(c_ref[...]).astype(BF16)
    o_ref[0] = _dot(a, w_ref[0].astype(BF16)) + b_ref[0]


def _modulation(cond, mod_w, mod_b):
    nc = cond.shape[0]
    tn = 1536
    return pl.pallas_call(
        _mod_kernel,
        grid=(DEPTH, 6 * D_MODEL // tn),
        in_specs=[pl.BlockSpec((nc, D_MODEL), lambda l, j: (0, 0)),
                  pl.BlockSpec((1, D_MODEL, tn), lambda l, j: (l, 0, j)),
                  pl.BlockSpec((1, 1, tn), lambda l, j: (l, 0, j))],
        out_specs=pl.BlockSpec((1, nc, tn), lambda l, j: (l, 0, j)),
        out_shape=jax.ShapeDtypeStruct((DEPTH, nc, 6 * D_MODEL), F32),
        compiler_params=_cparams("arbitrary", "arbitrary"),
        name="modulation",
    )(cond, mod_w, mod_b.reshape(DEPTH, 1, 6 * D_MODEL))


def _inproj_kernel(x_ref, xp_ref, xn_ref, mod_ref, g_ref, w_ref, wt_ref, cw_ref, cb_ref, *rest,
                   plan, nt_plan, tiles_per_seq, conv0):
    nseg = len(plan)
    outs = rest[:nseg]
    nt_outs = rest[nseg:nseg + len(nt_plan)]
    hbuf = rest[nseg + len(nt_plan)]
    tm = x_ref.shape[0]
    i = pl.program_id(0)
    j = i % tiles_per_seq
    pmask = (j != 0).astype(F32)
    nmask = (j != tiles_per_seq - 1).astype(F32)
    mod = mod_ref[0]
    sh, sc = mod[0:1], mod[1:2]
    g = g_ref[...]

    def normmod(xv):
        return (_rms(xv) * g * (1.0 + sc) + sh).astype(BF16)

    hbuf[0:tm] = normmod(x_ref[...])
    hbuf[tm:tm + HALO] = normmod(xp_ref[...])
    hbuf[tm + HALO:tm + 2 * HALO] = normmod(xn_ref[...])
    rowid = lax.broadcasted_iota(jnp.int32, (tm, COL_CHUNK), 0)

    for (c0, width, kind, scale, transposed), o_ref in zip(plan, outs):
        for cc in range(c0, c0 + width, COL_CHUNK):
            w = w_ref[0, :, cc:cc + COL_CHUNK]
            if kind == "raw":
                y = _dot(hbuf[0:tm], w)
            else:
                full = _dot(hbuf[...], w)
                acc = full[0:tm]
                ap = full[tm + HALO - 1:tm + HALO] * pmask
                an = full[tm + HALO:tm + HALO + 1] * nmask
                prev = jnp.where(rowid == 0, ap, pltpu.roll(acc, 1, 0))
                nxt = jnp.where(rowid == tm - 1, an, pltpu.roll(acc, tm - 1, 0))
                cw = cw_ref[0, :, cc - conv0:cc - conv0 + COL_CHUNK]
                cb = cb_ref[0, :, cc - conv0:cc - conv0 + COL_CHUNK]
                y = cb + cw[0:1] * prev + cw[1:2] * acc + cw[2:3] * nxt
                if kind == "conv_silu":
                    y = _silu(y)
                if scale != 1.0:
                    y = y * scale
            if transposed:
                o_ref[cc - c0:cc - c0 + COL_CHUNK, :] = y.T.astype(o_ref.dtype)
            else:
                o_ref[:, cc - c0:cc - c0 + COL_CHUNK] = y.astype(o_ref.dtype)

    for (r0, rows), o_ref in zip(nt_plan, nt_outs):
        o_ref[...] = _dot_nt(wt_ref[0, r0:r0 + rows, :], hbuf[0:tm]).astype(o_ref.dtype)


def _inproj(x, mods_l, g_l, w_all, wt_all, cw_all, cb_all, layer_idx, *, plan, nt_plan, conv0, seq_len,
            per_batch_cond):
    M = x.shape[0]
    tm = ROW_TILE
    tps = seq_len // tm
    nt = M // tm
    hb = tm // HALO
    ntot = w_all.shape[2]
    tw = wt_all.shape[1]
    cdim = cw_all.shape[2]
    e = layer_idx
    cond_map = (lambda i: (1 + i // tps, 0, 0)) if per_batch_cond else (lambda i: (0, 0, 0))
    in_specs = [
        pl.BlockSpec((tm, D_MODEL), lambda i: (i, 0)),
        pl.BlockSpec((HALO, D_MODEL), lambda i: (jnp.maximum(i * hb - 1, 0), 0)),
        pl.BlockSpec((HALO, D_MODEL), lambda i: (jnp.minimum((i + 1) * hb, nt * hb - 1), 0)),
        pl.BlockSpec((1, 6, D_MODEL), cond_map),
        pl.BlockSpec((1, D_MODEL), lambda i: (0, 0)),
        pl.BlockSpec((1, D_MODEL, ntot), lambda i: (e, 0, 0), pipeline_mode=pl.Buffered(1)),
        pl.BlockSpec((1, tw, D_MODEL), lambda i: (e, 0, 0)),
        pl.BlockSpec((1, 3, cdim), lambda i: (e, 0, 0)),
        pl.BlockSpec((1, 1, cdim), lambda i: (e, 0, 0)),
    ]
    out_specs, out_shape = [], []
    for (c0, width, kind, scale, transposed, dt) in plan:
        if transposed:
            out_specs.append(pl.BlockSpec((width, tm), lambda i: (0, i)))
            out_shape.append(jax.ShapeDtypeStruct((width, M), dt))
        else:
            out_specs.append(pl.BlockSpec((tm, width), lambda i: (i, 0)))
            out_shape.append(jax.ShapeDtypeStruct((M, width), dt))
    for (r0, rows, dt) in nt_plan:
        out_specs.append(pl.BlockSpec((rows, tm), lambda i: (0, i)))
        out_shape.append(jax.ShapeDtypeStruct((rows, M), dt))
    kplan = tuple(p[:5] for p in plan)
    knt = tuple(p[:2] for p in nt_plan)
    return pl.pallas_call(
        functools.partial(_inproj_kernel, plan=kplan, nt_plan=knt, tiles_per_seq=tps, conv0=conv0),
        grid=(nt,),
        in_specs=in_specs,
        out_specs=out_specs,
        out_shape=out_shape,
        scratch_shapes=[pltpu.VMEM((tm + 2 * HALO, D_MODEL), BF16)],
        compiler_params=_cparams("arbitrary"),
        name="inproj",
    )(x, x, x, mods_l, g_l, w_all, wt_all, cw_all, cb_all.reshape(cb_all.shape[0], 1, cdim))


def _outproj_even_kernel(yh_ref, ym_ref, x_ref, mod_ref, w_ref, o_ref):
    out = _dot(yh_ref[...], w_ref[0, 0:HY_W]) + _dot(ym_ref[...], w_ref[0, HY_W:])
    o_ref[...] = x_ref[...] + mod_ref[0][2:3] * out


def _outproj_odd_kernel(y_ref, z_ref, x_ref, mod_ref, ng_ref, w_ref, o_ref):
    u = y_ref[...] * _silu(z_ref[...])
    un = (_rms(u) * ng_ref[0]).astype(BF16)
    o_ref[...] = x_ref[...] + mod_ref[0][2:3] * _dot(un, w_ref[0])


def _cond_map(seq_len, per_batch_cond):
    tps = seq_len // ROW_TILE
    return (lambda i: (1 + i // tps, 0, 0)) if per_batch_cond else (lambda i: (0, 0, 0))


def _outproj_even(yh, ym, x, mods_l, w_all, e, *, seq_len, per_batch_cond):
    M = x.shape[0]
    tm = ROW_TILE
    return pl.pallas_call(
        _outproj_even_kernel,
        grid=(M // tm,),
        in_specs=[pl.BlockSpec((tm, HY_W), lambda i: (i, 0)),
                  pl.BlockSpec((tm, M_W), lambda i: (i, 0)),
                  pl.BlockSpec((tm, D_MODEL), lambda i: (i, 0)),
                  pl.BlockSpec((1, 6, D_MODEL), _cond_map(seq_len, per_batch_cond)),
                  pl.BlockSpec((1, HY_W + M_W, D_MODEL), lambda i: (e, 0, 0), pipeline_mode=pl.Buffered(1))],
        out_specs=pl.BlockSpec((tm, D_MODEL), lambda i: (i, 0)),
        out_shape=jax.ShapeDtypeStruct((M, D_MODEL), F32),
        compiler_params=_cparams("arbitrary"),
        name="outproj_even",
    )(yh, ym, x, mods_l, w_all)


def _outproj_odd(y, z, x, mods_l, ng_all, w_all, o, *, seq_len, per_batch_cond):
    M = x.shape[0]
    tm = ROW_TILE
    return pl.pallas_call(
        _outproj_odd_kernel,
        grid=(M // tm,),
        in_specs=[pl.BlockSpec((tm, SSD_INNER), lambda i: (i, 0)),
                  pl.BlockSpec((tm, SSD_INNER), lambda i: (i, 0)),
                  pl.BlockSpec((tm, D_MODEL), lambda i: (i, 0)),
                  pl.BlockSpec((1, 6, D_MODEL), _cond_map(seq_len, per_batch_cond)),
                  pl.BlockSpec((1, 1, SSD_INNER), lambda i: (o, 0, 0)),
                  pl.BlockSpec((1, SSD_INNER, D_MODEL), lambda i: (o, 0, 0), pipeline_mode=pl.Buffered(1))],
        out_specs=pl.BlockSpec((tm, D_MODEL), lambda i: (i, 0)),
        out_shape=jax.ShapeDtypeStruct((M, D_MODEL), F32),
        compiler_params=_cparams("arbitrary"),
        name="outproj_odd",
    )(y, z, x, mods_l, ng_all, w_all)


FFN_CHUNK = FFN_HIDDEN // 2


def _ffn_kernel(x_ref, mod_ref, g_ref, w1_ref, w3_ref, w2_ref, fg_ref, o_ref, *, final):
    x = x_ref[...]
    mod = mod_ref[0]
    h = (_rms(x) * g_ref[0] * (1.0 + mod[4:5]) + mod[3:4]).astype(BF16)
    ff = jnp.zeros(x.shape, F32)
    for c0 in range(0, FFN_HIDDEN, FFN_CHUNK):
        a = _dot(h, w1_ref[0, :, c0:c0 + FFN_CHUNK])
        b = _dot(h, w3_ref[0, :, c0:c0 + FFN_CHUNK])
        u = (_silu(a) * b).astype(BF16)
        ff = ff + _dot(u, w2_ref[0, c0:c0 + FFN_CHUNK, :])
    xo = x + mod[5:6] * ff
    if final:
        xo = _rms(xo) * fg_ref[...]
    o_ref[...] = xo


def _ffn(x, mods_l, g_all, w1, w3, w2, fg, l, *, seq_len, per_batch_cond, final):
    M = x.shape[0]
    tm = ROW_TILE
    wspec = lambda shape: pl.BlockSpec(shape, lambda i: (l, 0, 0), pipeline_mode=pl.Buffered(1))
    return pl.pallas_call(
        functools.partial(_ffn_kernel, final=final),
        grid=(M // tm,),
        in_specs=[pl.BlockSpec((tm, D_MODEL), lambda i: (i, 0)),
                  pl.BlockSpec((1, 6, D_MODEL), _cond_map(seq_len, per_batch_cond)),
                  pl.BlockSpec((1, 1, D_MODEL), lambda i: (l, 0, 0)),
                  wspec((1, D_MODEL, FFN_HIDDEN)), wspec((1, D_MODEL, FFN_HIDDEN)),
                  wspec((1, FFN_HIDDEN, D_MODEL)),
                  pl.BlockSpec((1, D_MODEL), lambda i: (0, 0))],
        out_specs=pl.BlockSpec((tm, D_MODEL), lambda i: (i, 0)),
        out_shape=jax.ShapeDtypeStruct((M, D_MODEL), F32),
        compiler_params=_cparams("arbitrary"),
        name="ffn",
    )(x, mods_l, g_all, w1, w3, w2, fg)


def _scan_consts(d):
    T = SCAN_T
    row = lax.broadcasted_iota(jnp.int32, (T, T), 0)
    col = lax.broadcasted_iota(jnp.int32, (T, T), 1)
    if d == 0:
        mask, strict, last = row <= col, col > row, T - 1
    else:
        mask, strict, last = row >= col, col < row, 0
    v_mat = jnp.where(mask, 1.0, 0.0).astype(BF16)
    return mask, strict, v_mat, last


def _mlstm_kernel(qt_ref, k_ref, vt_ref, o_ref, gt_ref, gb_ref, ng_ref, c0_ref, m0_ref,
                  y_ref, cst_ref, mst_ref, h_scr, c_scr, m_scr, lhs_scr, sb_scr, *, nc):
    T = SCAN_T
    c_scr[...] = c0_ref[0]
    m_scr[...] = m0_ref[0]
    row = lax.broadcasted_iota(jnp.int32, (T, T), 0)
    col = lax.broadcasted_iota(jnp.int32, (T, T), 1)
    eye = row == col
    ones_blk = jnp.where(lax.broadcasted_iota(jnp.int32, (M_DH, T), 0) == 0, 1.0, 0.0)
    consts = (_scan_consts(0), _scan_consts(1))
    gb = gb_ref[...]

    def chunk_dir(ci, d, accumulate):
        mask, strict, v_mat, last = consts[d]
        w2 = jnp.concatenate([v_mat, v_mat], axis=0)
        c0 = pl.multiple_of(ci * T, T)
        lanes = pl.ds(c0, T)
        graw = gt_ref[:, lanes] + gb
        lf = _log_sigmoid(graw)
        b_all = _dot2(lf, v_mat)
        for h in range(M_HEADS):
            fi = (2 * d + 1) * M_HEADS + h
            ii = (2 * d) * M_HEADS + h
            tr = slice(h * T, (h + 1) * T)
            a = jnp.concatenate([jnp.where(strict, lf[fi:fi + 1], 0.0), jnp.where(eye, graw[ii:ii + 1], 0.0)],
                                axis=1)
            hi, lo = _split2(a)
            lhs_scr[d, 0, tr, :] = hi
            lhs_scr[d, 1, tr, :] = lo
        dall = _dot(lhs_scr[d, 0], w2) + _dot(lhs_scr[d, 1], w2)
        rows_of = []
        for h in range(M_HEADS):
            fi = (2 * d + 1) * M_HEADS + h
            sidx = d * M_HEADS + h
            hl = slice(h * M_DH, (h + 1) * M_DH)
            tr = slice(h * T, (h + 1) * T)
            dlog = jnp.where(mask, dall[tr], -jnp.inf)
            m_row = m_scr[sidx]
            inter = b_all[fi:fi + 1] + m_row
            mt = jnp.maximum(inter, jnp.max(dlog, axis=0, keepdims=True))
            w = jnp.exp(dlog - mt)
            sb_scr[d, tr, :] = (_dot(k_ref[pl.ds(c0, T), hl], qt_ref[hl, lanes]) * w).astype(BF16)
            rows_of.append((m_row, inter, mt))
        for h in range(M_HEADS):
            fi = (2 * d + 1) * M_HEADS + h
            ii = (2 * d) * M_HEADS + h
            sidx = d * M_HEADS + h
            hl = slice(h * M_DH, (h + 1) * M_DH)
            tr = slice(h * T, (h + 1) * T)
            m_row, inter, mt = rows_of[h]
            i_row, b_row = graw[ii:ii + 1], b_all[fi:fi + 1]
            a_in = jnp.exp(inter - mt)
            kh = k_ref[pl.ds(c0, T), hl]
            qt = qt_ref[hl, lanes]
            vt = vt_ref[hl, lanes]
            vext = jnp.concatenate([vt, ones_blk.astype(BF16)], axis=0)
            ct = c_scr[sidx]
            tot = _dot(vext, sb_scr[d, tr, :]) + a_in * _dot(ct.astype(BF16), qt)
            den = tot[M_DH:M_DH + 1]
            ht = tot[0:M_DH] / jnp.maximum(jnp.abs(den), jnp.exp(-mt))
            if accumulate:
                h_scr[hl, lanes] = h_scr[hl, lanes] + ht
            else:
                h_scr[hl, lanes] = ht
            m_new = jnp.broadcast_to(mt[:, last:last + 1], (1, T))
            b_last = jnp.broadcast_to(b_row[:, last:last + 1], (1, T))
            w_end = jnp.exp(b_last - b_row + i_row - m_new)
            dec = jnp.exp(b_last + m_row - m_new)
            vw = jnp.concatenate([(vt.astype(F32) * w_end).astype(BF16), (ones_blk * w_end).astype(BF16)], axis=0)
            c_scr[sidx] = dec * ct + _dot(vw, kh)
            m_scr[sidx] = m_new

    half = nc // 2

    def first(c, carry):
        chunk_dir(c, 0, False)
        chunk_dir(nc - 1 - c, 1, False)
        return carry

    def second(c, carry):
        chunk_dir(c, 0, True)
        chunk_dir(nc - 1 - c, 1, True)
        return carry

    lax.fori_loop(0, half, first, 0)
    lax.fori_loop(half, nc, second, 0)

    def epilogue(i, carry):
        c0 = pl.multiple_of(i * T, T)
        for h in range(M_HEADS):
            hl = slice(h * M_DH, (h + 1) * M_DH)
            hh = h_scr[hl, pl.ds(c0, T)]
            hn = hh * lax.rsqrt(jnp.mean(hh * hh, axis=0, keepdims=True) + EPS)
            y = hn.T * ng_ref[:, hl] * _sigmoid(o_ref[pl.ds(c0, T), hl])
            y_ref[pl.ds(c0, T), hl] = y.astype(y_ref.dtype)
        return carry

    lax.fori_loop(0, nc, epilogue, 0)
    cst_ref[0] = c_scr[...]
    mst_ref[0] = m_scr[...]


def _mlstm(qt, k, vt, o, gt, gate_b, norm_g, c0, m0, B, L):
    assert L % (2 * SCAN_T) == 0 and M_DH == SCAN_T
    nc = L // SCAN_T
    ng = 4 * M_HEADS
    M = B * L
    one = pl.Buffered(1)
    st_spec = pl.BlockSpec((1, 2 * M_HEADS, 2 * M_DH, M_DH), lambda b: (b, 0, 0, 0))
    m_spec = pl.BlockSpec((1, 2 * M_HEADS, 1, M_DH), lambda b: (b, 0, 0, 0))
    return pl.pallas_call(
        functools.partial(_mlstm_kernel, nc=nc),
        grid=(B,),
        in_specs=[pl.BlockSpec((M_W, L), lambda b: (0, b), pipeline_mode=one),
                  pl.BlockSpec((L, M_W), lambda b: (b, 0), pipeline_mode=one),
                  pl.BlockSpec((M_W, L), lambda b: (0, b), pipeline_mode=one),
                  pl.BlockSpec((L, M_W), lambda b: (b, 0), pipeline_mode=one),
                  pl.BlockSpec((ng, L), lambda b: (0, b)),
                  pl.BlockSpec((ng, SCAN_T), lambda b: (0, 0)),
                  pl.BlockSpec((1, M_W), lambda b: (0, 0)),
                  st_spec, m_spec],
        out_specs=[pl.BlockSpec((L, M_W), lambda b: (b, 0)), st_spec, m_spec],
        out_shape=[jax.ShapeDtypeStruct((M, M_W), BF16),
                   jax.ShapeDtypeStruct((B, 2 * M_HEADS, 2 * M_DH, M_DH), F32),
                   jax.ShapeDtypeStruct((B, 2 * M_HEADS, 1, M_DH), F32)],
        scratch_shapes=[pltpu.VMEM((M_W, L), F32),
                        pltpu.VMEM((2 * M_HEADS, 2 * M_DH, M_DH), F32),
                        pltpu.VMEM((2 * M_HEADS, 1, M_DH), F32),
                        pltpu.VMEM((2, 2, M_HEADS * SCAN_T, 2 * SCAN_T), BF16),
                        pltpu.VMEM((2, M_HEADS * SCAN_T, SCAN_T), BF16)],
        compiler_params=_cparams("arbitrary"),
        name="mlstm",
    )(qt, k, vt, o, gt, jnp.broadcast_to(gate_b.reshape(ng, 1), (ng, SCAN_T)), norm_g.reshape(1, M_W), c0, m0)


def _ssd_kernel(xt_ref, b_ref, ct_ref, dtt_ref, db_ref, al_ref, dsk_ref, s0_ref,
                y_ref, sout_ref, yf_scr, yb_scr, s_scr, lhs_scr, p_scr, cbt_scr, xdec_scr, *, nc):
    T = SCAN_T
    s_scr[...] = s0_ref[0, :, 0]
    consts = (_scan_consts(0), _scan_consts(1))
    db = db_ref[0]
    a_neg = -jnp.exp(al_ref[0])

    def chunk_dir(ci, d):
        mask, strict, v_mat, last = consts[d]
        c0 = pl.multiple_of(ci * T, T)
        lanes = pl.ds(c0, T)
        hr = slice(d * SSD_HPG, (d + 1) * SSD_HPG)
        dt = _softplus(dtt_ref[hr, lanes] + db[hr])
        a = dt * a_neg[hr]
        acs = _dot2(a, v_mat)
        bc = b_ref[pl.ds(c0, T), :]
        ctc = ct_ref[:, lanes]
        cbt_scr[d] = _dot(bc, ctc)
        for r in range(SSD_HPG):
            tr = slice(r * T, (r + 1) * T)
            hi, lo = _split2(jnp.where(strict, a[r:r + 1], 0.0))
            lhs_scr[d, 0, tr, :] = hi
            lhs_scr[d, 1, tr, :] = lo
        dall = _dot(lhs_scr[d, 0], v_mat) + _dot(lhs_scr[d, 1], v_mat)
        for r in range(SSD_HPG):
            tr = slice(r * T, (r + 1) * T)
            lm = jnp.exp(jnp.where(mask, dall[tr], -jnp.inf))
            p_scr[d, tr, :] = (cbt_scr[d] * lm).astype(BF16)
        sc = _dot(s_scr[d].astype(BF16), ctc)
        y_scr = yf_scr if d == 0 else yb_scr
        a_lasts = []
        for r in range(SSD_HPG):
            pr = slice(r * SSD_HEADDIM, (r + 1) * SSD_HEADDIM)
            tr = slice(r * T, (r + 1) * T)
            acs_r = acs[r:r + 1]
            xd = xt_ref[pr, lanes] * dt[r:r + 1]
            y_scr[pr, lanes] = _dot(xd.astype(BF16), p_scr[d, tr, :]) + jnp.exp(acs_r) * sc[pr]
            a_last = jnp.broadcast_to(acs_r[:, last:last + 1], (1, T))
            xdec_scr[d, pr, :] = (xd * jnp.exp(a_last - acs_r)).astype(BF16)
            a_lasts.append(a_last)
        upd = _dot(xdec_scr[d], bc)
        for r in range(SSD_HPG):
            pr = slice(r * SSD_HEADDIM, (r + 1) * SSD_HEADDIM)
            s_scr[d, pr, :] = jnp.exp(a_lasts[r]) * s_scr[d, pr, :] + upd[pr]

    def body(c, carry):
        chunk_dir(c, 0)
        chunk_dir(nc - 1 - c, 1)
        return carry

    lax.fori_loop(0, nc, body, 0)

    dsk = dsk_ref[0]

    def epilogue(i, carry):
        c0 = pl.multiple_of(i * T, T)
        lanes = pl.ds(c0, T)
        yt = yf_scr[:, lanes] + yb_scr[:, lanes] + dsk * xt_ref[:, lanes]
        y_ref[pl.ds(c0, T), :] = yt.T
        return carry

    lax.fori_loop(0, nc, epilogue, 0)
    sout_ref[0, :, 0] = s_scr[...]


def _ssd(xt, bm, ct, dtt, dt_bias, a_log, dskip, s0, B, L):
    assert L % SCAN_T == 0 and SSD_STATE == SCAN_T
    nc = L // SCAN_T
    G = SSD_GROUPS
    nd = 2 * SSD_HPG
    M = B * L
    rep = lambda p, rows: jnp.broadcast_to(p.reshape(G, rows, 1), (G, rows, SCAN_T))
    small = lambda rows: pl.BlockSpec((1, rows, SCAN_T), lambda b, g: (g, 0, 0))
    st_spec = pl.BlockSpec((1, 2, 1, SSD_GW, SSD_STATE), lambda b, g: (b, 0, g, 0, 0))
    return pl.pallas_call(
        functools.partial(_ssd_kernel, nc=nc),
        grid=(B, G),
        in_specs=[pl.BlockSpec((SSD_GW, L), lambda b, g: (g, b)),
                  pl.BlockSpec((L, SSD_STATE), lambda b, g: (b, g)),
                  pl.BlockSpec((SSD_STATE, L), lambda b, g: (g, b)),
                  pl.BlockSpec((nd, L), lambda b, g: (g, b)),
                  small(nd), small(nd), small(SSD_GW), st_spec],
        out_specs=[pl.BlockSpec((L, SSD_GW), lambda b, g: (b, g)), st_spec],
        out_shape=[jax.ShapeDtypeStruct((M, SSD_INNER), F32),
                   jax.ShapeDtypeStruct((B, 2, G, SSD_GW, SSD_STATE), F32)],
        scratch_shapes=[pltpu.VMEM((SSD_GW, L), F32), pltpu.VMEM((SSD_GW, L), F32),
                        pltpu.VMEM((2, SSD_GW, SSD_STATE), F32),
                        pltpu.VMEM((2, 2, SSD_HPG * SCAN_T, SCAN_T), BF16),
                        pltpu.VMEM((2, SSD_HPG * SCAN_T, SCAN_T), BF16),
                        pltpu.VMEM((2, SCAN_T, SCAN_T), F32),
                        pltpu.VMEM((2, SSD_GW, SCAN_T), BF16)],
        compiler_params=_cparams("arbitrary", "arbitrary"),
        name="ssd",
    )(xt, bm, ct, dtt, rep(dt_bias, nd), rep(a_log, nd), rep(dskip, SSD_GW), s0)


def _hy_factors(L):
    return (128, 64) if L == 4096 else (32, (2 * L) // 32)


def _hilo(t):
    return _split2(t.astype(F32))


def _cblock(cr, ci):
    return jnp.concatenate([jnp.concatenate([cr, -ci], axis=-1), jnp.concatenate([ci, cr], axis=-1)], axis=-2)


def _dft_tables(L):
    n1c, n2c = _hy_factors(L)
    N = n1c * n2c
    h1 = n1c // 2
    i32 = jnp.int32
    k1 = jnp.arange(n1c, dtype=i32)
    n2 = jnp.arange(n2c, dtype=i32)

    def stage1(n1_count):
        n1 = jnp.arange(n1_count, dtype=i32)
        idx = (k1[None, :, None] * (n2[:, None, None] + n2c * n1[None, None, :])) % N
        ang = idx.astype(F32) * (2.0 * math.pi / N)
        return jnp.cos(ang), jnp.sin(ang)

    c, s = stage1(h1)
    g_fwd = _cblock(c, -s)
    g_inv = _cblock(jnp.swapaxes(c, 1, 2), jnp.swapaxes(s, 1, 2)) / n1c
    cf, sf = stage1(n1c)
    g_real = jnp.concatenate([cf, -sf], axis=1)
    idx2 = (n2[:, None] * n2[None, :]) % n2c
    ang2 = idx2.astype(F32) * (2.0 * math.pi / n2c)
    c2, s2 = jnp.cos(ang2), jnp.sin(ang2)
    f2 = _cblock(c2, -s2)
    f2_inv = _cblock(c2, s2) / n2c
    return dict(g_fwd=_hilo(g_fwd), g_inv=_hilo(g_inv), g_real=_hilo(g_real), f2=_hilo(f2), f2_inv=_hilo(f2_inv))


def _hy_s1_kernel(x_ref, gh_ref, gl_ref, o_ref, *, nb, C):
    for j in range(nb):
        lanes = slice(j * C, (j + 1) * C)
        o_ref[0, :, lanes] = _dot3(gh_ref[j], gl_ref[j], x_ref[0, :, lanes])


def _hy_stage1(x, tab, C, nb):
    P, R, W = x.shape
    gh, gl = tab
    n2c, rows_out, _ = gh.shape
    tspec = pl.BlockSpec((nb, rows_out, R), lambda p, t: (t, 0, 0))
    return pl.pallas_call(
        functools.partial(_hy_s1_kernel, nb=nb, C=C),
        grid=(P, n2c // nb),
        in_specs=[pl.BlockSpec((1, R, nb * C), lambda p, t: (p, 0, t)), tspec, tspec],
        out_specs=pl.BlockSpec((1, rows_out, nb * C), lambda p, t: (p, 0, t)),
        out_shape=jax.ShapeDtypeStruct((P, rows_out, W), F32),
        compiler_params=_cparams("arbitrary", "arbitrary"),
        name="hyena_stage1",
    )(x, gh, gl)


def _hy_spec_kernel(a_ref, inv_ref, fh_ref, fl_ref, o_ref, *, kb):
    n2c = a_ref.shape[2]
    for kk in range(kb):
        rhs = jnp.concatenate([a_ref[0, kk], a_ref[1, kk]], axis=0)
        X = _dot3(fh_ref[...], fl_ref[...], rhs) * inv_ref[...]
        o_ref[0, kk] = X[0:n2c]
        o_ref[1, kk] = X[n2c:]


def _hy_filter_spectrum(a, inv_norm, f2, kb):
    _, n1c, n2c, W = a.shape
    fh, fl = f2
    blk = pl.BlockSpec((2, kb, n2c, W), lambda t: (0, t, 0, 0))
    fspec = pl.BlockSpec((2 * n2c, 2 * n2c), lambda t: (0, 0))
    return pl.pallas_call(
        functools.partial(_hy_spec_kernel, kb=kb),
        grid=(n1c // kb,),
        in_specs=[blk, pl.BlockSpec((1, W), lambda t: (0, 0)), fspec, fspec],
        out_specs=blk,
        out_shape=jax.ShapeDtypeStruct(a.shape, F32),
        compiler_params=_cparams("arbitrary"),
        name="hyena_filter_spectrum",
    )(a, inv_norm, fh, fl)


def _hy_mid_kernel(a_ref, k_ref, fh_ref, fl_ref, ih_ref, il_ref, o_ref, *, kb):
    n2c = a_ref.shape[2]
    C = a_ref.shape[3]
    res = []
    for kk in range(kb):
        rhs = jnp.concatenate([a_ref[0, kk], a_ref[1, kk]], axis=0)
        X = _dot3(fh_ref[...], fl_ref[...], rhs)
        xr, xi = X[0:n2c], X[n2c:]
        kr, ki = k_ref[0, kk], k_ref[1, kk]
        Y = jnp.concatenate([xr * kr - xi * ki, xr * ki + xi * kr], axis=0)
        res.append(_dot3(ih_ref[...], il_ref[...], Y))
    o_ref[...] = jnp.swapaxes(jnp.stack(res, axis=0), 0, 1).reshape(2, n2c, kb, C)


def _hy_mid(a, kspec, order, f2, f2_inv, kb):
    assert kb == N2_SUB
    P, _, n1c, n2c, C = a.shape
    fh, fl = f2
    ih, il = f2_inv
    fspec = pl.BlockSpec((2 * n2c, 2 * n2c), lambda t, p: (0, 0))
    return pl.pallas_call(
        functools.partial(_hy_mid_kernel, kb=kb),
        grid=(n1c // kb, P),
        in_specs=[pl.BlockSpec((None, 2, kb, n2c, C), lambda t, p: (p, 0, t, 0, 0)),
                  pl.BlockSpec((2, kb, n2c, C), lambda t, p: (0, t, 0, order)),
                  fspec, fspec, fspec, fspec],
        out_specs=pl.BlockSpec((None, 2, n2c, kb, C), lambda t, p: (p, 0, 0, t, 0)),
        out_shape=jax.ShapeDtypeStruct((P, 2, n2c, n1c, C), F32),
        compiler_params=_cparams("arbitrary", "arbitrary"),
        name="hyena_mid",
    )(a, kspec, fh, fl, ih, il)


def _hy_last_kernel(b_ref, z_ref, gate_ref, bias_ref, gih_ref, gil_ref, *rest, fuse, n1c, C):
    if fuse:
        gfh_ref, gfl_ref, zo_ref, ao_ref = rest
    else:
        (zo_ref,) = rest
    for j in range(N2_SUB):
        lanes = slice(j * C, (j + 1) * C)
        bv = jnp.concatenate([b_ref[0, j], b_ref[1, j]], axis=0)
        y = _dot3(gih_ref[j], gil_ref[j], bv)
        znew = gate_ref[0, :, lanes] * (y + z_ref[0, :, lanes] * bias_ref[...])
        zo_ref[0, :, lanes] = znew.astype(zo_ref.dtype)
        if fuse:
            ao_ref[0, :, lanes] = _dot3(gfh_ref[j], gfl_ref[j], znew)


def _hy_last(bf, z, gate, bias, g_inv, g_fwd, out_dtype):
    P, _, n2c, n1c, C = bf.shape
    nhi = n2c // N2_SUB
    R = z.shape[1]
    fuse = g_fwd is not None
    zspec = pl.BlockSpec((1, R, N2_SUB * C), lambda p, t: (p, 0, t))
    in_specs = [pl.BlockSpec((None, 2, N2_SUB, n1c, C), lambda p, t: (p, 0, t, 0, 0)),
                zspec, zspec, pl.BlockSpec((1, C), lambda p, t: (0, 0)),
                pl.BlockSpec((N2_SUB, R, 2 * n1c), lambda p, t: (t, 0, 0)),
                pl.BlockSpec((N2_SUB, R, 2 * n1c), lambda p, t: (t, 0, 0))]
    args = [bf, z, gate, bias, g_inv[0], g_inv[1]]
    out_specs = [zspec]
    out_shape = [jax.ShapeDtypeStruct(z.shape, out_dtype)]
    if fuse:
        in_specs += [pl.BlockSpec((N2_SUB, 2 * n1c, R), lambda p, t: (t, 0, 0))] * 2
        args += [g_fwd[0], g_fwd[1]]
        out_specs.append(pl.BlockSpec((1, 2 * n1c, N2_SUB * C), lambda p, t: (p, 0, t)))
        out_shape.append(jax.ShapeDtypeStruct((P, 2 * n1c, n2c * C), F32))
    return pl.pallas_call(
        functools.partial(_hy_last_kernel, fuse=fuse, n1c=n1c, C=C),
        grid=(P, nhi),
        in_specs=in_specs, out_specs=out_specs, out_shape=out_shape,
        compiler_params=_cparams("arbitrary", "arbitrary"),
        name="hyena_last",
    )(*args)


def _hy_filter_kernel(f_ref, w1_ref, b1_ref, w2_ref, b2_ref, w3_ref, fr_ref, dl_ref, hk_ref, s_ref):
    i = pl.program_id(0)
    f = f_ref[...]
    fr = fr_ref[...]
    h = jnp.sin(fr * (_dot(f.astype(BF16), w1_ref[...].astype(BF16)) + b1_ref[...]))
    h = jnp.sin(fr * (_dot(h.astype(BF16), w2_ref[...].astype(BF16)) + b2_ref[...]))
    hk = _dot(h.astype(BF16), w3_ref[...].astype(BF16)) * jnp.exp(-f[:, 0:1] * dl_ref[...])
    hk_ref[...] = hk
    a = jnp.abs(hk)
    tot = jnp.sum(a, axis=0, keepdims=True)

    @pl.when(i == 0)
    def _():
        back = lax.broadcasted_iota(jnp.int32, tot.shape, 1) >= (tot.shape[1] // 2)
        s_ref[...] = tot - jnp.where(back, a[0:1], 0.0)

    @pl.when(i != 0)
    def _():
        s_ref[...] = s_ref[...] + tot


def _hy_filter(L, w1, b1, w2, b2, w3, freq):
    f32 = F32
    t = jnp.linspace(0.0, 1.0, L, dtype=f32)[:, None]
    ang = (2.0 * math.pi / L) * jnp.arange(L, dtype=f32)[:, None] * \
        jnp.linspace(1e-4, HY_BANDS - 1, HY_BANDS, dtype=f32)[None]
    feats = jnp.concatenate([t, jnp.cos(ang), -jnp.sin(ang)], axis=-1)
    fpad = 128
    feats = jnp.pad(feats, ((0, 0), (0, fpad - HY_EMB)))
    w1p = jnp.pad(w1, ((0, fpad - HY_EMB), (0, 0)))
    deltas = jnp.abs(jnp.linspace(math.log(HY_TARGET) / HY_SLOW, math.log(HY_TARGET) / HY_FAST, HY_W, dtype=f32))
    W = 2 * HY_ORDER * HY_W
    dl = jnp.tile(deltas, 2 * HY_ORDER).reshape(1, W)
    tl = min(L, 512)
    full = lambda shp: pl.BlockSpec(shp, lambda i: (0, 0))
    return pl.pallas_call(
        _hy_filter_kernel,
        grid=(L // tl,),
        in_specs=[pl.BlockSpec((tl, fpad), lambda i: (i, 0)),
                  full((fpad, HY_HID)), full((1, HY_HID)), full((HY_HID, HY_HID)), full((1, HY_HID)),
                  full((HY_HID, W)), full((1, HY_HID)), full((1, W))],
        out_specs=[pl.BlockSpec((tl, W), lambda i: (i, 0)), full((1, W))],
        out_shape=[jax.ShapeDtypeStruct((L, W), F32), jax.ShapeDtypeStruct((1, W), F32)],
        compiler_params=_cparams("arbitrary"),
        name="hyena_filter",
    )(feats, w1p, b1.reshape(1, HY_HID), w2, b2.reshape(1, HY_HID), w3, freq.reshape(1, HY_HID), dl)


def _hy_spectrum(L, tabs, w1, b1, w2, b2, w3, freq):
    n1c, n2c = _hy_factors(L)
    OC = HY_ORDER * HY_W
    hk, s = _hy_filter(L, w1, b1, w2, b2, w3, freq)
    inv_norm = 1.0 / (s[:, :OC] + s[:, OC:])
    ktime = jnp.concatenate([hk[:, :OC], jnp.zeros((1, OC), F32), hk[:0:-1, OC:]], axis=0)
    a = _hy_stage1(ktime.reshape(1, n1c, n2c * OC), tabs["g_real"], OC, 1)
    a = a.reshape(2, n1c, n2c, OC)
    return _hy_filter_spectrum(a, inv_norm, tabs["f2"], 8)


def _hyena(v, x1, x2, kspec, bias, tabs, B, L):
    n1c, n2c = _hy_factors(L)
    C = HY_W
    P = B // 2
    R = n1c
    view = lambda t: t.reshape(P, R, n2c * C)
    nb = 8 if n2c % 8 == 0 else 1
    kb = N2_SUB
    z0, g1, g2 = view(v), view(x1), view(x2)
    a = _hy_stage1(z0, tabs["g_fwd"], C, nb)
    bf = _hy_mid(a.reshape(P, 2, n1c, n2c, C), kspec, 0, tabs["f2"], tabs["f2_inv"], kb)
    z1, a = _hy_last(bf, z0, g1, bias[0:1], tabs["g_inv"], tabs["g_fwd"], F32)
    bf = _hy_mid(a.reshape(P, 2, n1c, n2c, C), kspec, 1, tabs["f2"], tabs["f2_inv"], kb)
    (z2,) = _hy_last(bf, z1, g2, bias[1:2], tabs["g_inv"], None, BF16)
    return z2.reshape(B * L, C)


_EVEN_PLAN = (
    (0, HY_W, "conv", 1.0, False, F32),
    (HY_W, HY_W, "conv", 1.0, False, F32),
    (2 * HY_W, HY_W, "conv", 1.0, False, F32),
    (3 * HY_W, M_W, "conv_silu", 1.0, True, BF16),
    (3 * HY_W + M_W, M_W, "conv_silu", M_DH ** -0.5, False, BF16),
    (EV_CONV + M_W, M_W, "raw", 1.0, False, F32),
)
_EVEN_NT_PLAN = ((0, M_W, BF16), (M_W, 4 * M_HEADS, F32))
_ODD_NT_PLAN = ((0, 2 * SSD_HEADS, F32),)
_GN = SSD_GROUPS * SSD_STATE
_ODD_PLAN = (
    (0, SSD_INNER, "raw", 1.0, False, F32),
    (SSD_INNER, SSD_INNER, "conv_silu", 1.0, True, F32),
    (2 * SSD_INNER, _GN, "conv_silu", 1.0, False, BF16),
    (2 * SSD_INNER + _GN, _GN, "conv_silu", 1.0, True, BF16),
)


def _even_layer(x, mods_l, P, e, l, B, L, st, per_batch_cond, kspec, tabs):
    hv, hx1, hx2, qt, k, o, vt, gt = _inproj(
        x, mods_l, P["norm1_g"][l:l + 1], P["ev_in_w"], P["ev_in_wt"], P["ev_conv_w"], P["ev_conv_b"], e,
        plan=_EVEN_PLAN, nt_plan=_EVEN_NT_PLAN, conv0=0, seq_len=L, per_batch_cond=per_batch_cond)
    y_hy = _hyena(hv, hx1, hx2, kspec, P["hy_bias"][e], tabs, B, L)
    c0, m0 = st
    y_m, c_new, m_new = _mlstm(qt, k, vt, o, gt, P["m_gate_b"][e], P["m_norm_g"][e], c0, m0, B, L)
    x = _outproj_even(y_hy, y_m, x, mods_l, P["ev_out_w"], e, seq_len=L, per_batch_cond=per_batch_cond)
    return x, (c_new, m_new)


def _by_group(p):
    return p.reshape(2, SSD_GROUPS, SSD_HPG).transpose(1, 0, 2).reshape(SSD_GROUPS, 2 * SSD_HPG)


def _odd_layer(x, mods_l, P, o, l, B, L, s0, per_batch_cond):
    z, xt, bm, ct, dtt = _inproj(
        x, mods_l, P["norm1_g"][l:l + 1], P["od_in_w"], P["od_in_wt"], P["od_conv_w"], P["od_conv_b"], o,
        plan=_ODD_PLAN, nt_plan=_ODD_NT_PLAN, conv0=SSD_INNER, seq_len=L, per_batch_cond=per_batch_cond)
    dskip = jnp.repeat(P["ssd_D"][o], SSD_HEADDIM).reshape(SSD_GROUPS, SSD_GW)
    y, s_new = _ssd(xt, bm, ct, dtt, _by_group(P["ssd_dt_bias"][o]), _by_group(P["ssd_A_log"][o]), dskip, s0, B, L)
    x = _outproj_odd(y, z, x, mods_l, P["ssd_norm_g"], P["od_out_w"], o, seq_len=L, per_batch_cond=per_batch_cond)
    return x, s_new


def _to_colmajor(x, B, L):
    rows = L // GRID_W
    return x.reshape(B, rows, GRID_W, D_MODEL).swapaxes(1, 2).reshape(B * L, D_MODEL)


def _from_colmajor(x, B, L):
    rows = L // GRID_W
    return x.reshape(B, GRID_W, rows, D_MODEL).swapaxes(1, 2).reshape(B * L, D_MODEL)


def _trunk(x, mods, P, B, L, init_states, per_batch_cond, grid, kspecs, tabs):
    states = []
    for l in range(DEPTH):
        if grid and l == 2:
            x = _to_colmajor(x, B, L)
        mods_l = mods[l]
        if l % 2 == 0:
            x, st = _even_layer(x, mods_l, P, l // 2, l, B, L, init_states[l], per_batch_cond, kspecs[l // 2], tabs)
        else:
            x, st = _odd_layer(x, mods_l, P, l // 2, l, B, L, init_states[l], per_batch_cond)
        states.append(st)
        x = _ffn(x, mods_l, P["norm2_g"].reshape(DEPTH, 1, D_MODEL), P["ffn_w1"], P["ffn_w3"], P["ffn_w2"],
                 P["final_g"].reshape(1, D_MODEL), l, seq_len=L, per_batch_cond=per_batch_cond,
                 final=(l == DEPTH - 1))
    if grid:
        x = _from_colmajor(x, B, L)
    return x, states


def _pack_mlstm_state(C, n, m):
    B = C.shape[0]
    ext = jnp.concatenate([jnp.swapaxes(C, -1, -2), n[..., None, :],
                           jnp.zeros(C.shape[:-2] + (M_DH - 1, M_DH), F32)], axis=-2)
    m_rep = jnp.broadcast_to(m.reshape(B, 2 * M_HEADS, 1, 1), (B, 2 * M_HEADS, 1, M_DH))
    return ext.reshape(B, 2 * M_HEADS, 2 * M_DH, M_DH), m_rep


def kernel(x_prompt, x_sample, state_mlstm_C, state_mlstm_n, state_mlstm_m, state_ssd, c, c_ctx,
           mod_w, mod_b, norm1_g, norm2_g, ffn_w1, ffn_w3, ffn_w2, final_g,
           ev_in_w, ev_conv_w, ev_conv_b, hy_w1, hy_b1, hy_w2, hy_b2, hy_w3, hy_freq, hy_bias,
           m_gate_b, m_norm_g, ev_out_w,
           od_in_w, od_conv_w, od_conv_b, ssd_dt_bias, ssd_A_log, ssd_D, ssd_norm_g, od_out_w):
    Bp, Lp, _ = x_prompt.shape
    Bs, Ls, _ = x_sample.shape
    bf = lambda w: w.astype(BF16)
    n_odd = od_in_w.shape[0]
    od_dt_w = od_in_w[:, :, OD_PROJ - 2 * SSD_HEADS:].reshape(n_odd, D_MODEL, 2, SSD_GROUPS, SSD_HPG)
    od_dt_w = od_dt_w.transpose(0, 3, 2, 4, 1).reshape(n_odd, 2 * SSD_HEADS, D_MODEL)
    P = dict(norm1_g=norm1_g, norm2_g=norm2_g, ffn_w1=bf(ffn_w1), ffn_w3=bf(ffn_w3), ffn_w2=bf(ffn_w2),
             final_g=final_g, ev_in_w=bf(ev_in_w),
             ev_in_wt=bf(jnp.concatenate([ev_in_w[:, :, EV_CONV:EV_CONV + M_W],
                                          ev_in_w[:, :, EV_PROJ - 4 * M_HEADS:]], axis=2)).swapaxes(1, 2),
             ev_conv_w=ev_conv_w, ev_conv_b=ev_conv_b, hy_bias=hy_bias, m_gate_b=m_gate_b, m_norm_g=m_norm_g,
             ev_out_w=bf(ev_out_w), od_in_w=bf(od_in_w), od_in_wt=bf(od_dt_w),
             od_conv_w=od_conv_w, od_conv_b=od_conv_b, ssd_dt_bias=ssd_dt_bias, ssd_A_log=ssd_A_log,
             ssd_D=ssd_D, ssd_norm_g=ssd_norm_g.reshape(-1, 1, SSD_INNER), od_out_w=bf(od_out_w))

    ncond = 16
    cond = jnp.concatenate([c_ctx[None, :], c, jnp.zeros((ncond - 1 - Bs, D_MODEL), F32)], axis=0)
    mods = _modulation(cond, mod_w, mod_b).reshape(DEPTH, ncond, 6, D_MODEL)

    tabs = {L: _dft_tables(L) for L in (Lp, Ls)}
    kspecs = {L: [_hy_spectrum(L, tabs[L], hy_w1[e], hy_b1[e], hy_w2[e], hy_b2[e], hy_w3[e], hy_freq[e])
                  for e in range((DEPTH + 1) // 2)] for L in (Lp, Ls)}

    G = SSD_GROUPS
    zero_m = _pack_mlstm_state(jnp.zeros((Bp, 2, M_HEADS, M_DH, M_DH), F32), jnp.zeros((Bp, 2, M_HEADS, M_DH), F32),
                               jnp.zeros((Bp, 2, M_HEADS), F32))
    zero_s = jnp.zeros((Bp, 2, G, SSD_GW, SSD_STATE), F32)
    ctx_init = [zero_m if l % 2 == 0 else zero_s for l in range(DEPTH)]
    y_prompt, ctx_states = _trunk(x_prompt.reshape(Bp * Lp, D_MODEL), mods, P, Bp, Lp, ctx_init, False, False,
                                  kspecs[Lp], tabs[Lp])

    lat_init = []
    for l in range(DEPTH):
        if l % 2 == 0:
            e = l // 2
            lat_init.append(_pack_mlstm_state(state_mlstm_C[:, e], state_mlstm_n[:, e], state_mlstm_m[:, e]))
        else:
            lat_init.append(state_ssd[:, l // 2].reshape(Bs, 2, G, SSD_GW, SSD_STATE))
    y_sample, _ = _trunk(x_sample.reshape(Bs * Ls, D_MODEL), mods, P, Bs, Ls, lat_init, True, True,
                         kspecs[Ls], tabs[Ls])

    cs = [ctx_states[l][0].reshape(Bp, 2, M_HEADS, 2 * M_DH, M_DH) for l in range(0, DEPTH, 2)]
    ms = [ctx_states[l][1][:, :, 0, 0].reshape(Bp, 2, M_HEADS) for l in range(0, DEPTH, 2)]
    new_C = jnp.stack([jnp.swapaxes(t[..., :M_DH, :], -1, -2) for t in cs], axis=1)
    new_n = jnp.stack([t[..., M_DH, :] for t in cs], axis=1)
    new_m = jnp.stack(ms, axis=1)
    new_ssd = jnp.stack([ctx_states[l].reshape(Bp, 2, SSD_HEADS, SSD_HEADDIM, SSD_STATE)
                         for l in range(1, DEPTH, 2)], axis=1)
    return (y_prompt.reshape(Bp, Lp, D_MODEL), y_sample.reshape(Bs, Ls, D_MODEL), new_C, new_n, new_m, new_ssd)
```

```python
import functools
import math

import jax
import jax.numpy as jnp
from jax import lax
from jax.experimental import pallas as pl
from jax.experimental.pallas import tpu as pltpu

F32 = jnp.float32
BF16 = jnp.bfloat16

D_MODEL = 1024
DEPTH = 4
GRID_W = 64
EPS = 1e-6
HY_W = D_MODEL // 2
HY_ORDER = 2
HY_EMB = 33
HY_BANDS = (HY_EMB - 1) // 2
HY_HID = 64
HY_TARGET = 1e-2
HY_FAST = 0.3
HY_SLOW = 1.5
M_HEADS = 4
M_DH = 128
M_W = M_HEADS * M_DH
EV_CONV = 3 * HY_W + 2 * M_W
EV_PROJ = EV_CONV + 2 * M_W + 4 * M_HEADS
SSD_INNER = 2 * D_MODEL
SSD_HEADDIM = 64
SSD_HEADS = SSD_INNER // SSD_HEADDIM
SSD_GROUPS = 8
SSD_HPG = SSD_HEADS // SSD_GROUPS
SSD_STATE = 128
SSD_GW = SSD_HPG * SSD_HEADDIM
SSD_CONV_DIM = SSD_INNER + 2 * SSD_GROUPS * SSD_STATE
OD_PROJ = SSD_INNER + SSD_CONV_DIM + 2 * SSD_HEADS
FFN_HIDDEN = -(-8 * D_MODEL // (3 * 256)) * 256

V7X_VMEM_BYTES = 64 * 1024 * 1024
VMEM_LIMIT = V7X_VMEM_BYTES * 3 // 4
ROW_TILE = 256
WIDE_TILE = 512
HALO = 16
COL_CHUNK = 512
HY_NB = 16
SCAN_T = 128


def _cparams(*sem):
    return pltpu.CompilerParams(dimension_semantics=sem, vmem_limit_bytes=VMEM_LIMIT)


def _dot(a, b):
    return jnp.dot(a, b, preferred_element_type=F32)


def _dot_nt(a, b):
    return lax.dot_general(a, b, (((1,), (1,)), ((), ())), preferred_element_type=F32)


def _split2(x):
    h = x.astype(BF16)
    return h, (x - h.astype(F32)).astype(BF16)


def _dot3(th, tl, x):
    xh, xl = _split2(x)
    return _dot(th, xh) + _dot(th, xl) + _dot(tl, xh)


def _dot2(x, t):
    xh, xl = _split2(x)
    return _dot(xh, t) + _dot(xl, t)


def _sigmoid(x):
    return 1.0 / (1.0 + jnp.exp(-x))


def _silu(x):
    return x * _sigmoid(x)


def _log_sigmoid(x):
    return jnp.minimum(x, 0.0) - jnp.log(1.0 + jnp.exp(-jnp.abs(x)))


def _softplus(x):
    return jnp.maximum(x, 0.0) + jnp.log(1.0 + jnp.exp(-jnp.abs(x)))


def _rms(x):
    return x * lax.rsqrt(jnp.mean(x * x, axis=-1, keepdims=True) + EPS)


def _mod_kernel(c_ref, w_ref, b_ref, o_ref):
    a = _silu(c_ref[...]).astype(BF16)
    o_ref[0] = _dot(a, w_ref[0].astype(BF16)) + b_ref[0]


def _modulation(cond, mod_w, mod_b):
    nc = cond.shape[0]
    tn = 1536
    return pl.pallas_call(
        _mod_kernel,
        grid=(DEPTH, 6 * D_MODEL // tn),
        in_specs=[pl.BlockSpec((nc, D_MODEL), lambda l, j: (0, 0)),
                  pl.BlockSpec((1, D_MODEL, tn), lambda l, j: (l, 0, j)),
                  pl.BlockSpec((1, 1, tn), lambda l, j: (l, 0, j))],
        out_specs=pl.BlockSpec((1, nc, tn), lambda l, j: (l, 0, j)),
        out_shape=jax.ShapeDtypeStruct((DEPTH, nc, 6 * D_MODEL), F32),
        compiler_params=_cparams("arbitrary", "arbitrary"),
        name="modulation",
    )(cond, mod_w, mod_b.reshape(DEPTH, 1, 6 * D_MODEL))


def _inproj_kernel(x_ref, xp_ref, xn_ref, mod_ref, g_ref, w_ref, wt_ref, cw_ref, cb_ref, *rest,
                   plan, nt_plan, tiles_per_seq, conv0):
    nseg = len(plan)
    outs = rest[:nseg]
    nt_outs = rest[nseg:nseg + len(nt_plan)]
    hbuf = rest[nseg + len(nt_plan)]
    tm = x_ref.shape[0]
    i = pl.program_id(0)
    j = i % tiles_per_seq
    pmask = (j != 0).astype(F32)
    nmask = (j != tiles_per_seq - 1).astype(F32)
    mod = mod_ref[0]
    sh, sc = mod[0:1], mod[1:2]
    g = g_ref[...]

    def normmod(xv):
        return (_rms(xv) * g * (1.0 + sc) + sh).astype(BF16)

    hbuf[0:tm] = normmod(x_ref[...])
    hbuf[tm:tm + HALO] = normmod(xp_ref[...])
    hbuf[tm + HALO:tm + 2 * HALO] = normmod(xn_ref[...])
    rowid = lax.broadcasted_iota(jnp.int32, (tm, COL_CHUNK), 0)

    for (c0, width, kind, scale, transposed), o_ref in zip(plan, outs):
        for cc in range(c0, c0 + width, COL_CHUNK):
            w = w_ref[0, :, cc:cc + COL_CHUNK]
            if kind == "raw":
                y = _dot(hbuf[0:tm], w)
            else:
                full = _dot(hbuf[...], w)
                acc = full[0:tm]
                ap = full[tm + HALO - 1:tm + HALO] * pmask
                an = full[tm + HALO:tm + HALO + 1] * nmask
                prev = jnp.where(rowid == 0, ap, pltpu.roll(acc, 1, 0))
                nxt = jnp.where(rowid == tm - 1, an, pltpu.roll(acc, tm - 1, 0))
                cw = cw_ref[0, :, cc - conv0:cc - conv0 + COL_CHUNK]
                cb = cb_ref[0, :, cc - conv0:cc - conv0 + COL_CHUNK]
                y = cb + cw[0:1] * prev + cw[1:2] * acc + cw[2:3] * nxt
                if kind == "conv_silu":
                    y = _silu(y)
                if scale != 1.0:
                    y = y * scale
            if transposed:
                o_ref[cc - c0:cc - c0 + COL_CHUNK, :] = y.T.astype(o_ref.dtype)
            else:
                o_ref[:, cc - c0:cc - c0 + COL_CHUNK] = y.astype(o_ref.dtype)

    for (r0, rows), o_ref in zip(nt_plan, nt_outs):
        o_ref[...] = _dot_nt(wt_ref[0, r0:r0 + rows, :], hbuf[0:tm]).astype(o_ref.dtype)


def _inproj(x, mods_l, g_l, w_all, wt_all, cw_all, cb_all, layer_idx, *, plan, nt_plan, conv0, seq_len,
            per_batch_cond):
    M = x.shape[0]
    tm = ROW_TILE
    tps = seq_len // tm
    nt = M // tm
    hb = tm // HALO
    ntot = w_all.shape[2]
    tw = wt_all.shape[1]
    cdim = cw_all.shape[2]
    e = layer_idx
    cond_map = (lambda i: (1 + i // tps, 0, 0)) if per_batch_cond else (lambda i: (0, 0, 0))
    in_specs = [
        pl.BlockSpec((tm, D_MODEL), lambda i: (i, 0)),
        pl.BlockSpec((HALO, D_MODEL), lambda i: (jnp.maximum(i * hb - 1, 0), 0)),
        pl.BlockSpec((HALO, D_MODEL), lambda i: (jnp.minimum((i + 1) * hb, nt * hb - 1), 0)),
        pl.BlockSpec((1, 6, D_MODEL), cond_map),
        pl.BlockSpec((1, D_MODEL), lambda i: (0, 0)),
        pl.BlockSpec((1, D_MODEL, ntot), lambda i: (e, 0, 0), pipeline_mode=pl.Buffered(1)),
        pl.BlockSpec((1, tw, D_MODEL), lambda i: (e, 0, 0)),
        pl.BlockSpec((1, 3, cdim), lambda i: (e, 0, 0)),
        pl.BlockSpec((1, 1, cdim), lambda i: (e, 0, 0)),
    ]
    out_specs, out_shape = [], []
    for (c0, width, kind, scale, transposed, dt) in plan:
        if transposed:
            out_specs.append(pl.BlockSpec((width, tm), lambda i: (0, i)))
            out_shape.append(jax.ShapeDtypeStruct((width, M), dt))
        else:
            out_specs.append(pl.BlockSpec((tm, width), lambda i: (i, 0)))
            out_shape.append(jax.ShapeDtypeStruct((M, width), dt))
    for (r0, rows, dt) in nt_plan:
        out_specs.append(pl.BlockSpec((rows, tm), lambda i: (0, i)))
        out_shape.append(jax.ShapeDtypeStruct((rows, M), dt))
    kplan = tuple(p[:5] for p in plan)
    knt = tuple(p[:2] for p in nt_plan)
    return pl.pallas_call(
        functools.partial(_inproj_kernel, plan=kplan, nt_plan=knt, tiles_per_seq=tps, conv0=conv0),
        grid=(nt,),
        in_specs=in_specs,
        out_specs=out_specs,
        out_shape=out_shape,
        scratch_shapes=[pltpu.VMEM((tm + 2 * HALO, D_MODEL), BF16)],
        compiler_params=_cparams("arbitrary"),
        name="inproj",
    )(x, x, x, mods_l, g_l, w_all, wt_all, cw_all, cb_all.reshape(cb_all.shape[0], 1, cdim))


def _outproj_even_kernel(yh_ref, ym_ref, x_ref, mod_ref, w_ref, o_ref):
    out = _dot(yh_ref[...], w_ref[0, 0:HY_W]) + _dot(ym_ref[...], w_ref[0, HY_W:])
    o_ref[...] = x_ref[...] + mod_ref[0][2:3] * out


def _outproj_odd_kernel(y_ref, z_ref, x_ref, mod_ref, ng_ref, w_ref, o_ref):
    u = y_ref[...] * _silu(z_ref[...])
    un = (_rms(u) * ng_ref[0]).astype(BF16)
    o_ref[...] = x_ref[...] + mod_ref[0][2:3] * _dot(un, w_ref[0])


def _cond_map(seq_len, per_batch_cond):
    tps = seq_len // WIDE_TILE
    return (lambda i: (1 + i // tps, 0, 0)) if per_batch_cond else (lambda i: (0, 0, 0))


def _outproj_even(yh, ym, x, mods_l, w_all, e, *, seq_len, per_batch_cond):
    M = x.shape[0]
    tm = WIDE_TILE
    return pl.pallas_call(
        _outproj_even_kernel,
        grid=(M // tm,),
        in_specs=[pl.BlockSpec((tm, HY_W), lambda i: (i, 0)),
                  pl.BlockSpec((tm, M_W), lambda i: (i, 0)),
                  pl.BlockSpec((tm, D_MODEL), lambda i: (i, 0)),
                  pl.BlockSpec((1, 6, D_MODEL), _cond_map(seq_len, per_batch_cond)),
                  pl.BlockSpec((1, HY_W + M_W, D_MODEL), lambda i: (e, 0, 0), pipeline_mode=pl.Buffered(1))],
        out_specs=pl.BlockSpec((tm, D_MODEL), lambda i: (i, 0)),
        out_shape=jax.ShapeDtypeStruct((M, D_MODEL), F32),
        compiler_params=_cparams("arbitrary"),
        name="outproj_even",
    )(yh, ym, x, mods_l, w_all)


def _outproj_odd(y, z, x, mods_l, ng_all, w_all, o, *, seq_len, per_batch_cond):
    M = x.shape[0]
    tm = WIDE_TILE
    return pl.pallas_call(
        _outproj_odd_kernel,
        grid=(M // tm,),
        in_specs=[pl.BlockSpec((tm, SSD_INNER), lambda i: (i, 0)),
                  pl.BlockSpec((tm, SSD_INNER), lambda i: (i, 0)),
                  pl.BlockSpec((tm, D_MODEL), lambda i: (i, 0)),
                  pl.BlockSpec((1, 6, D_MODEL), _cond_map(seq_len, per_batch_cond)),
                  pl.BlockSpec((1, 1, SSD_INNER), lambda i: (o, 0, 0)),
                  pl.BlockSpec((1, SSD_INNER, D_MODEL), lambda i: (o, 0, 0), pipeline_mode=pl.Buffered(1))],
        out_specs=pl.BlockSpec((tm, D_MODEL), lambda i: (i, 0)),
        out_shape=jax.ShapeDtypeStruct((M, D_MODEL), F32),
        compiler_params=_cparams("arbitrary"),
        name="outproj_odd",
    )(y, z, x, mods_l, ng_all, w_all)


FFN_CHUNK = FFN_HIDDEN // 2


def _ffn_kernel(x_ref, mod_ref, g_ref, w1_ref, w3_ref, w2_ref, fg_ref, o_ref, *, final):
    x = x_ref[...]
    mod = mod_ref[0]
    h = (_rms(x) * g_ref[0] * (1.0 + mod[4:5]) + mod[3:4]).astype(BF16)
    ff = jnp.zeros(x.shape, F32)
    for c0 in range(0, FFN_HIDDEN, FFN_CHUNK):
        a = _dot(h, w1_ref[0, :, c0:c0 + FFN_CHUNK])
        b = _dot(h, w3_ref[0, :, c0:c0 + FFN_CHUNK])
        u = (_silu(a) * b).astype(BF16)
        ff = ff + _dot(u, w2_ref[0, c0:c0 + FFN_CHUNK, :])
    xo = x + mod[5:6] * ff
    if final:
        xo = _rms(xo) * fg_ref[...]
    o_ref[...] = xo


def _ffn(x, mods_l, g_all, w1, w3, w2, fg, l, *, seq_len, per_batch_cond, final):
    M = x.shape[0]
    tm = WIDE_TILE
    wspec = lambda shape: pl.BlockSpec(shape, lambda i: (l, 0, 0), pipeline_mode=pl.Buffered(1))
    return pl.pallas_call(
        functools.partial(_ffn_kernel, final=final),
        grid=(M // tm,),
        in_specs=[pl.BlockSpec((tm, D_MODEL), lambda i: (i, 0)),
                  pl.BlockSpec((1, 6, D_MODEL), _cond_map(seq_len, per_batch_cond)),
                  pl.BlockSpec((1, 1, D_MODEL), lambda i: (l, 0, 0)),
                  wspec((1, D_MODEL, FFN_HIDDEN)), wspec((1, D_MODEL, FFN_HIDDEN)),
                  wspec((1, FFN_HIDDEN, D_MODEL)),
                  pl.BlockSpec((1, D_MODEL), lambda i: (0, 0))],
        out_specs=pl.BlockSpec((tm, D_MODEL), lambda i: (i, 0)),
        out_shape=jax.ShapeDtypeStruct((M, D_MODEL), F32),
        compiler_params=_cparams("arbitrary"),
        name="ffn",
    )(x, mods_l, g_all, w1, w3, w2, fg)


def _scan_consts(d):
    T = SCAN_T
    row = lax.broadcasted_iota(jnp.int32, (T, T), 0)
    col = lax.broadcasted_iota(jnp.int32, (T, T), 1)
    if d == 0:
        mask, strict, last = row <= col, col > row, T - 1
    else:
        mask, strict, last = row >= col, col < row, 0
    v_mat = jnp.where(mask, 1.0, 0.0).astype(BF16)
    return mask, strict, v_mat, last


def _mlstm_kernel(qt_ref, k_ref, vt_ref, o_ref, gt_ref, gb_ref, ng_ref, c0_ref, m0_ref,
                  y_ref, cst_ref, mst_ref, h_scr, c_scr, m_scr, lhs_scr, sb_scr, vw_scr, rows_scr, *, nc):
    T = SCAN_T
    H = M_HEADS
    c_scr[...] = c0_ref[0]
    m_scr[...] = m0_ref[0]
    row = lax.broadcasted_iota(jnp.int32, (T, T), 0)
    col = lax.broadcasted_iota(jnp.int32, (T, T), 1)
    eye = row == col
    ones_blk = jnp.where(lax.broadcasted_iota(jnp.int32, (M_DH, T), 0) == 0, 1.0, 0.0)
    consts = (_scan_consts(0), _scan_consts(1))
    gb = gb_ref[...]

    def gates(ci, d, slot):
        mask, strict, v_mat, last = consts[d]
        w2 = jnp.concatenate([v_mat, v_mat], axis=0)
        c0 = pl.multiple_of(ci * T, T)
        lanes = pl.ds(c0, T)
        graw = gt_ref[:, lanes] + gb
        lf = _log_sigmoid(graw)
        b_all = _dot2(lf, v_mat)
        for h in range(H):
            fi = (2 * d + 1) * H + h
            ii = (2 * d) * H + h
            tr = slice(h * T, (h + 1) * T)
            a = jnp.concatenate([jnp.where(strict, lf[fi:fi + 1], 0.0), jnp.where(eye, graw[ii:ii + 1], 0.0)],
                                axis=1)
            hi, lo = _split2(a)
            lhs_scr[d, 0, tr, :] = hi
            lhs_scr[d, 1, tr, :] = lo
        dall = _dot(lhs_scr[d, 0], w2) + _dot(lhs_scr[d, 1], w2)
        for h in range(H):
            fi = (2 * d + 1) * H + h
            ii = (2 * d) * H + h
            hl = slice(h * M_DH, (h + 1) * M_DH)
            tr = slice(h * T, (h + 1) * T)
            dlog = jnp.where(mask, dall[tr], -jnp.inf)
            mloc = jnp.max(dlog, axis=0, keepdims=True)
            sb_scr[slot, d, tr, :] = (_dot(k_ref[pl.ds(c0, T), hl], qt_ref[hl, lanes])
                                      * jnp.exp(dlog - mloc)).astype(BF16)
            b_row, i_row = b_all[fi:fi + 1], graw[ii:ii + 1]
            mloc_last = jnp.broadcast_to(mloc[:, last:last + 1], (1, T))
            b_last = jnp.broadcast_to(b_row[:, last:last + 1], (1, T))
            w_end = jnp.exp(b_last - b_row + i_row - mloc_last)
            vw_scr[slot, d, h, 0:M_DH, :] = (vt_ref[hl, lanes].astype(F32) * w_end).astype(BF16)
            vw_scr[slot, d, h, M_DH:, :] = (ones_blk * w_end).astype(BF16)
            rows_scr[slot, d, 4 * h:4 * h + 4, :] = jnp.concatenate([b_row, mloc, mloc_last, b_last], axis=0)

    def update(ci, d, slot, accumulate):
        c0 = pl.multiple_of(ci * T, T)
        lanes = pl.ds(c0, T)
        for h in range(H):
            sidx = d * H + h
            hl = slice(h * M_DH, (h + 1) * M_DH)
            tr = slice(h * T, (h + 1) * T)
            rows = rows_scr[slot, d, 4 * h:4 * h + 4, :]
            b_row, mloc, mloc_last, b_last = rows[0:1], rows[1:2], rows[2:3], rows[3:4]
            m_row = m_scr[sidx]
            inter = b_row + m_row
            mt = jnp.maximum(inter, mloc)
            kh = k_ref[pl.ds(c0, T), hl]
            qt = qt_ref[hl, lanes]
            vext = jnp.concatenate([vt_ref[hl, lanes], ones_blk.astype(BF16)], axis=0)
            ct = c_scr[sidx]
            tot = (jnp.exp(mloc - mt) * _dot(vext, sb_scr[slot, d, tr, :])
                   + jnp.exp(inter - mt) * _dot(ct.astype(BF16), qt))
            den = tot[M_DH:M_DH + 1]
            ht = tot[0:M_DH] / jnp.maximum(jnp.abs(den), jnp.exp(-mt))
            if accumulate:
                h_scr[hl, lanes] = h_scr[hl, lanes] + ht
            else:
                h_scr[hl, lanes] = ht
            m_new = jnp.maximum(b_last + m_row, mloc_last)
            c_scr[sidx] = (jnp.exp(b_last + m_row - m_new) * ct
                           + jnp.exp(mloc_last - m_new) * _dot(vw_scr[slot, d, h], kh))
            m_scr[sidx] = m_new

    def step(c, slot, accumulate):
        update(c, 0, slot, accumulate)
        update(nc - 1 - c, 1, slot, accumulate)
        gates(jnp.minimum(c + 1, nc - 1), 0, 1 - slot)
        gates(jnp.maximum(nc - 2 - c, 0), 1, 1 - slot)

    gates(0, 0, 0)
    gates(nc - 1, 1, 0)
    if nc % 4 == 0:
        def pair(accumulate):
            def body(i, carry):
                step(2 * i, 0, accumulate)
                step(2 * i + 1, 1, accumulate)
                return carry
            return body
        lax.fori_loop(0, nc // 4, pair(False), 0)
        lax.fori_loop(nc // 4, nc // 2, pair(True), 0)
    else:
        for c in range(nc):
            step(c, c % 2, c >= nc // 2)

    def epilogue(i, carry):
        c0 = pl.multiple_of(i * T, T)
        for h in range(H):
            hl = slice(h * M_DH, (h + 1) * M_DH)
            hh = h_scr[hl, pl.ds(c0, T)]
            hn = hh * lax.rsqrt(jnp.mean(hh * hh, axis=0, keepdims=True) + EPS)
            y = hn.T * ng_ref[:, hl] * _sigmoid(o_ref[pl.ds(c0, T), hl])
            y_ref[pl.ds(c0, T), hl] = y.astype(y_ref.dtype)
        return carry

    lax.fori_loop(0, nc, epilogue, 0)
    cst_ref[0] = c_scr[...]
    mst_ref[0] = m_scr[...]


def _mlstm(qt, k, vt, o, gt, gate_b, norm_g, c0, m0, B, L):
    assert L % (2 * SCAN_T) == 0 and M_DH == SCAN_T
    nc = L // SCAN_T
    ng = 4 * M_HEADS
    M = B * L
    one = pl.Buffered(1)
    st_spec = pl.BlockSpec((1, 2 * M_HEADS, 2 * M_DH, M_DH), lambda b: (b, 0, 0, 0))
    m_spec = pl.BlockSpec((1, 2 * M_HEADS, 1, M_DH), lambda b: (b, 0, 0, 0))
    return pl.pallas_call(
        functools.partial(_mlstm_kernel, nc=nc),
        grid=(B,),
        in_specs=[pl.BlockSpec((M_W, L), lambda b: (0, b), pipeline_mode=one),
                  pl.BlockSpec((L, M_W), lambda b: (b, 0), pipeline_mode=one),
                  pl.BlockSpec((M_W, L), lambda b: (0, b), pipeline_mode=one),
                  pl.BlockSpec((L, M_W), lambda b: (b, 0), pipeline_mode=one),
                  pl.BlockSpec((ng, L), lambda b: (0, b)),
                  pl.BlockSpec((ng, SCAN_T), lambda b: (0, 0)),
                  pl.BlockSpec((1, M_W), lambda b: (0, 0)),
                  st_spec, m_spec],
        out_specs=[pl.BlockSpec((L, M_W), lambda b: (b, 0)), st_spec, m_spec],
        out_shape=[jax.ShapeDtypeStruct((M, M_W), BF16),
                   jax.ShapeDtypeStruct((B, 2 * M_HEADS, 2 * M_DH, M_DH), F32),
                   jax.ShapeDtypeStruct((B, 2 * M_HEADS, 1, M_DH), F32)],
        scratch_shapes=[pltpu.VMEM((M_W, L), F32),
                        pltpu.VMEM((2 * M_HEADS, 2 * M_DH, M_DH), F32),
                        pltpu.VMEM((2 * M_HEADS, 1, M_DH), F32),
                        pltpu.VMEM((2, 2, M_HEADS * SCAN_T, 2 * SCAN_T), BF16),
                        pltpu.VMEM((2, 2, M_HEADS * SCAN_T, SCAN_T), BF16),
                        pltpu.VMEM((2, 2, M_HEADS, 2 * M_DH, SCAN_T), BF16),
                        pltpu.VMEM((2, 2, 4 * M_HEADS, SCAN_T), F32)],
        compiler_params=_cparams("arbitrary"),
        name="mlstm",
    )(qt, k, vt, o, gt, jnp.broadcast_to(gate_b.reshape(ng, 1), (ng, SCAN_T)), norm_g.reshape(1, M_W), c0, m0)


def _ssd_kernel(xt_ref, b_ref, ct_ref, dtt_ref, db_ref, al_ref, dsk_ref, s0_ref,
                y_ref, sout_ref, yf_scr, yb_scr, s_scr, lhs_scr, cbt_scr, p_scr, xd_scr, xdec_scr, rows_scr, *, nc):
    T = SCAN_T
    H = SSD_HPG
    s_scr[...] = s0_ref[0, :, 0]
    consts = (_scan_consts(0), _scan_consts(1))
    db = db_ref[0]
    a_neg = -jnp.exp(al_ref[0])

    def gates(ci, d, slot):
        mask, strict, v_mat, last = consts[d]
        c0 = pl.multiple_of(ci * T, T)
        lanes = pl.ds(c0, T)
        hr = slice(d * H, (d + 1) * H)
        dt = _softplus(dtt_ref[hr, lanes] + db[hr])
        a = dt * a_neg[hr]
        acs = _dot2(a, v_mat)
        a_last = jnp.broadcast_to(acs[:, last:last + 1], (H, T))
        rows_scr[slot, d, 0:H, :] = jnp.exp(acs)
        rows_scr[slot, d, H:2 * H, :] = jnp.exp(a_last)
        cbt_scr[d] = _dot(b_ref[pl.ds(c0, T), :], ct_ref[:, lanes])
        for r in range(H):
            tr = slice(r * T, (r + 1) * T)
            hi, lo = _split2(jnp.where(strict, a[r:r + 1], 0.0))
            lhs_scr[d, 0, tr, :] = hi
            lhs_scr[d, 1, tr, :] = lo
        dall = _dot(lhs_scr[d, 0], v_mat) + _dot(lhs_scr[d, 1], v_mat)
        for r in range(H):
            tr = slice(r * T, (r + 1) * T)
            pr = slice(r * SSD_HEADDIM, (r + 1) * SSD_HEADDIM)
            lm = jnp.exp(jnp.where(mask, dall[tr], -jnp.inf))
            p_scr[slot, d, tr, :] = (cbt_scr[d] * lm).astype(BF16)
            xd = xt_ref[pr, lanes] * dt[r:r + 1]
            xd_scr[slot, d, pr, :] = xd.astype(BF16)
            xdec_scr[slot, d, pr, :] = (xd * jnp.exp(a_last[r:r + 1] - acs[r:r + 1])).astype(BF16)

    def update(ci, d, slot):
        c0 = pl.multiple_of(ci * T, T)
        lanes = pl.ds(c0, T)
        rows = rows_scr[slot, d]
        sc = _dot(s_scr[d].astype(BF16), ct_ref[:, lanes])
        upd = _dot(xdec_scr[slot, d], b_ref[pl.ds(c0, T), :])
        y_scr = yf_scr if d == 0 else yb_scr
        for r in range(H):
            pr = slice(r * SSD_HEADDIM, (r + 1) * SSD_HEADDIM)
            tr = slice(r * T, (r + 1) * T)
            y_scr[pr, lanes] = _dot(xd_scr[slot, d, pr, :], p_scr[slot, d, tr, :]) + rows[r:r + 1] * sc[pr]
            s_scr[d, pr, :] = rows[H + r:H + r + 1] * s_scr[d, pr, :] + upd[pr]

    def step(c, slot):
        update(c, 0, slot)
        update(nc - 1 - c, 1, slot)
        gates(jnp.minimum(c + 1, nc - 1), 0, 1 - slot)
        gates(jnp.maximum(nc - 2 - c, 0), 1, 1 - slot)

    def body(i, carry):
        step(2 * i, 0)
        step(2 * i + 1, 1)
        return carry

    gates(0, 0, 0)
    gates(nc - 1, 1, 0)
    lax.fori_loop(0, nc // 2, body, 0)

    dsk = dsk_ref[0]

    def epilogue(i, carry):
        c0 = pl.multiple_of(i * T, T)
        lanes = pl.ds(c0, T)
        yt = yf_scr[:, lanes] + yb_scr[:, lanes] + dsk * xt_ref[:, lanes]
        y_ref[pl.ds(c0, T), :] = yt.T
        return carry

    lax.fori_loop(0, nc, epilogue, 0)
    sout_ref[0, :, 0] = s_scr[...]


def _ssd(xt, bm, ct, dtt, dt_bias, a_log, dskip, s0, B, L):
    assert L % (2 * SCAN_T) == 0 and SSD_STATE == SCAN_T
    nc = L // SCAN_T
    G = SSD_GROUPS
    nd = 2 * SSD_HPG
    M = B * L
    rep = lambda p, rows: jnp.broadcast_to(p.reshape(G, rows, 1), (G, rows, SCAN_T))
    small = lambda rows: pl.BlockSpec((1, rows, SCAN_T), lambda b, g: (g, 0, 0))
    st_spec = pl.BlockSpec((1, 2, 1, SSD_GW, SSD_STATE), lambda b, g: (b, 0, g, 0, 0))
    return pl.pallas_call(
        functools.partial(_ssd_kernel, nc=nc),
        grid=(B, G),
        in_specs=[pl.BlockSpec((SSD_GW, L), lambda b, g: (g, b)),
                  pl.BlockSpec((L, SSD_STATE), lambda b, g: (b, g)),
                  pl.BlockSpec((SSD_STATE, L), lambda b, g: (g, b)),
                  pl.BlockSpec((nd, L), lambda b, g: (g, b)),
                  small(nd), small(nd), small(SSD_GW), st_spec],
        out_specs=[pl.BlockSpec((L, SSD_GW), lambda b, g: (b, g)), st_spec],
        out_shape=[jax.ShapeDtypeStruct((M, SSD_INNER), F32),
                   jax.ShapeDtypeStruct((B, 2, G, SSD_GW, SSD_STATE), F32)],
        scratch_shapes=[pltpu.VMEM((SSD_GW, L), F32), pltpu.VMEM((SSD_GW, L), F32),
                        pltpu.VMEM((2, SSD_GW, SSD_STATE), F32),
                        pltpu.VMEM((2, 2, SSD_HPG * SCAN_T, SCAN_T), BF16),
                        pltpu.VMEM((2, SCAN_T, SCAN_T), F32),
                        pltpu.VMEM((2, 2, SSD_HPG * SCAN_T, SCAN_T), BF16),
                        pltpu.VMEM((2, 2, SSD_GW, SCAN_T), BF16),
                        pltpu.VMEM((2, 2, SSD_GW, SCAN_T), BF16),
                        pltpu.VMEM((2, 2, 2 * SSD_HPG, SCAN_T), F32)],
        compiler_params=_cparams("arbitrary", "arbitrary"),
        name="ssd",
    )(xt, bm, ct, dtt, rep(dt_bias, nd), rep(a_log, nd), rep(dskip, SSD_GW), s0)


def _hy_factors(L):
    return (128, 64) if L == 4096 else (32, (2 * L) // 32)


def _hilo(t):
    return _split2(t.astype(F32))


def _cblock(cr, ci):
    return jnp.concatenate([jnp.concatenate([cr, -ci], axis=-1), jnp.concatenate([ci, cr], axis=-1)], axis=-2)


def _dft_tables(L):
    n1c, n2c = _hy_factors(L)
    N = n1c * n2c
    h1 = n1c // 2
    i32 = jnp.int32
    k1 = jnp.arange(n1c, dtype=i32)
    n2 = jnp.arange(n2c, dtype=i32)

    def stage1(n1_count):
        n1 = jnp.arange(n1_count, dtype=i32)
        idx = (k1[None, :, None] * (n2[:, None, None] + n2c * n1[None, None, :])) % N
        ang = idx.astype(F32) * (2.0 * math.pi / N)
        return jnp.cos(ang), jnp.sin(ang)

    c, s = stage1(h1)
    g_fwd = _cblock(c, -s)
    g_inv = _cblock(jnp.swapaxes(c, 1, 2), jnp.swapaxes(s, 1, 2)) / n1c
    cf, sf = stage1(n1c)
    g_real = jnp.concatenate([cf, -sf], axis=1)
    idx2 = (n2[:, None] * n2[None, :]) % n2c
    ang2 = idx2.astype(F32) * (2.0 * math.pi / n2c)
    c2, s2 = jnp.cos(ang2), jnp.sin(ang2)
    f2 = _cblock(c2, -s2)
    f2_inv = _cblock(c2, s2) / n2c
    return dict(g_fwd=g_fwd.astype(BF16), g_inv=g_inv.astype(BF16), f2=f2.astype(BF16),
                f2_inv=f2_inv.astype(BF16), g_real=_hilo(g_real), f2_exact=_hilo(f2))


def _swap01(x):
    return jnp.swapaxes(x, 0, 1)


def _hy_s1_kernel(x_ref, *rest, nb, exact):
    if exact:
        gh_ref, gl_ref, o_ref = rest
    else:
        g_ref, o_ref = rest
    pb, rows, _, C = x_ref.shape
    xs = _swap01(x_ref[...].reshape(pb * rows, nb, C))
    if exact:
        res = [_dot3(gh_ref[j], gl_ref[j], xs[j]) for j in range(nb)]
    else:
        res = [_dot(g_ref[j], xs[j].astype(BF16)) for j in range(nb)]
    out = _swap01(jnp.stack(res, axis=0))
    o_ref[...] = out.reshape(o_ref.shape).astype(o_ref.dtype)


def _hy_stage1(x, tab, out_dtype):
    exact = isinstance(tab, tuple)
    tabs = tab if exact else (tab,)
    n2c, rows_out, R = tabs[0].shape
    Bx, rows, _, C = x.shape
    pb = R // rows
    nb = HY_NB
    cb = min(C, HY_W)
    tspec = pl.BlockSpec((nb, rows_out, R), lambda p, t, c: (t, 0, 0))
    return pl.pallas_call(
        functools.partial(_hy_s1_kernel, nb=nb, exact=exact),
        grid=(Bx // pb, n2c // nb, C // cb),
        in_specs=[pl.BlockSpec((pb, rows, nb, cb), lambda p, t, c: (p, 0, t, c))] + [tspec] * len(tabs),
        out_specs=pl.BlockSpec((None, 2, rows_out // 2, nb, cb), lambda p, t, c: (p, 0, 0, t, c)),
        out_shape=jax.ShapeDtypeStruct((Bx // pb, 2, rows_out // 2, n2c, C), out_dtype),
        compiler_params=_cparams("arbitrary", "arbitrary", "arbitrary"),
        name="hyena_stage1",
    )(x, *tabs)


def _hy_spec_kernel(a_ref, inv_ref, fh_ref, fl_ref, o_ref, *, kb):
    n2c = a_ref.shape[2]
    for kk in range(kb):
        rhs = jnp.concatenate([a_ref[0, kk], a_ref[1, kk]], axis=0)
        X = _dot3(fh_ref[...], fl_ref[...], rhs) * inv_ref[...]
        o_ref[0, kk] = X[0:n2c]
        o_ref[1, kk] = X[n2c:]


def _hy_filter_spectrum(a, inv_norm, f2, kb):
    _, n1c, n2c, W = a.shape
    fh, fl = f2
    blk = pl.BlockSpec((2, kb, n2c, W), lambda t: (0, t, 0, 0))
    fspec = pl.BlockSpec((2 * n2c, 2 * n2c), lambda t: (0, 0))
    return pl.pallas_call(
        functools.partial(_hy_spec_kernel, kb=kb),
        grid=(n1c // kb,),
        in_specs=[blk, pl.BlockSpec((1, W), lambda t: (0, 0)), fspec, fspec],
        out_specs=blk,
        out_shape=jax.ShapeDtypeStruct(a.shape, F32),
        compiler_params=_cparams("arbitrary"),
        name="hyena_filter_spectrum",
    )(a, inv_norm, fh, fl)


def _hy_mid_kernel(a_ref, k_ref, f_ref, i_ref, o_ref, *, kb):
    n2c = a_ref.shape[2]
    res = []
    for kk in range(kb):
        rhs = jnp.concatenate([a_ref[0, kk], a_ref[1, kk]], axis=0)
        X = _dot(f_ref[...], rhs)
        xr, xi = X[0:n2c], X[n2c:]
        kr, ki = k_ref[0, kk], k_ref[1, kk]
        Y = jnp.concatenate([xr * kr - xi * ki, xr * ki + xi * kr], axis=0)
        res.append(_dot(i_ref[...], Y.astype(BF16)))
    o_ref[...] = _swap01(jnp.stack(res, axis=0)).reshape(o_ref.shape).astype(o_ref.dtype)


def _hy_mid(a, kspec, order, f2, f2_inv):
    P, _, n1c, n2c, C = a.shape
    kb = HY_NB
    fspec = pl.BlockSpec((2 * n2c, 2 * n2c), lambda t, p: (0, 0))
    return pl.pallas_call(
        functools.partial(_hy_mid_kernel, kb=kb),
        grid=(n1c // kb, P),
        in_specs=[pl.BlockSpec((None, 2, kb, n2c, C), lambda t, p: (p, 0, t, 0, 0)),
                  pl.BlockSpec((2, kb, n2c, C), lambda t, p: (0, t, 0, order)),
                  fspec, fspec],
        out_specs=pl.BlockSpec((None, 2, n2c, kb, C), lambda t, p: (p, 0, 0, t, 0)),
        out_shape=jax.ShapeDtypeStruct((P, 2, n2c, n1c, C), BF16),
        compiler_params=_cparams("arbitrary", "arbitrary"),
        name="hyena_mid",
    )(a, kspec, f2, f2_inv)


def _hy_last_kernel(b_ref, z_ref, gate_ref, bias_ref, gi_ref, *rest, fuse, nb):
    if fuse:
        gf_ref, zo_ref, ao_ref = rest
    else:
        (zo_ref,) = rest
    _, h1, _, C = z_ref.shape
    zs = _swap01(z_ref[...].reshape(2 * h1, nb, C))
    gs = _swap01(gate_ref[...].reshape(2 * h1, nb, C))
    bias = bias_ref[...]
    zn, an = [], []
    for j in range(nb):
        bv = jnp.concatenate([b_ref[0, j], b_ref[1, j]], axis=0)
        y = _dot(gi_ref[j], bv)
        znew = gs[j] * (y + zs[j] * bias)
        zn.append(znew)
        if fuse:
            an.append(_dot(gf_ref[j], znew.astype(BF16)))
    zo_ref[...] = _swap01(jnp.stack(zn, axis=0)).reshape(zo_ref.shape).astype(zo_ref.dtype)
    if fuse:
        ao_ref[...] = _swap01(jnp.stack(an, axis=0)).reshape(ao_ref.shape).astype(ao_ref.dtype)


def _hy_last(bf, z, gate, bias, g_inv, g_fwd, out_dtype):
    P, _, n2c, n1c, C = bf.shape
    B, h1, _, _ = z.shape
    nb = HY_NB
    cb = C // 2
    fuse = g_fwd is not None
    zspec = pl.BlockSpec((2, h1, nb, cb), lambda p, t, c: (p, 0, t, c))
    in_specs = [pl.BlockSpec((None, 2, nb, n1c, cb), lambda p, t, c: (p, 0, t, 0, c)),
                zspec, zspec, pl.BlockSpec((1, cb), lambda p, t, c: (0, c)),
                pl.BlockSpec((nb, 2 * h1, 2 * n1c), lambda p, t, c: (t, 0, 0))]
    args = [bf, z, gate, bias, g_inv]
    out_specs = [zspec]
    out_shape = [jax.ShapeDtypeStruct(z.shape, out_dtype)]
    if fuse:
        in_specs.append(pl.BlockSpec((nb, 2 * n1c, 2 * h1), lambda p, t, c: (t, 0, 0)))
        args.append(g_fwd)
        out_specs.append(pl.BlockSpec((None, 2, n1c, nb, cb), lambda p, t, c: (p, 0, 0, t, c)))
        out_shape.append(jax.ShapeDtypeStruct((P, 2, n1c, n2c, C), BF16))
    return pl.pallas_call(
        functools.partial(_hy_last_kernel, fuse=fuse, nb=nb),
        grid=(P, n2c // nb, C // cb),
        in_specs=in_specs, out_specs=out_specs, out_shape=out_shape,
        compiler_params=_cparams("arbitrary", "arbitrary", "arbitrary"),
        name="hyena_last",
    )(*args)


def _hy_filter_kernel(f_ref, w1_ref, b1_ref, w2_ref, b2_ref, w3_ref, fr_ref, dl_ref, hk_ref, s_ref, *, L):
    i = pl.program_id(0)
    tl = f_ref.shape[0]
    f = f_ref[...]
    fr = fr_ref[...]
    h = jnp.sin(fr * (_dot(f.astype(BF16), w1_ref[...].astype(BF16)) + b1_ref[...]))
    h = jnp.sin(fr * (_dot(h.astype(BF16), w2_ref[...].astype(BF16)) + b2_ref[...]))
    hk = _dot(h.astype(BF16), w3_ref[...].astype(BF16)) * jnp.exp(-f[:, 0:1] * dl_ref[...])
    n = i * tl + lax.broadcasted_iota(jnp.int32, hk.shape, 0)
    hk = jnp.where(n == L, 0.0, hk)
    hk_ref[...] = hk
    tot = jnp.sum(jnp.abs(hk), axis=0, keepdims=True)

    @pl.when(i == 0)
    def _():
        s_ref[...] = tot

    @pl.when(i != 0)
    def _():
        s_ref[...] = s_ref[...] + tot


def _hy_filter(L, w1, b1, w2, b2, w3, freq):
    f32 = F32
    t = jnp.linspace(0.0, 1.0, L, dtype=f32)[:, None]
    ang = (2.0 * math.pi / L) * jnp.arange(L, dtype=f32)[:, None] * \
        jnp.linspace(1e-4, HY_BANDS - 1, HY_BANDS, dtype=f32)[None]
    feats = jnp.concatenate([t, jnp.cos(ang), -jnp.sin(ang)], axis=-1)
    feats = jnp.concatenate([feats, feats[0:1], feats[:0:-1]], axis=0)
    fpad = 128
    feats = jnp.pad(feats, ((0, 0), (0, fpad - HY_EMB)))
    w1p = jnp.pad(w1, ((0, fpad - HY_EMB), (0, 0)))
    deltas = jnp.abs(jnp.linspace(math.log(HY_TARGET) / HY_SLOW, math.log(HY_TARGET) / HY_FAST, HY_W, dtype=f32))
    OC = HY_ORDER * HY_W
    dl = jnp.tile(deltas, HY_ORDER).reshape(1, OC)
    tl = min(L, 512)
    nblk = L // tl
    full = lambda shp: pl.BlockSpec(shp, lambda i: (0, 0))
    return pl.pallas_call(
        functools.partial(_hy_filter_kernel, L=L),
        grid=(2 * nblk,),
        in_specs=[pl.BlockSpec((tl, fpad), lambda i: (i, 0)),
                  full((fpad, HY_HID)), full((1, HY_HID)), full((HY_HID, HY_HID)), full((1, HY_HID)),
                  pl.BlockSpec((HY_HID, OC), lambda i: (0, i // nblk)), full((1, HY_HID)), full((1, OC))],
        out_specs=[pl.BlockSpec((tl, OC), lambda i: (i, 0)), full((1, OC))],
        out_shape=[jax.ShapeDtypeStruct((2 * L, OC), F32), jax.ShapeDtypeStruct((1, OC), F32)],
        compiler_params=_cparams("arbitrary"),
        name="hyena_filter",
    )(feats, w1p, b1.reshape(1, HY_HID), w2, b2.reshape(1, HY_HID), w3, freq.reshape(1, HY_HID), dl)


def _hy_spectrum(L, tabs, w1, b1, w2, b2, w3, freq):
    n1c, n2c = _hy_factors(L)
    OC = HY_ORDER * HY_W
    ktime, s = _hy_filter(L, w1, b1, w2, b2, w3, freq)
    a = _hy_stage1(ktime.reshape(1, n1c, n2c, OC), tabs["g_real"], F32)[0]
    return _hy_filter_spectrum(a, 1.0 / s, tabs["f2_exact"], 8)


def _hyena(v, x1, x2, kspec, bias, tabs, B, L):
    n1c, n2c = _hy_factors(L)
    C = HY_W
    view = lambda t: t.reshape(B, n1c // 2, n2c, C)
    z0, g1, g2 = view(v), view(x1), view(x2)
    a = _hy_stage1(z0, tabs["g_fwd"], BF16)
    bf = _hy_mid(a, kspec, 0, tabs["f2"], tabs["f2_inv"])
    z1, a = _hy_last(bf, z0, g1, bias[0:1], tabs["g_inv"], tabs["g_fwd"], F32)
    bf = _hy_mid(a, kspec, 1, tabs["f2"], tabs["f2_inv"])
    (z2,) = _hy_last(bf, z1, g2, bias[1:2], tabs["g_inv"], None, BF16)
    return z2.reshape(B * L, C)


_EVEN_PLAN = (
    (0, HY_W, "conv", 1.0, False, F32),
    (HY_W, HY_W, "conv", 1.0, False, F32),
    (2 * HY_W, HY_W, "conv", 1.0, False, F32),
    (3 * HY_W, M_W, "conv_silu", 1.0, True, BF16),
    (3 * HY_W + M_W, M_W, "conv_silu", M_DH ** -0.5, False, BF16),
    (EV_CONV + M_W, M_W, "raw", 1.0, False, F32),
)
_EVEN_NT_PLAN = ((0, M_W, BF16), (M_W, 4 * M_HEADS, F32))
_ODD_NT_PLAN = ((0, 2 * SSD_HEADS, F32),)
_GN = SSD_GROUPS * SSD_STATE
_ODD_PLAN = (
    (0, SSD_INNER, "raw", 1.0, False, F32),
    (SSD_INNER, SSD_INNER, "conv_silu", 1.0, True, F32),
    (2 * SSD_INNER, _GN, "conv_silu", 1.0, False, BF16),
    (2 * SSD_INNER + _GN, _GN, "conv_silu", 1.0, True, BF16),
)


def _even_layer(x, mods_l, P, e, l, B, L, st, per_batch_cond, kspec, tabs):
    hv, hx1, hx2, qt, k, o, vt, gt = _inproj(
        x, mods_l, P["norm1_g"][l:l + 1], P["ev_in_w"], P["ev_in_wt"], P["ev_conv_w"], P["ev_conv_b"], e,
        plan=_EVEN_PLAN, nt_plan=_EVEN_NT_PLAN, conv0=0, seq_len=L, per_batch_cond=per_batch_cond)
    y_hy = _hyena(hv, hx1, hx2, kspec, P["hy_bias"][e], tabs, B, L)
    c0, m0 = st
    y_m, c_new, m_new = _mlstm(qt, k, vt, o, gt, P["m_gate_b"][e], P["m_norm_g"][e], c0, m0, B, L)
    x = _outproj_even(y_hy, y_m, x, mods_l, P["ev_out_w"], e, seq_len=L, per_batch_cond=per_batch_cond)
    return x, (c_new, m_new)


def _by_group(p):
    return p.reshape(2, SSD_GROUPS, SSD_HPG).transpose(1, 0, 2).reshape(SSD_GROUPS, 2 * SSD_HPG)


def _odd_layer(x, mods_l, P, o, l, B, L, s0, per_batch_cond):
    z, xt, bm, ct, dtt = _inproj(
        x, mods_l, P["norm1_g"][l:l + 1], P["od_in_w"], P["od_in_wt"], P["od_conv_w"], P["od_conv_b"], o,
        plan=_ODD_PLAN, nt_plan=_ODD_NT_PLAN, conv0=SSD_INNER, seq_len=L, per_batch_cond=per_batch_cond)
    dskip = jnp.repeat(P["ssd_D"][o], SSD_HEADDIM).reshape(SSD_GROUPS, SSD_GW)
    y, s_new = _ssd(xt, bm, ct, dtt, _by_group(P["ssd_dt_bias"][o]), _by_group(P["ssd_A_log"][o]), dskip, s0, B, L)
    x = _outproj_odd(y, z, x, mods_l, P["ssd_norm_g"], P["od_out_w"], o, seq_len=L, per_batch_cond=per_batch_cond)
    return x, s_new


def _to_colmajor(x, B, L):
    rows = L // GRID_W
    return x.reshape(B, rows, GRID_W, D_MODEL).swapaxes(1, 2).reshape(B * L, D_MODEL)


def _from_colmajor(x, B, L):
    rows = L // GRID_W
    return x.reshape(B, GRID_W, rows, D_MODEL).swapaxes(1, 2).reshape(B * L, D_MODEL)


def _trunk(x, mods, P, B, L, init_states, per_batch_cond, grid, kspecs, tabs):
    states = []
    for l in range(DEPTH):
        if grid and l == 2:
            x = _to_colmajor(x, B, L)
        mods_l = mods[l]
        if l % 2 == 0:
            x, st = _even_layer(x, mods_l, P, l // 2, l, B, L, init_states[l], per_batch_cond, kspecs[l // 2], tabs)
        else:
            x, st = _odd_layer(x, mods_l, P, l // 2, l, B, L, init_states[l], per_batch_cond)
        states.append(st)
        x = _ffn(x, mods_l, P["norm2_g"].reshape(DEPTH, 1, D_MODEL), P["ffn_w1"], P["ffn_w3"], P["ffn_w2"],
                 P["final_g"].reshape(1, D_MODEL), l, seq_len=L, per_batch_cond=per_batch_cond,
                 final=(l == DEPTH - 1))
    if grid:
        x = _from_colmajor(x, B, L)
    return x, states


def _pack_mlstm_state(C, n, m):
    B = C.shape[0]
    ext = jnp.concatenate([jnp.swapaxes(C, -1, -2), n[..., None, :],
                           jnp.zeros(C.shape[:-2] + (M_DH - 1, M_DH), F32)], axis=-2)
    m_rep = jnp.broadcast_to(m.reshape(B, 2 * M_HEADS, 1, 1), (B, 2 * M_HEADS, 1, M_DH))
    return ext.reshape(B, 2 * M_HEADS, 2 * M_DH, M_DH), m_rep


def kernel(x_prompt, x_sample, state_mlstm_C, state_mlstm_n, state_mlstm_m, state_ssd, c, c_ctx,
           mod_w, mod_b, norm1_g, norm2_g, ffn_w1, ffn_w3, ffn_w2, final_g,
           ev_in_w, ev_conv_w, ev_conv_b, hy_w1, hy_b1, hy_w2, hy_b2, hy_w3, hy_freq, hy_bias,
           m_gate_b, m_norm_g, ev_out_w,
           od_in_w, od_conv_w, od_conv_b, ssd_dt_bias, ssd_A_log, ssd_D, ssd_norm_g, od_out_w):
    Bp, Lp, _ = x_prompt.shape
    Bs, Ls, _ = x_sample.shape
    bf = lambda w: w.astype(BF16)
    n_odd = od_in_w.shape[0]
    od_dt_w = od_in_w[:, :, OD_PROJ - 2 * SSD_HEADS:].reshape(n_odd, D_MODEL, 2, SSD_GROUPS, SSD_HPG)
    od_dt_w = od_dt_w.transpose(0, 3, 2, 4, 1).reshape(n_odd, 2 * SSD_HEADS, D_MODEL)
    P = dict(norm1_g=norm1_g, norm2_g=norm2_g, ffn_w1=bf(ffn_w1), ffn_w3=bf(ffn_w3), ffn_w2=bf(ffn_w2),
             final_g=final_g, ev_in_w=bf(ev_in_w),
             ev_in_wt=bf(jnp.concatenate([ev_in_w[:, :, EV_CONV:EV_CONV + M_W],
                                          ev_in_w[:, :, EV_PROJ - 4 * M_HEADS:]], axis=2)).swapaxes(1, 2),
             ev_conv_w=ev_conv_w, ev_conv_b=ev_conv_b, hy_bias=hy_bias, m_gate_b=m_gate_b, m_norm_g=m_norm_g,
             ev_out_w=bf(ev_out_w), od_in_w=bf(od_in_w), od_in_wt=bf(od_dt_w),
             od_conv_w=od_conv_w, od_conv_b=od_conv_b, ssd_dt_bias=ssd_dt_bias, ssd_A_log=ssd_A_log,
             ssd_D=ssd_D, ssd_norm_g=ssd_norm_g.reshape(-1, 1, SSD_INNER), od_out_w=bf(od_out_w))

    ncond = 16
    cond = jnp.concatenate([c_ctx[None, :], c, jnp.zeros((ncond - 1 - Bs, D_MODEL), F32)], axis=0)
    mods = _modulation(cond, mod_w, mod_b).reshape(DEPTH, ncond, 6, D_MODEL)

    tabs = {L: _dft_tables(L) for L in (Lp, Ls)}
    kspecs = {L: [_hy_spectrum(L, tabs[L], hy_w1[e], hy_b1[e], hy_w2[e], hy_b2[e], hy_w3[e], hy_freq[e])
                  for e in range((DEPTH + 1) // 2)] for L in (Lp, Ls)}

    G = SSD_GROUPS
    zero_m = _pack_mlstm_state(jnp.zeros((Bp, 2, M_HEADS, M_DH, M_DH), F32), jnp.zeros((Bp, 2, M_HEADS, M_DH), F32),
                               jnp.zeros((Bp, 2, M_HEADS), F32))
    zero_s = jnp.zeros((Bp, 2, G, SSD_GW, SSD_STATE), F32)
    ctx_init = [zero_m if l % 2 == 0 else zero_s for l in range(DEPTH)]
    y_prompt, ctx_states = _trunk(x_prompt.reshape(Bp * Lp, D_MODEL), mods, P, Bp, Lp, ctx_init, False, False,
                                  kspecs[Lp], tabs[Lp])

    lat_init = []
    for l in range(DEPTH):
        if l % 2 == 0:
            e = l // 2
            lat_init.append(_pack_mlstm_state(state_mlstm_C[:, e], state_mlstm_n[:, e], state_mlstm_m[:, e]))
        else:
            lat_init.append(state_ssd[:, l // 2].reshape(Bs, 2, G, SSD_GW, SSD_STATE))
    y_sample, _ = _trunk(x_sample.reshape(Bs * Ls, D_MODEL), mods, P, Bs, Ls, lat_init, True, True,
                         kspecs[Ls], tabs[Ls])

    cs = [ctx_states[l][0].reshape(Bp, 2, M_HEADS, 2 * M_DH, M_DH) for l in range(0, DEPTH, 2)]
    ms = [ctx_states[l][1][:, :, 0, 0].reshape(Bp, 2, M_HEADS) for l in range(0, DEPTH, 2)]
    new_C = jnp.stack([jnp.swapaxes(t[..., :M_DH, :], -1, -2) for t in cs], axis=1)
    new_n = jnp.stack([t[..., M_DH, :] for t in cs], axis=1)
    new_m = jnp.stack(ms, axis=1)
    new_ssd = jnp.stack([ctx_states[l].reshape(Bp, 2, SSD_HEADS, SSD_HEADDIM, SSD_STATE)
                         for l in range(1, DEPTH, 2)], axis=1)
    return (y_prompt.reshape(Bp, Lp, D_MODEL), y_sample.reshape(Bs, Ls, D_MODEL), new_C, new_n, new_m, new_ssd)
```

```python
import functools
import math

import jax
import jax.numpy as jnp
from jax import lax
from jax.experimental import pallas as pl
from jax.experimental.pallas import tpu as pltpu

F32 = jnp.float32
BF16 = jnp.bfloat16

D_MODEL = 1024
DEPTH = 4
GRID_W = 64
EPS = 1e-6
HY_W = D_MODEL // 2
HY_ORDER = 2
HY_EMB = 33
HY_BANDS = (HY_EMB - 1) // 2
HY_HID = 64
HY_TARGET = 1e-2
HY_FAST = 0.3
HY_SLOW = 1.5
M_HEADS = 4
M_DH = 128
M_W = M_HEADS * M_DH
M_EXT = M_DH + 16
EV_CONV = 3 * HY_W + 2 * M_W
EV_PROJ = EV_CONV + 2 * M_W + 4 * M_HEADS
SSD_INNER = 2 * D_MODEL
SSD_HEADDIM = 64
SSD_HEADS = SSD_INNER // SSD_HEADDIM
SSD_GROUPS = 8
SSD_HPG = SSD_HEADS // SSD_GROUPS
SSD_STATE = 128
SSD_GW = SSD_HPG * SSD_HEADDIM
SSD_CONV_DIM = SSD_INNER + 2 * SSD_GROUPS * SSD_STATE
OD_PROJ = SSD_INNER + SSD_CONV_DIM + 2 * SSD_HEADS
FFN_HIDDEN = -(-8 * D_MODEL // (3 * 256)) * 256

V7X_VMEM_BYTES = 64 * 1024 * 1024
VMEM_LIMIT = V7X_VMEM_BYTES * 3 // 4
ROW_TILE = 256
HALO = 16
COL_CHUNK = 512
HY_NB = 16
SCAN_T = 128


def _cparams(*sem):
    return pltpu.CompilerParams(dimension_semantics=sem, vmem_limit_bytes=VMEM_LIMIT)


def _dot(a, b):
    return jnp.dot(a, b, preferred_element_type=F32)


def _dot_nt(a, b):
    return lax.dot_general(a, b, (((1,), (1,)), ((), ())), preferred_element_type=F32)


def _split2(x):
    h = x.astype(BF16)
    return h, (x - h.astype(F32)).astype(BF16)


def _dot3(th, tl, x):
    xh, xl = _split2(x)
    return _dot(th, xh) + _dot(th, xl) + _dot(tl, xh)


def _dot2(x, t):
    xh, xl = _split2(x)
    return _dot(xh, t) + _dot(xl, t)


def _sigmoid(x):
    return 1.0 / (1.0 + jnp.exp(-x))


def _silu(x):
    return x * _sigmoid(x)


def _log_sigmoid(x):
    return jnp.minimum(x, 0.0) - jnp.log(1.0 + jnp.exp(-jnp.abs(x)))


def _softplus(x):
    return jnp.maximum(x, 0.0) + jnp.log(1.0 + jnp.exp(-jnp.abs(x)))


def _rms(x):
    return x * lax.rsqrt(jnp.mean(x * x, axis=-1, keepdims=True) + EPS)


def _mod_kernel(c_ref, w_ref, b_ref, o_ref):
    a = _silu(c_ref[...]).astype(BF16)
    o_ref[0] = _dot(a, w_ref[0].astype(BF16)) + b_ref[0]


def _modulation(cond, mod_w, mod_b):
    nc = cond.shape[0]
    tn = 1536
    return pl.pallas_call(
        _mod_kernel,
        grid=(DEPTH, 6 * D_MODEL // tn),
        in_specs=[pl.BlockSpec((nc, D_MODEL), lambda l, j: (0, 0)),
                  pl.BlockSpec((1, D_MODEL, tn), lambda l, j: (l, 0, j)),
                  pl.BlockSpec((1, 1, tn), lambda l, j: (l, 0, j))],
        out_specs=pl.BlockSpec((1, nc, tn), lambda l, j: (l, 0, j)),
        out_shape=jax.ShapeDtypeStruct((DEPTH, nc, 6 * D_MODEL), F32),
        compiler_params=_cparams("arbitrary", "arbitrary"),
        name="modulation",
    )(cond, mod_w, mod_b.reshape(DEPTH, 1, 6 * D_MODEL))


def _inproj_kernel(x_ref, xp_ref, xn_ref, mod_ref, g_ref, w_ref, wt_ref, cw_ref, cb_ref, *rest,
                   plan, nt_plan, tiles_per_seq, conv0):
    nseg = len(plan)
    outs = rest[:nseg]
    nt_outs = rest[nseg:nseg + len(nt_plan)]
    hbuf = rest[nseg + len(nt_plan)]
    tm = x_ref.shape[0]
    i = pl.program_id(0)
    j = i % tiles_per_seq
    pmask = (j != 0).astype(F32)
    nmask = (j != tiles_per_seq - 1).astype(F32)
    mod = mod_ref[0]
    sh, sc = mod[0:1], mod[1:2]
    g = g_ref[...]

    def normmod(xv):
        return (_rms(xv) * g * (1.0 + sc) + sh).astype(BF16)

    hbuf[0:tm] = normmod(x_ref[...])
    hbuf[tm:tm + HALO] = normmod(xp_ref[...])
    hbuf[tm + HALO:tm + 2 * HALO] = normmod(xn_ref[...])
    rowid = lax.broadcasted_iota(jnp.int32, (tm, COL_CHUNK), 0)

    for (c0, width, kind, scale, transposed), o_ref in zip(plan, outs):
        for cc in range(c0, c0 + width, COL_CHUNK):
            w = w_ref[0, :, cc:cc + COL_CHUNK]
            if kind == "raw":
                y = _dot(hbuf[0:tm], w)
            else:
                full = _dot(hbuf[...], w)
                acc = full[0:tm]
                ap = full[tm + HALO - 1:tm + HALO] * pmask
                an = full[tm + HALO:tm + HALO + 1] * nmask
                prev = jnp.where(rowid == 0, ap, pltpu.roll(acc, 1, 0))
                nxt = jnp.where(rowid == tm - 1, an, pltpu.roll(acc, tm - 1, 0))
                cw = cw_ref[0, :, cc - conv0:cc - conv0 + COL_CHUNK]
                cb = cb_ref[0, :, cc - conv0:cc - conv0 + COL_CHUNK]
                y = cb + cw[0:1] * prev + cw[1:2] * acc + cw[2:3] * nxt
                if kind == "conv_silu":
                    y = _silu(y)
                if scale != 1.0:
                    y = y * scale
            if transposed:
                o_ref[cc - c0:cc - c0 + COL_CHUNK, :] = y.T.astype(o_ref.dtype)
            else:
                o_ref[:, cc - c0:cc - c0 + COL_CHUNK] = y.astype(o_ref.dtype)

    for (r0, rows), o_ref in zip(nt_plan, nt_outs):
        o_ref[...] = _dot_nt(wt_ref[0, r0:r0 + rows, :], hbuf[0:tm]).astype(o_ref.dtype)


def _inproj(x, mods_l, g_l, w_all, wt_all, cw_all, cb_all, layer_idx, *, plan, nt_plan, conv0, seq_len,
            per_batch_cond):
    M = x.shape[0]
    tm = ROW_TILE
    tps = seq_len // tm
    nt = M // tm
    hb = tm // HALO
    ntot = w_all.shape[2]
    tw = wt_all.shape[1]
    cdim = cw_all.shape[2]
    e = layer_idx
    cond_map = (lambda i: (1 + i // tps, 0, 0)) if per_batch_cond else (lambda i: (0, 0, 0))
    in_specs = [
        pl.BlockSpec((tm, D_MODEL), lambda i: (i, 0)),
        pl.BlockSpec((HALO, D_MODEL), lambda i: (jnp.maximum(i * hb - 1, 0), 0)),
        pl.BlockSpec((HALO, D_MODEL), lambda i: (jnp.minimum((i + 1) * hb, nt * hb - 1), 0)),
        pl.BlockSpec((1, 6, D_MODEL), cond_map),
        pl.BlockSpec((1, D_MODEL), lambda i: (0, 0)),
        pl.BlockSpec((1, D_MODEL, ntot), lambda i: (e, 0, 0), pipeline_mode=pl.Buffered(1)),
        pl.BlockSpec((1, tw, D_MODEL), lambda i: (e, 0, 0)),
        pl.BlockSpec((1, 3, cdim), lambda i: (e, 0, 0)),
        pl.BlockSpec((1, 1, cdim), lambda i: (e, 0, 0)),
    ]
    out_specs, out_shape = [], []
    for (c0, width, kind, scale, transposed, dt) in plan:
        if transposed:
            out_specs.append(pl.BlockSpec((width, tm), lambda i: (0, i)))
            out_shape.append(jax.ShapeDtypeStruct((width, M), dt))
        else:
            out_specs.append(pl.BlockSpec((tm, width), lambda i: (i, 0)))
            out_shape.append(jax.ShapeDtypeStruct((M, width), dt))
    for (r0, rows, dt) in nt_plan:
        out_specs.append(pl.BlockSpec((rows, tm), lambda i: (0, i)))
        out_shape.append(jax.ShapeDtypeStruct((rows, M), dt))
    kplan = tuple(p[:5] for p in plan)
    knt = tuple(p[:2] for p in nt_plan)
    return pl.pallas_call(
        functools.partial(_inproj_kernel, plan=kplan, nt_plan=knt, tiles_per_seq=tps, conv0=conv0),
        grid=(nt,),
        in_specs=in_specs,
        out_specs=out_specs,
        out_shape=out_shape,
        scratch_shapes=[pltpu.VMEM((tm + 2 * HALO, D_MODEL), BF16)],
        compiler_params=_cparams("arbitrary"),
        name="inproj",
    )(x, x, x, mods_l, g_l, w_all, wt_all, cw_all, cb_all.reshape(cb_all.shape[0], 1, cdim))


FFN_CHUNK = FFN_HIDDEN // 2


def _ffn_body(x, mod, g, w1_ref, w3_ref, w2_ref, fg_ref, final):
    h = (_rms(x) * g * (1.0 + mod[4:5]) + mod[3:4]).astype(BF16)
    ff = jnp.zeros(x.shape, F32)
    for c0 in range(0, FFN_HIDDEN, FFN_CHUNK):
        a = _dot(h, w1_ref[0, :, c0:c0 + FFN_CHUNK])
        b = _dot(h, w3_ref[0, :, c0:c0 + FFN_CHUNK])
        u = (_silu(a) * b).astype(BF16)
        ff = ff + _dot(u, w2_ref[0, c0:c0 + FFN_CHUNK, :])
    xo = x + mod[5:6] * ff
    if final:
        xo = _rms(xo) * fg_ref[...]
    return xo


def _even_tail_kernel(yh_ref, ym_ref, x_ref, mod_ref, wo_ref, g_ref, w1_ref, w3_ref, w2_ref, fg_ref, o_ref, *, final):
    mod = mod_ref[0]
    out = _dot(yh_ref[...], wo_ref[0, 0:HY_W]) + _dot(ym_ref[...], wo_ref[0, HY_W:])
    x = x_ref[...] + mod[2:3] * out
    o_ref[...] = _ffn_body(x, mod, g_ref[0], w1_ref, w3_ref, w2_ref, fg_ref, final)


def _odd_tail_kernel(y_ref, z_ref, ng_ref, x_ref, mod_ref, wo_ref, g_ref, w1_ref, w3_ref, w2_ref, fg_ref, o_ref, *,
                     final):
    mod = mod_ref[0]
    u = y_ref[...].astype(F32) * _silu(z_ref[...].astype(F32))
    un = (_rms(u) * ng_ref[0]).astype(BF16)
    x = x_ref[...] + mod[2:3] * _dot(un, wo_ref[0])
    o_ref[...] = _ffn_body(x, mod, g_ref[0], w1_ref, w3_ref, w2_ref, fg_ref, final)


def _layer_tail(kern, mix_args, mix_specs, x, mods_l, wo_all, widx, P, l, *, seq_len, per_batch_cond, final):
    M = x.shape[0]
    tm = ROW_TILE
    tps = seq_len // tm
    cond_map = (lambda i: (1 + i // tps, 0, 0)) if per_batch_cond else (lambda i: (0, 0, 0))
    wspec = lambda shape, idx: pl.BlockSpec(shape, lambda i: (idx, 0, 0), pipeline_mode=pl.Buffered(1))
    return pl.pallas_call(
        functools.partial(kern, final=final),
        grid=(M // tm,),
        in_specs=mix_specs + [
            pl.BlockSpec((tm, D_MODEL), lambda i: (i, 0)),
            pl.BlockSpec((1, 6, D_MODEL), cond_map),
            wspec((1,) + wo_all.shape[1:], widx),
            pl.BlockSpec((1, 1, D_MODEL), lambda i: (l, 0, 0)),
            wspec((1, D_MODEL, FFN_HIDDEN), l), wspec((1, D_MODEL, FFN_HIDDEN), l),
            wspec((1, FFN_HIDDEN, D_MODEL), l),
            pl.BlockSpec((1, D_MODEL), lambda i: (0, 0))],
        out_specs=pl.BlockSpec((tm, D_MODEL), lambda i: (i, 0)),
        out_shape=jax.ShapeDtypeStruct((M, D_MODEL), F32),
        compiler_params=_cparams("arbitrary"),
        name="layer_tail",
    )(*mix_args, x, mods_l, wo_all, P["norm2_g"].reshape(DEPTH, 1, D_MODEL), P["ffn_w1"], P["ffn_w3"], P["ffn_w2"],
      P["final_g"].reshape(1, D_MODEL))


def _row_spec(width):
    return pl.BlockSpec((ROW_TILE, width), lambda i: (i, 0))


def _scan_consts(d):
    T = SCAN_T
    row = lax.broadcasted_iota(jnp.int32, (T, T), 0)
    col = lax.broadcasted_iota(jnp.int32, (T, T), 1)
    if d == 0:
        mask, strict, last = row <= col, col > row, T - 1
    else:
        mask, strict, last = row >= col, col < row, 0
    v_mat = jnp.where(mask, 1.0, 0.0).astype(BF16)
    return mask, strict, v_mat, last


def _mlstm_kernel(qt_ref, k_ref, vt_ref, o_ref, gt_ref, gb_ref, ng_ref, c0_ref, m0_ref,
                  y_ref, cst_ref, mst_ref, h_scr, c_scr, m_scr, lhs_scr, sb_scr, vw_scr, rows_scr, *, nc):
    T = SCAN_T
    H = M_HEADS
    c_scr[...] = c0_ref[0]
    m_scr[...] = m0_ref[0]
    row = lax.broadcasted_iota(jnp.int32, (T, T), 0)
    col = lax.broadcasted_iota(jnp.int32, (T, T), 1)
    eye = row == col
    ones_blk = jnp.where(lax.broadcasted_iota(jnp.int32, (M_EXT - M_DH, T), 0) == 0, 1.0, 0.0)
    consts = (_scan_consts(0), _scan_consts(1))
    gb = gb_ref[...]

    def gates(ci, d, slot):
        mask, strict, v_mat, last = consts[d]
        w2 = jnp.concatenate([v_mat, v_mat], axis=0)
        c0 = pl.multiple_of(ci * T, T)
        lanes = pl.ds(c0, T)
        graw = gt_ref[:, lanes] + gb
        lf = _log_sigmoid(graw)
        b_all = _dot2(lf, v_mat)
        for h in range(H):
            fi = (2 * d + 1) * H + h
            ii = (2 * d) * H + h
            tr = slice(h * T, (h + 1) * T)
            a = jnp.concatenate([jnp.where(strict, lf[fi:fi + 1], 0.0), jnp.where(eye, graw[ii:ii + 1], 0.0)],
                                axis=1)
            hi, lo = _split2(a)
            lhs_scr[d, 0, tr, :] = hi
            lhs_scr[d, 1, tr, :] = lo
        dall = _dot(lhs_scr[d, 0], w2) + _dot(lhs_scr[d, 1], w2)
        for h in range(H):
            fi = (2 * d + 1) * H + h
            ii = (2 * d) * H + h
            hl = slice(h * M_DH, (h + 1) * M_DH)
            tr = slice(h * T, (h + 1) * T)
            dlog = jnp.where(mask, dall[tr], -jnp.inf)
            mloc = jnp.max(dlog, axis=0, keepdims=True)
            sb_scr[slot, d, tr, :] = (_dot(k_ref[pl.ds(c0, T), hl], qt_ref[hl, lanes])
                                      * jnp.exp(dlog - mloc)).astype(BF16)
            b_row, i_row = b_all[fi:fi + 1], graw[ii:ii + 1]
            mloc_last = jnp.broadcast_to(mloc[:, last:last + 1], (1, T))
            b_last = jnp.broadcast_to(b_row[:, last:last + 1], (1, T))
            w_end = jnp.exp(b_last - b_row + i_row - mloc_last)
            vw_scr[slot, d, h, 0:M_DH, :] = (vt_ref[hl, lanes].astype(F32) * w_end).astype(BF16)
            vw_scr[slot, d, h, M_DH:, :] = (ones_blk * w_end).astype(BF16)
            rows_scr[slot, d, 4 * h:4 * h + 4, :] = jnp.concatenate([b_row, mloc, mloc_last, b_last], axis=0)

    def update(ci, d, slot, accumulate):
        c0 = pl.multiple_of(ci * T, T)
        lanes = pl.ds(c0, T)
        for h in range(H):
            sidx = d * H + h
            hl = slice(h * M_DH, (h + 1) * M_DH)
            tr = slice(h * T, (h + 1) * T)
            rows = rows_scr[slot, d, 4 * h:4 * h + 4, :]
            b_row, mloc, mloc_last, b_last = rows[0:1], rows[1:2], rows[2:3], rows[3:4]
            m_row = m_scr[sidx]
            inter = b_row + m_row
            mt = jnp.maximum(inter, mloc)
            kh = k_ref[pl.ds(c0, T), hl]
            qt = qt_ref[hl, lanes]
            vext = jnp.concatenate([vt_ref[hl, lanes], ones_blk.astype(BF16)], axis=0)
            ct = c_scr[sidx]
            tot = (jnp.exp(mloc - mt) * _dot(vext, sb_scr[slot, d, tr, :])
                   + jnp.exp(inter - mt) * _dot(ct.astype(BF16), qt))
            den = tot[M_DH:M_DH + 1]
            ht = tot[0:M_DH] / jnp.maximum(jnp.abs(den), jnp.exp(-mt))
            if accumulate:
                h_scr[hl, lanes] = h_scr[hl, lanes] + ht
            else:
                h_scr[hl, lanes] = ht
            m_new = jnp.maximum(b_last + m_row, mloc_last)
            c_scr[sidx] = (jnp.exp(b_last + m_row - m_new) * ct
                           + jnp.exp(mloc_last - m_new) * _dot(vw_scr[slot, d, h], kh))
            m_scr[sidx] = m_new

    def step(c, slot, accumulate):
        update(c, 0, slot, accumulate)
        update(nc - 1 - c, 1, slot, accumulate)
        gates(jnp.minimum(c + 1, nc - 1), 0, 1 - slot)
        gates(jnp.maximum(nc - 2 - c, 0), 1, 1 - slot)

    gates(0, 0, 0)
    gates(nc - 1, 1, 0)
    if nc % 4 == 0:
        def pair(accumulate):
            def body(i, carry):
                step(2 * i, 0, accumulate)
                step(2 * i + 1, 1, accumulate)
                return carry
            return body
        lax.fori_loop(0, nc // 4, pair(False), 0)
        lax.fori_loop(nc // 4, nc // 2, pair(True), 0)
    else:
        for c in range(nc):
            step(c, c % 2, c >= nc // 2)

    def epilogue(i, carry):
        c0 = pl.multiple_of(i * T, T)
        for h in range(H):
            hl = slice(h * M_DH, (h + 1) * M_DH)
            hh = h_scr[hl, pl.ds(c0, T)]
            hn = hh * lax.rsqrt(jnp.mean(hh * hh, axis=0, keepdims=True) + EPS)
            y = hn.T * ng_ref[:, hl] * _sigmoid(o_ref[pl.ds(c0, T), hl])
            y_ref[pl.ds(c0, T), hl] = y.astype(y_ref.dtype)
        return carry

    lax.fori_loop(0, nc, epilogue, 0)
    cst_ref[0] = c_scr[...]
    mst_ref[0] = m_scr[...]


def _mlstm(qt, k, vt, o, gt, gate_b, norm_g, c0, m0, B, L):
    assert L % (2 * SCAN_T) == 0 and M_DH == SCAN_T
    nc = L // SCAN_T
    ng = 4 * M_HEADS
    M = B * L
    one = pl.Buffered(1)
    st_spec = pl.BlockSpec((1, 2 * M_HEADS, M_EXT, M_DH), lambda b: (b, 0, 0, 0))
    m_spec = pl.BlockSpec((1, 2 * M_HEADS, 1, M_DH), lambda b: (b, 0, 0, 0))
    return pl.pallas_call(
        functools.partial(_mlstm_kernel, nc=nc),
        grid=(B,),
        in_specs=[pl.BlockSpec((M_W, L), lambda b: (0, b), pipeline_mode=one),
                  pl.BlockSpec((L, M_W), lambda b: (b, 0), pipeline_mode=one),
                  pl.BlockSpec((M_W, L), lambda b: (0, b), pipeline_mode=one),
                  pl.BlockSpec((L, M_W), lambda b: (b, 0), pipeline_mode=one),
                  pl.BlockSpec((ng, L), lambda b: (0, b)),
                  pl.BlockSpec((ng, SCAN_T), lambda b: (0, 0)),
                  pl.BlockSpec((1, M_W), lambda b: (0, 0)),
                  st_spec, m_spec],
        out_specs=[pl.BlockSpec((L, M_W), lambda b: (b, 0)), st_spec, m_spec],
        out_shape=[jax.ShapeDtypeStruct((M, M_W), BF16),
                   jax.ShapeDtypeStruct((B, 2 * M_HEADS, M_EXT, M_DH), F32),
                   jax.ShapeDtypeStruct((B, 2 * M_HEADS, 1, M_DH), F32)],
        scratch_shapes=[pltpu.VMEM((M_W, L), F32),
                        pltpu.VMEM((2 * M_HEADS, M_EXT, M_DH), F32),
                        pltpu.VMEM((2 * M_HEADS, 1, M_DH), F32),
                        pltpu.VMEM((2, 2, M_HEADS * SCAN_T, 2 * SCAN_T), BF16),
                        pltpu.VMEM((2, 2, M_HEADS * SCAN_T, SCAN_T), BF16),
                        pltpu.VMEM((2, 2, M_HEADS, M_EXT, SCAN_T), BF16),
                        pltpu.VMEM((2, 2, 4 * M_HEADS, SCAN_T), F32)],
        compiler_params=_cparams("arbitrary"),
        name="mlstm",
    )(qt, k, vt, o, gt, jnp.broadcast_to(gate_b.reshape(ng, 1), (ng, SCAN_T)), norm_g.reshape(1, M_W), c0, m0)


def _ssd_kernel(xt_ref, b_ref, ct_ref, dtt_ref, db_ref, al_ref, dsk_ref, s0_ref,
                y_ref, sout_ref, yf_scr, yb_scr, s_scr, lhs_scr, cbt_scr, p_scr, xd_scr, xdec_scr, rows_scr, *, nc):
    T = SCAN_T
    H = SSD_HPG
    s_scr[...] = s0_ref[0, :, 0]
    consts = (_scan_consts(0), _scan_consts(1))
    db = db_ref[0]
    a_neg = -jnp.exp(al_ref[0])

    def gates(ci, d, slot):
        mask, strict, v_mat, last = consts[d]
        c0 = pl.multiple_of(ci * T, T)
        lanes = pl.ds(c0, T)
        hr = slice(d * H, (d + 1) * H)
        dt = _softplus(dtt_ref[hr, lanes] + db[hr])
        a = dt * a_neg[hr]
        acs = _dot2(a, v_mat)
        a_last = jnp.broadcast_to(acs[:, last:last + 1], (H, T))
        rows_scr[slot, d, 0:H, :] = jnp.exp(acs)
        rows_scr[slot, d, H:2 * H, :] = jnp.exp(a_last)
        cbt_scr[d] = _dot(b_ref[pl.ds(c0, T), :], ct_ref[:, lanes])
        for r in range(H):
            tr = slice(r * T, (r + 1) * T)
            hi, lo = _split2(jnp.where(strict, a[r:r + 1], 0.0))
            lhs_scr[d, tr, 0:T] = hi
            lhs_scr[d, tr, T:2 * T] = lo
        dall = _dot(lhs_scr[d], jnp.concatenate([v_mat, v_mat], axis=0))
        for r in range(H):
            tr = slice(r * T, (r + 1) * T)
            pr = slice(r * SSD_HEADDIM, (r + 1) * SSD_HEADDIM)
            lm = jnp.exp(jnp.where(mask, dall[tr], -jnp.inf))
            p_scr[slot, d, tr, :] = (cbt_scr[d] * lm).astype(BF16)
            xd = xt_ref[pr, lanes] * dt[r:r + 1]
            xd_scr[slot, d, pr, :] = xd.astype(BF16)
            xdec_scr[slot, d, pr, :] = (xd * jnp.exp(a_last[r:r + 1] - acs[r:r + 1])).astype(BF16)

    def update(ci, d, slot):
        c0 = pl.multiple_of(ci * T, T)
        lanes = pl.ds(c0, T)
        rows = rows_scr[slot, d]
        sc = _dot(s_scr[d].astype(BF16), ct_ref[:, lanes])
        upd = _dot(xdec_scr[slot, d], b_ref[pl.ds(c0, T), :])
        y_scr = yf_scr if d == 0 else yb_scr
        for r in range(H):
            pr = slice(r * SSD_HEADDIM, (r + 1) * SSD_HEADDIM)
            tr = slice(r * T, (r + 1) * T)
            y_scr[pr, lanes] = _dot(xd_scr[slot, d, pr, :], p_scr[slot, d, tr, :]) + rows[r:r + 1] * sc[pr]
            s_scr[d, pr, :] = rows[H + r:H + r + 1] * s_scr[d, pr, :] + upd[pr]

    def step(c, slot):
        update(c, 0, slot)
        update(nc - 1 - c, 1, slot)
        gates(jnp.minimum(c + 1, nc - 1), 0, 1 - slot)
        gates(jnp.maximum(nc - 2 - c, 0), 1, 1 - slot)

    def body(i, carry):
        step(2 * i, 0)
        step(2 * i + 1, 1)
        return carry

    gates(0, 0, 0)
    gates(nc - 1, 1, 0)
    lax.fori_loop(0, nc // 2, body, 0)

    dsk = dsk_ref[0]

    def epilogue(i, carry):
        c0 = pl.multiple_of(i * T, T)
        lanes = pl.ds(c0, T)
        yt = yf_scr[:, lanes] + yb_scr[:, lanes] + dsk * xt_ref[:, lanes]
        y_ref[pl.ds(c0, T), :] = yt.T.astype(y_ref.dtype)
        return carry

    lax.fori_loop(0, nc, epilogue, 0)
    sout_ref[0, :, 0] = s_scr[...]


def _ssd(xt, bm, ct, dtt, dt_bias, a_log, dskip, s0, B, L):
    assert L % (2 * SCAN_T) == 0 and SSD_STATE == SCAN_T
    nc = L // SCAN_T
    G = SSD_GROUPS
    nd = 2 * SSD_HPG
    M = B * L
    rep = lambda p, rows: jnp.broadcast_to(p.reshape(G, rows, 1), (G, rows, SCAN_T))
    small = lambda rows: pl.BlockSpec((1, rows, SCAN_T), lambda b, g: (g, 0, 0))
    st_spec = pl.BlockSpec((1, 2, 1, SSD_GW, SSD_STATE), lambda b, g: (b, 0, g, 0, 0))
    return pl.pallas_call(
        functools.partial(_ssd_kernel, nc=nc),
        grid=(B, G),
        in_specs=[pl.BlockSpec((SSD_GW, L), lambda b, g: (g, b)),
                  pl.BlockSpec((L, SSD_STATE), lambda b, g: (b, g)),
                  pl.BlockSpec((SSD_STATE, L), lambda b, g: (g, b)),
                  pl.BlockSpec((nd, L), lambda b, g: (g, b)),
                  small(nd), small(nd), small(SSD_GW), st_spec],
        out_specs=[pl.BlockSpec((L, SSD_GW), lambda b, g: (b, g)), st_spec],
        out_shape=[jax.ShapeDtypeStruct((M, SSD_INNER), BF16),
                   jax.ShapeDtypeStruct((B, 2, G, SSD_GW, SSD_STATE), F32)],
        scratch_shapes=[pltpu.VMEM((SSD_GW, L), F32), pltpu.VMEM((SSD_GW, L), F32),
                        pltpu.VMEM((2, SSD_GW, SSD_STATE), F32),
                        pltpu.VMEM((2, SSD_HPG * SCAN_T, 2 * SCAN_T), BF16),
                        pltpu.VMEM((2, SCAN_T, SCAN_T), F32),
                        pltpu.VMEM((2, 2, SSD_HPG * SCAN_T, SCAN_T), BF16),
                        pltpu.VMEM((2, 2, SSD_GW, SCAN_T), BF16),
                        pltpu.VMEM((2, 2, SSD_GW, SCAN_T), BF16),
                        pltpu.VMEM((2, 2, 2 * SSD_HPG, SCAN_T), F32)],
        compiler_params=_cparams("arbitrary", "arbitrary"),
        name="ssd",
    )(xt, bm, ct, dtt, rep(dt_bias, nd), rep(a_log, nd), rep(dskip, SSD_GW), s0)


def _hy_factors(L):
    return (128, 64) if L == 4096 else (32, (2 * L) // 32)


def _hilo(t):
    return _split2(t.astype(F32))


def _cblock(cr, ci):
    return jnp.concatenate([jnp.concatenate([cr, -ci], axis=-1), jnp.concatenate([ci, cr], axis=-1)], axis=-2)


def _dft_tables(L):
    n1c, n2c = _hy_factors(L)
    N = n1c * n2c
    h1 = n1c // 2
    i32 = jnp.int32
    k1 = jnp.arange(n1c, dtype=i32)
    n2 = jnp.arange(n2c, dtype=i32)

    def stage1(n1_count):
        n1 = jnp.arange(n1_count, dtype=i32)
        idx = (k1[None, :, None] * (n2[:, None, None] + n2c * n1[None, None, :])) % N
        ang = idx.astype(F32) * (2.0 * math.pi / N)
        return jnp.cos(ang), jnp.sin(ang)

    c, s = stage1(h1)
    g_fwd = _cblock(c, -s)
    g_inv = _cblock(jnp.swapaxes(c, 1, 2), jnp.swapaxes(s, 1, 2)) / n1c
    cf, sf = stage1(n1c)
    g_real = jnp.concatenate([cf, -sf], axis=1)
    idx2 = (n2[:, None] * n2[None, :]) % n2c
    ang2 = idx2.astype(F32) * (2.0 * math.pi / n2c)
    c2, s2 = jnp.cos(ang2), jnp.sin(ang2)
    f2 = _cblock(c2, -s2)
    f2_inv = _cblock(c2, s2) / n2c
    return dict(g_fwd=g_fwd.astype(BF16), g_inv=g_inv.astype(BF16), f2=f2.astype(BF16),
                f2_inv=f2_inv.astype(BF16), g_real=_hilo(g_real), f2_exact=_hilo(f2))


def _swap01(x):
    return jnp.swapaxes(x, 0, 1)


def _hy_s1_kernel(x_ref, *rest, nb, exact):
    if exact:
        gh_ref, gl_ref, o_ref = rest
    else:
        g_ref, o_ref = rest
    pb, rows, _, C = x_ref.shape
    xs = _swap01(x_ref[...].reshape(pb * rows, nb, C))
    if exact:
        res = [_dot3(gh_ref[j], gl_ref[j], xs[j]) for j in range(nb)]
    else:
        res = [_dot(g_ref[j], xs[j].astype(BF16)) for j in range(nb)]
    out = _swap01(jnp.stack(res, axis=0))
    o_ref[...] = out.reshape(o_ref.shape).astype(o_ref.dtype)


def _hy_stage1(x, tab, out_dtype):
    exact = isinstance(tab, tuple)
    tabs = tab if exact else (tab,)
    n2c, rows_out, R = tabs[0].shape
    Bx, rows, _, C = x.shape
    pb = R // rows
    nb = HY_NB
    cb = min(C, HY_W)
    tspec = pl.BlockSpec((nb, rows_out, R), lambda p, t, c: (t, 0, 0))
    return pl.pallas_call(
        functools.partial(_hy_s1_kernel, nb=nb, exact=exact),
        grid=(Bx // pb, n2c // nb, C // cb),
        in_specs=[pl.BlockSpec((pb, rows, nb, cb), lambda p, t, c: (p, 0, t, c))] + [tspec] * len(tabs),
        out_specs=pl.BlockSpec((None, 2, rows_out // 2, nb, cb), lambda p, t, c: (p, 0, 0, t, c)),
        out_shape=jax.ShapeDtypeStruct((Bx // pb, 2, rows_out // 2, n2c, C), out_dtype),
        compiler_params=_cparams("arbitrary", "arbitrary", "arbitrary"),
        name="hyena_stage1",
    )(x, *tabs)


def _hy_spec_kernel(a_ref, inv_ref, fh_ref, fl_ref, o_ref, *, kb):
    n2c = a_ref.shape[2]
    for kk in range(kb):
        rhs = jnp.concatenate([a_ref[0, kk], a_ref[1, kk]], axis=0)
        X = _dot3(fh_ref[...], fl_ref[...], rhs) * inv_ref[...]
        o_ref[0, kk] = X[0:n2c]
        o_ref[1, kk] = X[n2c:]


def _hy_filter_spectrum(a, inv_norm, f2, kb):
    _, n1c, n2c, W = a.shape
    fh, fl = f2
    blk = pl.BlockSpec((2, kb, n2c, W), lambda t: (0, t, 0, 0))
    fspec = pl.BlockSpec((2 * n2c, 2 * n2c), lambda t: (0, 0))
    return pl.pallas_call(
        functools.partial(_hy_spec_kernel, kb=kb),
        grid=(n1c // kb,),
        in_specs=[blk, pl.BlockSpec((1, W), lambda t: (0, 0)), fspec, fspec],
        out_specs=blk,
        out_shape=jax.ShapeDtypeStruct(a.shape, F32),
        compiler_params=_cparams("arbitrary"),
        name="hyena_filter_spectrum",
    )(a, inv_norm, fh, fl)


def _hy_mid_kernel(a_ref, k_ref, f_ref, i_ref, o_ref, *, kb):
    n2c = a_ref.shape[2]
    res = []
    for kk in range(kb):
        rhs = jnp.concatenate([a_ref[0, kk], a_ref[1, kk]], axis=0)
        X = _dot(f_ref[...], rhs)
        xr, xi = X[0:n2c], X[n2c:]
        kr, ki = k_ref[0, kk], k_ref[1, kk]
        Y = jnp.concatenate([xr * kr - xi * ki, xr * ki + xi * kr], axis=0)
        res.append(_dot(i_ref[...], Y.astype(BF16)))
    o_ref[...] = _swap01(jnp.stack(res, axis=0)).reshape(o_ref.shape).astype(o_ref.dtype)


def _hy_mid(a, kspec, order, f2, f2_inv):
    P, _, n1c, n2c, C = a.shape
    kb = HY_NB
    fspec = pl.BlockSpec((2 * n2c, 2 * n2c), lambda t, p: (0, 0))
    return pl.pallas_call(
        functools.partial(_hy_mid_kernel, kb=kb),
        grid=(n1c // kb, P),
        in_specs=[pl.BlockSpec((None, 2, kb, n2c, C), lambda t, p: (p, 0, t, 0, 0)),
                  pl.BlockSpec((2, kb, n2c, C), lambda t, p: (0, t, 0, order)),
                  fspec, fspec],
        out_specs=pl.BlockSpec((None, 2, n2c, kb, C), lambda t, p: (p, 0, 0, t, 0)),
        out_shape=jax.ShapeDtypeStruct((P, 2, n2c, n1c, C), BF16),
        compiler_params=_cparams("arbitrary", "arbitrary"),
        name="hyena_mid",
    )(a, kspec, f2, f2_inv)


def _hy_last_kernel(b_ref, z_ref, gate_ref, bias_ref, gi_ref, *rest, fuse, nb):
    if fuse:
        gf_ref, zo_ref, ao_ref = rest
    else:
        (zo_ref,) = rest
    _, h1, _, C = z_ref.shape
    zs = _swap01(z_ref[...].reshape(2 * h1, nb, C))
    gs = _swap01(gate_ref[...].reshape(2 * h1, nb, C))
    bias = bias_ref[...]
    zn, an = [], []
    for j in range(nb):
        bv = jnp.concatenate([b_ref[0, j], b_ref[1, j]], axis=0)
        y = _dot(gi_ref[j], bv)
        znew = gs[j] * (y + zs[j] * bias)
        zn.append(znew)
        if fuse:
            an.append(_dot(gf_ref[j], znew.astype(BF16)))
    zo_ref[...] = _swap01(jnp.stack(zn, axis=0)).reshape(zo_ref.shape).astype(zo_ref.dtype)
    if fuse:
        ao_ref[...] = _swap01(jnp.stack(an, axis=0)).reshape(ao_ref.shape).astype(ao_ref.dtype)


def _hy_last(bf, z, gate, bias, g_inv, g_fwd, out_dtype):
    P, _, n2c, n1c, C = bf.shape
    B, h1, _, _ = z.shape
    nb = HY_NB
    cb = C // 2
    fuse = g_fwd is not None
    zspec = pl.BlockSpec((2, h1, nb, cb), lambda p, t, c: (p, 0, t, c))
    in_specs = [pl.BlockSpec((None, 2, nb, n1c, cb), lambda p, t, c: (p, 0, t, 0, c)),
                zspec, zspec, pl.BlockSpec((1, cb), lambda p, t, c: (0, c)),
                pl.BlockSpec((nb, 2 * h1, 2 * n1c), lambda p, t, c: (t, 0, 0))]
    args = [bf, z, gate, bias, g_inv]
    out_specs = [zspec]
    out_shape = [jax.ShapeDtypeStruct(z.shape, out_dtype)]
    if fuse:
        in_specs.append(pl.BlockSpec((nb, 2 * n1c, 2 * h1), lambda p, t, c: (t, 0, 0)))
        args.append(g_fwd)
        out_specs.append(pl.BlockSpec((None, 2, n1c, nb, cb), lambda p, t, c: (p, 0, 0, t, c)))
        out_shape.append(jax.ShapeDtypeStruct((P, 2, n1c, n2c, C), BF16))
    return pl.pallas_call(
        functools.partial(_hy_last_kernel, fuse=fuse, nb=nb),
        grid=(P, n2c // nb, C // cb),
        in_specs=in_specs, out_specs=out_specs, out_shape=out_shape,
        compiler_params=_cparams("arbitrary", "arbitrary", "arbitrary"),
        name="hyena_last",
    )(*args)


def _hy_filter_kernel(f_ref, w1_ref, b1_ref, w2_ref, b2_ref, w3_ref, fr_ref, dl_ref, hk_ref, s_ref, *, L):
    i = pl.program_id(0)
    tl = f_ref.shape[0]
    f = f_ref[...]
    fr = fr_ref[...]
    h = jnp.sin(fr * (_dot(f.astype(BF16), w1_ref[...].astype(BF16)) + b1_ref[...]))
    h = jnp.sin(fr * (_dot(h.astype(BF16), w2_ref[...].astype(BF16)) + b2_ref[...]))
    hk = _dot(h.astype(BF16), w3_ref[...].astype(BF16)) * jnp.exp(-f[:, 0:1] * dl_ref[...])
    n = i * tl + lax.broadcasted_iota(jnp.int32, hk.shape, 0)
    hk = jnp.where(n == L, 0.0, hk)
    hk_ref[...] = hk
    tot = jnp.sum(jnp.abs(hk), axis=0, keepdims=True)

    @pl.when(i == 0)
    def _():
        s_ref[...] = tot

    @pl.when(i != 0)
    def _():
        s_ref[...] = s_ref[...] + tot


def _hy_filter(L, w1, b1, w2, b2, w3, freq):
    f32 = F32
    t = jnp.linspace(0.0, 1.0, L, dtype=f32)[:, None]
    ang = (2.0 * math.pi / L) * jnp.arange(L, dtype=f32)[:, None] * \
        jnp.linspace(1e-4, HY_BANDS - 1, HY_BANDS, dtype=f32)[None]
    feats = jnp.concatenate([t, jnp.cos(ang), -jnp.sin(ang)], axis=-1)
    feats = jnp.concatenate([feats, feats[0:1], feats[:0:-1]], axis=0)
    fpad = 128
    feats = jnp.pad(feats, ((0, 0), (0, fpad - HY_EMB)))
    w1p = jnp.pad(w1, ((0, fpad - HY_EMB), (0, 0)))
    deltas = jnp.abs(jnp.linspace(math.log(HY_TARGET) / HY_SLOW, math.log(HY_TARGET) / HY_FAST, HY_W, dtype=f32))
    OC = HY_ORDER * HY_W
    dl = jnp.tile(deltas, HY_ORDER).reshape(1, OC)
    tl = min(L, 512)
    nblk = L // tl
    full = lambda shp: pl.BlockSpec(shp, lambda i: (0, 0))
    return pl.pallas_call(
        functools.partial(_hy_filter_kernel, L=L),
        grid=(2 * nblk,),
        in_specs=[pl.BlockSpec((tl, fpad), lambda i: (i, 0)),
                  full((fpad, HY_HID)), full((1, HY_HID)), full((HY_HID, HY_HID)), full((1, HY_HID)),
                  pl.BlockSpec((HY_HID, OC), lambda i: (0, i // nblk)), full((1, HY_HID)), full((1, OC))],
        out_specs=[pl.BlockSpec((tl, OC), lambda i: (i, 0)), full((1, OC))],
        out_shape=[jax.ShapeDtypeStruct((2 * L, OC), F32), jax.ShapeDtypeStruct((1, OC), F32)],
        compiler_params=_cparams("arbitrary"),
        name="hyena_filter",
    )(feats, w1p, b1.reshape(1, HY_HID), w2, b2.reshape(1, HY_HID), w3, freq.reshape(1, HY_HID), dl)


def _hy_spectrum(L, tabs, w1, b1, w2, b2, w3, freq):
    n1c, n2c = _hy_factors(L)
    OC = HY_ORDER * HY_W
    ktime, s = _hy_filter(L, w1, b1, w2, b2, w3, freq)
    a = _hy_stage1(ktime.reshape(1, n1c, n2c, OC), tabs["g_real"], F32)[0]
    return _hy_filter_spectrum(a, 1.0 / s, tabs["f2_exact"], 8)


def _hyena(v, x1, x2, kspec, bias, tabs, B, L):
    n1c, n2c = _hy_factors(L)
    C = HY_W
    view = lambda t: t.reshape(B, n1c // 2, n2c, C)
    z0, g1, g2 = view(v), view(x1), view(x2)
    a = _hy_stage1(z0, tabs["g_fwd"], BF16)
    bf = _hy_mid(a, kspec, 0, tabs["f2"], tabs["f2_inv"])
    z1, a = _hy_last(bf, z0, g1, bias[0:1], tabs["g_inv"], tabs["g_fwd"], F32)
    bf = _hy_mid(a, kspec, 1, tabs["f2"], tabs["f2_inv"])
    (z2,) = _hy_last(bf, z1, g2, bias[1:2], tabs["g_inv"], None, BF16)
    return z2.reshape(B * L, C)


_EVEN_PLAN = (
    (0, HY_W, "conv", 1.0, False, F32),
    (HY_W, HY_W, "conv", 1.0, False, F32),
    (2 * HY_W, HY_W, "conv", 1.0, False, F32),
    (3 * HY_W, M_W, "conv_silu", 1.0, True, BF16),
    (3 * HY_W + M_W, M_W, "conv_silu", M_DH ** -0.5, False, BF16),
    (EV_CONV + M_W, M_W, "raw", 1.0, False, F32),
)
_EVEN_NT_PLAN = ((0, M_W, BF16), (M_W, 4 * M_HEADS, F32))
_ODD_NT_PLAN = ((0, 2 * SSD_HEADS, F32),)
_GN = SSD_GROUPS * SSD_STATE
_ODD_PLAN = (
    (0, SSD_INNER, "raw", 1.0, False, BF16),
    (SSD_INNER, SSD_INNER, "conv_silu", 1.0, True, F32),
    (2 * SSD_INNER, _GN, "conv_silu", 1.0, False, BF16),
    (2 * SSD_INNER + _GN, _GN, "conv_silu", 1.0, True, BF16),
)


def _even_layer(x, mods_l, P, e, l, B, L, st, per_batch_cond, kspec, tabs):
    hv, hx1, hx2, qt, k, o, vt, gt = _inproj(
        x, mods_l, P["norm1_g"][l:l + 1], P["ev_in_w"], P["ev_in_wt"], P["ev_conv_w"], P["ev_conv_b"], e,
        plan=_EVEN_PLAN, nt_plan=_EVEN_NT_PLAN, conv0=0, seq_len=L, per_batch_cond=per_batch_cond)
    y_hy = _hyena(hv, hx1, hx2, kspec, P["hy_bias"][e], tabs, B, L)
    c0, m0 = st
    y_m, c_new, m_new = _mlstm(qt, k, vt, o, gt, P["m_gate_b"][e], P["m_norm_g"][e], c0, m0, B, L)
    x = _layer_tail(_even_tail_kernel, [y_hy, y_m], [_row_spec(HY_W), _row_spec(M_W)], x, mods_l, P["ev_out_w"], e,
                    P, l, seq_len=L, per_batch_cond=per_batch_cond, final=False)
    return x, (c_new, m_new)


def _by_group(p):
    return p.reshape(2, SSD_GROUPS, SSD_HPG).transpose(1, 0, 2).reshape(SSD_GROUPS, 2 * SSD_HPG)


def _odd_layer(x, mods_l, P, o, l, B, L, s0, per_batch_cond):
    z, xt, bm, ct, dtt = _inproj(
        x, mods_l, P["norm1_g"][l:l + 1], P["od_in_w"], P["od_in_wt"], P["od_conv_w"], P["od_conv_b"], o,
        plan=_ODD_PLAN, nt_plan=_ODD_NT_PLAN, conv0=SSD_INNER, seq_len=L, per_batch_cond=per_batch_cond)
    dskip = jnp.repeat(P["ssd_D"][o], SSD_HEADDIM).reshape(SSD_GROUPS, SSD_GW)
    y, s_new = _ssd(xt, bm, ct, dtt, _by_group(P["ssd_dt_bias"][o]), _by_group(P["ssd_A_log"][o]), dskip, s0, B, L)
    ng_spec = pl.BlockSpec((1, 1, SSD_INNER), lambda i: (o, 0, 0))
    x = _layer_tail(_odd_tail_kernel, [y, z, P["ssd_norm_g"]], [_row_spec(SSD_INNER), _row_spec(SSD_INNER), ng_spec],
                    x, mods_l, P["od_out_w"], o, P, l, seq_len=L, per_batch_cond=per_batch_cond,
                    final=(l == DEPTH - 1))
    return x, s_new


def _to_colmajor(x, B, L):
    rows = L // GRID_W
    return x.reshape(B, rows, GRID_W, D_MODEL).swapaxes(1, 2).reshape(B * L, D_MODEL)


def _from_colmajor(x, B, L):
    rows = L // GRID_W
    return x.reshape(B, GRID_W, rows, D_MODEL).swapaxes(1, 2).reshape(B * L, D_MODEL)


def _trunk(x, mods, P, B, L, init_states, per_batch_cond, grid, kspecs, tabs):
    states = []
    for l in range(DEPTH):
        if grid and l == 2:
            x = _to_colmajor(x, B, L)
        mods_l = mods[l]
        if l % 2 == 0:
            x, st = _even_layer(x, mods_l, P, l // 2, l, B, L, init_states[l], per_batch_cond, kspecs[l // 2], tabs)
        else:
            x, st = _odd_layer(x, mods_l, P, l // 2, l, B, L, init_states[l], per_batch_cond)
        states.append(st)
    if grid:
        x = _from_colmajor(x, B, L)
    return x, states


def _pack_mlstm_state(C, n, m):
    B = C.shape[0]
    ext = jnp.concatenate([jnp.swapaxes(C, -1, -2), n[..., None, :],
                           jnp.zeros(C.shape[:-2] + (M_EXT - M_DH - 1, M_DH), F32)], axis=-2)
    m_rep = jnp.broadcast_to(m.reshape(B, 2 * M_HEADS, 1, 1), (B, 2 * M_HEADS, 1, M_DH))
    return ext.reshape(B, 2 * M_HEADS, M_EXT, M_DH), m_rep


def kernel(x_prompt, x_sample, state_mlstm_C, state_mlstm_n, state_mlstm_m, state_ssd, c, c_ctx,
           mod_w, mod_b, norm1_g, norm2_g, ffn_w1, ffn_w3, ffn_w2, final_g,
           ev_in_w, ev_conv_w, ev_conv_b, hy_w1, hy_b1, hy_w2, hy_b2, hy_w3, hy_freq, hy_bias,
           m_gate_b, m_norm_g, ev_out_w,
           od_in_w, od_conv_w, od_conv_b, ssd_dt_bias, ssd_A_log, ssd_D, ssd_norm_g, od_out_w):
    Bp, Lp, _ = x_prompt.shape
    Bs, Ls, _ = x_sample.shape
    bf = lambda w: w.astype(BF16)
    n_odd = od_in_w.shape[0]
    od_dt_w = od_in_w[:, :, OD_PROJ - 2 * SSD_HEADS:].reshape(n_odd, D_MODEL, 2, SSD_GROUPS, SSD_HPG)
    od_dt_w = od_dt_w.transpose(0, 3, 2, 4, 1).reshape(n_odd, 2 * SSD_HEADS, D_MODEL)
    P = dict(norm1_g=norm1_g, norm2_g=norm2_g, ffn_w1=bf(ffn_w1), ffn_w3=bf(ffn_w3), ffn_w2=bf(ffn_w2),
             final_g=final_g, ev_in_w=bf(ev_in_w),
             ev_in_wt=bf(jnp.concatenate([ev_in_w[:, :, EV_CONV:EV_CONV + M_W],
                                          ev_in_w[:, :, EV_PROJ - 4 * M_HEADS:]], axis=2)).swapaxes(1, 2),
             ev_conv_w=ev_conv_w, ev_conv_b=ev_conv_b, hy_bias=hy_bias, m_gate_b=m_gate_b, m_norm_g=m_norm_g,
             ev_out_w=bf(ev_out_w), od_in_w=bf(od_in_w), od_in_wt=bf(od_dt_w),
             od_conv_w=od_conv_w, od_conv_b=od_conv_b, ssd_dt_bias=ssd_dt_bias, ssd_A_log=ssd_A_log,
             ssd_D=ssd_D, ssd_norm_g=ssd_norm_g.reshape(-1, 1, SSD_INNER), od_out_w=bf(od_out_w))

    ncond = 16
    cond = jnp.concatenate([c_ctx[None, :], c, jnp.zeros((ncond - 1 - Bs, D_MODEL), F32)], axis=0)
    mods = _modulation(cond, mod_w, mod_b).reshape(DEPTH, ncond, 6, D_MODEL)

    tabs = {L: _dft_tables(L) for L in (Lp, Ls)}
    kspecs = {L: [_hy_spectrum(L, tabs[L], hy_w1[e], hy_b1[e], hy_w2[e], hy_b2[e], hy_w3[e], hy_freq[e])
                  for e in range((DEPTH + 1) // 2)] for L in (Lp, Ls)}

    G = SSD_GROUPS
    zero_m = _pack_mlstm_state(jnp.zeros((Bp, 2, M_HEADS, M_DH, M_DH), F32), jnp.zeros((Bp, 2, M_HEADS, M_DH), F32),
                               jnp.zeros((Bp, 2, M_HEADS), F32))
    zero_s = jnp.zeros((Bp, 2, G, SSD_GW, SSD_STATE), F32)
    ctx_init = [zero_m if l % 2 == 0 else zero_s for l in range(DEPTH)]
    y_prompt, ctx_states = _trunk(x_prompt.reshape(Bp * Lp, D_MODEL), mods, P, Bp, Lp, ctx_init, False, False,
                                  kspecs[Lp], tabs[Lp])

    lat_init = []
    for l in range(DEPTH):
        if l % 2 == 0:
            e = l // 2
            lat_init.append(_pack_mlstm_state(state_mlstm_C[:, e], state_mlstm_n[:, e], state_mlstm_m[:, e]))
        else:
            lat_init.append(state_ssd[:, l // 2].reshape(Bs, 2, G, SSD_GW, SSD_STATE))
    y_sample, _ = _trunk(x_sample.reshape(Bs * Ls, D_MODEL), mods, P, Bs, Ls, lat_init, True, True,
                         kspecs[Ls], tabs[Ls])

    cs = [ctx_states[l][0].reshape(Bp, 2, M_HEADS, M_EXT, M_DH) for l in range(0, DEPTH, 2)]
    ms = [ctx_states[l][1][:, :, 0, 0].reshape(Bp, 2, M_HEADS) for l in range(0, DEPTH, 2)]
    new_C = jnp.stack([jnp.swapaxes(t[..., :M_DH, :], -1, -2) for t in cs], axis=1)
    new_n = jnp.stack([t[..., M_DH, :] for t in cs], axis=1)
    new_m = jnp.stack(ms, axis=1)
    new_ssd = jnp.stack([ctx_states[l].reshape(Bp, 2, SSD_HEADS, SSD_HEADDIM, SSD_STATE)
                         for l in range(1, DEPTH, 2)], axis=1)
    return (y_prompt.reshape(Bp, Lp, D_MODEL), y_sample.reshape(Bs, Ls, D_MODEL), new_C, new_n, new_m, new_ssd)
```

```python
import functools
import math

import jax
import jax.numpy as jnp
from jax import lax
from jax.experimental import pallas as pl
from jax.experimental.pallas import tpu as pltpu

F32 = jnp.float32
BF16 = jnp.bfloat16

D_MODEL = 1024
DEPTH = 4
GRID_W = 64
EPS = 1e-6
HY_W = D_MODEL // 2
HY_ORDER = 2
HY_EMB = 33
HY_BANDS = (HY_EMB - 1) // 2
HY_HID = 64
HY_TARGET = 1e-2
HY_FAST = 0.3
HY_SLOW = 1.5
M_HEADS = 4
M_DH = 128
M_W = M_HEADS * M_DH
EV_CONV = 3 * HY_W + 2 * M_W
EV_PROJ = EV_CONV + 2 * M_W + 4 * M_HEADS
SSD_INNER = 2 * D_MODEL
SSD_HEADDIM = 64
SSD_HEADS = SSD_INNER // SSD_HEADDIM
SSD_GROUPS = 8
SSD_HPG = SSD_HEADS // SSD_GROUPS
SSD_STATE = 128
SSD_GW = SSD_HPG * SSD_HEADDIM
SSD_CONV_DIM = SSD_INNER + 2 * SSD_GROUPS * SSD_STATE
OD_PROJ = SSD_INNER + SSD_CONV_DIM + 2 * SSD_HEADS
FFN_HIDDEN = -(-8 * D_MODEL // (3 * 256)) * 256

V7X_VMEM_BYTES = 64 * 1024 * 1024
VMEM_LIMIT = V7X_VMEM_BYTES * 3 // 4
ROW_TILE = 256
WIDE_TILE = 512
HALO = 16
COL_CHUNK = 512
HY_NB = 16
SCAN_T = 128


def _cparams(*sem):
    return pltpu.CompilerParams(dimension_semantics=sem, vmem_limit_bytes=VMEM_LIMIT)


def _dot(a, b):
    return jnp.dot(a, b, preferred_element_type=F32)


def _dot_nt(a, b):
    return lax.dot_general(a, b, (((1,), (1,)), ((), ())), preferred_element_type=F32)


def _split2(x):
    h = x.astype(BF16)
    return h, (x - h.astype(F32)).astype(BF16)


def _dot3(th, tl, x):
    xh, xl = _split2(x)
    return _dot(th, xh) + _dot(th, xl) + _dot(tl, xh)


def _dot2(x, t):
    xh, xl = _split2(x)
    return _dot(xh, t) + _dot(xl, t)


def _sigmoid(x):
    return 1.0 / (1.0 + jnp.exp(-x))


def _silu(x):
    return x * _sigmoid(x)


def _log_sigmoid(x):
    return jnp.minimum(x, 0.0) - jnp.log(1.0 + jnp.exp(-jnp.abs(x)))


def _softplus(x):
    return jnp.maximum(x, 0.0) + jnp.log(1.0 + jnp.exp(-jnp.abs(x)))


def _rms(x):
    return x * lax.rsqrt(jnp.mean(x * x, axis=-1, keepdims=True) + EPS)


def _mod_kernel(c_ref, w_ref, b_ref, o_ref):
    a = _silu(c_ref[...]).astype(BF16)
    o_ref[0] = _dot(a, w_ref[0].astype(BF16)) + b_ref[0]


def _modulation(cond, mod_w, mod_b):
    nc = cond.shape[0]
    tn = 1536
    return pl.pallas_call(
        _mod_kernel,
        grid=(DEPTH, 6 * D_MODEL // tn),
        in_specs=[pl.BlockSpec((nc, D_MODEL), lambda l, j: (0, 0)),
                  pl.BlockSpec((1, D_MODEL, tn), lambda l, j: (l, 0, j)),
                  pl.BlockSpec((1, 1, tn), lambda l, j: (l, 0, j))],
        out_specs=pl.BlockSpec((1, nc, tn), lambda l, j: (l, 0, j)),
        out_shape=jax.ShapeDtypeStruct((DEPTH, nc, 6 * D_MODEL), F32),
        compiler_params=_cparams("arbitrary", "arbitrary"),
        name="modulation",
    )(cond, mod_w, mod_b.reshape(DEPTH, 1, 6 * D_MODEL))


def _inproj_kernel(x_ref, xp_ref, xn_ref, mod_ref, g_ref, w_ref, wt_ref, cw_ref, cb_ref, *rest,
                   plan, nt_plan, tiles_per_seq, conv0):
    nseg = len(plan)
    outs = rest[:nseg]
    nt_outs = rest[nseg:nseg + len(nt_plan)]
    hbuf = rest[nseg + len(nt_plan)]
    tm = x_ref.shape[0]
    i = pl.program_id(0)
    j = i % tiles_per_seq
    pmask = (j != 0).astype(F32)
    nmask = (j != tiles_per_seq - 1).astype(F32)
    mod = mod_ref[0]
    sh, sc = mod[0:1], mod[1:2]
    g = g_ref[...]

    def normmod(xv):
        return (_rms(xv) * g * (1.0 + sc) + sh).astype(BF16)

    hbuf[0:tm] = normmod(x_ref[...])
    hbuf[tm:tm + HALO] = normmod(xp_ref[...])
    hbuf[tm + HALO:tm + 2 * HALO] = normmod(xn_ref[...])
    rowid = lax.broadcasted_iota(jnp.int32, (tm, COL_CHUNK), 0)

    for (c0, width, kind, scale, transposed), o_ref in zip(plan, outs):
        for cc in range(c0, c0 + width, COL_CHUNK):
            w = w_ref[0, :, cc:cc + COL_CHUNK]
            if kind == "raw":
                y = _dot(hbuf[0:tm], w)
            else:
                full = _dot(hbuf[...], w)
                acc = full[0:tm]
                ap = full[tm + HALO - 1:tm + HALO] * pmask
                an = full[tm + HALO:tm + HALO + 1] * nmask
                prev = jnp.where(rowid == 0, ap, pltpu.roll(acc, 1, 0))
                nxt = jnp.where(rowid == tm - 1, an, pltpu.roll(acc, tm - 1, 0))
                cw = cw_ref[0, :, cc - conv0:cc - conv0 + COL_CHUNK]
                cb = cb_ref[0, :, cc - conv0:cc - conv0 + COL_CHUNK]
                y = cb + cw[0:1] * prev + cw[1:2] * acc + cw[2:3] * nxt
                if kind == "conv_silu":
                    y = _silu(y)
                if scale != 1.0:
                    y = y * scale
            if transposed:
                o_ref[cc - c0:cc - c0 + COL_CHUNK, :] = y.T.astype(o_ref.dtype)
            else:
                o_ref[:, cc - c0:cc - c0 + COL_CHUNK] = y.astype(o_ref.dtype)

    for (r0, rows), o_ref in zip(nt_plan, nt_outs):
        o_ref[...] = _dot_nt(wt_ref[0, r0:r0 + rows, :], hbuf[0:tm]).astype(o_ref.dtype)


def _inproj(x, mods_l, g_l, w_all, wt_all, cw_all, cb_all, layer_idx, *, plan, nt_plan, conv0, seq_len,
            per_batch_cond):
    M = x.shape[0]
    tm = ROW_TILE
    tps = seq_len // tm
    nt = M // tm
    hb = tm // HALO
    ntot = w_all.shape[2]
    tw = wt_all.shape[1]
    cdim = cw_all.shape[2]
    e = layer_idx
    cond_map = (lambda i: (1 + i // tps, 0, 0)) if per_batch_cond else (lambda i: (0, 0, 0))
    in_specs = [
        pl.BlockSpec((tm, D_MODEL), lambda i: (i, 0)),
        pl.BlockSpec((HALO, D_MODEL), lambda i: (jnp.maximum(i * hb - 1, 0), 0)),
        pl.BlockSpec((HALO, D_MODEL), lambda i: (jnp.minimum((i + 1) * hb, nt * hb - 1), 0)),
        pl.BlockSpec((1, 6, D_MODEL), cond_map),
        pl.BlockSpec((1, D_MODEL), lambda i: (0, 0)),
        pl.BlockSpec((1, D_MODEL, ntot), lambda i: (e, 0, 0), pipeline_mode=pl.Buffered(1)),
        pl.BlockSpec((1, tw, D_MODEL), lambda i: (e, 0, 0)),
        pl.BlockSpec((1, 3, cdim), lambda i: (e, 0, 0)),
        pl.BlockSpec((1, 1, cdim), lambda i: (e, 0, 0)),
    ]
    out_specs, out_shape = [], []
    for (c0, width, kind, scale, transposed, dt) in plan:
        if transposed:
            out_specs.append(pl.BlockSpec((width, tm), lambda i: (0, i)))
            out_shape.append(jax.ShapeDtypeStruct((width, M), dt))
        else:
            out_specs.append(pl.BlockSpec((tm, width), lambda i: (i, 0)))
            out_shape.append(jax.ShapeDtypeStruct((M, width), dt))
    for (r0, rows, dt) in nt_plan:
        out_specs.append(pl.BlockSpec((rows, tm), lambda i: (0, i)))
        out_shape.append(jax.ShapeDtypeStruct((rows, M), dt))
    kplan = tuple(p[:5] for p in plan)
    knt = tuple(p[:2] for p in nt_plan)
    return pl.pallas_call(
        functools.partial(_inproj_kernel, plan=kplan, nt_plan=knt, tiles_per_seq=tps, conv0=conv0),
        grid=(nt,),
        in_specs=in_specs,
        out_specs=out_specs,
        out_shape=out_shape,
        scratch_shapes=[pltpu.VMEM((tm + 2 * HALO, D_MODEL), BF16)],
        compiler_params=_cparams("arbitrary"),
        name="inproj",
    )(x, x, x, mods_l, g_l, w_all, wt_all, cw_all, cb_all.reshape(cb_all.shape[0], 1, cdim))


def _outproj_even_kernel(yh_ref, ym_ref, x_ref, mod_ref, w_ref, o_ref):
    out = _dot(yh_ref[...], w_ref[0, 0:HY_W]) + _dot(ym_ref[...], w_ref[0, HY_W:])
    o_ref[...] = x_ref[...] + mod_ref[0][2:3] * out


def _outproj_odd_kernel(y_ref, z_ref, x_ref, mod_ref, ng_ref, w_ref, o_ref):
    u = y_ref[...] * _silu(z_ref[...])
    un = (_rms(u) * ng_ref[0]).astype(BF16)
    o_ref[...] = x_ref[...] + mod_ref[0][2:3] * _dot(un, w_ref[0])


def _cond_map(seq_len, per_batch_cond):
    tps = seq_len // WIDE_TILE
    return (lambda i: (1 + i // tps, 0, 0)) if per_batch_cond else (lambda i: (0, 0, 0))


def _outproj_even(yh, ym, x, mods_l, w_all, e, *, seq_len, per_batch_cond):
    M = x.shape[0]
    tm = WIDE_TILE
    return pl.pallas_call(
        _outproj_even_kernel,
        grid=(M // tm,),
        in_specs=[pl.BlockSpec((tm, HY_W), lambda i: (i, 0)),
                  pl.BlockSpec((tm, M_W), lambda i: (i, 0)),
                  pl.BlockSpec((tm, D_MODEL), lambda i: (i, 0)),
                  pl.BlockSpec((1, 6, D_MODEL), _cond_map(seq_len, per_batch_cond)),
                  pl.BlockSpec((1, HY_W + M_W, D_MODEL), lambda i: (e, 0, 0), pipeline_mode=pl.Buffered(1))],
        out_specs=pl.BlockSpec((tm, D_MODEL), lambda i: (i, 0)),
        out_shape=jax.ShapeDtypeStruct((M, D_MODEL), F32),
        compiler_params=_cparams("arbitrary"),
        name="outproj_even",
    )(yh, ym, x, mods_l, w_all)


def _outproj_odd(y, z, x, mods_l, ng_all, w_all, o, *, seq_len, per_batch_cond):
    M = x.shape[0]
    tm = WIDE_TILE
    return pl.pallas_call(
        _outproj_odd_kernel,
        grid=(M // tm,),
        in_specs=[pl.BlockSpec((tm, SSD_INNER), lambda i: (i, 0)),
                  pl.BlockSpec((tm, SSD_INNER), lambda i: (i, 0)),
                  pl.BlockSpec((tm, D_MODEL), lambda i: (i, 0)),
                  pl.BlockSpec((1, 6, D_MODEL), _cond_map(seq_len, per_batch_cond)),
                  pl.BlockSpec((1, 1, SSD_INNER), lambda i: (o, 0, 0)),
                  pl.BlockSpec((1, SSD_INNER, D_MODEL), lambda i: (o, 0, 0), pipeline_mode=pl.Buffered(1))],
        out_specs=pl.BlockSpec((tm, D_MODEL), lambda i: (i, 0)),
        out_shape=jax.ShapeDtypeStruct((M, D_MODEL), F32),
        compiler_params=_cparams("arbitrary"),
        name="outproj_odd",
    )(y, z, x, mods_l, ng_all, w_all)


FFN_CHUNK = FFN_HIDDEN // 2


def _ffn_kernel(x_ref, mod_ref, g_ref, w1_ref, w3_ref, w2_ref, fg_ref, o_ref, *, final):
    x = x_ref[...]
    mod = mod_ref[0]
    h = (_rms(x) * g_ref[0] * (1.0 + mod[4:5]) + mod[3:4]).astype(BF16)
    ff = jnp.zeros(x.shape, F32)
    for c0 in range(0, FFN_HIDDEN, FFN_CHUNK):
        a = _dot(h, w1_ref[0, :, c0:c0 + FFN_CHUNK])
        b = _dot(h, w3_ref[0, :, c0:c0 + FFN_CHUNK])
        u = (_silu(a) * b).astype(BF16)
        ff = ff + _dot(u, w2_ref[0, c0:c0 + FFN_CHUNK, :])
    xo = x + mod[5:6] * ff
    if final:
        xo = _rms(xo) * fg_ref[...]
    o_ref[...] = xo


def _ffn(x, mods_l, g_all, w1, w3, w2, fg, l, *, seq_len, per_batch_cond, final):
    M = x.shape[0]
    tm = WIDE_TILE
    wspec = lambda shape: pl.BlockSpec(shape, lambda i: (l, 0, 0), pipeline_mode=pl.Buffered(1))
    return pl.pallas_call(
        functools.partial(_ffn_kernel, final=final),
        grid=(M // tm,),
        in_specs=[pl.BlockSpec((tm, D_MODEL), lambda i: (i, 0)),
                  pl.BlockSpec((1, 6, D_MODEL), _cond_map(seq_len, per_batch_cond)),
                  pl.BlockSpec((1, 1, D_MODEL), lambda i: (l, 0, 0)),
                  wspec((1, D_MODEL, FFN_HIDDEN)), wspec((1, D_MODEL, FFN_HIDDEN)),
                  wspec((1, FFN_HIDDEN, D_MODEL)),
                  pl.BlockSpec((1, D_MODEL), lambda i: (0, 0))],
        out_specs=pl.BlockSpec((tm, D_MODEL), lambda i: (i, 0)),
        out_shape=jax.ShapeDtypeStruct((M, D_MODEL), F32),
        compiler_params=_cparams("arbitrary"),
        name="ffn",
    )(x, mods_l, g_all, w1, w3, w2, fg)


def _scan_consts(d):
    T = SCAN_T
    row = lax.broadcasted_iota(jnp.int32, (T, T), 0)
    col = lax.broadcasted_iota(jnp.int32, (T, T), 1)
    mask, last = (row <= col, T - 1) if d == 0 else (row >= col, 0)
    v_mat = jnp.where(mask, 1.0, 0.0).astype(BF16)
    return mask, v_mat, last


def _mlstm_kernel(qt_ref, k_ref, vt_ref, o_ref, gt_ref, gb_ref, ng_ref, c0_ref, m0_ref,
                  y_ref, cst_ref, mst_ref, h_scr, c_scr, m_scr, sb_scr, vw_scr, rows_scr, *, nc):
    T = SCAN_T
    H = M_HEADS
    c_scr[...] = c0_ref[0]
    m_scr[...] = m0_ref[0]
    ones_blk = jnp.where(lax.broadcasted_iota(jnp.int32, (M_DH, T), 0) == 0, 1.0, 0.0)
    consts = (_scan_consts(0), _scan_consts(1))
    gb = gb_ref[...]

    def gates(ci, d, slot):
        mask, v_mat, last = consts[d]
        c0 = pl.multiple_of(ci * T, T)
        lanes = pl.ds(c0, T)
        graw = gt_ref[:, lanes] + gb
        lf = _log_sigmoid(graw)
        b_all = _dot2(lf, v_mat)
        for h in range(H):
            fi = (2 * d + 1) * H + h
            ii = (2 * d) * H + h
            hl = slice(h * M_DH, (h + 1) * M_DH)
            tr = slice(h * T, (h + 1) * T)
            b_row, i_row = b_all[fi:fi + 1], graw[ii:ii + 1]
            dlog = jnp.where(mask, jnp.broadcast_to(b_row, (T, T)) - jnp.broadcast_to(b_row - i_row, (T, T)).T,
                             -jnp.inf)
            mloc = jnp.max(dlog, axis=0, keepdims=True)
            sb_scr[slot, d, tr, :] = (_dot(k_ref[pl.ds(c0, T), hl], qt_ref[hl, lanes])
                                      * jnp.exp(dlog - mloc)).astype(BF16)
            mloc_last = jnp.broadcast_to(mloc[:, last:last + 1], (1, T))
            b_last = jnp.broadcast_to(b_row[:, last:last + 1], (1, T))
            w_end = jnp.exp(b_last - b_row + i_row - mloc_last)
            vw_scr[slot, d, h, 0:M_DH, :] = (vt_ref[hl, lanes].astype(F32) * w_end).astype(BF16)
            vw_scr[slot, d, h, M_DH:, :] = (ones_blk * w_end).astype(BF16)
            rows_scr[slot, d, 4 * h:4 * h + 4, :] = jnp.concatenate([b_row, mloc, mloc_last, b_last], axis=0)

    def update(ci, d, slot, accumulate):
        c0 = pl.multiple_of(ci * T, T)
        lanes = pl.ds(c0, T)
        for h in range(H):
            sidx = d * H + h
            hl = slice(h * M_DH, (h + 1) * M_DH)
            tr = slice(h * T, (h + 1) * T)
            rows = rows_scr[slot, d, 4 * h:4 * h + 4, :]
            b_row, mloc, mloc_last, b_last = rows[0:1], rows[1:2], rows[2:3], rows[3:4]
            m_row = m_scr[sidx]
            inter = b_row + m_row
            mt = jnp.maximum(inter, mloc)
            kh = k_ref[pl.ds(c0, T), hl]
            qt = qt_ref[hl, lanes]
            vext = jnp.concatenate([vt_ref[hl, lanes], ones_blk.astype(BF16)], axis=0)
            ct = c_scr[sidx]
            tot = (jnp.exp(mloc - mt) * _dot(vext, sb_scr[slot, d, tr, :])
                   + jnp.exp(inter - mt) * _dot(ct.astype(BF16), qt))
            den = tot[M_DH:M_DH + 1]
            ht = tot[0:M_DH] / jnp.maximum(jnp.abs(den), jnp.exp(-mt))
            if accumulate:
                h_scr[hl, lanes] = h_scr[hl, lanes] + ht
            else:
                h_scr[hl, lanes] = ht
            m_new = jnp.maximum(b_last + m_row, mloc_last)
            c_scr[sidx] = (jnp.exp(b_last + m_row - m_new) * ct
                           + jnp.exp(mloc_last - m_new) * _dot(vw_scr[slot, d, h], kh))
            m_scr[sidx] = m_new

    def pair(p, slots, accumulate=None):
        s0, s1 = slots
        for c, slot in ((2 * p, s0), (2 * p + 1, s1)):
            acc = (c >= nc // 2) if accumulate is None else accumulate
            update(c, 0, slot, acc)
            update(nc - 1 - c, 1, slot, acc)
        for c, slot in ((2 * p + 2, 2 - s0), (2 * p + 3, 4 - s1)):
            gates(jnp.minimum(c, nc - 1), 0, slot)
            gates(jnp.maximum(nc - 1 - c, 0), 1, slot)

    for c in (0, 1):
        gates(c, 0, c)
        gates(nc - 1 - c, 1, c)
    if nc % 8 == 0:
        def body(accumulate):
            def run(i, carry):
                pair(2 * i, (0, 1), accumulate)
                pair(2 * i + 1, (2, 3), accumulate)
                return carry
            return run
        lax.fori_loop(0, nc // 8, body(False), 0)
        lax.fori_loop(nc // 8, nc // 4, body(True), 0)
    else:
        for p in range(nc // 2):
            pair(p, ((0, 1), (2, 3))[p % 2])

    def epilogue(i, carry):
        c0 = pl.multiple_of(i * T, T)
        for h in range(H):
            hl = slice(h * M_DH, (h + 1) * M_DH)
            hh = h_scr[hl, pl.ds(c0, T)]
            hn = hh * lax.rsqrt(jnp.mean(hh * hh, axis=0, keepdims=True) + EPS)
            y = hn.T * ng_ref[:, hl] * _sigmoid(o_ref[pl.ds(c0, T), hl])
            y_ref[pl.ds(c0, T), hl] = y.astype(y_ref.dtype)
        return carry

    lax.fori_loop(0, nc, epilogue, 0)
    cst_ref[0] = c_scr[...]
    mst_ref[0] = m_scr[...]


def _mlstm(qt, k, vt, o, gt, gate_b, norm_g, c0, m0, B, L):
    assert L % (2 * SCAN_T) == 0 and M_DH == SCAN_T
    nc = L // SCAN_T
    ng = 4 * M_HEADS
    M = B * L
    one = pl.Buffered(1)
    st_spec = pl.BlockSpec((1, 2 * M_HEADS, 2 * M_DH, M_DH), lambda b: (b, 0, 0, 0))
    m_spec = pl.BlockSpec((1, 2 * M_HEADS, 1, M_DH), lambda b: (b, 0, 0, 0))
    return pl.pallas_call(
        functools.partial(_mlstm_kernel, nc=nc),
        grid=(B,),
        in_specs=[pl.BlockSpec((M_W, L), lambda b: (0, b), pipeline_mode=one),
                  pl.BlockSpec((L, M_W), lambda b: (b, 0), pipeline_mode=one),
                  pl.BlockSpec((M_W, L), lambda b: (0, b), pipeline_mode=one),
                  pl.BlockSpec((L, M_W), lambda b: (b, 0), pipeline_mode=one),
                  pl.BlockSpec((ng, L), lambda b: (0, b)),
                  pl.BlockSpec((ng, SCAN_T), lambda b: (0, 0)),
                  pl.BlockSpec((1, M_W), lambda b: (0, 0)),
                  st_spec, m_spec],
        out_specs=[pl.BlockSpec((L, M_W), lambda b: (b, 0)), st_spec, m_spec],
        out_shape=[jax.ShapeDtypeStruct((M, M_W), BF16),
                   jax.ShapeDtypeStruct((B, 2 * M_HEADS, 2 * M_DH, M_DH), F32),
                   jax.ShapeDtypeStruct((B, 2 * M_HEADS, 1, M_DH), F32)],
        scratch_shapes=[pltpu.VMEM((M_W, L), F32),
                        pltpu.VMEM((2 * M_HEADS, 2 * M_DH, M_DH), F32),
                        pltpu.VMEM((2 * M_HEADS, 1, M_DH), F32),
                        pltpu.VMEM((4, 2, M_HEADS * SCAN_T, SCAN_T), BF16),
                        pltpu.VMEM((4, 2, M_HEADS, 2 * M_DH, SCAN_T), BF16),
                        pltpu.VMEM((4, 2, 4 * M_HEADS, SCAN_T), F32)],
        compiler_params=_cparams("arbitrary"),
        name="mlstm",
    )(qt, k, vt, o, gt, jnp.broadcast_to(gate_b.reshape(ng, 1), (ng, SCAN_T)), norm_g.reshape(1, M_W), c0, m0)


def _ssd_kernel(xt_ref, b_ref, ct_ref, dtt_ref, db_ref, al_ref, dsk_ref, s0_ref,
                y_ref, sout_ref, yf_scr, yb_scr, s_scr, cbt_scr, p_scr, xd_scr, xdec_scr, rows_scr, *, nc):
    T = SCAN_T
    H = SSD_HPG
    s_scr[...] = s0_ref[0, :, 0]
    consts = (_scan_consts(0), _scan_consts(1))
    db = db_ref[0]
    a_neg = -jnp.exp(al_ref[0])

    def gates(ci, d, slot):
        mask, v_mat, last = consts[d]
        c0 = pl.multiple_of(ci * T, T)
        lanes = pl.ds(c0, T)
        hr = slice(d * H, (d + 1) * H)
        dt = _softplus(dtt_ref[hr, lanes] + db[hr])
        a = dt * a_neg[hr]
        acs = _dot2(a, v_mat)
        a_last = jnp.broadcast_to(acs[:, last:last + 1], (H, T))
        rows_scr[slot, d, 0:H, :] = jnp.exp(acs)
        rows_scr[slot, d, H:2 * H, :] = jnp.exp(a_last)
        cbt_scr[d] = _dot(b_ref[pl.ds(c0, T), :], ct_ref[:, lanes])
        for r in range(H):
            tr = slice(r * T, (r + 1) * T)
            pr = slice(r * SSD_HEADDIM, (r + 1) * SSD_HEADDIM)
            al = jnp.broadcast_to(acs[r:r + 1], (T, T))
            lm = jnp.exp(jnp.where(mask, al - al.T, -jnp.inf))
            p_scr[slot, d, tr, :] = (cbt_scr[d] * lm).astype(BF16)
            xd = xt_ref[pr, lanes] * dt[r:r + 1]
            xd_scr[slot, d, pr, :] = xd.astype(BF16)
            xdec_scr[slot, d, pr, :] = (xd * jnp.exp(a_last[r:r + 1] - acs[r:r + 1])).astype(BF16)

    def update(ci, d, slot):
        c0 = pl.multiple_of(ci * T, T)
        lanes = pl.ds(c0, T)
        rows = rows_scr[slot, d]
        sc = _dot(s_scr[d].astype(BF16), ct_ref[:, lanes])
        upd = _dot(xdec_scr[slot, d], b_ref[pl.ds(c0, T), :])
        y_scr = yf_scr if d == 0 else yb_scr
        for r in range(H):
            pr = slice(r * SSD_HEADDIM, (r + 1) * SSD_HEADDIM)
            tr = slice(r * T, (r + 1) * T)
            y_scr[pr, lanes] = _dot(xd_scr[slot, d, pr, :], p_scr[slot, d, tr, :]) + rows[r:r + 1] * sc[pr]
            s_scr[d, pr, :] = rows[H + r:H + r + 1] * s_scr[d, pr, :] + upd[pr]

    def pair(p, slots):
        s0, s1 = slots
        for c, slot in ((2 * p, s0), (2 * p + 1, s1)):
            update(c, 0, slot)
            update(nc - 1 - c, 1, slot)
        for c, slot in ((2 * p + 2, 2 - s0), (2 * p + 3, 4 - s1)):
            gates(jnp.minimum(c, nc - 1), 0, slot)
            gates(jnp.maximum(nc - 1 - c, 0), 1, slot)

    for c in (0, 1):
        gates(c, 0, c)
        gates(nc - 1 - c, 1, c)
    if nc % 4 == 0:
        def body(i, carry):
            pair(2 * i, (0, 1))
            pair(2 * i + 1, (2, 3))
            return carry
        lax.fori_loop(0, nc // 4, body, 0)
    else:
        for p in range(nc // 2):
            pair(p, ((0, 1), (2, 3))[p % 2])

    dsk = dsk_ref[0]

    def epilogue(i, carry):
        c0 = pl.multiple_of(i * T, T)
        lanes = pl.ds(c0, T)
        yt = yf_scr[:, lanes] + yb_scr[:, lanes] + dsk * xt_ref[:, lanes]
        y_ref[pl.ds(c0, T), :] = yt.T
        return carry

    lax.fori_loop(0, nc, epilogue, 0)
    sout_ref[0, :, 0] = s_scr[...]


def _ssd(xt, bm, ct, dtt, dt_bias, a_log, dskip, s0, B, L):
    assert L % (2 * SCAN_T) == 0 and SSD_STATE == SCAN_T
    nc = L // SCAN_T
    G = SSD_GROUPS
    nd = 2 * SSD_HPG
    M = B * L
    rep = lambda p, rows: jnp.broadcast_to(p.reshape(G, rows, 1), (G, rows, SCAN_T))
    small = lambda rows: pl.BlockSpec((1, rows, SCAN_T), lambda b, g: (g, 0, 0))
    st_spec = pl.BlockSpec((1, 2, 1, SSD_GW, SSD_STATE), lambda b, g: (b, 0, g, 0, 0))
    return pl.pallas_call(
        functools.partial(_ssd_kernel, nc=nc),
        grid=(B, G),
        in_specs=[pl.BlockSpec((SSD_GW, L), lambda b, g: (g, b)),
                  pl.BlockSpec((L, SSD_STATE), lambda b, g: (b, g)),
                  pl.BlockSpec((SSD_STATE, L), lambda b, g: (g, b)),
                  pl.BlockSpec((nd, L), lambda b, g: (g, b)),
                  small(nd), small(nd), small(SSD_GW), st_spec],
        out_specs=[pl.BlockSpec((L, SSD_GW), lambda b, g: (b, g)), st_spec],
        out_shape=[jax.ShapeDtypeStruct((M, SSD_INNER), F32),
                   jax.ShapeDtypeStruct((B, 2, G, SSD_GW, SSD_STATE), F32)],
        scratch_shapes=[pltpu.VMEM((SSD_GW, L), F32), pltpu.VMEM((SSD_GW, L), F32),
                        pltpu.VMEM((2, SSD_GW, SSD_STATE), F32),
                        pltpu.VMEM((2, SCAN_T, SCAN_T), F32),
                        pltpu.VMEM((4, 2, SSD_HPG * SCAN_T, SCAN_T), BF16),
                        pltpu.VMEM((4, 2, SSD_GW, SCAN_T), BF16),
                        pltpu.VMEM((4, 2, SSD_GW, SCAN_T), BF16),
                        pltpu.VMEM((4, 2, 2 * SSD_HPG, SCAN_T), F32)],
        compiler_params=_cparams("arbitrary", "arbitrary"),
        name="ssd",
    )(xt, bm, ct, dtt, rep(dt_bias, nd), rep(a_log, nd), rep(dskip, SSD_GW), s0)


def _hy_factors(L):
    return (128, 64) if L == 4096 else (32, (2 * L) // 32)


def _hilo(t):
    return _split2(t.astype(F32))


def _cblock(cr, ci):
    return jnp.concatenate([jnp.concatenate([cr, -ci], axis=-1), jnp.concatenate([ci, cr], axis=-1)], axis=-2)


def _dft_tables(L):
    n1c, n2c = _hy_factors(L)
    N = n1c * n2c
    h1 = n1c // 2
    i32 = jnp.int32
    k1 = jnp.arange(n1c, dtype=i32)
    n2 = jnp.arange(n2c, dtype=i32)

    def stage1(n1_count):
        n1 = jnp.arange(n1_count, dtype=i32)
        idx = (k1[None, :, None] * (n2[:, None, None] + n2c * n1[None, None, :])) % N
        ang = idx.astype(F32) * (2.0 * math.pi / N)
        return jnp.cos(ang), jnp.sin(ang)

    c, s = stage1(h1)
    g_fwd = _cblock(c, -s)
    g_inv = _cblock(jnp.swapaxes(c, 1, 2), jnp.swapaxes(s, 1, 2)) / n1c
    cf, sf = stage1(n1c)
    g_real = jnp.concatenate([cf, -sf], axis=1)
    idx2 = (n2[:, None] * n2[None, :]) % n2c
    ang2 = idx2.astype(F32) * (2.0 * math.pi / n2c)
    c2, s2 = jnp.cos(ang2), jnp.sin(ang2)
    f2 = _cblock(c2, -s2)
    f2_inv = _cblock(c2, s2) / n2c
    return dict(g_fwd=g_fwd.astype(BF16), g_inv=g_inv.astype(BF16), f2=f2.astype(BF16),
                f2_inv=f2_inv.astype(BF16), g_real=_hilo(g_real), f2_exact=_hilo(f2))


def _swap01(x):
    return jnp.swapaxes(x, 0, 1)


def _hy_s1_kernel(x_ref, *rest, nb, exact):
    if exact:
        gh_ref, gl_ref, o_ref = rest
    else:
        g_ref, o_ref = rest
    pb, rows, _, C = x_ref.shape
    xs = _swap01(x_ref[...].reshape(pb * rows, nb, C))
    if exact:
        res = [_dot3(gh_ref[j], gl_ref[j], xs[j]) for j in range(nb)]
    else:
        res = [_dot(g_ref[j], xs[j].astype(BF16)) for j in range(nb)]
    out = _swap01(jnp.stack(res, axis=0))
    o_ref[...] = out.reshape(o_ref.shape).astype(o_ref.dtype)


def _hy_stage1(x, tab, out_dtype):
    exact = isinstance(tab, tuple)
    tabs = tab if exact else (tab,)
    n2c, rows_out, R = tabs[0].shape
    Bx, rows, _, C = x.shape
    pb = R // rows
    nb = HY_NB
    cb = min(C, HY_W)
    tspec = pl.BlockSpec((nb, rows_out, R), lambda p, t, c: (t, 0, 0))
    return pl.pallas_call(
        functools.partial(_hy_s1_kernel, nb=nb, exact=exact),
        grid=(Bx // pb, n2c // nb, C // cb),
        in_specs=[pl.BlockSpec((pb, rows, nb, cb), lambda p, t, c: (p, 0, t, c))] + [tspec] * len(tabs),
        out_specs=pl.BlockSpec((None, 2, rows_out // 2, nb, cb), lambda p, t, c: (p, 0, 0, t, c)),
        out_shape=jax.ShapeDtypeStruct((Bx // pb, 2, rows_out // 2, n2c, C), out_dtype),
        compiler_params=_cparams("arbitrary", "arbitrary", "arbitrary"),
        name="hyena_stage1",
    )(x, *tabs)


def _hy_spec_kernel(a_ref, inv_ref, fh_ref, fl_ref, o_ref, *, kb):
    n2c = a_ref.shape[2]
    for kk in range(kb):
        rhs = jnp.concatenate([a_ref[0, kk], a_ref[1, kk]], axis=0)
        X = _dot3(fh_ref[...], fl_ref[...], rhs) * inv_ref[...]
        o_ref[0, kk] = X[0:n2c]
        o_ref[1, kk] = X[n2c:]


def _hy_filter_spectrum(a, inv_norm, f2, kb):
    _, n1c, n2c, W = a.shape
    fh, fl = f2
    blk = pl.BlockSpec((2, kb, n2c, W), lambda t: (0, t, 0, 0))
    fspec = pl.BlockSpec((2 * n2c, 2 * n2c), lambda t: (0, 0))
    return pl.pallas_call(
        functools.partial(_hy_spec_kernel, kb=kb),
        grid=(n1c // kb,),
        in_specs=[blk, pl.BlockSpec((1, W), lambda t: (0, 0)), fspec, fspec],
        out_specs=blk,
        out_shape=jax.ShapeDtypeStruct(a.shape, F32),
        compiler_params=_cparams("arbitrary"),
        name="hyena_filter_spectrum",
    )(a, inv_norm, fh, fl)


def _hy_mid_kernel(a_ref, k_ref, f_ref, i_ref, o_ref, *, kb):
    n2c = a_ref.shape[2]
    res = []
    for kk in range(kb):
        rhs = jnp.concatenate([a_ref[0, kk], a_ref[1, kk]], axis=0)
        X = _dot(f_ref[...], rhs)
        xr, xi = X[0:n2c], X[n2c:]
        kr, ki = k_ref[0, kk], k_ref[1, kk]
        Y = jnp.concatenate([xr * kr - xi * ki, xr * ki + xi * kr], axis=0)
        res.append(_dot(i_ref[...], Y.astype(BF16)))
    o_ref[...] = _swap01(jnp.stack(res, axis=0)).reshape(o_ref.shape).astype(o_ref.dtype)


def _hy_mid(a, kspec, order, f2, f2_inv):
    P, _, n1c, n2c, C = a.shape
    kb = HY_NB
    fspec = pl.BlockSpec((2 * n2c, 2 * n2c), lambda t, p: (0, 0))
    return pl.pallas_call(
        functools.partial(_hy_mid_kernel, kb=kb),
        grid=(n1c // kb, P),
        in_specs=[pl.BlockSpec((None, 2, kb, n2c, C), lambda t, p: (p, 0, t, 0, 0)),
                  pl.BlockSpec((2, kb, n2c, C), lambda t, p: (0, t, 0, order)),
                  fspec, fspec],
        out_specs=pl.BlockSpec((None, 2, n2c, kb, C), lambda t, p: (p, 0, 0, t, 0)),
        out_shape=jax.ShapeDtypeStruct((P, 2, n2c, n1c, C), BF16),
        compiler_params=_cparams("arbitrary", "arbitrary"),
        name="hyena_mid",
    )(a, kspec, f2, f2_inv)


def _hy_last_kernel(b_ref, z_ref, gate_ref, bias_ref, gi_ref, *rest, fuse, nb):
    if fuse:
        gf_ref, zo_ref, ao_ref = rest
    else:
        (zo_ref,) = rest
    _, h1, _, C = z_ref.shape
    zs = _swap01(z_ref[...].reshape(2 * h1, nb, C))
    gs = _swap01(gate_ref[...].reshape(2 * h1, nb, C))
    bias = bias_ref[...]
    zn, an = [], []
    for j in range(nb):
        bv = jnp.concatenate([b_ref[0, j], b_ref[1, j]], axis=0)
        y = _dot(gi_ref[j], bv)
        znew = gs[j] * (y + zs[j] * bias)
        zn.append(znew)
        if fuse:
            an.append(_dot(gf_ref[j], znew.astype(BF16)))
    zo_ref[...] = _swap01(jnp.stack(zn, axis=0)).reshape(zo_ref.shape).astype(zo_ref.dtype)
    if fuse:
        ao_ref[...] = _swap01(jnp.stack(an, axis=0)).reshape(ao_ref.shape).astype(ao_ref.dtype)


def _hy_last(bf, z, gate, bias, g_inv, g_fwd, out_dtype):
    P, _, n2c, n1c, C = bf.shape
    B, h1, _, _ = z.shape
    nb = HY_NB
    cb = C // 2
    fuse = g_fwd is not None
    zspec = pl.BlockSpec((2, h1, nb, cb), lambda p, t, c: (p, 0, t, c))
    in_specs = [pl.BlockSpec((None, 2, nb, n1c, cb), lambda p, t, c: (p, 0, t, 0, c)),
                zspec, zspec, pl.BlockSpec((1, cb), lambda p, t, c: (0, c)),
                pl.BlockSpec((nb, 2 * h1, 2 * n1c), lambda p, t, c: (t, 0, 0))]
    args = [bf, z, gate, bias, g_inv]
    out_specs = [zspec]
    out_shape = [jax.ShapeDtypeStruct(z.shape, out_dtype)]
    if fuse:
        in_specs.append(pl.BlockSpec((nb, 2 * n1c, 2 * h1), lambda p, t, c: (t, 0, 0)))
        args.append(g_fwd)
        out_specs.append(pl.BlockSpec((None, 2, n1c, nb, cb), lambda p, t, c: (p, 0, 0, t, c)))
        out_shape.append(jax.ShapeDtypeStruct((P, 2, n1c, n2c, C), BF16))
    return pl.pallas_call(
        functools.partial(_hy_last_kernel, fuse=fuse, nb=nb),
        grid=(P, n2c // nb, C // cb),
        in_specs=in_specs, out_specs=out_specs, out_shape=out_shape,
        compiler_params=_cparams("arbitrary", "arbitrary", "arbitrary"),
        name="hyena_last",
    )(*args)


def _hy_filter_kernel(f_ref, w1_ref, b1_ref, w2_ref, b2_ref, w3_ref, fr_ref, dl_ref, hk_ref, s_ref, *, L):
    i = pl.program_id(0)
    tl = f_ref.shape[0]
    f = f_ref[...]
    fr = fr_ref[...]
    h = jnp.sin(fr * (_dot(f.astype(BF16), w1_ref[...].astype(BF16)) + b1_ref[...]))
    h = jnp.sin(fr * (_dot(h.astype(BF16), w2_ref[...].astype(BF16)) + b2_ref[...]))
    hk = _dot(h.astype(BF16), w3_ref[...].astype(BF16)) * jnp.exp(-f[:, 0:1] * dl_ref[...])
    n = i * tl + lax.broadcasted_iota(jnp.int32, hk.shape, 0)
    hk = jnp.where(n == L, 0.0, hk)
    hk_ref[...] = hk
    tot = jnp.sum(jnp.abs(hk), axis=0, keepdims=True)

    @pl.when(i == 0)
    def _():
        s_ref[...] = tot

    @pl.when(i != 0)
    def _():
        s_ref[...] = s_ref[...] + tot


def _hy_filter(L, w1, b1, w2, b2, w3, freq):
    f32 = F32
    t = jnp.linspace(0.0, 1.0, L, dtype=f32)[:, None]
    ang = (2.0 * math.pi / L) * jnp.arange(L, dtype=f32)[:, None] * \
        jnp.linspace(1e-4, HY_BANDS - 1, HY_BANDS, dtype=f32)[None]
    feats = jnp.concatenate([t, jnp.cos(ang), -jnp.sin(ang)], axis=-1)
    feats = jnp.concatenate([feats, feats[0:1], feats[:0:-1]], axis=0)
    fpad = 128
    feats = jnp.pad(feats, ((0, 0), (0, fpad - HY_EMB)))
    w1p = jnp.pad(w1, ((0, fpad - HY_EMB), (0, 0)))
    deltas = jnp.abs(jnp.linspace(math.log(HY_TARGET) / HY_SLOW, math.log(HY_TARGET) / HY_FAST, HY_W, dtype=f32))
    OC = HY_ORDER * HY_W
    dl = jnp.tile(deltas, HY_ORDER).reshape(1, OC)
    tl = min(L, 512)
    nblk = L // tl
    full = lambda shp: pl.BlockSpec(shp, lambda i: (0, 0))
    return pl.pallas_call(
        functools.partial(_hy_filter_kernel, L=L),
        grid=(2 * nblk,),
        in_specs=[pl.BlockSpec((tl, fpad), lambda i: (i, 0)),
                  full((fpad, HY_HID)), full((1, HY_HID)), full((HY_HID, HY_HID)), full((1, HY_HID)),
                  pl.BlockSpec((HY_HID, OC), lambda i: (0, i // nblk)), full((1, HY_HID)), full((1, OC))],
        out_specs=[pl.BlockSpec((tl, OC), lambda i: (i, 0)), full((1, OC))],
        out_shape=[jax.ShapeDtypeStruct((2 * L, OC), F32), jax.ShapeDtypeStruct((1, OC), F32)],
        compiler_params=_cparams("arbitrary"),
        name="hyena_filter",
    )(feats, w1p, b1.reshape(1, HY_HID), w2, b2.reshape(1, HY_HID), w3, freq.reshape(1, HY_HID), dl)


def _hy_spectrum(L, tabs, w1, b1, w2, b2, w3, freq):
    n1c, n2c = _hy_factors(L)
    OC = HY_ORDER * HY_W
    ktime, s = _hy_filter(L, w1, b1, w2, b2, w3, freq)
    a = _hy_stage1(ktime.reshape(1, n1c, n2c, OC), tabs["g_real"], F32)[0]
    return _hy_filter_spectrum(a, 1.0 / s, tabs["f2_exact"], 8)


def _hyena(v, x1, x2, kspec, bias, tabs, B, L):
    n1c, n2c = _hy_factors(L)
    C = HY_W
    view = lambda t: t.reshape(B, n1c // 2, n2c, C)
    z0, g1, g2 = view(v), view(x1), view(x2)
    a = _hy_stage1(z0, tabs["g_fwd"], BF16)
    bf = _hy_mid(a, kspec, 0, tabs["f2"], tabs["f2_inv"])
    z1, a = _hy_last(bf, z0, g1, bias[0:1], tabs["g_inv"], tabs["g_fwd"], F32)
    bf = _hy_mid(a, kspec, 1, tabs["f2"], tabs["f2_inv"])
    (z2,) = _hy_last(bf, z1, g2, bias[1:2], tabs["g_inv"], None, BF16)
    return z2.reshape(B * L, C)


_EVEN_PLAN = (
    (0, HY_W, "conv", 1.0, False, F32),
    (HY_W, HY_W, "conv", 1.0, False, F32),
    (2 * HY_W, HY_W, "conv", 1.0, False, F32),
    (3 * HY_W, M_W, "conv_silu", 1.0, True, BF16),
    (3 * HY_W + M_W, M_W, "conv_silu", M_DH ** -0.5, False, BF16),
    (EV_CONV + M_W, M_W, "raw", 1.0, False, F32),
)
_EVEN_NT_PLAN = ((0, M_W, BF16), (M_W, 4 * M_HEADS, F32))
_ODD_NT_PLAN = ((0, 2 * SSD_HEADS, F32),)
_GN = SSD_GROUPS * SSD_STATE
_ODD_PLAN = (
    (0, SSD_INNER, "raw", 1.0, False, F32),
    (SSD_INNER, SSD_INNER, "conv_silu", 1.0, True, F32),
    (2 * SSD_INNER, _GN, "conv_silu", 1.0, False, BF16),
    (2 * SSD_INNER + _GN, _GN, "conv_silu", 1.0, True, BF16),
)


def _even_layer(x, mods_l, P, e, l, B, L, st, per_batch_cond, kspec, tabs):
    hv, hx1, hx2, qt, k, o, vt, gt = _inproj(
        x, mods_l, P["norm1_g"][l:l + 1], P["ev_in_w"], P["ev_in_wt"], P["ev_conv_w"], P["ev_conv_b"], e,
        plan=_EVEN_PLAN, nt_plan=_EVEN_NT_PLAN, conv0=0, seq_len=L, per_batch_cond=per_batch_cond)
    y_hy = _hyena(hv, hx1, hx2, kspec, P["hy_bias"][e], tabs, B, L)
    c0, m0 = st
    y_m, c_new, m_new = _mlstm(qt, k, vt, o, gt, P["m_gate_b"][e], P["m_norm_g"][e], c0, m0, B, L)
    x = _outproj_even(y_hy, y_m, x, mods_l, P["ev_out_w"], e, seq_len=L, per_batch_cond=per_batch_cond)
    return x, (c_new, m_new)


def _by_group(p):
    return p.reshape(2, SSD_GROUPS, SSD_HPG).transpose(1, 0, 2).reshape(SSD_GROUPS, 2 * SSD_HPG)


def _odd_layer(x, mods_l, P, o, l, B, L, s0, per_batch_cond):
    z, xt, bm, ct, dtt = _inproj(
        x, mods_l, P["norm1_g"][l:l + 1], P["od_in_w"], P["od_in_wt"], P["od_conv_w"], P["od_conv_b"], o,
        plan=_ODD_PLAN, nt_plan=_ODD_NT_PLAN, conv0=SSD_INNER, seq_len=L, per_batch_cond=per_batch_cond)
    dskip = jnp.repeat(P["ssd_D"][o], SSD_HEADDIM).reshape(SSD_GROUPS, SSD_GW)
    y, s_new = _ssd(xt, bm, ct, dtt, _by_group(P["ssd_dt_bias"][o]), _by_group(P["ssd_A_log"][o]), dskip, s0, B, L)
    x = _outproj_odd(y, z, x, mods_l, P["ssd_norm_g"], P["od_out_w"], o, seq_len=L, per_batch_cond=per_batch_cond)
    return x, s_new


def _to_colmajor(x, B, L):
    rows = L // GRID_W
    return x.reshape(B, rows, GRID_W, D_MODEL).swapaxes(1, 2).reshape(B * L, D_MODEL)


def _from_colmajor(x, B, L):
    rows = L // GRID_W
    return x.reshape(B, GRID_W, rows, D_MODEL).swapaxes(1, 2).reshape(B * L, D_MODEL)


def _trunk(x, mods, P, B, L, init_states, per_batch_cond, grid, kspecs, tabs):
    states = []
    for l in range(DEPTH):
        if grid and l == 2:
            x = _to_colmajor(x, B, L)
        mods_l = mods[l]
        if l % 2 == 0:
            x, st = _even_layer(x, mods_l, P, l // 2, l, B, L, init_states[l], per_batch_cond, kspecs[l // 2], tabs)
        else:
            x, st = _odd_layer(x, mods_l, P, l // 2, l, B, L, init_states[l], per_batch_cond)
        states.append(st)
        x = _ffn(x, mods_l, P["norm2_g"].reshape(DEPTH, 1, D_MODEL), P["ffn_w1"], P["ffn_w3"], P["ffn_w2"],
                 P["final_g"].reshape(1, D_MODEL), l, seq_len=L, per_batch_cond=per_batch_cond,
                 final=(l == DEPTH - 1))
    if grid:
        x = _from_colmajor(x, B, L)
    return x, states


def _pack_mlstm_state(C, n, m):
    B = C.shape[0]
    ext = jnp.concatenate([jnp.swapaxes(C, -1, -2), n[..., None, :],
                           jnp.zeros(C.shape[:-2] + (M_DH - 1, M_DH), F32)], axis=-2)
    m_rep = jnp.broadcast_to(m.reshape(B, 2 * M_HEADS, 1, 1), (B, 2 * M_HEADS, 1, M_DH))
    return ext.reshape(B, 2 * M_HEADS, 2 * M_DH, M_DH), m_rep


def kernel(x_prompt, x_sample, state_mlstm_C, state_mlstm_n, state_mlstm_m, state_ssd, c, c_ctx,
           mod_w, mod_b, norm1_g, norm2_g, ffn_w1, ffn_w3, ffn_w2, final_g,
           ev_in_w, ev_conv_w, ev_conv_b, hy_w1, hy_b1, hy_w2, hy_b2, hy_w3, hy_freq, hy_bias,
           m_gate_b, m_norm_g, ev_out_w,
           od_in_w, od_conv_w, od_conv_b, ssd_dt_bias, ssd_A_log, ssd_D, ssd_norm_g, od_out_w):
    Bp, Lp, _ = x_prompt.shape
    Bs, Ls, _ = x_sample.shape
    bf = lambda w: w.astype(BF16)
    n_odd = od_in_w.shape[0]
    od_dt_w = od_in_w[:, :, OD_PROJ - 2 * SSD_HEADS:].reshape(n_odd, D_MODEL, 2, SSD_GROUPS, SSD_HPG)
    od_dt_w = od_dt_w.transpose(0, 3, 2, 4, 1).reshape(n_odd, 2 * SSD_HEADS, D_MODEL)
    P = dict(norm1_g=norm1_g, norm2_g=norm2_g, ffn_w1=bf(ffn_w1), ffn_w3=bf(ffn_w3), ffn_w2=bf(ffn_w2),
             final_g=final_g, ev_in_w=bf(ev_in_w),
             ev_in_wt=bf(jnp.concatenate([ev_in_w[:, :, EV_CONV:EV_CONV + M_W],
                                          ev_in_w[:, :, EV_PROJ - 4 * M_HEADS:]], axis=2)).swapaxes(1, 2),
             ev_conv_w=ev_conv_w, ev_conv_b=ev_conv_b, hy_bias=hy_bias, m_gate_b=m_gate_b, m_norm_g=m_norm_g,
             ev_out_w=bf(ev_out_w), od_in_w=bf(od_in_w), od_in_wt=bf(od_dt_w),
             od_conv_w=od_conv_w, od_conv_b=od_conv_b, ssd_dt_bias=ssd_dt_bias, ssd_A_log=ssd_A_log,
             ssd_D=ssd_D, ssd_norm_g=ssd_norm_g.reshape(-1, 1, SSD_INNER), od_out_w=bf(od_out_w))

    ncond = 16
    cond = jnp.concatenate([c_ctx[None, :], c, jnp.zeros((ncond - 1 - Bs, D_MODEL), F32)], axis=0)
    mods = _modulation(cond, mod_w, mod_b).reshape(DEPTH, ncond, 6, D_MODEL)

    tabs = {L: _dft_tables(L) for L in (Lp, Ls)}
    kspecs = {L: [_hy_spectrum(L, tabs[L], hy_w1[e], hy_b1[e], hy_w2[e], hy_b2[e], hy_w3[e], hy_freq[e])
                  for e in range((DEPTH + 1) // 2)] for L in (Lp, Ls)}

    G = SSD_GROUPS
    zero_m = _pack_mlstm_state(jnp.zeros((Bp, 2, M_HEADS, M_DH, M_DH), F32), jnp.zeros((Bp, 2, M_HEADS, M_DH), F32),
                               jnp.zeros((Bp, 2, M_HEADS), F32))
    zero_s = jnp.zeros((Bp, 2, G, SSD_GW, SSD_STATE), F32)
    ctx_init = [zero_m if l % 2 == 0 else zero_s for l in range(DEPTH)]
    y_prompt, ctx_states = _trunk(x_prompt.reshape(Bp * Lp, D_MODEL), mods, P, Bp, Lp, ctx_init, False, False,
                                  kspecs[Lp], tabs[Lp])

    lat_init = []
    for l in range(DEPTH):
        if l % 2 == 0:
            e = l // 2
            lat_init.append(_pack_mlstm_state(state_mlstm_C[:, e], state_mlstm_n[:, e], state_mlstm_m[:, e]))
        else:
            lat_init.append(state_ssd[:, l // 2].reshape(Bs, 2, G, SSD_GW, SSD_STATE))
    y_sample, _ = _trunk(x_sample.reshape(Bs * Ls, D_MODEL), mods, P, Bs, Ls, lat_init, True, True,
                         kspecs[Ls], tabs[Ls])

    cs = [ctx_states[l][0].reshape(Bp, 2, M_HEADS, 2 * M_DH, M_DH) for l in range(0, DEPTH, 2)]
    ms = [ctx_states[l][1][:, :, 0, 0].reshape(Bp, 2, M_HEADS) for l in range(0, DEPTH, 2)]
    new_C = jnp.stack([jnp.swapaxes(t[..., :M_DH, :], -1, -2) for t in cs], axis=1)
    new_n = jnp.stack([t[..., M_DH, :] for t in cs], axis=1)
    new_m = jnp.stack(ms, axis=1)
    new_ssd = jnp.stack([ctx_states[l].reshape(Bp, 2, SSD_HEADS, SSD_HEADDIM, SSD_STATE)
                         for l in range(1, DEPTH, 2)], axis=1)
    return (y_prompt.reshape(Bp, Lp, D_MODEL), y_sample.reshape(Bs, Ls, D_MODEL), new_C, new_n, new_m, new_ssd)
```

```python
import functools
import math

import jax
import jax.numpy as jnp
from jax import lax
from jax.experimental import pallas as pl
from jax.experimental.pallas import tpu as pltpu

F32 = jnp.float32
BF16 = jnp.bfloat16

D_MODEL = 1024
DEPTH = 4
GRID_W = 64
EPS = 1e-6
HY_W = D_MODEL // 2
HY_ORDER = 2
HY_EMB = 33
HY_BANDS = (HY_EMB - 1) // 2
HY_HID = 64
HY_TARGET = 1e-2
HY_FAST = 0.3
HY_SLOW = 1.5
M_HEADS = 4
M_DH = 128
M_W = M_HEADS * M_DH
EV_CONV = 3 * HY_W + 2 * M_W
EV_PROJ = EV_CONV + 2 * M_W + 4 * M_HEADS
SSD_INNER = 2 * D_MODEL
SSD_HEADDIM = 64
SSD_HEADS = SSD_INNER // SSD_HEADDIM
SSD_GROUPS = 8
SSD_HPG = SSD_HEADS // SSD_GROUPS
SSD_STATE = 128
SSD_GW = SSD_HPG * SSD_HEADDIM
SSD_CONV_DIM = SSD_INNER + 2 * SSD_GROUPS * SSD_STATE
OD_PROJ = SSD_INNER + SSD_CONV_DIM + 2 * SSD_HEADS
FFN_HIDDEN = -(-8 * D_MODEL // (3 * 256)) * 256

V7X_VMEM_BYTES = 64 * 1024 * 1024
VMEM_LIMIT = V7X_VMEM_BYTES * 3 // 4
ROW_TILE = 256
WIDE_TILE = 512
HALO = 16
COL_CHUNK = 512
HY_NB = 16
SCAN_T = 128


def _cparams(*sem):
    return pltpu.CompilerParams(dimension_semantics=sem, vmem_limit_bytes=VMEM_LIMIT)


def _dot(a, b):
    return jnp.dot(a, b, preferred_element_type=F32)


def _dot_nt(a, b):
    return lax.dot_general(a, b, (((1,), (1,)), ((), ())), preferred_element_type=F32)


def _split2(x):
    h = x.astype(BF16)
    return h, (x - h.astype(F32)).astype(BF16)


def _dot3(th, tl, x):
    xh, xl = _split2(x)
    return _dot(th, xh) + _dot(th, xl) + _dot(tl, xh)


def _dot2(x, t):
    xh, xl = _split2(x)
    return _dot(xh, t) + _dot(xl, t)


def _sigmoid(x):
    return 1.0 / (1.0 + jnp.exp(-x))


def _silu(x):
    return x * _sigmoid(x)


def _log_sigmoid(x):
    return jnp.minimum(x, 0.0) - jnp.log(1.0 + jnp.exp(-jnp.abs(x)))


def _softplus(x):
    return jnp.maximum(x, 0.0) + jnp.log(1.0 + jnp.exp(-jnp.abs(x)))


def _rms(x):
    return x * lax.rsqrt(jnp.mean(x * x, axis=-1, keepdims=True) + EPS)


def _mod_kernel(c_ref, w_ref, b_ref, o_ref):
    a = _silu(c_ref[...]).astype(BF16)
    o_ref[0] = _dot(a, w_ref[0].astype(BF16)) + b_ref[0]


def _modulation(cond, mod_w, mod_b):
    nc = cond.shape[0]
    tn = 1536
    return pl.pallas_call(
        _mod_kernel,
        grid=(DEPTH, 6 * D_MODEL // tn),
        in_specs=[pl.BlockSpec((nc, D_MODEL), lambda l, j: (0, 0)),
                  pl.BlockSpec((1, D_MODEL, tn), lambda l, j: (l, 0, j)),
                  pl.BlockSpec((1, 1, tn), lambda l, j: (l, 0, j))],
        out_specs=pl.BlockSpec((1, nc, tn), lambda l, j: (l, 0, j)),
        out_shape=jax.ShapeDtypeStruct((DEPTH, nc, 6 * D_MODEL), F32),
        compiler_params=_cparams("arbitrary", "arbitrary"),
        name="modulation",
    )(cond, mod_w, mod_b.reshape(DEPTH, 1, 6 * D_MODEL))


def _inproj_kernel(x_ref, xp_ref, xn_ref, mod_ref, g_ref, w_ref, wt_ref, cw_ref, cb_ref, *rest,
                   plan, nt_plan, tiles_per_seq, conv0):
    nseg = len(plan)
    outs = rest[:nseg]
    nt_outs = rest[nseg:nseg + len(nt_plan)]
    hbuf = rest[nseg + len(nt_plan)]
    tm = x_ref.shape[0]
    i = pl.program_id(0)
    j = i % tiles_per_seq
    pmask = (j != 0).astype(F32)
    nmask = (j != tiles_per_seq - 1).astype(F32)
    mod = mod_ref[0]
    sh, sc = mod[0:1], mod[1:2]
    g = g_ref[...]

    def normmod(xv):
        return (_rms(xv) * g * (1.0 + sc) + sh).astype(BF16)

    hbuf[0:tm] = normmod(x_ref[...])
    hbuf[tm:tm + HALO] = normmod(xp_ref[...])
    hbuf[tm + HALO:tm + 2 * HALO] = normmod(xn_ref[...])
    rowid = lax.broadcasted_iota(jnp.int32, (tm, COL_CHUNK), 0)

    for (c0, width, kind, scale, transposed), o_ref in zip(plan, outs):
        for cc in range(c0, c0 + width, COL_CHUNK):
            w = w_ref[0, :, cc:cc + COL_CHUNK]
            if kind == "raw":
                y = _dot(hbuf[0:tm], w)
            else:
                full = _dot(hbuf[...], w)
                acc = full[0:tm]
                ap = full[tm + HALO - 1:tm + HALO] * pmask
                an = full[tm + HALO:tm + HALO + 1] * nmask
                prev = jnp.where(rowid == 0, ap, pltpu.roll(acc, 1, 0))
                nxt = jnp.where(rowid == tm - 1, an, pltpu.roll(acc, tm - 1, 0))
                cw = cw_ref[0, :, cc - conv0:cc - conv0 + COL_CHUNK]
                cb = cb_ref[0, :, cc - conv0:cc - conv0 + COL_CHUNK]
                y = cb + cw[0:1] * prev + cw[1:2] * acc + cw[2:3] * nxt
                if kind == "conv_silu":
                    y = _silu(y)
                if scale != 1.0:
                    y = y * scale
            if transposed:
                o_ref[cc - c0:cc - c0 + COL_CHUNK, :] = y.T.astype(o_ref.dtype)
            else:
                o_ref[:, cc - c0:cc - c0 + COL_CHUNK] = y.astype(o_ref.dtype)

    for (r0, rows), o_ref in zip(nt_plan, nt_outs):
        o_ref[...] = _dot_nt(wt_ref[0, r0:r0 + rows, :], hbuf[0:tm]).astype(o_ref.dtype)


def _inproj(x, mods_l, g_l, w_all, wt_all, cw_all, cb_all, layer_idx, *, plan, nt_plan, conv0, seq_len,
            per_batch_cond):
    M = x.shape[0]
    tm = ROW_TILE
    tps = seq_len // tm
    nt = M // tm
    hb = tm // HALO
    ntot = w_all.shape[2]
    tw = wt_all.shape[1]
    cdim = cw_all.shape[2]
    e = layer_idx
    cond_map = (lambda i: (1 + i // tps, 0, 0)) if per_batch_cond else (lambda i: (0, 0, 0))
    in_specs = [
        pl.BlockSpec((tm, D_MODEL), lambda i: (i, 0)),
        pl.BlockSpec((HALO, D_MODEL), lambda i: (jnp.maximum(i * hb - 1, 0), 0)),
        pl.BlockSpec((HALO, D_MODEL), lambda i: (jnp.minimum((i + 1) * hb, nt * hb - 1), 0)),
        pl.BlockSpec((1, 6, D_MODEL), cond_map),
        pl.BlockSpec((1, D_MODEL), lambda i: (0, 0)),
        pl.BlockSpec((1, D_MODEL, ntot), lambda i: (e, 0, 0), pipeline_mode=pl.Buffered(1)),
        pl.BlockSpec((1, tw, D_MODEL), lambda i: (e, 0, 0)),
        pl.BlockSpec((1, 3, cdim), lambda i: (e, 0, 0)),
        pl.BlockSpec((1, 1, cdim), lambda i: (e, 0, 0)),
    ]
    out_specs, out_shape = [], []
    for (c0, width, kind, scale, transposed, dt) in plan:
        if transposed:
            out_specs.append(pl.BlockSpec((width, tm), lambda i: (0, i)))
            out_shape.append(jax.ShapeDtypeStruct((width, M), dt))
        else:
            out_specs.append(pl.BlockSpec((tm, width), lambda i: (i, 0)))
            out_shape.append(jax.ShapeDtypeStruct((M, width), dt))
    for (r0, rows, dt) in nt_plan:
        out_specs.append(pl.BlockSpec((rows, tm), lambda i: (0, i)))
        out_shape.append(jax.ShapeDtypeStruct((rows, M), dt))
    kplan = tuple(p[:5] for p in plan)
    knt = tuple(p[:2] for p in nt_plan)
    return pl.pallas_call(
        functools.partial(_inproj_kernel, plan=kplan, nt_plan=knt, tiles_per_seq=tps, conv0=conv0),
        grid=(nt,),
        in_specs=in_specs,
        out_specs=out_specs,
        out_shape=out_shape,
        scratch_shapes=[pltpu.VMEM((tm + 2 * HALO, D_MODEL), BF16)],
        compiler_params=_cparams("arbitrary"),
        name="inproj",
    )(x, x, x, mods_l, g_l, w_all, wt_all, cw_all, cb_all.reshape(cb_all.shape[0], 1, cdim))


def _outproj_even_kernel(yh_ref, ym_ref, x_ref, mod_ref, w_ref, o_ref):
    out = _dot(yh_ref[...], w_ref[0, 0:HY_W]) + _dot(ym_ref[...], w_ref[0, HY_W:])
    o_ref[...] = x_ref[...] + mod_ref[0][2:3] * out


def _outproj_odd_kernel(y_ref, z_ref, x_ref, mod_ref, ng_ref, w_ref, o_ref):
    u = y_ref[...] * _silu(z_ref[...])
    un = (_rms(u) * ng_ref[0]).astype(BF16)
    o_ref[...] = x_ref[...] + mod_ref[0][2:3] * _dot(un, w_ref[0])


def _cond_map(seq_len, per_batch_cond):
    tps = seq_len // WIDE_TILE
    return (lambda i: (1 + i // tps, 0, 0)) if per_batch_cond else (lambda i: (0, 0, 0))


def _outproj_even(yh, ym, x, mods_l, w_all, e, *, seq_len, per_batch_cond):
    M = x.shape[0]
    tm = WIDE_TILE
    return pl.pallas_call(
        _outproj_even_kernel,
        grid=(M // tm,),
        in_specs=[pl.BlockSpec((tm, HY_W), lambda i: (i, 0)),
                  pl.BlockSpec((tm, M_W), lambda i: (i, 0)),
                  pl.BlockSpec((tm, D_MODEL), lambda i: (i, 0)),
                  pl.BlockSpec((1, 6, D_MODEL), _cond_map(seq_len, per_batch_cond)),
                  pl.BlockSpec((1, HY_W + M_W, D_MODEL), lambda i: (e, 0, 0), pipeline_mode=pl.Buffered(1))],
        out_specs=pl.BlockSpec((tm, D_MODEL), lambda i: (i, 0)),
        out_shape=jax.ShapeDtypeStruct((M, D_MODEL), F32),
        compiler_params=_cparams("arbitrary"),
        name="outproj_even",
    )(yh, ym, x, mods_l, w_all)


def _outproj_odd(y, z, x, mods_l, ng_all, w_all, o, *, seq_len, per_batch_cond):
    M = x.shape[0]
    tm = WIDE_TILE
    return pl.pallas_call(
        _outproj_odd_kernel,
        grid=(M // tm,),
        in_specs=[pl.BlockSpec((tm, SSD_INNER), lambda i: (i, 0)),
                  pl.BlockSpec((tm, SSD_INNER), lambda i: (i, 0)),
                  pl.BlockSpec((tm, D_MODEL), lambda i: (i, 0)),
                  pl.BlockSpec((1, 6, D_MODEL), _cond_map(seq_len, per_batch_cond)),
                  pl.BlockSpec((1, 1, SSD_INNER), lambda i: (o, 0, 0)),
                  pl.BlockSpec((1, SSD_INNER, D_MODEL), lambda i: (o, 0, 0), pipeline_mode=pl.Buffered(1))],
        out_specs=pl.BlockSpec((tm, D_MODEL), lambda i: (i, 0)),
        out_shape=jax.ShapeDtypeStruct((M, D_MODEL), F32),
        compiler_params=_cparams("arbitrary"),
        name="outproj_odd",
    )(y, z, x, mods_l, ng_all, w_all)


FFN_CHUNK = FFN_HIDDEN // 2


def _ffn_kernel(x_ref, mod_ref, g_ref, w1_ref, w3_ref, w2_ref, fg_ref, o_ref, *, final):
    x = x_ref[...]
    mod = mod_ref[0]
    h = (_rms(x) * g_ref[0] * (1.0 + mod[4:5]) + mod[3:4]).astype(BF16)
    ff = jnp.zeros(x.shape, F32)
    for c0 in range(0, FFN_HIDDEN, FFN_CHUNK):
        a = _dot(h, w1_ref[0, :, c0:c0 + FFN_CHUNK])
        b = _dot(h, w3_ref[0, :, c0:c0 + FFN_CHUNK])
        u = (_silu(a) * b).astype(BF16)
        ff = ff + _dot(u, w2_ref[0, c0:c0 + FFN_CHUNK, :])
    xo = x + mod[5:6] * ff
    if final:
        xo = _rms(xo) * fg_ref[...]
    o_ref[...] = xo


def _ffn(x, mods_l, g_all, w1, w3, w2, fg, l, *, seq_len, per_batch_cond, final):
    M = x.shape[0]
    tm = WIDE_TILE
    wspec = lambda shape: pl.BlockSpec(shape, lambda i: (l, 0, 0), pipeline_mode=pl.Buffered(1))
    return pl.pallas_call(
        functools.partial(_ffn_kernel, final=final),
        grid=(M // tm,),
        in_specs=[pl.BlockSpec((tm, D_MODEL), lambda i: (i, 0)),
                  pl.BlockSpec((1, 6, D_MODEL), _cond_map(seq_len, per_batch_cond)),
                  pl.BlockSpec((1, 1, D_MODEL), lambda i: (l, 0, 0)),
                  wspec((1, D_MODEL, FFN_HIDDEN)), wspec((1, D_MODEL, FFN_HIDDEN)),
                  wspec((1, FFN_HIDDEN, D_MODEL)),
                  pl.BlockSpec((1, D_MODEL), lambda i: (0, 0))],
        out_specs=pl.BlockSpec((tm, D_MODEL), lambda i: (i, 0)),
        out_shape=jax.ShapeDtypeStruct((M, D_MODEL), F32),
        compiler_params=_cparams("arbitrary"),
        name="ffn",
    )(x, mods_l, g_all, w1, w3, w2, fg)


def _scan_consts(d):
    T = SCAN_T
    row = lax.broadcasted_iota(jnp.int32, (T, T), 0)
    col = lax.broadcasted_iota(jnp.int32, (T, T), 1)
    mask, last = (row <= col, T - 1) if d == 0 else (row >= col, 0)
    v_mat = jnp.where(mask, 1.0, 0.0).astype(BF16)
    return mask, v_mat, last


def _mlstm_kernel(qt_ref, k_ref, vt_ref, o_ref, gt_ref, gb_ref, ng_ref, c0_ref, m0_ref,
                  y_ref, cst_ref, mst_ref, h_scr, c_scr, m_scr, sb_scr, vw_scr, rows_scr, *, nc):
    T = SCAN_T
    H = M_HEADS
    c_scr[...] = c0_ref[0]
    m_scr[...] = m0_ref[0]
    ones_blk = jnp.where(lax.broadcasted_iota(jnp.int32, (M_DH, T), 0) == 0, 1.0, 0.0)
    consts = (_scan_consts(0), _scan_consts(1))
    gb = gb_ref[...]

    def gates(ci, d, slot):
        mask, v_mat, last = consts[d]
        c0 = pl.multiple_of(ci * T, T)
        lanes = pl.ds(c0, T)
        graw = gt_ref[:, lanes] + gb
        lf = _log_sigmoid(graw)
        b_all = _dot2(lf, v_mat)
        for h in range(H):
            fi = (2 * d + 1) * H + h
            ii = (2 * d) * H + h
            hl = slice(h * M_DH, (h + 1) * M_DH)
            tr = slice(h * T, (h + 1) * T)
            b_row, i_row = b_all[fi:fi + 1], graw[ii:ii + 1]
            dlog = jnp.where(mask, jnp.broadcast_to(b_row, (T, T)) - jnp.broadcast_to(b_row - i_row, (T, T)).T,
                             -jnp.inf)
            mloc = jnp.max(dlog, axis=0, keepdims=True)
            sb_scr[slot, d, tr, :] = (_dot(k_ref[pl.ds(c0, T), hl], qt_ref[hl, lanes])
                                      * jnp.exp(dlog - mloc)).astype(BF16)
            mloc_last = jnp.broadcast_to(mloc[:, last:last + 1], (1, T))
            b_last = jnp.broadcast_to(b_row[:, last:last + 1], (1, T))
            w_end = jnp.exp(b_last - b_row + i_row - mloc_last)
            vw_scr[slot, d, h, 0:M_DH, :] = (vt_ref[hl, lanes].astype(F32) * w_end).astype(BF16)
            vw_scr[slot, d, h, M_DH:, :] = (ones_blk * w_end).astype(BF16)
            rows_scr[slot, d, 4 * h:4 * h + 4, :] = jnp.concatenate([b_row, mloc, mloc_last, b_last], axis=0)

    def update(ci, d, slot, accumulate):
        c0 = pl.multiple_of(ci * T, T)
        lanes = pl.ds(c0, T)
        for h in range(H):
            sidx = d * H + h
            hl = slice(h * M_DH, (h + 1) * M_DH)
            tr = slice(h * T, (h + 1) * T)
            rows = rows_scr[slot, d, 4 * h:4 * h + 4, :]
            b_row, mloc, mloc_last, b_last = rows[0:1], rows[1:2], rows[2:3], rows[3:4]
            m_row = m_scr[sidx]
            inter = b_row + m_row
            mt = jnp.maximum(inter, mloc)
            kh = k_ref[pl.ds(c0, T), hl]
            qt = qt_ref[hl, lanes]
            vext = jnp.concatenate([vt_ref[hl, lanes], ones_blk.astype(BF16)], axis=0)
            ct = c_scr[sidx]
            tot = (jnp.exp(mloc - mt) * _dot(vext, sb_scr[slot, d, tr, :])
                   + jnp.exp(inter - mt) * _dot(ct.astype(BF16), qt))
            den = tot[M_DH:M_DH + 1]
            ht = tot[0:M_DH] / jnp.maximum(jnp.abs(den), jnp.exp(-mt))
            if accumulate:
                h_scr[hl, lanes] = h_scr[hl, lanes] + ht
            else:
                h_scr[hl, lanes] = ht
            m_new = jnp.maximum(b_last + m_row, mloc_last)
            c_scr[sidx] = (jnp.exp(b_last + m_row - m_new) * ct
                           + jnp.exp(mloc_last - m_new) * _dot(vw_scr[slot, d, h], kh))
            m_scr[sidx] = m_new

    def pair(p, slots, accumulate=None):
        s0, s1 = slots
        for c, slot in ((2 * p, s0), (2 * p + 1, s1)):
            acc = (c >= nc // 2) if accumulate is None else accumulate
            update(c, 0, slot, acc)
            update(nc - 1 - c, 1, slot, acc)
        for c, slot in ((2 * p + 2, 2 - s0), (2 * p + 3, 4 - s1)):
            gates(jnp.minimum(c, nc - 1), 0, slot)
            gates(jnp.maximum(nc - 1 - c, 0), 1, slot)

    for c in (0, 1):
        gates(c, 0, c)
        gates(nc - 1 - c, 1, c)
    if nc % 8 == 0:
        def body(accumulate):
            def run(i, carry):
                pair(2 * i, (0, 1), accumulate)
                pair(2 * i + 1, (2, 3), accumulate)
                return carry
            return run
        lax.fori_loop(0, nc // 8, body(False), 0)
        lax.fori_loop(nc // 8, nc // 4, body(True), 0)
    else:
        for p in range(nc // 2):
            pair(p, ((0, 1), (2, 3))[p % 2])

    def epilogue(i, carry):
        c0 = pl.multiple_of(i * T, T)
        for h in range(H):
            hl = slice(h * M_DH, (h + 1) * M_DH)
            hh = h_scr[hl, pl.ds(c0, T)]
            hn = hh * lax.rsqrt(jnp.mean(hh * hh, axis=0, keepdims=True) + EPS)
            y = hn.T * ng_ref[:, hl] * _sigmoid(o_ref[pl.ds(c0, T), hl])
            y_ref[pl.ds(c0, T), hl] = y.astype(y_ref.dtype)
        return carry

    lax.fori_loop(0, nc, epilogue, 0)
    cst_ref[0] = c_scr[...]
    mst_ref[0] = m_scr[...]


def _mlstm(qt, k, vt, o, gt, gate_b, norm_g, c0, m0, B, L):
    assert L % (2 * SCAN_T) == 0 and M_DH == SCAN_T
    nc = L // SCAN_T
    ng = 4 * M_HEADS
    M = B * L
    one = pl.Buffered(1)
    st_spec = pl.BlockSpec((1, 2 * M_HEADS, 2 * M_DH, M_DH), lambda b: (b, 0, 0, 0))
    m_spec = pl.BlockSpec((1, 2 * M_HEADS, 1, M_DH), lambda b: (b, 0, 0, 0))
    return pl.pallas_call(
        functools.partial(_mlstm_kernel, nc=nc),
        grid=(B,),
        in_specs=[pl.BlockSpec((M_W, L), lambda b: (0, b), pipeline_mode=one),
                  pl.BlockSpec((L, M_W), lambda b: (b, 0), pipeline_mode=one),
                  pl.BlockSpec((M_W, L), lambda b: (0, b), pipeline_mode=one),
                  pl.BlockSpec((L, M_W), lambda b: (b, 0), pipeline_mode=one),
                  pl.BlockSpec((ng, L), lambda b: (0, b)),
                  pl.BlockSpec((ng, SCAN_T), lambda b: (0, 0)),
                  pl.BlockSpec((1, M_W), lambda b: (0, 0)),
                  st_spec, m_spec],
        out_specs=[pl.BlockSpec((L, M_W), lambda b: (b, 0)), st_spec, m_spec],
        out_shape=[jax.ShapeDtypeStruct((M, M_W), BF16),
                   jax.ShapeDtypeStruct((B, 2 * M_HEADS, 2 * M_DH, M_DH), F32),
                   jax.ShapeDtypeStruct((B, 2 * M_HEADS, 1, M_DH), F32)],
        scratch_shapes=[pltpu.VMEM((M_W, L), F32),
                        pltpu.VMEM((2 * M_HEADS, 2 * M_DH, M_DH), F32),
                        pltpu.VMEM((2 * M_HEADS, 1, M_DH), F32),
                        pltpu.VMEM((4, 2, M_HEADS * SCAN_T, SCAN_T), BF16),
                        pltpu.VMEM((4, 2, M_HEADS, 2 * M_DH, SCAN_T), BF16),
                        pltpu.VMEM((4, 2, 4 * M_HEADS, SCAN_T), F32)],
        compiler_params=_cparams("arbitrary"),
        name="mlstm",
    )(qt, k, vt, o, gt, jnp.broadcast_to(gate_b.reshape(ng, 1), (ng, SCAN_T)), norm_g.reshape(1, M_W), c0, m0)


def _ssd_kernel(xt_ref, b_ref, ct_ref, dtt_ref, db_ref, al_ref, dsk_ref, s0_ref,
                y_ref, sout_ref, yf_scr, yb_scr, s_scr, cbt_scr, p_scr, xd_scr, xdec_scr, rows_scr, *, nc):
    T = SCAN_T
    H = SSD_HPG
    s_scr[...] = s0_ref[0, :, 0]
    consts = (_scan_consts(0), _scan_consts(1))
    db = db_ref[0]
    a_neg = -jnp.exp(al_ref[0])

    def gates(ci, d, slot):
        mask, v_mat, last = consts[d]
        c0 = pl.multiple_of(ci * T, T)
        lanes = pl.ds(c0, T)
        hr = slice(d * H, (d + 1) * H)
        dt = _softplus(dtt_ref[hr, lanes] + db[hr])
        a = dt * a_neg[hr]
        acs = _dot2(a, v_mat)
        a_last = jnp.broadcast_to(acs[:, last:last + 1], (H, T))
        rows_scr[slot, d, 0:H, :] = jnp.exp(acs)
        rows_scr[slot, d, H:2 * H, :] = jnp.exp(a_last)
        cbt_scr[d] = _dot(b_ref[pl.ds(c0, T), :], ct_ref[:, lanes])
        for r in range(H):
            tr = slice(r * T, (r + 1) * T)
            pr = slice(r * SSD_HEADDIM, (r + 1) * SSD_HEADDIM)
            al = jnp.broadcast_to(acs[r:r + 1], (T, T))
            lm = jnp.exp(jnp.where(mask, al - al.T, -jnp.inf))
            p_scr[slot, d, tr, :] = (cbt_scr[d] * lm).astype(BF16)
            xd = xt_ref[pr, lanes] * dt[r:r + 1]
            xd_scr[slot, d, pr, :] = xd.astype(BF16)
            xdec_scr[slot, d, pr, :] = (xd * jnp.exp(a_last[r:r + 1] - acs[r:r + 1])).astype(BF16)

    def update(ci, d, slot):
        c0 = pl.multiple_of(ci * T, T)
        lanes = pl.ds(c0, T)
        rows = rows_scr[slot, d]
        sc = _dot(s_scr[d].astype(BF16), ct_ref[:, lanes])
        upd = _dot(xdec_scr[slot, d], b_ref[pl.ds(c0, T), :])
        y_scr = yf_scr if d == 0 else yb_scr
        for r in range(H):
            pr = slice(r * SSD_HEADDIM, (r + 1) * SSD_HEADDIM)
            tr = slice(r * T, (r + 1) * T)
            y_scr[pr, lanes] = _dot(xd_scr[slot, d, pr, :], p_scr[slot, d, tr, :]) + rows[r:r + 1] * sc[pr]
            s_scr[d, pr, :] = rows[H + r:H + r + 1] * s_scr[d, pr, :] + upd[pr]

    def pair(p, slots):
        s0, s1 = slots
        for c, slot in ((2 * p, s0), (2 * p + 1, s1)):
            update(c, 0, slot)
            update(nc - 1 - c, 1, slot)
        for c, slot in ((2 * p + 2, 2 - s0), (2 * p + 3, 4 - s1)):
            gates(jnp.minimum(c, nc - 1), 0, slot)
            gates(jnp.maximum(nc - 1 - c, 0), 1, slot)

    for c in (0, 1):
        gates(c, 0, c)
        gates(nc - 1 - c, 1, c)
    if nc % 4 == 0:
        def body(i, carry):
            pair(2 * i, (0, 1))
            pair(2 * i + 1, (2, 3))
            return carry
        lax.fori_loop(0, nc // 4, body, 0)
    else:
        for p in range(nc // 2):
            pair(p, ((0, 1), (2, 3))[p % 2])

    dsk = dsk_ref[0]

    def epilogue(i, carry):
        c0 = pl.multiple_of(i * T, T)
        lanes = pl.ds(c0, T)
        yt = yf_scr[:, lanes] + yb_scr[:, lanes] + dsk * xt_ref[:, lanes]
        y_ref[pl.ds(c0, T), :] = yt.T
        return carry

    lax.fori_loop(0, nc, epilogue, 0)
    sout_ref[0, :, 0] = s_scr[...]


def _ssd(xt, bm, ct, dtt, dt_bias, a_log, dskip, s0, B, L):
    assert L % (2 * SCAN_T) == 0 and SSD_STATE == SCAN_T
    nc = L // SCAN_T
    G = SSD_GROUPS
    nd = 2 * SSD_HPG
    M = B * L
    rep = lambda p, rows: jnp.broadcast_to(p.reshape(G, rows, 1), (G, rows, SCAN_T))
    small = lambda rows: pl.BlockSpec((1, rows, SCAN_T), lambda b, g: (g, 0, 0))
    st_spec = pl.BlockSpec((1, 2, 1, SSD_GW, SSD_STATE), lambda b, g: (b, 0, g, 0, 0))
    return pl.pallas_call(
        functools.partial(_ssd_kernel, nc=nc),
        grid=(B, G),
        in_specs=[pl.BlockSpec((SSD_GW, L), lambda b, g: (g, b)),
                  pl.BlockSpec((L, SSD_STATE), lambda b, g: (b, g)),
                  pl.BlockSpec((SSD_STATE, L), lambda b, g: (g, b)),
                  pl.BlockSpec((nd, L), lambda b, g: (g, b)),
                  small(nd), small(nd), small(SSD_GW), st_spec],
        out_specs=[pl.BlockSpec((L, SSD_GW), lambda b, g: (b, g)), st_spec],
        out_shape=[jax.ShapeDtypeStruct((M, SSD_INNER), F32),
                   jax.ShapeDtypeStruct((B, 2, G, SSD_GW, SSD_STATE), F32)],
        scratch_shapes=[pltpu.VMEM((SSD_GW, L), F32), pltpu.VMEM((SSD_GW, L), F32),
                        pltpu.VMEM((2, SSD_GW, SSD_STATE), F32),
                        pltpu.VMEM((2, SCAN_T, SCAN_T), F32),
                        pltpu.VMEM((4, 2, SSD_HPG * SCAN_T, SCAN_T), BF16),
                        pltpu.VMEM((4, 2, SSD_GW, SCAN_T), BF16),
                        pltpu.VMEM((4, 2, SSD_GW, SCAN_T), BF16),
                        pltpu.VMEM((4, 2, 2 * SSD_HPG, SCAN_T), F32)],
        compiler_params=_cparams("arbitrary", "arbitrary"),
        name="ssd",
    )(xt, bm, ct, dtt, rep(dt_bias, nd), rep(a_log, nd), rep(dskip, SSD_GW), s0)


def _hy_factors(L):
    return (128, 64) if L == 4096 else (32, (2 * L) // 32)


def _hilo(t):
    return _split2(t.astype(F32))


def _cblock(cr, ci):
    return jnp.concatenate([jnp.concatenate([cr, -ci], axis=-1), jnp.concatenate([ci, cr], axis=-1)], axis=-2)


def _dft_tables(L):
    n1c, n2c = _hy_factors(L)
    N = n1c * n2c
    h1 = n1c // 2
    i32 = jnp.int32
    k1 = jnp.arange(n1c, dtype=i32)
    n2 = jnp.arange(n2c, dtype=i32)

    def stage1(n1_count):
        n1 = jnp.arange(n1_count, dtype=i32)
        idx = (k1[None, :, None] * (n2[:, None, None] + n2c * n1[None, None, :])) % N
        ang = idx.astype(F32) * (2.0 * math.pi / N)
        return jnp.cos(ang), jnp.sin(ang)

    c, s = stage1(h1)
    g_fwd = _cblock(c, -s)
    g_inv = _cblock(jnp.swapaxes(c, 1, 2), jnp.swapaxes(s, 1, 2)) / n1c
    cf, sf = stage1(n1c)
    g_real = jnp.concatenate([cf, -sf], axis=1)
    idx2 = (n2[:, None] * n2[None, :]) % n2c
    ang2 = idx2.astype(F32) * (2.0 * math.pi / n2c)
    c2, s2 = jnp.cos(ang2), jnp.sin(ang2)
    f2 = _cblock(c2, -s2)
    f2_inv = _cblock(c2, s2) / n2c
    return dict(g_fwd=g_fwd.astype(BF16), g_inv=g_inv.astype(BF16), f2=f2.astype(BF16),
                f2_inv=f2_inv.astype(BF16), g_real=_hilo(g_real), f2_exact=_hilo(f2))


def _swap01(x):
    return jnp.swapaxes(x, 0, 1)


def _hy_s1_kernel(x_ref, *rest, nb, exact):
    if exact:
        gh_ref, gl_ref, o_ref = rest
    else:
        g_ref, o_ref = rest
    pb, rows, _, C = x_ref.shape
    xs = _swap01(x_ref[...].reshape(pb * rows, nb, C))
    if exact:
        res = [_dot3(gh_ref[j], gl_ref[j], xs[j]) for j in range(nb)]
    else:
        res = [_dot(g_ref[j], xs[j].astype(BF16)) for j in range(nb)]
    out = _swap01(jnp.stack(res, axis=0))
    o_ref[...] = out.reshape(o_ref.shape).astype(o_ref.dtype)


def _hy_stage1(x, tab, out_dtype):
    exact = isinstance(tab, tuple)
    tabs = tab if exact else (tab,)
    n2c, rows_out, R = tabs[0].shape
    Bx, rows, _, C = x.shape
    pb = R // rows
    nb = HY_NB
    cb = min(C, HY_W)
    tspec = pl.BlockSpec((nb, rows_out, R), lambda p, t, c: (t, 0, 0))
    return pl.pallas_call(
        functools.partial(_hy_s1_kernel, nb=nb, exact=exact),
        grid=(Bx // pb, n2c // nb, C // cb),
        in_specs=[pl.BlockSpec((pb, rows, nb, cb), lambda p, t, c: (p, 0, t, c))] + [tspec] * len(tabs),
        out_specs=pl.BlockSpec((None, 2, rows_out // 2, nb, cb), lambda p, t, c: (p, 0, 0, t, c)),
        out_shape=jax.ShapeDtypeStruct((Bx // pb, 2, rows_out // 2, n2c, C), out_dtype),
        compiler_params=_cparams("arbitrary", "arbitrary", "arbitrary"),
        name="hyena_stage1",
    )(x, *tabs)


def _hy_spec_kernel(a_ref, inv_ref, fh_ref, fl_ref, o_ref, *, kb):
    n2c = a_ref.shape[2]
    for kk in range(kb):
        rhs = jnp.concatenate([a_ref[0, kk], a_ref[1, kk]], axis=0)
        X = _dot3(fh_ref[...], fl_ref[...], rhs) * inv_ref[...]
        o_ref[0, kk] = X[0:n2c]
        o_ref[1, kk] = X[n2c:]


def _hy_filter_spectrum(a, inv_norm, f2, kb):
    _, n1c, n2c, W = a.shape
    fh, fl = f2
    blk = pl.BlockSpec((2, kb, n2c, W), lambda t: (0, t, 0, 0))
    fspec = pl.BlockSpec((2 * n2c, 2 * n2c), lambda t: (0, 0))
    return pl.pallas_call(
        functools.partial(_hy_spec_kernel, kb=kb),
        grid=(n1c // kb,),
        in_specs=[blk, pl.BlockSpec((1, W), lambda t: (0, 0)), fspec, fspec],
        out_specs=blk,
        out_shape=jax.ShapeDtypeStruct(a.shape, F32),
        compiler_params=_cparams("arbitrary"),
        name="hyena_filter_spectrum",
    )(a, inv_norm, fh, fl)


def _hy_mid_kernel(a_ref, k_ref, f_ref, i_ref, o_ref, *, kb):
    n2c = a_ref.shape[2]
    res = []
    for kk in range(kb):
        rhs = jnp.concatenate([a_ref[0, kk], a_ref[1, kk]], axis=0)
        X = _dot(f_ref[...], rhs)
        xr, xi = X[0:n2c], X[n2c:]
        kr, ki = k_ref[0, kk], k_ref[1, kk]
        Y = jnp.concatenate([xr * kr - xi * ki, xr * ki + xi * kr], axis=0)
        res.append(_dot(i_ref[...], Y.astype(BF16)))
    o_ref[...] = _swap01(jnp.stack(res, axis=0)).reshape(o_ref.shape).astype(o_ref.dtype)


def _hy_mid(a, kspec, order, f2, f2_inv):
    P, _, n1c, n2c, C = a.shape
    kb = HY_NB
    fspec = pl.BlockSpec((2 * n2c, 2 * n2c), lambda t, p: (0, 0))
    return pl.pallas_call(
        functools.partial(_hy_mid_kernel, kb=kb),
        grid=(n1c // kb, P),
        in_specs=[pl.BlockSpec((None, 2, kb, n2c, C), lambda t, p: (p, 0, t, 0, 0)),
                  pl.BlockSpec((2, kb, n2c, C), lambda t, p: (0, t, 0, order)),
                  fspec, fspec],
        out_specs=pl.BlockSpec((None, 2, n2c, kb, C), lambda t, p: (p, 0, 0, t, 0)),
        out_shape=jax.ShapeDtypeStruct((P, 2, n2c, n1c, C), BF16),
        compiler_params=_cparams("arbitrary", "arbitrary"),
        name="hyena_mid",
    )(a, kspec, f2, f2_inv)


def _hy_last_kernel(b_ref, z_ref, gate_ref, bias_ref, gi_ref, *rest, fuse, nb):
    if fuse:
        gf_ref, zo_ref, ao_ref = rest
    else:
        (zo_ref,) = rest
    _, h1, _, C = z_ref.shape
    zs = _swap01(z_ref[...].reshape(2 * h1, nb, C))
    gs = _swap01(gate_ref[...].reshape(2 * h1, nb, C))
    bias = bias_ref[...]
    zn, an = [], []
    for j in range(nb):
        bv = jnp.concatenate([b_ref[0, j], b_ref[1, j]], axis=0)
        y = _dot(gi_ref[j], bv)
        znew = gs[j] * (y + zs[j] * bias)
        zn.append(znew)
        if fuse:
            an.append(_dot(gf_ref[j], znew.astype(BF16)))
    zo_ref[...] = _swap01(jnp.stack(zn, axis=0)).reshape(zo_ref.shape).astype(zo_ref.dtype)
    if fuse:
        ao_ref[...] = _swap01(jnp.stack(an, axis=0)).reshape(ao_ref.shape).astype(ao_ref.dtype)


def _hy_last(bf, z, gate, bias, g_inv, g_fwd, out_dtype):
    P, _, n2c, n1c, C = bf.shape
    B, h1, _, _ = z.shape
    nb = HY_NB
    cb = C // 2
    fuse = g_fwd is not None
    zspec = pl.BlockSpec((2, h1, nb, cb), lambda p, t, c: (p, 0, t, c))
    in_specs = [pl.BlockSpec((None, 2, nb, n1c, cb), lambda p, t, c: (p, 0, t, 0, c)),
                zspec, zspec, pl.BlockSpec((1, cb), lambda p, t, c: (0, c)),
                pl.BlockSpec((nb, 2 * h1, 2 * n1c), lambda p, t, c: (t, 0, 0))]
    args = [bf, z, gate, bias, g_inv]
    out_specs = [zspec]
    out_shape = [jax.ShapeDtypeStruct(z.shape, out_dtype)]
    if fuse:
        in_specs.append(pl.BlockSpec((nb, 2 * n1c, 2 * h1), lambda p, t, c: (t, 0, 0)))
        args.append(g_fwd)
        out_specs.append(pl.BlockSpec((None, 2, n1c, nb, cb), lambda p, t, c: (p, 0, 0, t, c)))
        out_shape.append(jax.ShapeDtypeStruct((P, 2, n1c, n2c, C), BF16))
    return pl.pallas_call(
        functools.partial(_hy_last_kernel, fuse=fuse, nb=nb),
        grid=(P, n2c // nb, C // cb),
        in_specs=in_specs, out_specs=out_specs, out_shape=out_shape,
        compiler_params=_cparams("arbitrary", "arbitrary", "arbitrary"),
        name="hyena_last",
    )(*args)


def _hy_filter_kernel(f_ref, w1_ref, b1_ref, w2_ref, b2_ref, w3_ref, fr_ref, dl_ref, hk_ref, s_ref, *, L):
    i = pl.program_id(0)
    tl = f_ref.shape[0]
    f = f_ref[...]
    fr = fr_ref[...]
    h = jnp.sin(fr * (_dot(f.astype(BF16), w1_ref[...].astype(BF16)) + b1_ref[...]))
    h = jnp.sin(fr * (_dot(h.astype(BF16), w2_ref[...].astype(BF16)) + b2_ref[...]))
    hk = _dot(h.astype(BF16), w3_ref[...].astype(BF16)) * jnp.exp(-f[:, 0:1] * dl_ref[...])
    n = i * tl + lax.broadcasted_iota(jnp.int32, hk.shape, 0)
    hk = jnp.where(n == L, 0.0, hk)
    hk_ref[...] = hk
    tot = jnp.sum(jnp.abs(hk), axis=0, keepdims=True)

    @pl.when(i == 0)
    def _():
        s_ref[...] = tot

    @pl.when(i != 0)
    def _():
        s_ref[...] = s_ref[...] + tot


def _hy_filter(L, w1, b1, w2, b2, w3, freq):
    f32 = F32
    t = jnp.linspace(0.0, 1.0, L, dtype=f32)[:, None]
    ang = (2.0 * math.pi / L) * jnp.arange(L, dtype=f32)[:, None] * \
        jnp.linspace(1e-4, HY_BANDS - 1, HY_BANDS, dtype=f32)[None]
    feats = jnp.concatenate([t, jnp.cos(ang), -jnp.sin(ang)], axis=-1)
    feats = jnp.concatenate([feats, feats[0:1], feats[:0:-1]], axis=0)
    fpad = 128
    feats = jnp.pad(feats, ((0, 0), (0, fpad - HY_EMB)))
    w1p = jnp.pad(w1, ((0, fpad - HY_EMB), (0, 0)))
    deltas = jnp.abs(jnp.linspace(math.log(HY_TARGET) / HY_SLOW, math.log(HY_TARGET) / HY_FAST, HY_W, dtype=f32))
    OC = HY_ORDER * HY_W
    dl = jnp.tile(deltas, HY_ORDER).reshape(1, OC)
    tl = min(L, 512)
    nblk = L // tl
    full = lambda shp: pl.BlockSpec(shp, lambda i: (0, 0))
    return pl.pallas_call(
        functools.partial(_hy_filter_kernel, L=L),
        grid=(2 * nblk,),
        in_specs=[pl.BlockSpec((tl, fpad), lambda i: (i, 0)),
                  full((fpad, HY_HID)), full((1, HY_HID)), full((HY_HID, HY_HID)), full((1, HY_HID)),
                  pl.BlockSpec((HY_HID, OC), lambda i: (0, i // nblk)), full((1, HY_HID)), full((1, OC))],
        out_specs=[pl.BlockSpec((tl, OC), lambda i: (i, 0)), full((1, OC))],
        out_shape=[jax.ShapeDtypeStruct((2 * L, OC), F32), jax.ShapeDtypeStruct((1, OC), F32)],
        compiler_params=_cparams("arbitrary"),
        name="hyena_filter",
    )(feats, w1p, b1.reshape(1, HY_HID), w2, b2.reshape(1, HY_HID), w3, freq.reshape(1, HY_HID), dl)


def _hy_spectrum(L, tabs, w1, b1, w2, b2, w3, freq):
    n1c, n2c = _hy_factors(L)
    OC = HY_ORDER * HY_W
    ktime, s = _hy_filter(L, w1, b1, w2, b2, w3, freq)
    a = _hy_stage1(ktime.reshape(1, n1c, n2c, OC), tabs["g_real"], F32)[0]
    return _hy_filter_spectrum(a, 1.0 / s, tabs["f2_exact"], 8)


def _hyena(v, x1, x2, kspec, bias, tabs, B, L):
    n1c, n2c = _hy_factors(L)
    C = HY_W
    view = lambda t: t.reshape(B, n1c // 2, n2c, C)
    z0, g1, g2 = view(v), view(x1), view(x2)
    a = _hy_stage1(z0, tabs["g_fwd"], BF16)
    bf = _hy_mid(a, kspec, 0, tabs["f2"], tabs["f2_inv"])
    z1, a = _hy_last(bf, z0, g1, bias[0:1], tabs["g_inv"], tabs["g_fwd"], F32)
    bf = _hy_mid(a, kspec, 1, tabs["f2"], tabs["f2_inv"])
    (z2,) = _hy_last(bf, z1, g2, bias[1:2], tabs["g_inv"], None, BF16)
    return z2.reshape(B * L, C)


def _direct_tables(L):
    N = 2 * L
    k = jnp.arange(N, dtype=jnp.int32)
    ang = ((k[:, None] * k[None, :]) % N).astype(F32) * (2.0 * math.pi / N)
    c, s = jnp.cos(ang), jnp.sin(ang)
    fwd = _cblock(c[:, :L], -s[:, :L])
    inv = _cblock(c[:L, :], s[:L, :]) / N
    real = jnp.concatenate([c, -s], axis=0)
    return dict(fwd=fwd.astype(BF16), inv=inv.astype(BF16), real=_hilo(real))


def _hy_direct_spec_kernel(kt_ref, inv_ref, fh_ref, fl_ref, o_ref):
    N = kt_ref.shape[0]
    X = _dot3(fh_ref[...], fl_ref[...], kt_ref[...]) * inv_ref[...]
    o_ref[0] = X[0:N]
    o_ref[1] = X[N:]


def _hy_direct_spectrum(L, tabs, w1, b1, w2, b2, w3, freq):
    N = 2 * L
    OC = HY_ORDER * HY_W
    cb = HY_W // 2
    ktime, s = _hy_filter(L, w1, b1, w2, b2, w3, freq)
    fh, fl = tabs["real"]
    fspec = pl.BlockSpec((2 * N, N), lambda c: (0, 0))
    return pl.pallas_call(
        _hy_direct_spec_kernel,
        grid=(OC // cb,),
        in_specs=[pl.BlockSpec((N, cb), lambda c: (0, c)), pl.BlockSpec((1, cb), lambda c: (0, c)), fspec, fspec],
        out_specs=pl.BlockSpec((2, N, cb), lambda c: (0, 0, c)),
        out_shape=jax.ShapeDtypeStruct((2, N, OC), F32),
        compiler_params=_cparams("arbitrary"),
        name="hyena_direct_spectrum",
    )(ktime, 1.0 / s, fh, fl)


def _hy_direct_kernel(v_ref, g1_ref, g2_ref, k0_ref, k1_ref, bias_ref, f_ref, fi_ref, o_ref):
    L = v_ref.shape[1]
    N = 2 * L
    z = jnp.concatenate([v_ref[0], v_ref[1]], axis=0)
    for o, (k_ref, g_ref) in enumerate(((k0_ref, g1_ref), (k1_ref, g2_ref))):
        X = _dot(f_ref[...], z.astype(BF16))
        xr, xi = X[0:N], X[N:]
        kr, ki = k_ref[0], k_ref[1]
        Y = jnp.concatenate([xr * kr - xi * ki, xr * ki + xi * kr], axis=0)
        y = _dot(fi_ref[...], Y.astype(BF16))
        gate = jnp.concatenate([g_ref[0], g_ref[1]], axis=0)
        z = gate * (y + z * bias_ref[o:o + 1])
    o_ref[0] = z[0:L].astype(o_ref.dtype)
    o_ref[1] = z[L:].astype(o_ref.dtype)


def _hyena_direct(v, x1, x2, kspec, bias, tabs, B, L):
    C = HY_W
    N = 2 * L
    cb = C // 2
    ncb = C // cb
    seq = pl.BlockSpec((2, L, cb), lambda p, c: (p, 0, c))
    view = lambda t: t.reshape(B, L, C)
    out = pl.pallas_call(
        _hy_direct_kernel,
        grid=(B // 2, ncb),
        in_specs=[seq, seq, seq,
                  pl.BlockSpec((2, N, cb), lambda p, c: (0, 0, c)),
                  pl.BlockSpec((2, N, cb), lambda p, c: (0, 0, ncb + c)),
                  pl.BlockSpec((HY_ORDER, cb), lambda p, c: (0, c)),
                  pl.BlockSpec((2 * N, 2 * L), lambda p, c: (0, 0)),
                  pl.BlockSpec((2 * L, 2 * N), lambda p, c: (0, 0))],
        out_specs=seq,
        out_shape=jax.ShapeDtypeStruct((B, L, C), BF16),
        compiler_params=_cparams("arbitrary", "arbitrary"),
        name="hyena_direct",
    )(view(v), view(x1), view(x2), kspec, kspec, bias, tabs["fwd"], tabs["inv"])
    return out.reshape(B * L, C)


HY_DIRECT_MAX_L = 512


_EVEN_PLAN = (
    (0, HY_W, "conv", 1.0, False, F32),
    (HY_W, HY_W, "conv", 1.0, False, F32),
    (2 * HY_W, HY_W, "conv", 1.0, False, F32),
    (3 * HY_W, M_W, "conv_silu", 1.0, True, BF16),
    (3 * HY_W + M_W, M_W, "conv_silu", M_DH ** -0.5, False, BF16),
    (EV_CONV + M_W, M_W, "raw", 1.0, False, F32),
)
_EVEN_NT_PLAN = ((0, M_W, BF16), (M_W, 4 * M_HEADS, F32))
_ODD_NT_PLAN = ((0, 2 * SSD_HEADS, F32),)
_GN = SSD_GROUPS * SSD_STATE
_ODD_PLAN = (
    (0, SSD_INNER, "raw", 1.0, False, F32),
    (SSD_INNER, SSD_INNER, "conv_silu", 1.0, True, F32),
    (2 * SSD_INNER, _GN, "conv_silu", 1.0, False, BF16),
    (2 * SSD_INNER + _GN, _GN, "conv_silu", 1.0, True, BF16),
)


def _even_layer(x, mods_l, P, e, l, B, L, st, per_batch_cond, kspec, tabs):
    hv, hx1, hx2, qt, k, o, vt, gt = _inproj(
        x, mods_l, P["norm1_g"][l:l + 1], P["ev_in_w"], P["ev_in_wt"], P["ev_conv_w"], P["ev_conv_b"], e,
        plan=_EVEN_PLAN, nt_plan=_EVEN_NT_PLAN, conv0=0, seq_len=L, per_batch_cond=per_batch_cond)
    hyena = _hyena_direct if L <= HY_DIRECT_MAX_L else _hyena
    y_hy = hyena(hv, hx1, hx2, kspec, P["hy_bias"][e], tabs, B, L)
    c0, m0 = st
    y_m, c_new, m_new = _mlstm(qt, k, vt, o, gt, P["m_gate_b"][e], P["m_norm_g"][e], c0, m0, B, L)
    x = _outproj_even(y_hy, y_m, x, mods_l, P["ev_out_w"], e, seq_len=L, per_batch_cond=per_batch_cond)
    return x, (c_new, m_new)


def _by_group(p):
    return p.reshape(2, SSD_GROUPS, SSD_HPG).transpose(1, 0, 2).reshape(SSD_GROUPS, 2 * SSD_HPG)


def _odd_layer(x, mods_l, P, o, l, B, L, s0, per_batch_cond):
    z, xt, bm, ct, dtt = _inproj(
        x, mods_l, P["norm1_g"][l:l + 1], P["od_in_w"], P["od_in_wt"], P["od_conv_w"], P["od_conv_b"], o,
        plan=_ODD_PLAN, nt_plan=_ODD_NT_PLAN, conv0=SSD_INNER, seq_len=L, per_batch_cond=per_batch_cond)
    dskip = jnp.repeat(P["ssd_D"][o], SSD_HEADDIM).reshape(SSD_GROUPS, SSD_GW)
    y, s_new = _ssd(xt, bm, ct, dtt, _by_group(P["ssd_dt_bias"][o]), _by_group(P["ssd_A_log"][o]), dskip, s0, B, L)
    x = _outproj_odd(y, z, x, mods_l, P["ssd_norm_g"], P["od_out_w"], o, seq_len=L, per_batch_cond=per_batch_cond)
    return x, s_new


def _to_colmajor(x, B, L):
    rows = L // GRID_W
    return x.reshape(B, rows, GRID_W, D_MODEL).swapaxes(1, 2).reshape(B * L, D_MODEL)


def _from_colmajor(x, B, L):
    rows = L // GRID_W
    return x.reshape(B, GRID_W, rows, D_MODEL).swapaxes(1, 2).reshape(B * L, D_MODEL)


def _trunk(x, mods, P, B, L, init_states, per_batch_cond, grid, kspecs, tabs):
    states = []
    for l in range(DEPTH):
        if grid and l == 2:
            x = _to_colmajor(x, B, L)
        mods_l = mods[l]
        if l % 2 == 0:
            x, st = _even_layer(x, mods_l, P, l // 2, l, B, L, init_states[l], per_batch_cond, kspecs[l // 2], tabs)
        else:
            x, st = _odd_layer(x, mods_l, P, l // 2, l, B, L, init_states[l], per_batch_cond)
        states.append(st)
        x = _ffn(x, mods_l, P["norm2_g"].reshape(DEPTH, 1, D_MODEL), P["ffn_w1"], P["ffn_w3"], P["ffn_w2"],
                 P["final_g"].reshape(1, D_MODEL), l, seq_len=L, per_batch_cond=per_batch_cond,
                 final=(l == DEPTH - 1))
    if grid:
        x = _from_colmajor(x, B, L)
    return x, states


def _pack_mlstm_state(C, n, m):
    B = C.shape[0]
    ext = jnp.concatenate([jnp.swapaxes(C, -1, -2), n[..., None, :],
                           jnp.zeros(C.shape[:-2] + (M_DH - 1, M_DH), F32)], axis=-2)
    m_rep = jnp.broadcast_to(m.reshape(B, 2 * M_HEADS, 1, 1), (B, 2 * M_HEADS, 1, M_DH))
    return ext.reshape(B, 2 * M_HEADS, 2 * M_DH, M_DH), m_rep


def kernel(x_prompt, x_sample, state_mlstm_C, state_mlstm_n, state_mlstm_m, state_ssd, c, c_ctx,
           mod_w, mod_b, norm1_g, norm2_g, ffn_w1, ffn_w3, ffn_w2, final_g,
           ev_in_w, ev_conv_w, ev_conv_b, hy_w1, hy_b1, hy_w2, hy_b2, hy_w3, hy_freq, hy_bias,
           m_gate_b, m_norm_g, ev_out_w,
           od_in_w, od_conv_w, od_conv_b, ssd_dt_bias, ssd_A_log, ssd_D, ssd_norm_g, od_out_w):
    Bp, Lp, _ = x_prompt.shape
    Bs, Ls, _ = x_sample.shape
    bf = lambda w: w.astype(BF16)
    n_odd = od_in_w.shape[0]
    od_dt_w = od_in_w[:, :, OD_PROJ - 2 * SSD_HEADS:].reshape(n_odd, D_MODEL, 2, SSD_GROUPS, SSD_HPG)
    od_dt_w = od_dt_w.transpose(0, 3, 2, 4, 1).reshape(n_odd, 2 * SSD_HEADS, D_MODEL)
    P = dict(norm1_g=norm1_g, norm2_g=norm2_g, ffn_w1=bf(ffn_w1), ffn_w3=bf(ffn_w3), ffn_w2=bf(ffn_w2),
             final_g=final_g, ev_in_w=bf(ev_in_w),
             ev_in_wt=bf(jnp.concatenate([ev_in_w[:, :, EV_CONV:EV_CONV + M_W],
                                          ev_in_w[:, :, EV_PROJ - 4 * M_HEADS:]], axis=2)).swapaxes(1, 2),
             ev_conv_w=ev_conv_w, ev_conv_b=ev_conv_b, hy_bias=hy_bias, m_gate_b=m_gate_b, m_norm_g=m_norm_g,
             ev_out_w=bf(ev_out_w), od_in_w=bf(od_in_w), od_in_wt=bf(od_dt_w),
             od_conv_w=od_conv_w, od_conv_b=od_conv_b, ssd_dt_bias=ssd_dt_bias, ssd_A_log=ssd_A_log,
             ssd_D=ssd_D, ssd_norm_g=ssd_norm_g.reshape(-1, 1, SSD_INNER), od_out_w=bf(od_out_w))

    ncond = 16
    cond = jnp.concatenate([c_ctx[None, :], c, jnp.zeros((ncond - 1 - Bs, D_MODEL), F32)], axis=0)
    mods = _modulation(cond, mod_w, mod_b).reshape(DEPTH, ncond, 6, D_MODEL)

    tabs, kspecs = {}, {}
    for L in (Lp, Ls):
        direct = L <= HY_DIRECT_MAX_L
        tabs[L] = _direct_tables(L) if direct else _dft_tables(L)
        spectrum = _hy_direct_spectrum if direct else _hy_spectrum
        kspecs[L] = [spectrum(L, tabs[L], hy_w1[e], hy_b1[e], hy_w2[e], hy_b2[e], hy_w3[e], hy_freq[e])
                     for e in range((DEPTH + 1) // 2)]

    G = SSD_GROUPS
    zero_m = _pack_mlstm_state(jnp.zeros((Bp, 2, M_HEADS, M_DH, M_DH), F32), jnp.zeros((Bp, 2, M_HEADS, M_DH), F32),
                               jnp.zeros((Bp, 2, M_HEADS), F32))
    zero_s = jnp.zeros((Bp, 2, G, SSD_GW, SSD_STATE), F32)
    ctx_init = [zero_m if l % 2 == 0 else zero_s for l in range(DEPTH)]
    y_prompt, ctx_states = _trunk(x_prompt.reshape(Bp * Lp, D_MODEL), mods, P, Bp, Lp, ctx_init, False, False,
                                  kspecs[Lp], tabs[Lp])

    lat_init = []
    for l in range(DEPTH):
        if l % 2 == 0:
            e = l // 2
            lat_init.append(_pack_mlstm_state(state_mlstm_C[:, e], state_mlstm_n[:, e], state_mlstm_m[:, e]))
        else:
            lat_init.append(state_ssd[:, l // 2].reshape(Bs, 2, G, SSD_GW, SSD_STATE))
    y_sample, _ = _trunk(x_sample.reshape(Bs * Ls, D_MODEL), mods, P, Bs, Ls, lat_init, True, True,
                         kspecs[Ls], tabs[Ls])

    cs = [ctx_states[l][0].reshape(Bp, 2, M_HEADS, 2 * M_DH, M_DH) for l in range(0, DEPTH, 2)]
    ms = [ctx_states[l][1][:, :, 0, 0].reshape(Bp, 2, M_HEADS) for l in range(0, DEPTH, 2)]
    new_C = jnp.stack([jnp.swapaxes(t[..., :M_DH, :], -1, -2) for t in cs], axis=1)
    new_n = jnp.stack([t[..., M_DH, :] for t in cs], axis=1)
    new_m = jnp.stack(ms, axis=1)
    new_ssd = jnp.stack([ctx_states[l].reshape(Bp, 2, SSD_HEADS, SSD_HEADDIM, SSD_STATE)
                         for l in range(1, DEPTH, 2)], axis=1)
    return (y_prompt.reshape(Bp, Lp, D_MODEL), y_sample.reshape(Bs, Ls, D_MODEL), new_C, new_n, new_m, new_ssd)
```

```python
import functools
import math

import jax
import jax.numpy as jnp
from jax import lax
from jax.experimental import pallas as pl
from jax.experimental.pallas import tpu as pltpu

F32 = jnp.float32
BF16 = jnp.bfloat16

D_MODEL = 1024
DEPTH = 4
GRID_W = 64
EPS = 1e-6
HY_W = D_MODEL // 2
HY_ORDER = 2
HY_EMB = 33
HY_BANDS = (HY_EMB - 1) // 2
HY_HID = 64
HY_TARGET = 1e-2
HY_FAST = 0.3
HY_SLOW = 1.5
M_HEADS = 4
M_DH = 128
M_W = M_HEADS * M_DH
EV_CONV = 3 * HY_W + 2 * M_W
EV_PROJ = EV_CONV + 2 * M_W + 4 * M_HEADS
SSD_INNER = 2 * D_MODEL
SSD_HEADDIM = 64
SSD_HEADS = SSD_INNER // SSD_HEADDIM
SSD_GROUPS = 8
SSD_HPG = SSD_HEADS // SSD_GROUPS
SSD_STATE = 128
SSD_GW = SSD_HPG * SSD_HEADDIM
SSD_CONV_DIM = SSD_INNER + 2 * SSD_GROUPS * SSD_STATE
OD_PROJ = SSD_INNER + SSD_CONV_DIM + 2 * SSD_HEADS
FFN_HIDDEN = -(-8 * D_MODEL // (3 * 256)) * 256

V7X_VMEM_BYTES = 64 * 1024 * 1024
VMEM_LIMIT = V7X_VMEM_BYTES * 3 // 4
ROW_TILE = 256
WIDE_TILE = 512
HALO = 16
COL_CHUNK = 512
HY_NB = 16
SCAN_T = 128
SSD_BLOCK_TOKENS = 2048
MLSTM_BLOCK_TOKENS = 1024


def _cparams(*sem):
    return pltpu.CompilerParams(dimension_semantics=sem, vmem_limit_bytes=VMEM_LIMIT)


def _dot(a, b):
    return jnp.dot(a, b, preferred_element_type=F32)


def _dot_nt(a, b):
    return lax.dot_general(a, b, (((1,), (1,)), ((), ())), preferred_element_type=F32)


def _split2(x):
    h = x.astype(BF16)
    return h, (x - h.astype(F32)).astype(BF16)


def _dot3(th, tl, x):
    xh, xl = _split2(x)
    return _dot(th, xh) + _dot(th, xl) + _dot(tl, xh)


def _dot2(x, t):
    xh, xl = _split2(x)
    return _dot(xh, t) + _dot(xl, t)


def _sigmoid(x):
    return 1.0 / (1.0 + jnp.exp(-x))


def _silu(x):
    return x * _sigmoid(x)


def _log_sigmoid(x):
    return jnp.minimum(x, 0.0) - jnp.log(1.0 + jnp.exp(-jnp.abs(x)))


def _softplus(x):
    return jnp.maximum(x, 0.0) + jnp.log(1.0 + jnp.exp(-jnp.abs(x)))


def _rms(x):
    return x * lax.rsqrt(jnp.mean(x * x, axis=-1, keepdims=True) + EPS)


def _mod_kernel(c_ref, w_ref, b_ref, o_ref):
    a = _silu(c_ref[...]).astype(BF16)
    o_ref[0] = _dot(a, w_ref[0].astype(BF16)) + b_ref[0]


def _modulation(cond, mod_w, mod_b):
    nc = cond.shape[0]
    tn = 1536
    return pl.pallas_call(
        _mod_kernel,
        grid=(DEPTH, 6 * D_MODEL // tn),
        in_specs=[pl.BlockSpec((nc, D_MODEL), lambda l, j: (0, 0)),
                  pl.BlockSpec((1, D_MODEL, tn), lambda l, j: (l, 0, j)),
                  pl.BlockSpec((1, 1, tn), lambda l, j: (l, 0, j))],
        out_specs=pl.BlockSpec((1, nc, tn), lambda l, j: (l, 0, j)),
        out_shape=jax.ShapeDtypeStruct((DEPTH, nc, 6 * D_MODEL), F32),
        compiler_params=_cparams("arbitrary", "arbitrary"),
        name="modulation",
    )(cond, mod_w, mod_b.reshape(DEPTH, 1, 6 * D_MODEL))


def _inproj_kernel(x_ref, xp_ref, xn_ref, mod_ref, g_ref, w_ref, wt_ref, cw_ref, cb_ref, *rest,
                   plan, nt_plan, tiles_per_seq, conv0):
    nseg = len(plan)
    outs = rest[:nseg]
    nt_outs = rest[nseg:nseg + len(nt_plan)]
    hbuf = rest[nseg + len(nt_plan)]
    tm = x_ref.shape[0]
    i = pl.program_id(0)
    j = i % tiles_per_seq
    pmask = (j != 0).astype(F32)
    nmask = (j != tiles_per_seq - 1).astype(F32)
    mod = mod_ref[0]
    sh, sc = mod[0:1], mod[1:2]
    g = g_ref[...]

    def normmod(xv):
        return (_rms(xv) * g * (1.0 + sc) + sh).astype(BF16)

    hbuf[0:tm] = normmod(x_ref[...])
    hbuf[tm:tm + HALO] = normmod(xp_ref[...])
    hbuf[tm + HALO:tm + 2 * HALO] = normmod(xn_ref[...])
    rowid = lax.broadcasted_iota(jnp.int32, (tm, COL_CHUNK), 0)

    for (c0, width, kind, scale, transposed), o_ref in zip(plan, outs):
        for cc in range(c0, c0 + width, COL_CHUNK):
            w = w_ref[0, :, cc:cc + COL_CHUNK]
            if kind == "raw":
                y = _dot(hbuf[0:tm], w)
            else:
                full = _dot(hbuf[...], w)
                acc = full[0:tm]
                ap = full[tm + HALO - 1:tm + HALO] * pmask
                an = full[tm + HALO:tm + HALO + 1] * nmask
                prev = jnp.where(rowid == 0, ap, pltpu.roll(acc, 1, 0))
                nxt = jnp.where(rowid == tm - 1, an, pltpu.roll(acc, tm - 1, 0))
                cw = cw_ref[0, :, cc - conv0:cc - conv0 + COL_CHUNK]
                cb = cb_ref[0, :, cc - conv0:cc - conv0 + COL_CHUNK]
                y = cb + cw[0:1] * prev + cw[1:2] * acc + cw[2:3] * nxt
                if kind == "conv_silu":
                    y = _silu(y)
                if scale != 1.0:
                    y = y * scale
            if transposed:
                o_ref[cc - c0:cc - c0 + COL_CHUNK, :] = y.T.astype(o_ref.dtype)
            else:
                o_ref[:, cc - c0:cc - c0 + COL_CHUNK] = y.astype(o_ref.dtype)

    for (r0, rows), o_ref in zip(nt_plan, nt_outs):
        o_ref[...] = _dot_nt(wt_ref[0, r0:r0 + rows, :], hbuf[0:tm]).astype(o_ref.dtype)


def _inproj(x, mods_l, g_l, w_all, wt_all, cw_all, cb_all, layer_idx, *, plan, nt_plan, conv0, seq_len,
            per_batch_cond):
    M = x.shape[0]
    tm = ROW_TILE
    tps = seq_len // tm
    nt = M // tm
    hb = tm // HALO
    ntot = w_all.shape[2]
    tw = wt_all.shape[1]
    cdim = cw_all.shape[2]
    e = layer_idx
    cond_map = (lambda i: (1 + i // tps, 0, 0)) if per_batch_cond else (lambda i: (0, 0, 0))
    in_specs = [
        pl.BlockSpec((tm, D_MODEL), lambda i: (i, 0)),
        pl.BlockSpec((HALO, D_MODEL), lambda i: (jnp.maximum(i * hb - 1, 0), 0)),
        pl.BlockSpec((HALO, D_MODEL), lambda i: (jnp.minimum((i + 1) * hb, nt * hb - 1), 0)),
        pl.BlockSpec((1, 6, D_MODEL), cond_map),
        pl.BlockSpec((1, D_MODEL), lambda i: (0, 0)),
        pl.BlockSpec((1, D_MODEL, ntot), lambda i: (e, 0, 0), pipeline_mode=pl.Buffered(1)),
        pl.BlockSpec((1, tw, D_MODEL), lambda i: (e, 0, 0)),
        pl.BlockSpec((1, 3, cdim), lambda i: (e, 0, 0)),
        pl.BlockSpec((1, 1, cdim), lambda i: (e, 0, 0)),
    ]
    out_specs, out_shape = [], []
    for (c0, width, kind, scale, transposed, dt) in plan:
        if transposed:
            out_specs.append(pl.BlockSpec((width, tm), lambda i: (0, i)))
            out_shape.append(jax.ShapeDtypeStruct((width, M), dt))
        else:
            out_specs.append(pl.BlockSpec((tm, width), lambda i: (i, 0)))
            out_shape.append(jax.ShapeDtypeStruct((M, width), dt))
    for (r0, rows, dt) in nt_plan:
        out_specs.append(pl.BlockSpec((rows, tm), lambda i: (0, i)))
        out_shape.append(jax.ShapeDtypeStruct((rows, M), dt))
    kplan = tuple(p[:5] for p in plan)
    knt = tuple(p[:2] for p in nt_plan)
    return pl.pallas_call(
        functools.partial(_inproj_kernel, plan=kplan, nt_plan=knt, tiles_per_seq=tps, conv0=conv0),
        grid=(nt,),
        in_specs=in_specs,
        out_specs=out_specs,
        out_shape=out_shape,
        scratch_shapes=[pltpu.VMEM((tm + 2 * HALO, D_MODEL), BF16)],
        compiler_params=_cparams("arbitrary"),
        name="inproj",
    )(x, x, x, mods_l, g_l, w_all, wt_all, cw_all, cb_all.reshape(cb_all.shape[0], 1, cdim))


def _outproj_even_kernel(yh_ref, ym_ref, x_ref, mod_ref, w_ref, o_ref):
    out = _dot(yh_ref[...], w_ref[0, 0:HY_W]) + _dot(ym_ref[...], w_ref[0, HY_W:])
    o_ref[...] = x_ref[...] + mod_ref[0][2:3] * out


def _outproj_odd_kernel(y_ref, z_ref, x_ref, mod_ref, ng_ref, w_ref, o_ref):
    u = y_ref[...] * _silu(z_ref[...])
    un = (_rms(u) * ng_ref[0]).astype(BF16)
    o_ref[...] = x_ref[...] + mod_ref[0][2:3] * _dot(un, w_ref[0])


def _cond_map(seq_len, per_batch_cond):
    tps = seq_len // WIDE_TILE
    return (lambda i: (1 + i // tps, 0, 0)) if per_batch_cond else (lambda i: (0, 0, 0))


def _outproj_even(yh, ym, x, mods_l, w_all, e, *, seq_len, per_batch_cond):
    M = x.shape[0]
    tm = WIDE_TILE
    return pl.pallas_call(
        _outproj_even_kernel,
        grid=(M // tm,),
        in_specs=[pl.BlockSpec((tm, HY_W), lambda i: (i, 0)),
                  pl.BlockSpec((tm, M_W), lambda i: (i, 0)),
                  pl.BlockSpec((tm, D_MODEL), lambda i: (i, 0)),
                  pl.BlockSpec((1, 6, D_MODEL), _cond_map(seq_len, per_batch_cond)),
                  pl.BlockSpec((1, HY_W + M_W, D_MODEL), lambda i: (e, 0, 0), pipeline_mode=pl.Buffered(1))],
        out_specs=pl.BlockSpec((tm, D_MODEL), lambda i: (i, 0)),
        out_shape=jax.ShapeDtypeStruct((M, D_MODEL), F32),
        compiler_params=_cparams("arbitrary"),
        name="outproj_even",
    )(yh, ym, x, mods_l, w_all)


def _outproj_odd(y, z, x, mods_l, ng_all, w_all, o, *, seq_len, per_batch_cond):
    M = x.shape[0]
    tm = WIDE_TILE
    return pl.pallas_call(
        _outproj_odd_kernel,
        grid=(M // tm,),
        in_specs=[pl.BlockSpec((tm, SSD_INNER), lambda i: (i, 0)),
                  pl.BlockSpec((tm, SSD_INNER), lambda i: (i, 0)),
                  pl.BlockSpec((tm, D_MODEL), lambda i: (i, 0)),
                  pl.BlockSpec((1, 6, D_MODEL), _cond_map(seq_len, per_batch_cond)),
                  pl.BlockSpec((1, 1, SSD_INNER), lambda i: (o, 0, 0)),
                  pl.BlockSpec((1, SSD_INNER, D_MODEL), lambda i: (o, 0, 0), pipeline_mode=pl.Buffered(1))],
        out_specs=pl.BlockSpec((tm, D_MODEL), lambda i: (i, 0)),
        out_shape=jax.ShapeDtypeStruct((M, D_MODEL), F32),
        compiler_params=_cparams("arbitrary"),
        name="outproj_odd",
    )(y, z, x, mods_l, ng_all, w_all)


FFN_CHUNK = FFN_HIDDEN // 2


def _ffn_kernel(x_ref, mod_ref, g_ref, w1_ref, w3_ref, w2_ref, fg_ref, o_ref, *, final):
    x = x_ref[...]
    mod = mod_ref[0]
    h = (_rms(x) * g_ref[0] * (1.0 + mod[4:5]) + mod[3:4]).astype(BF16)
    ff = jnp.zeros(x.shape, F32)
    for c0 in range(0, FFN_HIDDEN, FFN_CHUNK):
        a = _dot(h, w1_ref[0, :, c0:c0 + FFN_CHUNK])
        b = _dot(h, w3_ref[0, :, c0:c0 + FFN_CHUNK])
        u = (_silu(a) * b).astype(BF16)
        ff = ff + _dot(u, w2_ref[0, c0:c0 + FFN_CHUNK, :])
    xo = x + mod[5:6] * ff
    if final:
        xo = _rms(xo) * fg_ref[...]
    o_ref[...] = xo


def _ffn(x, mods_l, g_all, w1, w3, w2, fg, l, *, seq_len, per_batch_cond, final):
    M = x.shape[0]
    tm = WIDE_TILE
    wspec = lambda shape: pl.BlockSpec(shape, lambda i: (l, 0, 0), pipeline_mode=pl.Buffered(1))
    return pl.pallas_call(
        functools.partial(_ffn_kernel, final=final),
        grid=(M // tm,),
        in_specs=[pl.BlockSpec((tm, D_MODEL), lambda i: (i, 0)),
                  pl.BlockSpec((1, 6, D_MODEL), _cond_map(seq_len, per_batch_cond)),
                  pl.BlockSpec((1, 1, D_MODEL), lambda i: (l, 0, 0)),
                  wspec((1, D_MODEL, FFN_HIDDEN)), wspec((1, D_MODEL, FFN_HIDDEN)),
                  wspec((1, FFN_HIDDEN, D_MODEL)),
                  pl.BlockSpec((1, D_MODEL), lambda i: (0, 0))],
        out_specs=pl.BlockSpec((tm, D_MODEL), lambda i: (i, 0)),
        out_shape=jax.ShapeDtypeStruct((M, D_MODEL), F32),
        compiler_params=_cparams("arbitrary"),
        name="ffn",
    )(x, mods_l, g_all, w1, w3, w2, fg)


def _scan_consts(d):
    T = SCAN_T
    row = lax.broadcasted_iota(jnp.int32, (T, T), 0)
    col = lax.broadcasted_iota(jnp.int32, (T, T), 1)
    mask, last = (row <= col, T - 1) if d == 0 else (row >= col, 0)
    v_mat = jnp.where(mask, 1.0, 0.0).astype(BF16)
    return mask, v_mat, last


def _mlstm_kernel(qt_ref, k_ref, vt_ref, o_ref, gt_ref, gb_ref, ng_ref, c0_ref, m0_ref,
                  y_ref, cst_ref, mst_ref, h_scr, c_scr, m_scr, sb_scr, vw_scr, rows_scr, *, nc, ncs):
    T = SCAN_T
    H = M_HEADS
    nseq = nc // ncs
    ones_blk = jnp.where(lax.broadcasted_iota(jnp.int32, (M_DH, T), 0) == 0, 1.0, 0.0)
    consts = (_scan_consts(0), _scan_consts(1))
    gb = gb_ref[...]

    def gates(ci, d, slot):
        mask, v_mat, last = consts[d]
        c0 = pl.multiple_of(ci * T, T)
        lanes = pl.ds(c0, T)
        graw = gt_ref[:, lanes] + gb
        lf = _log_sigmoid(graw)
        b_all = _dot2(lf, v_mat)
        for h in range(H):
            fi = (2 * d + 1) * H + h
            ii = (2 * d) * H + h
            hl = slice(h * M_DH, (h + 1) * M_DH)
            tr = slice(h * T, (h + 1) * T)
            b_row, i_row = b_all[fi:fi + 1], graw[ii:ii + 1]
            dlog = jnp.where(mask, jnp.broadcast_to(b_row, (T, T)) - jnp.broadcast_to(b_row - i_row, (T, T)).T,
                             -jnp.inf)
            mloc = jnp.max(dlog, axis=0, keepdims=True)
            sb_scr[slot, d, tr, :] = (_dot(k_ref[pl.ds(c0, T), hl], qt_ref[hl, lanes])
                                      * jnp.exp(dlog - mloc)).astype(BF16)
            mloc_last = jnp.broadcast_to(mloc[:, last:last + 1], (1, T))
            b_last = jnp.broadcast_to(b_row[:, last:last + 1], (1, T))
            w_end = jnp.exp(b_last - b_row + i_row - mloc_last)
            vw_scr[slot, d, h, 0:M_DH, :] = (vt_ref[hl, lanes].astype(F32) * w_end).astype(BF16)
            vw_scr[slot, d, h, M_DH:, :] = (ones_blk * w_end).astype(BF16)
            rows_scr[slot, d, 4 * h:4 * h + 4, :] = jnp.concatenate([b_row, mloc, mloc_last, b_last], axis=0)

    def update(ci, d, slot, accumulate):
        c0 = pl.multiple_of(ci * T, T)
        lanes = pl.ds(c0, T)
        for h in range(H):
            sidx = d * H + h
            hl = slice(h * M_DH, (h + 1) * M_DH)
            tr = slice(h * T, (h + 1) * T)
            rows = rows_scr[slot, d, 4 * h:4 * h + 4, :]
            b_row, mloc, mloc_last, b_last = rows[0:1], rows[1:2], rows[2:3], rows[3:4]
            m_row = m_scr[sidx]
            inter = b_row + m_row
            mt = jnp.maximum(inter, mloc)
            kh = k_ref[pl.ds(c0, T), hl]
            qt = qt_ref[hl, lanes]
            vext = jnp.concatenate([vt_ref[hl, lanes], ones_blk.astype(BF16)], axis=0)
            ct = c_scr[sidx]
            tot = (jnp.exp(mloc - mt) * _dot(vext, sb_scr[slot, d, tr, :])
                   + jnp.exp(inter - mt) * _dot(ct.astype(BF16), qt))
            den = tot[M_DH:M_DH + 1]
            ht = tot[0:M_DH] / jnp.maximum(jnp.abs(den), jnp.exp(-mt))
            if accumulate:
                h_scr[hl, lanes] = h_scr[hl, lanes] + ht
            else:
                h_scr[hl, lanes] = ht
            m_new = jnp.maximum(b_last + m_row, mloc_last)
            c_scr[sidx] = (jnp.exp(b_last + m_row - m_new) * ct
                           + jnp.exp(mloc_last - m_new) * _dot(vw_scr[slot, d, h], kh))
            m_scr[sidx] = m_new

    def chunks_at(j):
        if nseq == 1:
            return j, nc - 1 - j
        seq, loc = divmod(j, ncs)
        return j, seq * ncs + ncs - 1 - loc

    def load_state(seq):
        c_scr[...] = c0_ref[seq]
        m_scr[...] = m0_ref[seq]

    def store_state(seq):
        cst_ref[seq] = c_scr[...]
        mst_ref[seq] = m_scr[...]

    def pair(p, slots, accumulate=None):
        s0, s1 = slots
        for j, slot in ((2 * p, s0), (2 * p + 1, s1)):
            if nseq > 1 and j % ncs == 0:
                load_state(j // ncs)
            acc = (j % ncs >= ncs // 2) if accumulate is None else accumulate
            cf, cb = chunks_at(j)
            update(cf, 0, slot, acc)
            update(cb, 1, slot, acc)
            if nseq > 1 and j % ncs == ncs - 1:
                store_state(j // ncs)
        for j, slot in ((2 * p + 2, 2 - s0), (2 * p + 3, 4 - s1)):
            if nseq == 1:
                gates(jnp.minimum(j, nc - 1), 0, slot)
                gates(jnp.maximum(nc - 1 - j, 0), 1, slot)
            elif j < nc:
                cf, cb = chunks_at(j)
                gates(cf, 0, slot)
                gates(cb, 1, slot)

    for j in (0, 1):
        cf, cb = chunks_at(j)
        gates(cf, 0, j)
        gates(cb, 1, j)
    if nseq == 1:
        load_state(0)
    if nseq == 1 and nc % 8 == 0:
        def body(accumulate):
            def run(i, carry):
                pair(2 * i, (0, 1), accumulate)
                pair(2 * i + 1, (2, 3), accumulate)
                return carry
            return run
        lax.fori_loop(0, nc // 8, body(False), 0)
        lax.fori_loop(nc // 8, nc // 4, body(True), 0)
    else:
        for p in range(nc // 2):
            pair(p, ((0, 1), (2, 3))[p % 2])
    if nseq == 1:
        store_state(0)

    def epilogue(i, carry):
        c0 = pl.multiple_of(i * T, T)
        for h in range(H):
            hl = slice(h * M_DH, (h + 1) * M_DH)
            hh = h_scr[hl, pl.ds(c0, T)]
            hn = hh * lax.rsqrt(jnp.mean(hh * hh, axis=0, keepdims=True) + EPS)
            y = hn.T * ng_ref[:, hl] * _sigmoid(o_ref[pl.ds(c0, T), hl])
            y_ref[pl.ds(c0, T), hl] = y.astype(y_ref.dtype)
        return carry

    lax.fori_loop(0, nc, epilogue, 0)


def _mlstm(qt, k, vt, o, gt, gate_b, norm_g, c0, m0, B, L):
    assert L % (2 * SCAN_T) == 0 and M_DH == SCAN_T
    bb = max(1, min(B, MLSTM_BLOCK_TOKENS // L))
    assert B % bb == 0
    ncs = L // SCAN_T
    nc = bb * ncs
    Lb = bb * L
    ng = 4 * M_HEADS
    M = B * L
    one = pl.Buffered(1)
    st_spec = pl.BlockSpec((bb, 2 * M_HEADS, 2 * M_DH, M_DH), lambda b: (b, 0, 0, 0))
    m_spec = pl.BlockSpec((bb, 2 * M_HEADS, 1, M_DH), lambda b: (b, 0, 0, 0))
    return pl.pallas_call(
        functools.partial(_mlstm_kernel, nc=nc, ncs=ncs),
        grid=(B // bb,),
        in_specs=[pl.BlockSpec((M_W, Lb), lambda b: (0, b), pipeline_mode=one),
                  pl.BlockSpec((Lb, M_W), lambda b: (b, 0), pipeline_mode=one),
                  pl.BlockSpec((M_W, Lb), lambda b: (0, b), pipeline_mode=one),
                  pl.BlockSpec((Lb, M_W), lambda b: (b, 0), pipeline_mode=one),
                  pl.BlockSpec((ng, Lb), lambda b: (0, b)),
                  pl.BlockSpec((ng, SCAN_T), lambda b: (0, 0)),
                  pl.BlockSpec((1, M_W), lambda b: (0, 0)),
                  st_spec, m_spec],
        out_specs=[pl.BlockSpec((Lb, M_W), lambda b: (b, 0)), st_spec, m_spec],
        out_shape=[jax.ShapeDtypeStruct((M, M_W), BF16),
                   jax.ShapeDtypeStruct((B, 2 * M_HEADS, 2 * M_DH, M_DH), F32),
                   jax.ShapeDtypeStruct((B, 2 * M_HEADS, 1, M_DH), F32)],
        scratch_shapes=[pltpu.VMEM((M_W, Lb), F32),
                        pltpu.VMEM((2 * M_HEADS, 2 * M_DH, M_DH), F32),
                        pltpu.VMEM((2 * M_HEADS, 1, M_DH), F32),
                        pltpu.VMEM((4, 2, M_HEADS * SCAN_T, SCAN_T), BF16),
                        pltpu.VMEM((4, 2, M_HEADS, 2 * M_DH, SCAN_T), BF16),
                        pltpu.VMEM((4, 2, 4 * M_HEADS, SCAN_T), F32)],
        compiler_params=_cparams("arbitrary"),
        name="mlstm",
    )(qt, k, vt, o, gt, jnp.broadcast_to(gate_b.reshape(ng, 1), (ng, SCAN_T)), norm_g.reshape(1, M_W), c0, m0)


def _ssd_kernel(xt_ref, b_ref, ct_ref, dtt_ref, db_ref, al_ref, dsk_ref, s0_ref,
                y_ref, sout_ref, yf_scr, yb_scr, s_scr, cbt_scr, p_scr, xd_scr, xdec_scr, rows_scr, *, nc, ncs):
    T = SCAN_T
    H = SSD_HPG
    nseq = nc // ncs
    consts = (_scan_consts(0), _scan_consts(1))
    db = db_ref[0]
    a_neg = -jnp.exp(al_ref[0])

    def gates(ci, d, slot):
        mask, v_mat, last = consts[d]
        c0 = pl.multiple_of(ci * T, T)
        lanes = pl.ds(c0, T)
        hr = slice(d * H, (d + 1) * H)
        dt = _softplus(dtt_ref[hr, lanes] + db[hr])
        a = dt * a_neg[hr]
        acs = _dot2(a, v_mat)
        a_last = jnp.broadcast_to(acs[:, last:last + 1], (H, T))
        rows_scr[slot, d, 0:H, :] = jnp.exp(acs)
        rows_scr[slot, d, H:2 * H, :] = jnp.exp(a_last)
        cbt_scr[d] = _dot(b_ref[pl.ds(c0, T), :], ct_ref[:, lanes])
        for r in range(H):
            tr = slice(r * T, (r + 1) * T)
            pr = slice(r * SSD_HEADDIM, (r + 1) * SSD_HEADDIM)
            al = jnp.broadcast_to(acs[r:r + 1], (T, T))
            lm = jnp.exp(jnp.where(mask, al - al.T, -jnp.inf))
            p_scr[slot, d, tr, :] = (cbt_scr[d] * lm).astype(BF16)
            xd = xt_ref[pr, lanes] * dt[r:r + 1]
            xd_scr[slot, d, pr, :] = xd.astype(BF16)
            xdec_scr[slot, d, pr, :] = (xd * jnp.exp(a_last[r:r + 1] - acs[r:r + 1])).astype(BF16)

    def update(ci, d, slot):
        c0 = pl.multiple_of(ci * T, T)
        lanes = pl.ds(c0, T)
        rows = rows_scr[slot, d]
        sc = _dot(s_scr[d].astype(BF16), ct_ref[:, lanes])
        upd = _dot(xdec_scr[slot, d], b_ref[pl.ds(c0, T), :])
        y_scr = yf_scr if d == 0 else yb_scr
        for r in range(H):
            pr = slice(r * SSD_HEADDIM, (r + 1) * SSD_HEADDIM)
            tr = slice(r * T, (r + 1) * T)
            y_scr[pr, lanes] = _dot(xd_scr[slot, d, pr, :], p_scr[slot, d, tr, :]) + rows[r:r + 1] * sc[pr]
            s_scr[d, pr, :] = rows[H + r:H + r + 1] * s_scr[d, pr, :] + upd[pr]

    def chunks_at(j):
        if nseq == 1:
            return j, nc - 1 - j
        seq, loc = divmod(j, ncs)
        return j, seq * ncs + ncs - 1 - loc

    def pair(p, slots):
        s0, s1 = slots
        for j, slot in ((2 * p, s0), (2 * p + 1, s1)):
            if nseq > 1 and j % ncs == 0:
                s_scr[...] = s0_ref[j // ncs, :, 0]
            cf, cb = chunks_at(j)
            update(cf, 0, slot)
            update(cb, 1, slot)
            if nseq > 1 and j % ncs == ncs - 1:
                sout_ref[j // ncs, :, 0] = s_scr[...]
        for j, slot in ((2 * p + 2, 2 - s0), (2 * p + 3, 4 - s1)):
            if nseq == 1:
                gates(jnp.minimum(j, nc - 1), 0, slot)
                gates(jnp.maximum(nc - 1 - j, 0), 1, slot)
            elif j < nc:
                cf, cb = chunks_at(j)
                gates(cf, 0, slot)
                gates(cb, 1, slot)

    for j in (0, 1):
        cf, cb = chunks_at(j)
        gates(cf, 0, j)
        gates(cb, 1, j)
    if nseq == 1:
        s_scr[...] = s0_ref[0, :, 0]
    if nseq == 1 and nc % 4 == 0:
        def body(i, carry):
            pair(2 * i, (0, 1))
            pair(2 * i + 1, (2, 3))
            return carry
        lax.fori_loop(0, nc // 4, body, 0)
    else:
        for p in range(nc // 2):
            pair(p, ((0, 1), (2, 3))[p % 2])
    if nseq == 1:
        sout_ref[0, :, 0] = s_scr[...]

    dsk = dsk_ref[0]

    def epilogue(i, carry):
        c0 = pl.multiple_of(i * T, T)
        lanes = pl.ds(c0, T)
        yt = yf_scr[:, lanes] + yb_scr[:, lanes] + dsk * xt_ref[:, lanes]
        y_ref[pl.ds(c0, T), :] = yt.T
        return carry

    lax.fori_loop(0, nc, epilogue, 0)


def _ssd(xt, bm, ct, dtt, dt_bias, a_log, dskip, s0, B, L):
    assert L % (2 * SCAN_T) == 0 and SSD_STATE == SCAN_T
    bb = max(1, min(B, SSD_BLOCK_TOKENS // L))
    assert B % bb == 0
    ncs = L // SCAN_T
    nc = bb * ncs
    Lb = bb * L
    G = SSD_GROUPS
    nd = 2 * SSD_HPG
    M = B * L
    rep = lambda p, rows: jnp.broadcast_to(p.reshape(G, rows, 1), (G, rows, SCAN_T))
    small = lambda rows: pl.BlockSpec((1, rows, SCAN_T), lambda b, g: (g, 0, 0))
    st_spec = pl.BlockSpec((bb, 2, 1, SSD_GW, SSD_STATE), lambda b, g: (b, 0, g, 0, 0))
    return pl.pallas_call(
        functools.partial(_ssd_kernel, nc=nc, ncs=ncs),
        grid=(B // bb, G),
        in_specs=[pl.BlockSpec((SSD_GW, Lb), lambda b, g: (g, b)),
                  pl.BlockSpec((Lb, SSD_STATE), lambda b, g: (b, g)),
                  pl.BlockSpec((SSD_STATE, Lb), lambda b, g: (g, b)),
                  pl.BlockSpec((nd, Lb), lambda b, g: (g, b)),
                  small(nd), small(nd), small(SSD_GW), st_spec],
        out_specs=[pl.BlockSpec((Lb, SSD_GW), lambda b, g: (b, g)), st_spec],
        out_shape=[jax.ShapeDtypeStruct((M, SSD_INNER), F32),
                   jax.ShapeDtypeStruct((B, 2, G, SSD_GW, SSD_STATE), F32)],
        scratch_shapes=[pltpu.VMEM((SSD_GW, Lb), F32), pltpu.VMEM((SSD_GW, Lb), F32),
                        pltpu.VMEM((2, SSD_GW, SSD_STATE), F32),
                        pltpu.VMEM((2, SCAN_T, SCAN_T), F32),
                        pltpu.VMEM((4, 2, SSD_HPG * SCAN_T, SCAN_T), BF16),
                        pltpu.VMEM((4, 2, SSD_GW, SCAN_T), BF16),
                        pltpu.VMEM((4, 2, SSD_GW, SCAN_T), BF16),
                        pltpu.VMEM((4, 2, 2 * SSD_HPG, SCAN_T), F32)],
        compiler_params=_cparams("arbitrary", "arbitrary"),
        name="ssd",
    )(xt, bm, ct, dtt, rep(dt_bias, nd), rep(a_log, nd), rep(dskip, SSD_GW), s0)


def _hy_factors(L):
    return (128, 64) if L == 4096 else (32, (2 * L) // 32)


def _hilo(t):
    return _split2(t.astype(F32))


def _cblock(cr, ci):
    return jnp.concatenate([jnp.concatenate([cr, -ci], axis=-1), jnp.concatenate([ci, cr], axis=-1)], axis=-2)


def _dft_tables(L):
    n1c, n2c = _hy_factors(L)
    N = n1c * n2c
    h1 = n1c // 2
    i32 = jnp.int32
    k1 = jnp.arange(n1c, dtype=i32)
    n2 = jnp.arange(n2c, dtype=i32)

    def stage1(n1_count):
        n1 = jnp.arange(n1_count, dtype=i32)
        idx = (k1[None, :, None] * (n2[:, None, None] + n2c * n1[None, None, :])) % N
        ang = idx.astype(F32) * (2.0 * math.pi / N)
        return jnp.cos(ang), jnp.sin(ang)

    c, s = stage1(h1)
    g_fwd = _cblock(c, -s)
    g_inv = _cblock(jnp.swapaxes(c, 1, 2), jnp.swapaxes(s, 1, 2)) / n1c
    cf, sf = stage1(n1c)
    g_real = jnp.concatenate([cf, -sf], axis=1)
    idx2 = (n2[:, None] * n2[None, :]) % n2c
    ang2 = idx2.astype(F32) * (2.0 * math.pi / n2c)
    c2, s2 = jnp.cos(ang2), jnp.sin(ang2)
    f2 = _cblock(c2, -s2)
    f2_inv = _cblock(c2, s2) / n2c
    return dict(g_fwd=g_fwd.astype(BF16), g_inv=g_inv.astype(BF16), f2=f2.astype(BF16),
                f2_inv=f2_inv.astype(BF16), g_real=_hilo(g_real), f2_exact=_hilo(f2))


def _swap01(x):
    return jnp.swapaxes(x, 0, 1)


def _hy_s1_kernel(x_ref, *rest, nb, exact):
    if exact:
        gh_ref, gl_ref, o_ref = rest
    else:
        g_ref, o_ref = rest
    pb, rows, _, C = x_ref.shape
    xs = _swap01(x_ref[...].reshape(pb * rows, nb, C))
    if exact:
        res = [_dot3(gh_ref[j], gl_ref[j], xs[j]) for j in range(nb)]
    else:
        res = [_dot(g_ref[j], xs[j].astype(BF16)) for j in range(nb)]
    out = _swap01(jnp.stack(res, axis=0))
    o_ref[...] = out.reshape(o_ref.shape).astype(o_ref.dtype)


def _hy_stage1(x, tab, out_dtype):
    exact = isinstance(tab, tuple)
    tabs = tab if exact else (tab,)
    n2c, rows_out, R = tabs[0].shape
    Bx, rows, _, C = x.shape
    pb = R // rows
    nb = HY_NB
    cb = min(C, HY_W)
    tspec = pl.BlockSpec((nb, rows_out, R), lambda p, t, c: (t, 0, 0))
    return pl.pallas_call(
        functools.partial(_hy_s1_kernel, nb=nb, exact=exact),
        grid=(Bx // pb, n2c // nb, C // cb),
        in_specs=[pl.BlockSpec((pb, rows, nb, cb), lambda p, t, c: (p, 0, t, c))] + [tspec] * len(tabs),
        out_specs=pl.BlockSpec((None, 2, rows_out // 2, nb, cb), lambda p, t, c: (p, 0, 0, t, c)),
        out_shape=jax.ShapeDtypeStruct((Bx // pb, 2, rows_out // 2, n2c, C), out_dtype),
        compiler_params=_cparams("arbitrary", "arbitrary", "arbitrary"),
        name="hyena_stage1",
    )(x, *tabs)


def _hy_spec_kernel(a_ref, inv_ref, fh_ref, fl_ref, o_ref, *, kb):
    n2c = a_ref.shape[2]
    for kk in range(kb):
        rhs = jnp.concatenate([a_ref[0, kk], a_ref[1, kk]], axis=0)
        X = _dot3(fh_ref[...], fl_ref[...], rhs) * inv_ref[...]
        o_ref[0, kk] = X[0:n2c]
        o_ref[1, kk] = X[n2c:]


def _hy_filter_spectrum(a, inv_norm, f2, kb):
    _, n1c, n2c, W = a.shape
    fh, fl = f2
    blk = pl.BlockSpec((2, kb, n2c, W), lambda t: (0, t, 0, 0))
    fspec = pl.BlockSpec((2 * n2c, 2 * n2c), lambda t: (0, 0))
    return pl.pallas_call(
        functools.partial(_hy_spec_kernel, kb=kb),
        grid=(n1c // kb,),
        in_specs=[blk, pl.BlockSpec((1, W), lambda t: (0, 0)), fspec, fspec],
        out_specs=blk,
        out_shape=jax.ShapeDtypeStruct(a.shape, F32),
        compiler_params=_cparams("arbitrary"),
        name="hyena_filter_spectrum",
    )(a, inv_norm, fh, fl)


def _hy_mid_kernel(a_ref, k_ref, f_ref, i_ref, o_ref, *, kb):
    n2c = a_ref.shape[2]
    res = []
    for kk in range(kb):
        rhs = jnp.concatenate([a_ref[0, kk], a_ref[1, kk]], axis=0)
        X = _dot(f_ref[...], rhs)
        xr, xi = X[0:n2c], X[n2c:]
        kr, ki = k_ref[0, kk], k_ref[1, kk]
        Y = jnp.concatenate([xr * kr - xi * ki, xr * ki + xi * kr], axis=0)
        res.append(_dot(i_ref[...], Y.astype(BF16)))
    o_ref[...] = _swap01(jnp.stack(res, axis=0)).reshape(o_ref.shape).astype(o_ref.dtype)


def _hy_mid(a, kspec, order, f2, f2_inv):
    P, _, n1c, n2c, C = a.shape
    kb = HY_NB
    fspec = pl.BlockSpec((2 * n2c, 2 * n2c), lambda t, p: (0, 0))
    return pl.pallas_call(
        functools.partial(_hy_mid_kernel, kb=kb),
        grid=(n1c // kb, P),
        in_specs=[pl.BlockSpec((None, 2, kb, n2c, C), lambda t, p: (p, 0, t, 0, 0)),
                  pl.BlockSpec((2, kb, n2c, C), lambda t, p: (0, t, 0, order)),
                  fspec, fspec],
        out_specs=pl.BlockSpec((None, 2, n2c, kb, C), lambda t, p: (p, 0, 0, t, 0)),
        out_shape=jax.ShapeDtypeStruct((P, 2, n2c, n1c, C), BF16),
        compiler_params=_cparams("arbitrary", "arbitrary"),
        name="hyena_mid",
    )(a, kspec, f2, f2_inv)


def _hy_last_kernel(b_ref, z_ref, gate_ref, bias_ref, gi_ref, *rest, fuse, nb):
    if fuse:
        gf_ref, zo_ref, ao_ref = rest
    else:
        (zo_ref,) = rest
    _, h1, _, C = z_ref.shape
    zs = _swap01(z_ref[...].reshape(2 * h1, nb, C))
    gs = _swap01(gate_ref[...].reshape(2 * h1, nb, C))
    bias = bias_ref[...]
    zn, an = [], []
    for j in range(nb):
        bv = jnp.concatenate([b_ref[0, j], b_ref[1, j]], axis=0)
        y = _dot(gi_ref[j], bv)
        znew = gs[j] * (y + zs[j] * bias)
        zn.append(znew)
        if fuse:
            an.append(_dot(gf_ref[j], znew.astype(BF16)))
    zo_ref[...] = _swap01(jnp.stack(zn, axis=0)).reshape(zo_ref.shape).astype(zo_ref.dtype)
    if fuse:
        ao_ref[...] = _swap01(jnp.stack(an, axis=0)).reshape(ao_ref.shape).astype(ao_ref.dtype)


def _hy_last(bf, z, gate, bias, g_inv, g_fwd, out_dtype):
    P, _, n2c, n1c, C = bf.shape
    B, h1, _, _ = z.shape
    nb = HY_NB
    cb = C // 2
    fuse = g_fwd is not None
    zspec = pl.BlockSpec((2, h1, nb, cb), lambda p, t, c: (p, 0, t, c))
    in_specs = [pl.BlockSpec((None, 2, nb, n1c, cb), lambda p, t, c: (p, 0, t, 0, c)),
                zspec, zspec, pl.BlockSpec((1, cb), lambda p, t, c: (0, c)),
                pl.BlockSpec((nb, 2 * h1, 2 * n1c), lambda p, t, c: (t, 0, 0))]
    args = [bf, z, gate, bias, g_inv]
    out_specs = [zspec]
    out_shape = [jax.ShapeDtypeStruct(z.shape, out_dtype)]
    if fuse:
        in_specs.append(pl.BlockSpec((nb, 2 * n1c, 2 * h1), lambda p, t, c: (t, 0, 0)))
        args.append(g_fwd)
        out_specs.append(pl.BlockSpec((None, 2, n1c, nb, cb), lambda p, t, c: (p, 0, 0, t, c)))
        out_shape.append(jax.ShapeDtypeStruct((P, 2, n1c, n2c, C), BF16))
    return pl.pallas_call(
        functools.partial(_hy_last_kernel, fuse=fuse, nb=nb),
        grid=(P, n2c // nb, C // cb),
        in_specs=in_specs, out_specs=out_specs, out_shape=out_shape,
        compiler_params=_cparams("arbitrary", "arbitrary", "arbitrary"),
        name="hyena_last",
    )(*args)


def _hy_filter_kernel(f_ref, w1_ref, b1_ref, w2_ref, b2_ref, w3_ref, fr_ref, dl_ref, hk_ref, s_ref, *, L):
    i = pl.program_id(0)
    tl = f_ref.shape[0]
    f = f_ref[...]
    fr = fr_ref[...]
    h = jnp.sin(fr * (_dot(f.astype(BF16), w1_ref[...].astype(BF16)) + b1_ref[...]))
    h = jnp.sin(fr * (_dot(h.astype(BF16), w2_ref[...].astype(BF16)) + b2_ref[...]))
    hk = _dot(h.astype(BF16), w3_ref[...].astype(BF16)) * jnp.exp(-f[:, 0:1] * dl_ref[...])
    n = i * tl + lax.broadcasted_iota(jnp.int32, hk.shape, 0)
    hk = jnp.where(n == L, 0.0, hk)
    hk_ref[...] = hk
    tot = jnp.sum(jnp.abs(hk), axis=0, keepdims=True)

    @pl.when(i == 0)
    def _():
        s_ref[...] = tot

    @pl.when(i != 0)
    def _():
        s_ref[...] = s_ref[...] + tot


def _hy_filter(L, w1, b1, w2, b2, w3, freq):
    f32 = F32
    t = jnp.linspace(0.0, 1.0, L, dtype=f32)[:, None]
    ang = (2.0 * math.pi / L) * jnp.arange(L, dtype=f32)[:, None] * \
        jnp.linspace(1e-4, HY_BANDS - 1, HY_BANDS, dtype=f32)[None]
    feats = jnp.concatenate([t, jnp.cos(ang), -jnp.sin(ang)], axis=-1)
    feats = jnp.concatenate([feats, feats[0:1], feats[:0:-1]], axis=0)
    fpad = 128
    feats = jnp.pad(feats, ((0, 0), (0, fpad - HY_EMB)))
    w1p = jnp.pad(w1, ((0, fpad - HY_EMB), (0, 0)))
    deltas = jnp.abs(jnp.linspace(math.log(HY_TARGET) / HY_SLOW, math.log(HY_TARGET) / HY_FAST, HY_W, dtype=f32))
    OC = HY_ORDER * HY_W
    dl = jnp.tile(deltas, HY_ORDER).reshape(1, OC)
    tl = min(L, 512)
    nblk = L // tl
    full = lambda shp: pl.BlockSpec(shp, lambda i: (0, 0))
    return pl.pallas_call(
        functools.partial(_hy_filter_kernel, L=L),
        grid=(2 * nblk,),
        in_specs=[pl.BlockSpec((tl, fpad), lambda i: (i, 0)),
                  full((fpad, HY_HID)), full((1, HY_HID)), full((HY_HID, HY_HID)), full((1, HY_HID)),
                  pl.BlockSpec((HY_HID, OC), lambda i: (0, i // nblk)), full((1, HY_HID)), full((1, OC))],
        out_specs=[pl.BlockSpec((tl, OC), lambda i: (i, 0)), full((1, OC))],
        out_shape=[jax.ShapeDtypeStruct((2 * L, OC), F32), jax.ShapeDtypeStruct((1, OC), F32)],
        compiler_params=_cparams("arbitrary"),
        name="hyena_filter",
    )(feats, w1p, b1.reshape(1, HY_HID), w2, b2.reshape(1, HY_HID), w3, freq.reshape(1, HY_HID), dl)


def _hy_spectrum(L, tabs, w1, b1, w2, b2, w3, freq):
    n1c, n2c = _hy_factors(L)
    OC = HY_ORDER * HY_W
    ktime, s = _hy_filter(L, w1, b1, w2, b2, w3, freq)
    a = _hy_stage1(ktime.reshape(1, n1c, n2c, OC), tabs["g_real"], F32)[0]
    return _hy_filter_spectrum(a, 1.0 / s, tabs["f2_exact"], 8)


def _hyena(v, x1, x2, kspec, bias, tabs, B, L):
    n1c, n2c = _hy_factors(L)
    C = HY_W
    view = lambda t: t.reshape(B, n1c // 2, n2c, C)
    z0, g1, g2 = view(v), view(x1), view(x2)
    a = _hy_stage1(z0, tabs["g_fwd"], BF16)
    bf = _hy_mid(a, kspec, 0, tabs["f2"], tabs["f2_inv"])
    z1, a = _hy_last(bf, z0, g1, bias[0:1], tabs["g_inv"], tabs["g_fwd"], F32)
    bf = _hy_mid(a, kspec, 1, tabs["f2"], tabs["f2_inv"])
    (z2,) = _hy_last(bf, z1, g2, bias[1:2], tabs["g_inv"], None, BF16)
    return z2.reshape(B * L, C)


def _direct_tables(L):
    N = 2 * L
    k = jnp.arange(N, dtype=jnp.int32)
    ang = ((k[:, None] * k[None, :]) % N).astype(F32) * (2.0 * math.pi / N)
    c, s = jnp.cos(ang), jnp.sin(ang)
    fwd = _cblock(c[:, :L], -s[:, :L])
    inv = _cblock(c[:L, :], s[:L, :]) / N
    real = jnp.concatenate([c, -s], axis=0)
    return dict(fwd=fwd.astype(BF16), inv=inv.astype(BF16), real=_hilo(real))


def _hy_direct_spec_kernel(kt_ref, inv_ref, fh_ref, fl_ref, o_ref):
    N = kt_ref.shape[0]
    X = _dot3(fh_ref[...], fl_ref[...], kt_ref[...]) * inv_ref[...]
    o_ref[0] = X[0:N]
    o_ref[1] = X[N:]


def _hy_direct_spectrum(L, tabs, w1, b1, w2, b2, w3, freq):
    N = 2 * L
    OC = HY_ORDER * HY_W
    cb = HY_W // 2
    ktime, s = _hy_filter(L, w1, b1, w2, b2, w3, freq)
    fh, fl = tabs["real"]
    fspec = pl.BlockSpec((2 * N, N), lambda c: (0, 0))
    return pl.pallas_call(
        _hy_direct_spec_kernel,
        grid=(OC // cb,),
        in_specs=[pl.BlockSpec((N, cb), lambda c: (0, c)), pl.BlockSpec((1, cb), lambda c: (0, c)), fspec, fspec],
        out_specs=pl.BlockSpec((2, N, cb), lambda c: (0, 0, c)),
        out_shape=jax.ShapeDtypeStruct((2, N, OC), F32),
        compiler_params=_cparams("arbitrary"),
        name="hyena_direct_spectrum",
    )(ktime, 1.0 / s, fh, fl)


def _hy_direct_kernel(v_ref, g1_ref, g2_ref, k0_ref, k1_ref, bias_ref, f_ref, fi_ref, o_ref):
    L = v_ref.shape[1]
    N = 2 * L
    z = jnp.concatenate([v_ref[0], v_ref[1]], axis=0)
    for o, (k_ref, g_ref) in enumerate(((k0_ref, g1_ref), (k1_ref, g2_ref))):
        X = _dot(f_ref[...], z.astype(BF16))
        xr, xi = X[0:N], X[N:]
        kr, ki = k_ref[0], k_ref[1]
        Y = jnp.concatenate([xr * kr - xi * ki, xr * ki + xi * kr], axis=0)
        y = _dot(fi_ref[...], Y.astype(BF16))
        gate = jnp.concatenate([g_ref[0], g_ref[1]], axis=0)
        z = gate * (y + z * bias_ref[o:o + 1])
    o_ref[0] = z[0:L].astype(o_ref.dtype)
    o_ref[1] = z[L:].astype(o_ref.dtype)


def _hyena_direct(v, x1, x2, kspec, bias, tabs, B, L):
    C = HY_W
    N = 2 * L
    cb = C // 2
    ncb = C // cb
    seq = pl.BlockSpec((2, L, cb), lambda p, c: (p, 0, c))
    view = lambda t: t.reshape(B, L, C)
    out = pl.pallas_call(
        _hy_direct_kernel,
        grid=(B // 2, ncb),
        in_specs=[seq, seq, seq,
                  pl.BlockSpec((2, N, cb), lambda p, c: (0, 0, c)),
                  pl.BlockSpec((2, N, cb), lambda p, c: (0, 0, ncb + c)),
                  pl.BlockSpec((HY_ORDER, cb), lambda p, c: (0, c)),
                  pl.BlockSpec((2 * N, 2 * L), lambda p, c: (0, 0)),
                  pl.BlockSpec((2 * L, 2 * N), lambda p, c: (0, 0))],
        out_specs=seq,
        out_shape=jax.ShapeDtypeStruct((B, L, C), BF16),
        compiler_params=_cparams("arbitrary", "arbitrary"),
        name="hyena_direct",
    )(view(v), view(x1), view(x2), kspec, kspec, bias, tabs["fwd"], tabs["inv"])
    return out.reshape(B * L, C)


HY_DIRECT_MAX_L = 512


_EVEN_PLAN = (
    (0, HY_W, "conv", 1.0, False, F32),
    (HY_W, HY_W, "conv", 1.0, False, F32),
    (2 * HY_W, HY_W, "conv", 1.0, False, F32),
    (3 * HY_W, M_W, "conv_silu", 1.0, True, BF16),
    (3 * HY_W + M_W, M_W, "conv_silu", M_DH ** -0.5, False, BF16),
    (EV_CONV + M_W, M_W, "raw", 1.0, False, F32),
)
_EVEN_NT_PLAN = ((0, M_W, BF16), (M_W, 4 * M_HEADS, F32))
_ODD_NT_PLAN = ((0, 2 * SSD_HEADS, F32),)
_GN = SSD_GROUPS * SSD_STATE
_ODD_PLAN = (
    (0, SSD_INNER, "raw", 1.0, False, F32),
    (SSD_INNER, SSD_INNER, "conv_silu", 1.0, True, F32),
    (2 * SSD_INNER, _GN, "conv_silu", 1.0, False, BF16),
    (2 * SSD_INNER + _GN, _GN, "conv_silu", 1.0, True, BF16),
)


def _even_layer(x, mods_l, P, e, l, B, L, st, per_batch_cond, kspec, tabs):
    hv, hx1, hx2, qt, k, o, vt, gt = _inproj(
        x, mods_l, P["norm1_g"][l:l + 1], P["ev_in_w"], P["ev_in_wt"], P["ev_conv_w"], P["ev_conv_b"], e,
        plan=_EVEN_PLAN, nt_plan=_EVEN_NT_PLAN, conv0=0, seq_len=L, per_batch_cond=per_batch_cond)
    hyena = _hyena_direct if L <= HY_DIRECT_MAX_L else _hyena
    y_hy = hyena(hv, hx1, hx2, kspec, P["hy_bias"][e], tabs, B, L)
    c0, m0 = st
    y_m, c_new, m_new = _mlstm(qt, k, vt, o, gt, P["m_gate_b"][e], P["m_norm_g"][e], c0, m0, B, L)
    x = _outproj_even(y_hy, y_m, x, mods_l, P["ev_out_w"], e, seq_len=L, per_batch_cond=per_batch_cond)
    return x, (c_new, m_new)


def _by_group(p):
    return p.reshape(2, SSD_GROUPS, SSD_HPG).transpose(1, 0, 2).reshape(SSD_GROUPS, 2 * SSD_HPG)


def _odd_layer(x, mods_l, P, o, l, B, L, s0, per_batch_cond):
    z, xt, bm, ct, dtt = _inproj(
        x, mods_l, P["norm1_g"][l:l + 1], P["od_in_w"], P["od_in_wt"], P["od_conv_w"], P["od_conv_b"], o,
        plan=_ODD_PLAN, nt_plan=_ODD_NT_PLAN, conv0=SSD_INNER, seq_len=L, per_batch_cond=per_batch_cond)
    dskip = jnp.repeat(P["ssd_D"][o], SSD_HEADDIM).reshape(SSD_GROUPS, SSD_GW)
    y, s_new = _ssd(xt, bm, ct, dtt, _by_group(P["ssd_dt_bias"][o]), _by_group(P["ssd_A_log"][o]), dskip, s0, B, L)
    x = _outproj_odd(y, z, x, mods_l, P["ssd_norm_g"], P["od_out_w"], o, seq_len=L, per_batch_cond=per_batch_cond)
    return x, s_new


def _to_colmajor(x, B, L):
    rows = L // GRID_W
    return x.reshape(B, rows, GRID_W, D_MODEL).swapaxes(1, 2).reshape(B * L, D_MODEL)


def _from_colmajor(x, B, L):
    rows = L // GRID_W
    return x.reshape(B, GRID_W, rows, D_MODEL).swapaxes(1, 2).reshape(B * L, D_MODEL)


def _trunk(x, mods, P, B, L, init_states, per_batch_cond, grid, kspecs, tabs):
    states = []
    for l in range(DEPTH):
        if grid and l == 2:
            x = _to_colmajor(x, B, L)
        mods_l = mods[l]
        if l % 2 == 0:
            x, st = _even_layer(x, mods_l, P, l // 2, l, B, L, init_states[l], per_batch_cond, kspecs[l // 2], tabs)
        else:
            x, st = _odd_layer(x, mods_l, P, l // 2, l, B, L, init_states[l], per_batch_cond)
        states.append(st)
        x = _ffn(x, mods_l, P["norm2_g"].reshape(DEPTH, 1, D_MODEL), P["ffn_w1"], P["ffn_w3"], P["ffn_w2"],
                 P["final_g"].reshape(1, D_MODEL), l, seq_len=L, per_batch_cond=per_batch_cond,
                 final=(l == DEPTH - 1))
    if grid:
        x = _from_colmajor(x, B, L)
    return x, states


def _pack_mlstm_state(C, n, m):
    B = C.shape[0]
    ext = jnp.concatenate([jnp.swapaxes(C, -1, -2), n[..., None, :],
                           jnp.zeros(C.shape[:-2] + (M_DH - 1, M_DH), F32)], axis=-2)
    m_rep = jnp.broadcast_to(m.reshape(B, 2 * M_HEADS, 1, 1), (B, 2 * M_HEADS, 1, M_DH))
    return ext.reshape(B, 2 * M_HEADS, 2 * M_DH, M_DH), m_rep


def kernel(x_prompt, x_sample, state_mlstm_C, state_mlstm_n, state_mlstm_m, state_ssd, c, c_ctx,
           mod_w, mod_b, norm1_g, norm2_g, ffn_w1, ffn_w3, ffn_w2, final_g,
           ev_in_w, ev_conv_w, ev_conv_b, hy_w1, hy_b1, hy_w2, hy_b2, hy_w3, hy_freq, hy_bias,
           m_gate_b, m_norm_g, ev_out_w,
           od_in_w, od_conv_w, od_conv_b, ssd_dt_bias, ssd_A_log, ssd_D, ssd_norm_g, od_out_w):
    Bp, Lp, _ = x_prompt.shape
    Bs, Ls, _ = x_sample.shape
    bf = lambda w: w.astype(BF16)
    n_odd = od_in_w.shape[0]
    od_dt_w = od_in_w[:, :, OD_PROJ - 2 * SSD_HEADS:].reshape(n_odd, D_MODEL, 2, SSD_GROUPS, SSD_HPG)
    od_dt_w = od_dt_w.transpose(0, 3, 2, 4, 1).reshape(n_odd, 2 * SSD_HEADS, D_MODEL)
    P = dict(norm1_g=norm1_g, norm2_g=norm2_g, ffn_w1=bf(ffn_w1), ffn_w3=bf(ffn_w3), ffn_w2=bf(ffn_w2),
             final_g=final_g, ev_in_w=bf(ev_in_w),
             ev_in_wt=bf(jnp.concatenate([ev_in_w[:, :, EV_CONV:EV_CONV + M_W],
                                          ev_in_w[:, :, EV_PROJ - 4 * M_HEADS:]], axis=2)).swapaxes(1, 2),
             ev_conv_w=ev_conv_w, ev_conv_b=ev_conv_b, hy_bias=hy_bias, m_gate_b=m_gate_b, m_norm_g=m_norm_g,
             ev_out_w=bf(ev_out_w), od_in_w=bf(od_in_w), od_in_wt=bf(od_dt_w),
             od_conv_w=od_conv_w, od_conv_b=od_conv_b, ssd_dt_bias=ssd_dt_bias, ssd_A_log=ssd_A_log,
             ssd_D=ssd_D, ssd_norm_g=ssd_norm_g.reshape(-1, 1, SSD_INNER), od_out_w=bf(od_out_w))

    ncond = 16
    cond = jnp.concatenate([c_ctx[None, :], c, jnp.zeros((ncond - 1 - Bs, D_MODEL), F32)], axis=0)
    mods = _modulation(cond, mod_w, mod_b).reshape(DEPTH, ncond, 6, D_MODEL)

    tabs, kspecs = {}, {}
    for L in (Lp, Ls):
        direct = L <= HY_DIRECT_MAX_L
        tabs[L] = _direct_tables(L) if direct else _dft_tables(L)
        spectrum = _hy_direct_spectrum if direct else _hy_spectrum
        kspecs[L] = [spectrum(L, tabs[L], hy_w1[e], hy_b1[e], hy_w2[e], hy_b2[e], hy_w3[e], hy_freq[e])
                     for e in range((DEPTH + 1) // 2)]

    G = SSD_GROUPS
    zero_m = _pack_mlstm_state(jnp.zeros((Bp, 2, M_HEADS, M_DH, M_DH), F32), jnp.zeros((Bp, 2, M_HEADS, M_DH), F32),
                               jnp.zeros((Bp, 2, M_HEADS), F32))
    zero_s = jnp.zeros((Bp, 2, G, SSD_GW, SSD_STATE), F32)
    ctx_init = [zero_m if l % 2 == 0 else zero_s for l in range(DEPTH)]
    y_prompt, ctx_states = _trunk(x_prompt.reshape(Bp * Lp, D_MODEL), mods, P, Bp, Lp, ctx_init, False, False,
                                  kspecs[Lp], tabs[Lp])

    lat_init = []
    for l in range(DEPTH):
        if l % 2 == 0:
            e = l // 2
            lat_init.append(_pack_mlstm_state(state_mlstm_C[:, e], state_mlstm_n[:, e], state_mlstm_m[:, e]))
        else:
            lat_init.append(state_ssd[:, l // 2].reshape(Bs, 2, G, SSD_GW, SSD_STATE))
    y_sample, _ = _trunk(x_sample.reshape(Bs * Ls, D_MODEL), mods, P, Bs, Ls, lat_init, True, True,
                         kspecs[Ls], tabs[Ls])

    cs = [ctx_states[l][0].reshape(Bp, 2, M_HEADS, 2 * M_DH, M_DH) for l in range(0, DEPTH, 2)]
    ms = [ctx_states[l][1][:, :, 0, 0].reshape(Bp, 2, M_HEADS) for l in range(0, DEPTH, 2)]
    new_C = jnp.stack([jnp.swapaxes(t[..., :M_DH, :], -1, -2) for t in cs], axis=1)
    new_n = jnp.stack([t[..., M_DH, :] for t in cs], axis=1)
    new_m = jnp.stack(ms, axis=1)
    new_ssd = jnp.stack([ctx_states[l].reshape(Bp, 2, SSD_HEADS, SSD_HEADDIM, SSD_STATE)
                         for l in range(1, DEPTH, 2)], axis=1)
    return (y_prompt.reshape(Bp, Lp, D_MODEL), y_sample.reshape(Bs, Ls, D_MODEL), new_C, new_n, new_m, new_ssd)
```

```python
import functools
import math

import jax
import jax.numpy as jnp
from jax import lax
from jax.experimental import pallas as pl
from jax.experimental.pallas import tpu as pltpu

F32 = jnp.float32
BF16 = jnp.bfloat16

D_MODEL = 1024
DEPTH = 4
GRID_W = 64
EPS = 1e-6
HY_W = D_MODEL // 2
HY_ORDER = 2
HY_EMB = 33
HY_BANDS = (HY_EMB - 1) // 2
HY_HID = 64
HY_TARGET = 1e-2
HY_FAST = 0.3
HY_SLOW = 1.5
M_HEADS = 4
M_DH = 128
M_W = M_HEADS * M_DH
EV_CONV = 3 * HY_W + 2 * M_W
EV_PROJ = EV_CONV + 2 * M_W + 4 * M_HEADS
SSD_INNER = 2 * D_MODEL
SSD_HEADDIM = 64
SSD_HEADS = SSD_INNER // SSD_HEADDIM
SSD_GROUPS = 8
SSD_HPG = SSD_HEADS // SSD_GROUPS
SSD_STATE = 128
SSD_GW = SSD_HPG * SSD_HEADDIM
SSD_CONV_DIM = SSD_INNER + 2 * SSD_GROUPS * SSD_STATE
OD_PROJ = SSD_INNER + SSD_CONV_DIM + 2 * SSD_HEADS
FFN_HIDDEN = -(-8 * D_MODEL // (3 * 256)) * 256

V7X_VMEM_BYTES = 64 * 1024 * 1024
VMEM_LIMIT = V7X_VMEM_BYTES * 3 // 4
WIDE_TILE = 512
HALO = 16
COL_CHUNK = 512
HY_NB = 16
SCAN_T = 128
SSD_BLOCK_TOKENS = 2048
MLSTM_BLOCK_TOKENS = 1024


def _cparams(*sem):
    return pltpu.CompilerParams(dimension_semantics=sem, vmem_limit_bytes=VMEM_LIMIT)


def _dot(a, b):
    return jnp.dot(a, b, preferred_element_type=F32)


def _dot_nt(a, b):
    return lax.dot_general(a, b, (((1,), (1,)), ((), ())), preferred_element_type=F32)


def _split2(x):
    h = x.astype(BF16)
    return h, (x - h.astype(F32)).astype(BF16)


def _dot3(th, tl, x):
    xh, xl = _split2(x)
    return _dot(th, xh) + _dot(th, xl) + _dot(tl, xh)


def _dot2(x, t):
    xh, xl = _split2(x)
    return _dot(xh, t) + _dot(xl, t)


def _sigmoid(x):
    return 1.0 / (1.0 + jnp.exp(-x))


def _silu(x):
    return x * _sigmoid(x)


def _log_sigmoid(x):
    return jnp.minimum(x, 0.0) - jnp.log(1.0 + jnp.exp(-jnp.abs(x)))


def _softplus(x):
    return jnp.maximum(x, 0.0) + jnp.log(1.0 + jnp.exp(-jnp.abs(x)))


def _rms(x):
    return x * lax.rsqrt(jnp.mean(x * x, axis=-1, keepdims=True) + EPS)


def _mod_kernel(c_ref, w_ref, b_ref, o_ref):
    a = _silu(c_ref[...]).astype(BF16)
    o_ref[0] = _dot(a, w_ref[0].astype(BF16)) + b_ref[0]


def _modulation(cond, mod_w, mod_b):
    nc = cond.shape[0]
    tn = 1536
    return pl.pallas_call(
        _mod_kernel,
        grid=(DEPTH, 6 * D_MODEL // tn),
        in_specs=[pl.BlockSpec((nc, D_MODEL), lambda l, j: (0, 0)),
                  pl.BlockSpec((1, D_MODEL, tn), lambda l, j: (l, 0, j)),
                  pl.BlockSpec((1, 1, tn), lambda l, j: (l, 0, j))],
        out_specs=pl.BlockSpec((1, nc, tn), lambda l, j: (l, 0, j)),
        out_shape=jax.ShapeDtypeStruct((DEPTH, nc, 6 * D_MODEL), F32),
        compiler_params=_cparams("arbitrary", "arbitrary"),
        name="modulation",
    )(cond, mod_w, mod_b.reshape(DEPTH, 1, 6 * D_MODEL))


def _inproj_kernel(x_ref, xp_ref, xn_ref, mod_ref, g_ref, w_ref, wt_ref, cw_ref, cb_ref, *rest,
                   plan, nt_plan, tiles_per_seq, conv0):
    nseg = len(plan)
    outs = rest[:nseg]
    nt_outs = rest[nseg:nseg + len(nt_plan)]
    hbuf = rest[nseg + len(nt_plan)]
    tm = x_ref.shape[0]
    i = pl.program_id(0)
    j = i % tiles_per_seq
    pmask = (j != 0).astype(F32)
    nmask = (j != tiles_per_seq - 1).astype(F32)
    mod = mod_ref[0]
    sh, sc = mod[0:1], mod[1:2]
    g = g_ref[...]

    def normmod(xv):
        return (_rms(xv) * g * (1.0 + sc) + sh).astype(BF16)

    hbuf[0:tm] = normmod(x_ref[...])
    hbuf[tm:tm + HALO] = normmod(xp_ref[...])
    hbuf[tm + HALO:tm + 2 * HALO] = normmod(xn_ref[...])
    rowid = lax.broadcasted_iota(jnp.int32, (tm, COL_CHUNK), 0)

    for (c0, width, kind, scale, transposed), o_ref in zip(plan, outs):
        for cc in range(c0, c0 + width, COL_CHUNK):
            w = w_ref[0, :, cc:cc + COL_CHUNK]
            if kind == "raw":
                y = _dot(hbuf[0:tm], w)
            else:
                full = _dot(hbuf[...], w)
                acc = full[0:tm]
                ap = full[tm + HALO - 1:tm + HALO] * pmask
                an = full[tm + HALO:tm + HALO + 1] * nmask
                prev = jnp.where(rowid == 0, ap, pltpu.roll(acc, 1, 0))
                nxt = jnp.where(rowid == tm - 1, an, pltpu.roll(acc, tm - 1, 0))
                cw = cw_ref[0, :, cc - conv0:cc - conv0 + COL_CHUNK]
                cb = cb_ref[0, :, cc - conv0:cc - conv0 + COL_CHUNK]
                y = cb + cw[0:1] * prev + cw[1:2] * acc + cw[2:3] * nxt
                if kind == "conv_silu":
                    y = _silu(y)
                if scale != 1.0:
                    y = y * scale
            if transposed:
                o_ref[cc - c0:cc - c0 + COL_CHUNK, :] = y.T.astype(o_ref.dtype)
            else:
                o_ref[:, cc - c0:cc - c0 + COL_CHUNK] = y.astype(o_ref.dtype)

    for (r0, rows), o_ref in zip(nt_plan, nt_outs):
        o_ref[...] = _dot_nt(wt_ref[0, r0:r0 + rows, :], hbuf[0:tm]).astype(o_ref.dtype)


def _inproj(x, mods_l, g_l, w_all, wt_all, cw_all, cb_all, layer_idx, *, plan, nt_plan, conv0, seq_len,
            per_batch_cond):
    M = x.shape[0]
    tm = min(WIDE_TILE, seq_len)
    tps = seq_len // tm
    nt = M // tm
    hb = tm // HALO
    ntot = w_all.shape[2]
    tw = wt_all.shape[1]
    cdim = cw_all.shape[2]
    e = layer_idx
    cond_map = (lambda i: (1 + i // tps, 0, 0)) if per_batch_cond else (lambda i: (0, 0, 0))
    in_specs = [
        pl.BlockSpec((tm, D_MODEL), lambda i: (i, 0)),
        pl.BlockSpec((HALO, D_MODEL), lambda i: (jnp.maximum(i * hb - 1, 0), 0)),
        pl.BlockSpec((HALO, D_MODEL), lambda i: (jnp.minimum((i + 1) * hb, nt * hb - 1), 0)),
        pl.BlockSpec((1, 6, D_MODEL), cond_map),
        pl.BlockSpec((1, D_MODEL), lambda i: (0, 0)),
        pl.BlockSpec((1, D_MODEL, ntot), lambda i: (e, 0, 0), pipeline_mode=pl.Buffered(1)),
        pl.BlockSpec((1, tw, D_MODEL), lambda i: (e, 0, 0)),
        pl.BlockSpec((1, 3, cdim), lambda i: (e, 0, 0)),
        pl.BlockSpec((1, 1, cdim), lambda i: (e, 0, 0)),
    ]
    out_specs, out_shape = [], []
    for (c0, width, kind, scale, transposed, dt) in plan:
        if transposed:
            out_specs.append(pl.BlockSpec((width, tm), lambda i: (0, i)))
            out_shape.append(jax.ShapeDtypeStruct((width, M), dt))
        else:
            out_specs.append(pl.BlockSpec((tm, width), lambda i: (i, 0)))
            out_shape.append(jax.ShapeDtypeStruct((M, width), dt))
    for (r0, rows, dt) in nt_plan:
        out_specs.append(pl.BlockSpec((rows, tm), lambda i: (0, i)))
        out_shape.append(jax.ShapeDtypeStruct((rows, M), dt))
    kplan = tuple(p[:5] for p in plan)
    knt = tuple(p[:2] for p in nt_plan)
    return pl.pallas_call(
        functools.partial(_inproj_kernel, plan=kplan, nt_plan=knt, tiles_per_seq=tps, conv0=conv0),
        grid=(nt,),
        in_specs=in_specs,
        out_specs=out_specs,
        out_shape=out_shape,
        scratch_shapes=[pltpu.VMEM((tm + 2 * HALO, D_MODEL), BF16)],
        compiler_params=_cparams("arbitrary"),
        name="inproj",
    )(x, x, x, mods_l, g_l, w_all, wt_all, cw_all, cb_all.reshape(cb_all.shape[0], 1, cdim))


def _outproj_even_kernel(yh_ref, ym_ref, x_ref, mod_ref, w_ref, o_ref):
    out = _dot(yh_ref[...], w_ref[0, 0:HY_W]) + _dot(ym_ref[...], w_ref[0, HY_W:])
    o_ref[...] = x_ref[...] + mod_ref[0][2:3] * out


def _outproj_odd_kernel(u_ref, x_ref, mod_ref, ng_ref, w_ref, o_ref):
    un = (_rms(u_ref[...]) * ng_ref[0]).astype(BF16)
    o_ref[...] = x_ref[...] + mod_ref[0][2:3] * _dot(un, w_ref[0])


def _cond_map(seq_len, per_batch_cond):
    tps = seq_len // WIDE_TILE
    return (lambda i: (1 + i // tps, 0, 0)) if per_batch_cond else (lambda i: (0, 0, 0))


def _outproj_even(yh, ym, x, mods_l, w_all, e, *, seq_len, per_batch_cond):
    M = x.shape[0]
    tm = WIDE_TILE
    return pl.pallas_call(
        _outproj_even_kernel,
        grid=(M // tm,),
        in_specs=[pl.BlockSpec((tm, HY_W), lambda i: (i, 0)),
                  pl.BlockSpec((tm, M_W), lambda i: (i, 0)),
                  pl.BlockSpec((tm, D_MODEL), lambda i: (i, 0)),
                  pl.BlockSpec((1, 6, D_MODEL), _cond_map(seq_len, per_batch_cond)),
                  pl.BlockSpec((1, HY_W + M_W, D_MODEL), lambda i: (e, 0, 0), pipeline_mode=pl.Buffered(1))],
        out_specs=pl.BlockSpec((tm, D_MODEL), lambda i: (i, 0)),
        out_shape=jax.ShapeDtypeStruct((M, D_MODEL), F32),
        compiler_params=_cparams("arbitrary"),
        name="outproj_even",
    )(yh, ym, x, mods_l, w_all)


def _outproj_odd(u, x, mods_l, ng_all, w_all, o, *, seq_len, per_batch_cond):
    M = x.shape[0]
    tm = WIDE_TILE
    return pl.pallas_call(
        _outproj_odd_kernel,
        grid=(M // tm,),
        in_specs=[pl.BlockSpec((tm, SSD_INNER), lambda i: (i, 0)),
                  pl.BlockSpec((tm, D_MODEL), lambda i: (i, 0)),
                  pl.BlockSpec((1, 6, D_MODEL), _cond_map(seq_len, per_batch_cond)),
                  pl.BlockSpec((1, 1, SSD_INNER), lambda i: (o, 0, 0)),
                  pl.BlockSpec((1, SSD_INNER, D_MODEL), lambda i: (o, 0, 0), pipeline_mode=pl.Buffered(1))],
        out_specs=pl.BlockSpec((tm, D_MODEL), lambda i: (i, 0)),
        out_shape=jax.ShapeDtypeStruct((M, D_MODEL), F32),
        compiler_params=_cparams("arbitrary"),
        name="outproj_odd",
    )(u, x, mods_l, ng_all, w_all)


FFN_CHUNK = FFN_HIDDEN // 2


def _ffn_kernel(x_ref, mod_ref, g_ref, w1_ref, w3_ref, w2_ref, fg_ref, o_ref, *, final):
    x = x_ref[...]
    mod = mod_ref[0]
    h = (_rms(x) * g_ref[0] * (1.0 + mod[4:5]) + mod[3:4]).astype(BF16)
    ff = jnp.zeros(x.shape, F32)
    for c0 in range(0, FFN_HIDDEN, FFN_CHUNK):
        a = _dot(h, w1_ref[0, :, c0:c0 + FFN_CHUNK])
        b = _dot(h, w3_ref[0, :, c0:c0 + FFN_CHUNK])
        u = (_silu(a) * b).astype(BF16)
        ff = ff + _dot(u, w2_ref[0, c0:c0 + FFN_CHUNK, :])
    xo = x + mod[5:6] * ff
    if final:
        xo = _rms(xo) * fg_ref[...]
    o_ref[...] = xo


def _ffn(x, mods_l, g_all, w1, w3, w2, fg, l, *, seq_len, per_batch_cond, final):
    M = x.shape[0]
    tm = WIDE_TILE
    wspec = lambda shape: pl.BlockSpec(shape, lambda i: (l, 0, 0), pipeline_mode=pl.Buffered(1))
    return pl.pallas_call(
        functools.partial(_ffn_kernel, final=final),
        grid=(M // tm,),
        in_specs=[pl.BlockSpec((tm, D_MODEL), lambda i: (i, 0)),
                  pl.BlockSpec((1, 6, D_MODEL), _cond_map(seq_len, per_batch_cond)),
                  pl.BlockSpec((1, 1, D_MODEL), lambda i: (l, 0, 0)),
                  wspec((1, D_MODEL, FFN_HIDDEN)), wspec((1, D_MODEL, FFN_HIDDEN)),
                  wspec((1, FFN_HIDDEN, D_MODEL)),
                  pl.BlockSpec((1, D_MODEL), lambda i: (0, 0))],
        out_specs=pl.BlockSpec((tm, D_MODEL), lambda i: (i, 0)),
        out_shape=jax.ShapeDtypeStruct((M, D_MODEL), F32),
        compiler_params=_cparams("arbitrary"),
        name="ffn",
    )(x, mods_l, g_all, w1, w3, w2, fg)


def _scan_consts(d):
    T = SCAN_T
    row = lax.broadcasted_iota(jnp.int32, (T, T), 0)
    col = lax.broadcasted_iota(jnp.int32, (T, T), 1)
    mask, last = (row <= col, T - 1) if d == 0 else (row >= col, 0)
    v_mat = jnp.where(mask, 1.0, 0.0).astype(BF16)
    return mask, v_mat, last


def _mlstm_kernel(qt_ref, k_ref, vt_ref, o_ref, gt_ref, gb_ref, ng_ref, c0_ref, m0_ref,
                  y_ref, cst_ref, mst_ref, h_scr, c_scr, m_scr, sb_scr, vw_scr, rows_scr, *, nc, ncs):
    T = SCAN_T
    H = M_HEADS
    nseq = nc // ncs
    ones_blk = jnp.where(lax.broadcasted_iota(jnp.int32, (M_DH, T), 0) == 0, 1.0, 0.0)
    consts = (_scan_consts(0), _scan_consts(1))
    gb = gb_ref[...]

    def gates(ci, d, slot):
        mask, v_mat, last = consts[d]
        c0 = pl.multiple_of(ci * T, T)
        lanes = pl.ds(c0, T)
        graw = gt_ref[:, lanes] + gb
        lf = _log_sigmoid(graw)
        b_all = _dot2(lf, v_mat)
        for h in range(H):
            fi = (2 * d + 1) * H + h
            ii = (2 * d) * H + h
            hl = slice(h * M_DH, (h + 1) * M_DH)
            tr = slice(h * T, (h + 1) * T)
            b_row, i_row = b_all[fi:fi + 1], graw[ii:ii + 1]
            dlog = jnp.where(mask, jnp.broadcast_to(b_row, (T, T)) - jnp.broadcast_to(b_row - i_row, (T, T)).T,
                             -jnp.inf)
            mloc = jnp.max(dlog, axis=0, keepdims=True)
            sb_scr[slot, d, tr, :] = (_dot(k_ref[pl.ds(c0, T), hl], qt_ref[hl, lanes])
                                      * jnp.exp(dlog - mloc)).astype(BF16)
            mloc_last = jnp.broadcast_to(mloc[:, last:last + 1], (1, T))
            b_last = jnp.broadcast_to(b_row[:, last:last + 1], (1, T))
            w_end = jnp.exp(b_last - b_row + i_row - mloc_last)
            vw_scr[slot, d, h, 0:M_DH, :] = (vt_ref[hl, lanes].astype(F32) * w_end).astype(BF16)
            vw_scr[slot, d, h, M_DH:, :] = (ones_blk * w_end).astype(BF16)
            rows_scr[slot, d, 4 * h:4 * h + 4, :] = jnp.concatenate([b_row, mloc, mloc_last, b_last], axis=0)

    def update(ci, d, slot, accumulate):
        c0 = pl.multiple_of(ci * T, T)
        lanes = pl.ds(c0, T)
        for h in range(H):
            sidx = d * H + h
            hl = slice(h * M_DH, (h + 1) * M_DH)
            tr = slice(h * T, (h + 1) * T)
            rows = rows_scr[slot, d, 4 * h:4 * h + 4, :]
            b_row, mloc, mloc_last, b_last = rows[0:1], rows[1:2], rows[2:3], rows[3:4]
            m_row = m_scr[sidx]
            inter = b_row + m_row
            mt = jnp.maximum(inter, mloc)
            kh = k_ref[pl.ds(c0, T), hl]
            qt = qt_ref[hl, lanes]
            vext = jnp.concatenate([vt_ref[hl, lanes], ones_blk.astype(BF16)], axis=0)
            ct = c_scr[sidx]
            tot = (jnp.exp(mloc - mt) * _dot(vext, sb_scr[slot, d, tr, :])
                   + jnp.exp(inter - mt) * _dot(ct.astype(BF16), qt))
            den = tot[M_DH:M_DH + 1]
            ht = tot[0:M_DH] / jnp.maximum(jnp.abs(den), jnp.exp(-mt))
            if accumulate:
                h_scr[hl, lanes] = h_scr[hl, lanes] + ht
            else:
                h_scr[hl, lanes] = ht
            m_new = jnp.maximum(b_last + m_row, mloc_last)
            c_scr[sidx] = (jnp.exp(b_last + m_row - m_new) * ct
                           + jnp.exp(mloc_last - m_new) * _dot(vw_scr[slot, d, h], kh))
            m_scr[sidx] = m_new

    def chunks_at(j):
        if nseq == 1:
            return j, nc - 1 - j
        seq, loc = divmod(j, ncs)
        return j, seq * ncs + ncs - 1 - loc

    def load_state(seq):
        c_scr[...] = c0_ref[seq]
        m_scr[...] = m0_ref[seq]

    def store_state(seq):
        cst_ref[seq] = c_scr[...]
        mst_ref[seq] = m_scr[...]

    def pair(p, slots, accumulate=None):
        s0, s1 = slots
        for j, slot in ((2 * p, s0), (2 * p + 1, s1)):
            if nseq > 1 and j % ncs == 0:
                load_state(j // ncs)
            acc = (j % ncs >= ncs // 2) if accumulate is None else accumulate
            cf, cb = chunks_at(j)
            update(cf, 0, slot, acc)
            update(cb, 1, slot, acc)
            if nseq > 1 and j % ncs == ncs - 1:
                store_state(j // ncs)
        for j, slot in ((2 * p + 2, 2 - s0), (2 * p + 3, 4 - s1)):
            if nseq == 1:
                gates(jnp.minimum(j, nc - 1), 0, slot)
                gates(jnp.maximum(nc - 1 - j, 0), 1, slot)
            elif j < nc:
                cf, cb = chunks_at(j)
                gates(cf, 0, slot)
                gates(cb, 1, slot)

    for j in (0, 1):
        cf, cb = chunks_at(j)
        gates(cf, 0, j)
        gates(cb, 1, j)
    if nseq == 1:
        load_state(0)
    if nseq == 1 and nc % 8 == 0:
        def body(accumulate):
            def run(i, carry):
                pair(2 * i, (0, 1), accumulate)
                pair(2 * i + 1, (2, 3), accumulate)
                return carry
            return run
        lax.fori_loop(0, nc // 8, body(False), 0)
        lax.fori_loop(nc // 8, nc // 4, body(True), 0)
    else:
        for p in range(nc // 2):
            pair(p, ((0, 1), (2, 3))[p % 2])
    if nseq == 1:
        store_state(0)

    def epilogue(i, carry):
        c0 = pl.multiple_of(i * T, T)
        for h in range(H):
            hl = slice(h * M_DH, (h + 1) * M_DH)
            hh = h_scr[hl, pl.ds(c0, T)]
            hn = hh * lax.rsqrt(jnp.mean(hh * hh, axis=0, keepdims=True) + EPS)
            y = hn.T * ng_ref[:, hl] * _sigmoid(o_ref[pl.ds(c0, T), hl])
            y_ref[pl.ds(c0, T), hl] = y.astype(y_ref.dtype)
        return carry

    lax.fori_loop(0, nc, epilogue, 0)


def _mlstm(qt, k, vt, o, gt, gate_b, norm_g, c0, m0, B, L):
    assert L % (2 * SCAN_T) == 0 and M_DH == SCAN_T
    bb = max(1, min(B, MLSTM_BLOCK_TOKENS // L))
    assert B % bb == 0
    ncs = L // SCAN_T
    nc = bb * ncs
    Lb = bb * L
    ng = 4 * M_HEADS
    M = B * L
    one = pl.Buffered(1)
    st_spec = pl.BlockSpec((bb, 2 * M_HEADS, 2 * M_DH, M_DH), lambda b: (b, 0, 0, 0))
    m_spec = pl.BlockSpec((bb, 2 * M_HEADS, 1, M_DH), lambda b: (b, 0, 0, 0))
    return pl.pallas_call(
        functools.partial(_mlstm_kernel, nc=nc, ncs=ncs),
        grid=(B // bb,),
        in_specs=[pl.BlockSpec((M_W, Lb), lambda b: (0, b), pipeline_mode=one),
                  pl.BlockSpec((Lb, M_W), lambda b: (b, 0), pipeline_mode=one),
                  pl.BlockSpec((M_W, Lb), lambda b: (0, b), pipeline_mode=one),
                  pl.BlockSpec((Lb, M_W), lambda b: (b, 0), pipeline_mode=one),
                  pl.BlockSpec((ng, Lb), lambda b: (0, b)),
                  pl.BlockSpec((ng, SCAN_T), lambda b: (0, 0)),
                  pl.BlockSpec((1, M_W), lambda b: (0, 0)),
                  st_spec, m_spec],
        out_specs=[pl.BlockSpec((Lb, M_W), lambda b: (b, 0)), st_spec, m_spec],
        out_shape=[jax.ShapeDtypeStruct((M, M_W), BF16),
                   jax.ShapeDtypeStruct((B, 2 * M_HEADS, 2 * M_DH, M_DH), F32),
                   jax.ShapeDtypeStruct((B, 2 * M_HEADS, 1, M_DH), F32)],
        scratch_shapes=[pltpu.VMEM((M_W, Lb), F32),
                        pltpu.VMEM((2 * M_HEADS, 2 * M_DH, M_DH), F32),
                        pltpu.VMEM((2 * M_HEADS, 1, M_DH), F32),
                        pltpu.VMEM((4, 2, M_HEADS * SCAN_T, SCAN_T), BF16),
                        pltpu.VMEM((4, 2, M_HEADS, 2 * M_DH, SCAN_T), BF16),
                        pltpu.VMEM((4, 2, 4 * M_HEADS, SCAN_T), F32)],
        compiler_params=_cparams("arbitrary"),
        name="mlstm",
    )(qt, k, vt, o, gt, jnp.broadcast_to(gate_b.reshape(ng, 1), (ng, SCAN_T)), norm_g.reshape(1, M_W), c0, m0)


def _ssd_kernel(xt_ref, b_ref, ct_ref, dtt_ref, z_ref, db_ref, al_ref, dsk_ref, s0_ref,
                y_ref, sout_ref, yf_scr, yb_scr, s_scr, cbt_scr, p_scr, xd_scr, xdec_scr, rows_scr, *, nc, ncs):
    T = SCAN_T
    H = SSD_HPG
    nseq = nc // ncs
    consts = (_scan_consts(0), _scan_consts(1))
    db = db_ref[0]
    a_neg = -jnp.exp(al_ref[0])

    def gates(ci, d, slot):
        mask, v_mat, last = consts[d]
        c0 = pl.multiple_of(ci * T, T)
        lanes = pl.ds(c0, T)
        hr = slice(d * H, (d + 1) * H)
        dt = _softplus(dtt_ref[hr, lanes] + db[hr])
        a = dt * a_neg[hr]
        acs = _dot2(a, v_mat)
        a_last = jnp.broadcast_to(acs[:, last:last + 1], (H, T))
        rows_scr[slot, d, 0:H, :] = jnp.exp(acs)
        rows_scr[slot, d, H:2 * H, :] = jnp.exp(a_last)
        cbt_scr[d] = _dot(b_ref[pl.ds(c0, T), :], ct_ref[:, lanes])
        for r in range(H):
            tr = slice(r * T, (r + 1) * T)
            pr = slice(r * SSD_HEADDIM, (r + 1) * SSD_HEADDIM)
            al = jnp.broadcast_to(acs[r:r + 1], (T, T))
            lm = jnp.exp(jnp.where(mask, al - al.T, -jnp.inf))
            p_scr[slot, d, tr, :] = (cbt_scr[d] * lm).astype(BF16)
            xd = xt_ref[pr, lanes] * dt[r:r + 1]
            xd_scr[slot, d, pr, :] = xd.astype(BF16)
            xdec_scr[slot, d, pr, :] = (xd * jnp.exp(a_last[r:r + 1] - acs[r:r + 1])).astype(BF16)

    def update(ci, d, slot):
        c0 = pl.multiple_of(ci * T, T)
        lanes = pl.ds(c0, T)
        rows = rows_scr[slot, d]
        sc = _dot(s_scr[d].astype(BF16), ct_ref[:, lanes])
        upd = _dot(xdec_scr[slot, d], b_ref[pl.ds(c0, T), :])
        y_scr = yf_scr if d == 0 else yb_scr
        for r in range(H):
            pr = slice(r * SSD_HEADDIM, (r + 1) * SSD_HEADDIM)
            tr = slice(r * T, (r + 1) * T)
            y_scr[pr, lanes] = _dot(xd_scr[slot, d, pr, :], p_scr[slot, d, tr, :]) + rows[r:r + 1] * sc[pr]
            s_scr[d, pr, :] = rows[H + r:H + r + 1] * s_scr[d, pr, :] + upd[pr]

    def chunks_at(j):
        if nseq == 1:
            return j, nc - 1 - j
        seq, loc = divmod(j, ncs)
        return j, seq * ncs + ncs - 1 - loc

    def pair(p, slots):
        s0, s1 = slots
        for j, slot in ((2 * p, s0), (2 * p + 1, s1)):
            if nseq > 1 and j % ncs == 0:
                s_scr[...] = s0_ref[j // ncs, :, 0]
            cf, cb = chunks_at(j)
            update(cf, 0, slot)
            update(cb, 1, slot)
            if nseq > 1 and j % ncs == ncs - 1:
                sout_ref[j // ncs, :, 0] = s_scr[...]
        for j, slot in ((2 * p + 2, 2 - s0), (2 * p + 3, 4 - s1)):
            if nseq == 1:
                gates(jnp.minimum(j, nc - 1), 0, slot)
                gates(jnp.maximum(nc - 1 - j, 0), 1, slot)
            elif j < nc:
                cf, cb = chunks_at(j)
                gates(cf, 0, slot)
                gates(cb, 1, slot)

    for j in (0, 1):
        cf, cb = chunks_at(j)
        gates(cf, 0, j)
        gates(cb, 1, j)
    if nseq == 1:
        s_scr[...] = s0_ref[0, :, 0]
    if nseq == 1 and nc % 4 == 0:
        def body(i, carry):
            pair(2 * i, (0, 1))
            pair(2 * i + 1, (2, 3))
            return carry
        lax.fori_loop(0, nc // 4, body, 0)
    else:
        for p in range(nc // 2):
            pair(p, ((0, 1), (2, 3))[p % 2])
    if nseq == 1:
        sout_ref[0, :, 0] = s_scr[...]

    dsk = dsk_ref[0]

    def epilogue(i, carry):
        c0 = pl.multiple_of(i * T, T)
        lanes = pl.ds(c0, T)
        yt = yf_scr[:, lanes] + yb_scr[:, lanes] + dsk * xt_ref[:, lanes]
        y_ref[pl.ds(c0, T), :] = yt.T * _silu(z_ref[pl.ds(c0, T), :])
        return carry

    lax.fori_loop(0, nc, epilogue, 0)


def _ssd(xt, bm, ct, dtt, z, dt_bias, a_log, dskip, s0, B, L):
    assert L % (2 * SCAN_T) == 0 and SSD_STATE == SCAN_T
    bb = max(1, min(B, SSD_BLOCK_TOKENS // L))
    assert B % bb == 0
    ncs = L // SCAN_T
    nc = bb * ncs
    Lb = bb * L
    G = SSD_GROUPS
    nd = 2 * SSD_HPG
    M = B * L
    rep = lambda p, rows: jnp.broadcast_to(p.reshape(G, rows, 1), (G, rows, SCAN_T))
    small = lambda rows: pl.BlockSpec((1, rows, SCAN_T), lambda b, g: (g, 0, 0))
    st_spec = pl.BlockSpec((bb, 2, 1, SSD_GW, SSD_STATE), lambda b, g: (b, 0, g, 0, 0))
    return pl.pallas_call(
        functools.partial(_ssd_kernel, nc=nc, ncs=ncs),
        grid=(B // bb, G),
        in_specs=[pl.BlockSpec((SSD_GW, Lb), lambda b, g: (g, b)),
                  pl.BlockSpec((Lb, SSD_STATE), lambda b, g: (b, g)),
                  pl.BlockSpec((SSD_STATE, Lb), lambda b, g: (g, b)),
                  pl.BlockSpec((nd, Lb), lambda b, g: (g, b)),
                  pl.BlockSpec((Lb, SSD_GW), lambda b, g: (b, g)),
                  small(nd), small(nd), small(SSD_GW), st_spec],
        out_specs=[pl.BlockSpec((Lb, SSD_GW), lambda b, g: (b, g)), st_spec],
        out_shape=[jax.ShapeDtypeStruct((M, SSD_INNER), F32),
                   jax.ShapeDtypeStruct((B, 2, G, SSD_GW, SSD_STATE), F32)],
        scratch_shapes=[pltpu.VMEM((SSD_GW, Lb), F32), pltpu.VMEM((SSD_GW, Lb), F32),
                        pltpu.VMEM((2, SSD_GW, SSD_STATE), F32),
                        pltpu.VMEM((2, SCAN_T, SCAN_T), F32),
                        pltpu.VMEM((4, 2, SSD_HPG * SCAN_T, SCAN_T), BF16),
                        pltpu.VMEM((4, 2, SSD_GW, SCAN_T), BF16),
                        pltpu.VMEM((4, 2, SSD_GW, SCAN_T), BF16),
                        pltpu.VMEM((4, 2, 2 * SSD_HPG, SCAN_T), F32)],
        compiler_params=_cparams("arbitrary", "arbitrary"),
        name="ssd",
    )(xt, bm, ct, dtt, z, rep(dt_bias, nd), rep(a_log, nd), rep(dskip, SSD_GW), s0)


def _hy_factors(L):
    return (128, 64) if L == 4096 else (32, (2 * L) // 32)


def _hilo(t):
    return _split2(t.astype(F32))


def _cblock(cr, ci):
    return jnp.concatenate([jnp.concatenate([cr, -ci], axis=-1), jnp.concatenate([ci, cr], axis=-1)], axis=-2)


def _dft_tables(L):
    n1c, n2c = _hy_factors(L)
    N = n1c * n2c
    h1 = n1c // 2
    i32 = jnp.int32
    k1 = jnp.arange(n1c, dtype=i32)
    n2 = jnp.arange(n2c, dtype=i32)

    def stage1(n1_count):
        n1 = jnp.arange(n1_count, dtype=i32)
        idx = (k1[None, :, None] * (n2[:, None, None] + n2c * n1[None, None, :])) % N
        ang = idx.astype(F32) * (2.0 * math.pi / N)
        return jnp.cos(ang), jnp.sin(ang)

    c, s = stage1(h1)
    g_fwd = _cblock(c, -s)
    g_inv = _cblock(jnp.swapaxes(c, 1, 2), jnp.swapaxes(s, 1, 2)) / n1c
    cf, sf = stage1(n1c)
    g_real = jnp.concatenate([cf, -sf], axis=1)
    idx2 = (n2[:, None] * n2[None, :]) % n2c
    ang2 = idx2.astype(F32) * (2.0 * math.pi / n2c)
    c2, s2 = jnp.cos(ang2), jnp.sin(ang2)
    f2 = _cblock(c2, -s2)
    f2_inv = _cblock(c2, s2) / n2c
    return dict(g_fwd=g_fwd.astype(BF16), g_inv=g_inv.astype(BF16), f2=f2.astype(BF16),
                f2_inv=f2_inv.astype(BF16), g_real=_hilo(g_real), f2_exact=_hilo(f2))


def _swap01(x):
    return jnp.swapaxes(x, 0, 1)


def _hy_s1_kernel(x_ref, *rest, nb, exact):
    if exact:
        gh_ref, gl_ref, o_ref = rest
    else:
        g_ref, o_ref = rest
    pb, rows, _, C = x_ref.shape
    xs = _swap01(x_ref[...].reshape(pb * rows, nb, C))
    if exact:
        res = [_dot3(gh_ref[j], gl_ref[j], xs[j]) for j in range(nb)]
    else:
        res = [_dot(g_ref[j], xs[j].astype(BF16)) for j in range(nb)]
    out = _swap01(jnp.stack(res, axis=0))
    o_ref[...] = out.reshape(o_ref.shape).astype(o_ref.dtype)


def _hy_stage1(x, tab, out_dtype):
    exact = isinstance(tab, tuple)
    tabs = tab if exact else (tab,)
    n2c, rows_out, R = tabs[0].shape
    Bx, rows, _, C = x.shape
    pb = R // rows
    nb = HY_NB
    cb = min(C, HY_W)
    tspec = pl.BlockSpec((nb, rows_out, R), lambda p, t, c: (t, 0, 0))
    return pl.pallas_call(
        functools.partial(_hy_s1_kernel, nb=nb, exact=exact),
        grid=(Bx // pb, n2c // nb, C // cb),
        in_specs=[pl.BlockSpec((pb, rows, nb, cb), lambda p, t, c: (p, 0, t, c))] + [tspec] * len(tabs),
        out_specs=pl.BlockSpec((None, 2, rows_out // 2, nb, cb), lambda p, t, c: (p, 0, 0, t, c)),
        out_shape=jax.ShapeDtypeStruct((Bx // pb, 2, rows_out // 2, n2c, C), out_dtype),
        compiler_params=_cparams("arbitrary", "arbitrary", "arbitrary"),
        name="hyena_stage1",
    )(x, *tabs)


def _hy_spec_kernel(a_ref, inv_ref, fh_ref, fl_ref, o_ref, *, kb):
    n2c = a_ref.shape[2]
    for kk in range(kb):
        rhs = jnp.concatenate([a_ref[0, kk], a_ref[1, kk]], axis=0)
        X = _dot3(fh_ref[...], fl_ref[...], rhs) * inv_ref[...]
        o_ref[0, kk] = X[0:n2c]
        o_ref[1, kk] = X[n2c:]


def _hy_filter_spectrum(a, inv_norm, f2, kb):
    _, n1c, n2c, W = a.shape
    fh, fl = f2
    blk = pl.BlockSpec((2, kb, n2c, W), lambda t: (0, t, 0, 0))
    fspec = pl.BlockSpec((2 * n2c, 2 * n2c), lambda t: (0, 0))
    return pl.pallas_call(
        functools.partial(_hy_spec_kernel, kb=kb),
        grid=(n1c // kb,),
        in_specs=[blk, pl.BlockSpec((1, W), lambda t: (0, 0)), fspec, fspec],
        out_specs=blk,
        out_shape=jax.ShapeDtypeStruct(a.shape, F32),
        compiler_params=_cparams("arbitrary"),
        name="hyena_filter_spectrum",
    )(a, inv_norm, fh, fl)


def _hy_mid_kernel(a_ref, k_ref, f_ref, i_ref, o_ref, *, kb):
    n2c = a_ref.shape[2]
    res = []
    for kk in range(kb):
        rhs = jnp.concatenate([a_ref[0, kk], a_ref[1, kk]], axis=0)
        X = _dot(f_ref[...], rhs)
        xr, xi = X[0:n2c], X[n2c:]
        kr, ki = k_ref[0, kk], k_ref[1, kk]
        Y = jnp.concatenate([xr * kr - xi * ki, xr * ki + xi * kr], axis=0)
        res.append(_dot(i_ref[...], Y.astype(BF16)))
    o_ref[...] = _swap01(jnp.stack(res, axis=0)).reshape(o_ref.shape).astype(o_ref.dtype)


def _hy_mid(a, kspec, order, f2, f2_inv):
    P, _, n1c, n2c, C = a.shape
    kb = HY_NB
    fspec = pl.BlockSpec((2 * n2c, 2 * n2c), lambda t, p: (0, 0))
    return pl.pallas_call(
        functools.partial(_hy_mid_kernel, kb=kb),
        grid=(n1c // kb, P),
        in_specs=[pl.BlockSpec((None, 2, kb, n2c, C), lambda t, p: (p, 0, t, 0, 0)),
                  pl.BlockSpec((2, kb, n2c, C), lambda t, p: (0, t, 0, order)),
                  fspec, fspec],
        out_specs=pl.BlockSpec((None, 2, n2c, kb, C), lambda t, p: (p, 0, 0, t, 0)),
        out_shape=jax.ShapeDtypeStruct((P, 2, n2c, n1c, C), BF16),
        compiler_params=_cparams("arbitrary", "arbitrary"),
        name="hyena_mid",
    )(a, kspec, f2, f2_inv)


def _hy_last_kernel(b_ref, z_ref, gate_ref, bias_ref, gi_ref, *rest, fuse, nb):
    if fuse:
        gf_ref, zo_ref, ao_ref = rest
    else:
        (zo_ref,) = rest
    _, h1, _, C = z_ref.shape
    zs = _swap01(z_ref[...].reshape(2 * h1, nb, C))
    gs = _swap01(gate_ref[...].reshape(2 * h1, nb, C))
    bias = bias_ref[...]
    zn, an = [], []
    for j in range(nb):
        bv = jnp.concatenate([b_ref[0, j], b_ref[1, j]], axis=0)
        y = _dot(gi_ref[j], bv)
        znew = gs[j] * (y + zs[j] * bias)
        zn.append(znew)
        if fuse:
            an.append(_dot(gf_ref[j], znew.astype(BF16)))
    zo_ref[...] = _swap01(jnp.stack(zn, axis=0)).reshape(zo_ref.shape).astype(zo_ref.dtype)
    if fuse:
        ao_ref[...] = _swap01(jnp.stack(an, axis=0)).reshape(ao_ref.shape).astype(ao_ref.dtype)


def _hy_last(bf, z, gate, bias, g_inv, g_fwd, out_dtype):
    P, _, n2c, n1c, C = bf.shape
    B, h1, _, _ = z.shape
    nb = HY_NB
    cb = C // 2
    fuse = g_fwd is not None
    zspec = pl.BlockSpec((2, h1, nb, cb), lambda p, t, c: (p, 0, t, c))
    in_specs = [pl.BlockSpec((None, 2, nb, n1c, cb), lambda p, t, c: (p, 0, t, 0, c)),
                zspec, zspec, pl.BlockSpec((1, cb), lambda p, t, c: (0, c)),
                pl.BlockSpec((nb, 2 * h1, 2 * n1c), lambda p, t, c: (t, 0, 0))]
    args = [bf, z, gate, bias, g_inv]
    out_specs = [zspec]
    out_shape = [jax.ShapeDtypeStruct(z.shape, out_dtype)]
    if fuse:
        in_specs.append(pl.BlockSpec((nb, 2 * n1c, 2 * h1), lambda p, t, c: (t, 0, 0)))
        args.append(g_fwd)
        out_specs.append(pl.BlockSpec((None, 2, n1c, nb, cb), lambda p, t, c: (p, 0, 0, t, c)))
        out_shape.append(jax.ShapeDtypeStruct((P, 2, n1c, n2c, C), BF16))
    return pl.pallas_call(
        functools.partial(_hy_last_kernel, fuse=fuse, nb=nb),
        grid=(P, n2c // nb, C // cb),
        in_specs=in_specs, out_specs=out_specs, out_shape=out_shape,
        compiler_params=_cparams("arbitrary", "arbitrary", "arbitrary"),
        name="hyena_last",
    )(*args)


def _hy_filter_kernel(f_ref, w1_ref, b1_ref, w2_ref, b2_ref, w3_ref, fr_ref, dl_ref, hk_ref, s_ref, *, L):
    i = pl.program_id(0)
    tl = f_ref.shape[0]
    f = f_ref[...]
    fr = fr_ref[...]
    h = jnp.sin(fr * (_dot(f.astype(BF16), w1_ref[...].astype(BF16)) + b1_ref[...]))
    h = jnp.sin(fr * (_dot(h.astype(BF16), w2_ref[...].astype(BF16)) + b2_ref[...]))
    hk = _dot(h.astype(BF16), w3_ref[...].astype(BF16)) * jnp.exp(-f[:, 0:1] * dl_ref[...])
    n = i * tl + lax.broadcasted_iota(jnp.int32, hk.shape, 0)
    hk = jnp.where(n == L, 0.0, hk)
    hk_ref[...] = hk
    tot = jnp.sum(jnp.abs(hk), axis=0, keepdims=True)

    @pl.when(i == 0)
    def _():
        s_ref[...] = tot

    @pl.when(i != 0)
    def _():
        s_ref[...] = s_ref[...] + tot


def _hy_filter(L, w1, b1, w2, b2, w3, freq):
    f32 = F32
    t = jnp.linspace(0.0, 1.0, L, dtype=f32)[:, None]
    ang = (2.0 * math.pi / L) * jnp.arange(L, dtype=f32)[:, None] * \
        jnp.linspace(1e-4, HY_BANDS - 1, HY_BANDS, dtype=f32)[None]
    feats = jnp.concatenate([t, jnp.cos(ang), -jnp.sin(ang)], axis=-1)
    feats = jnp.concatenate([feats, feats[0:1], feats[:0:-1]], axis=0)
    fpad = 128
    feats = jnp.pad(feats, ((0, 0), (0, fpad - HY_EMB)))
    w1p = jnp.pad(w1, ((0, fpad - HY_EMB), (0, 0)))
    deltas = jnp.abs(jnp.linspace(math.log(HY_TARGET) / HY_SLOW, math.log(HY_TARGET) / HY_FAST, HY_W, dtype=f32))
    OC = HY_ORDER * HY_W
    dl = jnp.tile(deltas, HY_ORDER).reshape(1, OC)
    tl = min(L, 512)
    nblk = L // tl
    full = lambda shp: pl.BlockSpec(shp, lambda i: (0, 0))
    return pl.pallas_call(
        functools.partial(_hy_filter_kernel, L=L),
        grid=(2 * nblk,),
        in_specs=[pl.BlockSpec((tl, fpad), lambda i: (i, 0)),
                  full((fpad, HY_HID)), full((1, HY_HID)), full((HY_HID, HY_HID)), full((1, HY_HID)),
                  pl.BlockSpec((HY_HID, OC), lambda i: (0, i // nblk)), full((1, HY_HID)), full((1, OC))],
        out_specs=[pl.BlockSpec((tl, OC), lambda i: (i, 0)), full((1, OC))],
        out_shape=[jax.ShapeDtypeStruct((2 * L, OC), F32), jax.ShapeDtypeStruct((1, OC), F32)],
        compiler_params=_cparams("arbitrary"),
        name="hyena_filter",
    )(feats, w1p, b1.reshape(1, HY_HID), w2, b2.reshape(1, HY_HID), w3, freq.reshape(1, HY_HID), dl)


def _hy_spectrum(L, tabs, w1, b1, w2, b2, w3, freq):
    n1c, n2c = _hy_factors(L)
    OC = HY_ORDER * HY_W
    ktime, s = _hy_filter(L, w1, b1, w2, b2, w3, freq)
    a = _hy_stage1(ktime.reshape(1, n1c, n2c, OC), tabs["g_real"], F32)[0]
    return _hy_filter_spectrum(a, 1.0 / s, tabs["f2_exact"], 8)


def _hyena(v, x1, x2, kspec, bias, tabs, B, L):
    n1c, n2c = _hy_factors(L)
    C = HY_W
    view = lambda t: t.reshape(B, n1c // 2, n2c, C)
    z0, g1, g2 = view(v), view(x1), view(x2)
    a = _hy_stage1(z0, tabs["g_fwd"], BF16)
    bf = _hy_mid(a, kspec, 0, tabs["f2"], tabs["f2_inv"])
    z1, a = _hy_last(bf, z0, g1, bias[0:1], tabs["g_inv"], tabs["g_fwd"], F32)
    bf = _hy_mid(a, kspec, 1, tabs["f2"], tabs["f2_inv"])
    (z2,) = _hy_last(bf, z1, g2, bias[1:2], tabs["g_inv"], None, BF16)
    return z2.reshape(B * L, C)


def _direct_tables(L):
    N = 2 * L
    k = jnp.arange(N, dtype=jnp.int32)
    ang = ((k[:, None] * k[None, :]) % N).astype(F32) * (2.0 * math.pi / N)
    c, s = jnp.cos(ang), jnp.sin(ang)
    fwd = _cblock(c[:, :L], -s[:, :L])
    inv = _cblock(c[:L, :], s[:L, :]) / N
    real = jnp.concatenate([c, -s], axis=0)
    return dict(fwd=fwd.astype(BF16), inv=inv.astype(BF16), real=_hilo(real))


def _hy_direct_spec_kernel(kt_ref, inv_ref, fh_ref, fl_ref, o_ref):
    N = kt_ref.shape[0]
    X = _dot3(fh_ref[...], fl_ref[...], kt_ref[...]) * inv_ref[...]
    o_ref[0] = X[0:N]
    o_ref[1] = X[N:]


def _hy_direct_spectrum(L, tabs, w1, b1, w2, b2, w3, freq):
    N = 2 * L
    OC = HY_ORDER * HY_W
    cb = HY_W // 2
    ktime, s = _hy_filter(L, w1, b1, w2, b2, w3, freq)
    fh, fl = tabs["real"]
    fspec = pl.BlockSpec((2 * N, N), lambda c: (0, 0))
    return pl.pallas_call(
        _hy_direct_spec_kernel,
        grid=(OC // cb,),
        in_specs=[pl.BlockSpec((N, cb), lambda c: (0, c)), pl.BlockSpec((1, cb), lambda c: (0, c)), fspec, fspec],
        out_specs=pl.BlockSpec((2, N, cb), lambda c: (0, 0, c)),
        out_shape=jax.ShapeDtypeStruct((2, N, OC), F32),
        compiler_params=_cparams("arbitrary"),
        name="hyena_direct_spectrum",
    )(ktime, 1.0 / s, fh, fl)


def _hy_direct_kernel(v_ref, g1_ref, g2_ref, k0_ref, k1_ref, bias_ref, f_ref, fi_ref, o_ref):
    L = v_ref.shape[1]
    N = 2 * L
    z = jnp.concatenate([v_ref[0], v_ref[1]], axis=0)
    for o, (k_ref, g_ref) in enumerate(((k0_ref, g1_ref), (k1_ref, g2_ref))):
        X = _dot(f_ref[...], z.astype(BF16))
        xr, xi = X[0:N], X[N:]
        kr, ki = k_ref[0], k_ref[1]
        Y = jnp.concatenate([xr * kr - xi * ki, xr * ki + xi * kr], axis=0)
        y = _dot(fi_ref[...], Y.astype(BF16))
        gate = jnp.concatenate([g_ref[0], g_ref[1]], axis=0)
        z = gate * (y + z * bias_ref[o:o + 1])
    o_ref[0] = z[0:L].astype(o_ref.dtype)
    o_ref[1] = z[L:].astype(o_ref.dtype)


def _hyena_direct(v, x1, x2, kspec, bias, tabs, B, L):
    C = HY_W
    N = 2 * L
    cb = C // 2
    ncb = C // cb
    seq = pl.BlockSpec((2, L, cb), lambda p, c: (p, 0, c))
    view = lambda t: t.reshape(B, L, C)
    out = pl.pallas_call(
        _hy_direct_kernel,
        grid=(B // 2, ncb),
        in_specs=[seq, seq, seq,
                  pl.BlockSpec((2, N, cb), lambda p, c: (0, 0, c)),
                  pl.BlockSpec((2, N, cb), lambda p, c: (0, 0, ncb + c)),
                  pl.BlockSpec((HY_ORDER, cb), lambda p, c: (0, c)),
                  pl.BlockSpec((2 * N, 2 * L), lambda p, c: (0, 0)),
                  pl.BlockSpec((2 * L, 2 * N), lambda p, c: (0, 0))],
        out_specs=seq,
        out_shape=jax.ShapeDtypeStruct((B, L, C), BF16),
        compiler_params=_cparams("arbitrary", "arbitrary"),
        name="hyena_direct",
    )(view(v), view(x1), view(x2), kspec, kspec, bias, tabs["fwd"], tabs["inv"])
    return out.reshape(B * L, C)


HY_DIRECT_MAX_L = 512


_EVEN_PLAN = (
    (0, HY_W, "conv", 1.0, False, F32),
    (HY_W, HY_W, "conv", 1.0, False, F32),
    (2 * HY_W, HY_W, "conv", 1.0, False, F32),
    (3 * HY_W, M_W, "conv_silu", 1.0, True, BF16),
    (3 * HY_W + M_W, M_W, "conv_silu", M_DH ** -0.5, False, BF16),
    (EV_CONV + M_W, M_W, "raw", 1.0, False, F32),
)
_EVEN_NT_PLAN = ((0, M_W, BF16), (M_W, 4 * M_HEADS, F32))
_ODD_NT_PLAN = ((0, 2 * SSD_HEADS, F32),)
_GN = SSD_GROUPS * SSD_STATE
_ODD_PLAN = (
    (0, SSD_INNER, "raw", 1.0, False, F32),
    (SSD_INNER, SSD_INNER, "conv_silu", 1.0, True, F32),
    (2 * SSD_INNER, _GN, "conv_silu", 1.0, False, BF16),
    (2 * SSD_INNER + _GN, _GN, "conv_silu", 1.0, True, BF16),
)


def _even_layer(x, mods_l, P, e, l, B, L, st, per_batch_cond, kspec, tabs):
    hv, hx1, hx2, qt, k, o, vt, gt = _inproj(
        x, mods_l, P["norm1_g"][l:l + 1], P["ev_in_w"], P["ev_in_wt"], P["ev_conv_w"], P["ev_conv_b"], e,
        plan=_EVEN_PLAN, nt_plan=_EVEN_NT_PLAN, conv0=0, seq_len=L, per_batch_cond=per_batch_cond)
    hyena = _hyena_direct if L <= HY_DIRECT_MAX_L else _hyena
    y_hy = hyena(hv, hx1, hx2, kspec, P["hy_bias"][e], tabs, B, L)
    c0, m0 = st
    y_m, c_new, m_new = _mlstm(qt, k, vt, o, gt, P["m_gate_b"][e], P["m_norm_g"][e], c0, m0, B, L)
    x = _outproj_even(y_hy, y_m, x, mods_l, P["ev_out_w"], e, seq_len=L, per_batch_cond=per_batch_cond)
    return x, (c_new, m_new)


def _by_group(p):
    return p.reshape(2, SSD_GROUPS, SSD_HPG).transpose(1, 0, 2).reshape(SSD_GROUPS, 2 * SSD_HPG)


def _odd_layer(x, mods_l, P, o, l, B, L, s0, per_batch_cond):
    z, xt, bm, ct, dtt = _inproj(
        x, mods_l, P["norm1_g"][l:l + 1], P["od_in_w"], P["od_in_wt"], P["od_conv_w"], P["od_conv_b"], o,
        plan=_ODD_PLAN, nt_plan=_ODD_NT_PLAN, conv0=SSD_INNER, seq_len=L, per_batch_cond=per_batch_cond)
    dskip = jnp.repeat(P["ssd_D"][o], SSD_HEADDIM).reshape(SSD_GROUPS, SSD_GW)
    u, s_new = _ssd(xt, bm, ct, dtt, z, _by_group(P["ssd_dt_bias"][o]), _by_group(P["ssd_A_log"][o]), dskip, s0, B, L)
    x = _outproj_odd(u, x, mods_l, P["ssd_norm_g"], P["od_out_w"], o, seq_len=L, per_batch_cond=per_batch_cond)
    return x, s_new


def _to_colmajor(x, B, L):
    rows = L // GRID_W
    return x.reshape(B, rows, GRID_W, D_MODEL).swapaxes(1, 2).reshape(B * L, D_MODEL)


def _from_colmajor(x, B, L):
    rows = L // GRID_W
    return x.reshape(B, GRID_W, rows, D_MODEL).swapaxes(1, 2).reshape(B * L, D_MODEL)


def _trunk(x, mods, P, B, L, init_states, per_batch_cond, grid, kspecs, tabs):
    states = []
    for l in range(DEPTH):
        if grid and l == 2:
            x = _to_colmajor(x, B, L)
        mods_l = mods[l]
        if l % 2 == 0:
            x, st = _even_layer(x, mods_l, P, l // 2, l, B, L, init_states[l], per_batch_cond, kspecs[l // 2], tabs)
        else:
            x, st = _odd_layer(x, mods_l, P, l // 2, l, B, L, init_states[l], per_batch_cond)
        states.append(st)
        x = _ffn(x, mods_l, P["norm2_g"].reshape(DEPTH, 1, D_MODEL), P["ffn_w1"], P["ffn_w3"], P["ffn_w2"],
                 P["final_g"].reshape(1, D_MODEL), l, seq_len=L, per_batch_cond=per_batch_cond,
                 final=(l == DEPTH - 1))
    if grid:
        x = _from_colmajor(x, B, L)
    return x, states


def _pack_mlstm_state(C, n, m):
    B = C.shape[0]
    ext = jnp.concatenate([jnp.swapaxes(C, -1, -2), n[..., None, :],
                           jnp.zeros(C.shape[:-2] + (M_DH - 1, M_DH), F32)], axis=-2)
    m_rep = jnp.broadcast_to(m.reshape(B, 2 * M_HEADS, 1, 1), (B, 2 * M_HEADS, 1, M_DH))
    return ext.reshape(B, 2 * M_HEADS, 2 * M_DH, M_DH), m_rep


def kernel(x_prompt, x_sample, state_mlstm_C, state_mlstm_n, state_mlstm_m, state_ssd, c, c_ctx,
           mod_w, mod_b, norm1_g, norm2_g, ffn_w1, ffn_w3, ffn_w2, final_g,
           ev_in_w, ev_conv_w, ev_conv_b, hy_w1, hy_b1, hy_w2, hy_b2, hy_w3, hy_freq, hy_bias,
           m_gate_b, m_norm_g, ev_out_w,
           od_in_w, od_conv_w, od_conv_b, ssd_dt_bias, ssd_A_log, ssd_D, ssd_norm_g, od_out_w):
    Bp, Lp, _ = x_prompt.shape
    Bs, Ls, _ = x_sample.shape
    bf = lambda w: w.astype(BF16)
    n_odd = od_in_w.shape[0]
    od_dt_w = od_in_w[:, :, OD_PROJ - 2 * SSD_HEADS:].reshape(n_odd, D_MODEL, 2, SSD_GROUPS, SSD_HPG)
    od_dt_w = od_dt_w.transpose(0, 3, 2, 4, 1).reshape(n_odd, 2 * SSD_HEADS, D_MODEL)
    P = dict(norm1_g=norm1_g, norm2_g=norm2_g, ffn_w1=bf(ffn_w1), ffn_w3=bf(ffn_w3), ffn_w2=bf(ffn_w2),
             final_g=final_g, ev_in_w=bf(ev_in_w),
             ev_in_wt=bf(jnp.concatenate([ev_in_w[:, :, EV_CONV:EV_CONV + M_W],
                                          ev_in_w[:, :, EV_PROJ - 4 * M_HEADS:]], axis=2)).swapaxes(1, 2),
             ev_conv_w=ev_conv_w, ev_conv_b=ev_conv_b, hy_bias=hy_bias, m_gate_b=m_gate_b, m_norm_g=m_norm_g,
             ev_out_w=bf(ev_out_w), od_in_w=bf(od_in_w), od_in_wt=bf(od_dt_w),
             od_conv_w=od_conv_w, od_conv_b=od_conv_b, ssd_dt_bias=ssd_dt_bias, ssd_A_log=ssd_A_log,
             ssd_D=ssd_D, ssd_norm_g=ssd_norm_g.reshape(-1, 1, SSD_INNER), od_out_w=bf(od_out_w))

    ncond = 16
    cond = jnp.concatenate([c_ctx[None, :], c, jnp.zeros((ncond - 1 - Bs, D_MODEL), F32)], axis=0)
    mods = _modulation(cond, mod_w, mod_b).reshape(DEPTH, ncond, 6, D_MODEL)

    tabs, kspecs = {}, {}
    for L in (Lp, Ls):
        direct = L <= HY_DIRECT_MAX_L
        tabs[L] = _direct_tables(L) if direct else _dft_tables(L)
        spectrum = _hy_direct_spectrum if direct else _hy_spectrum
        kspecs[L] = [spectrum(L, tabs[L], hy_w1[e], hy_b1[e], hy_w2[e], hy_b2[e], hy_w3[e], hy_freq[e])
                     for e in range((DEPTH + 1) // 2)]

    G = SSD_GROUPS
    zero_m = _pack_mlstm_state(jnp.zeros((Bp, 2, M_HEADS, M_DH, M_DH), F32), jnp.zeros((Bp, 2, M_HEADS, M_DH), F32),
                               jnp.zeros((Bp, 2, M_HEADS), F32))
    zero_s = jnp.zeros((Bp, 2, G, SSD_GW, SSD_STATE), F32)
    ctx_init = [zero_m if l % 2 == 0 else zero_s for l in range(DEPTH)]
    y_prompt, ctx_states = _trunk(x_prompt.reshape(Bp * Lp, D_MODEL), mods, P, Bp, Lp, ctx_init, False, False,
                                  kspecs[Lp], tabs[Lp])

    lat_init = []
    for l in range(DEPTH):
        if l % 2 == 0:
            e = l // 2
            lat_init.append(_pack_mlstm_state(state_mlstm_C[:, e], state_mlstm_n[:, e], state_mlstm_m[:, e]))
        else:
            lat_init.append(state_ssd[:, l // 2].reshape(Bs, 2, G, SSD_GW, SSD_STATE))
    y_sample, _ = _trunk(x_sample.reshape(Bs * Ls, D_MODEL), mods, P, Bs, Ls, lat_init, True, True,
                         kspecs[Ls], tabs[Ls])

    cs = [ctx_states[l][0].reshape(Bp, 2, M_HEADS, 2 * M_DH, M_DH) for l in range(0, DEPTH, 2)]
    ms = [ctx_states[l][1][:, :, 0, 0].reshape(Bp, 2, M_HEADS) for l in range(0, DEPTH, 2)]
    new_C = jnp.stack([jnp.swapaxes(t[..., :M_DH, :], -1, -2) for t in cs], axis=1)
    new_n = jnp.stack([t[..., M_DH, :] for t in cs], axis=1)
    new_m = jnp.stack(ms, axis=1)
    new_ssd = jnp.stack([ctx_states[l].reshape(Bp, 2, SSD_HEADS, SSD_HEADDIM, SSD_STATE)
                         for l in range(1, DEPTH, 2)], axis=1)
    return (y_prompt.reshape(Bp, Lp, D_MODEL), y_sample.reshape(Bs, Ls, D_MODEL), new_C, new_n, new_m, new_ssd)
```

```python
import functools
import math

import jax
import jax.numpy as jnp
from jax import lax
from jax.experimental import pallas as pl
from jax.experimental.pallas import tpu as pltpu

F32 = jnp.float32
BF16 = jnp.bfloat16

D_MODEL = 1024
DEPTH = 4
GRID_W = 64
EPS = 1e-6
HY_W = D_MODEL // 2
HY_ORDER = 2
HY_EMB = 33
HY_BANDS = (HY_EMB - 1) // 2
HY_HID = 64
HY_TARGET = 1e-2
HY_FAST = 0.3
HY_SLOW = 1.5
M_HEADS = 4
M_DH = 128
M_W = M_HEADS * M_DH
EV_CONV = 3 * HY_W + 2 * M_W
EV_PROJ = EV_CONV + 2 * M_W + 4 * M_HEADS
SSD_INNER = 2 * D_MODEL
SSD_HEADDIM = 64
SSD_HEADS = SSD_INNER // SSD_HEADDIM
SSD_GROUPS = 8
SSD_HPG = SSD_HEADS // SSD_GROUPS
SSD_STATE = 128
SSD_GW = SSD_HPG * SSD_HEADDIM
SSD_CONV_DIM = SSD_INNER + 2 * SSD_GROUPS * SSD_STATE
OD_PROJ = SSD_INNER + SSD_CONV_DIM + 2 * SSD_HEADS
FFN_HIDDEN = -(-8 * D_MODEL // (3 * 256)) * 256

V7X_VMEM_BYTES = 64 * 1024 * 1024
VMEM_LIMIT = V7X_VMEM_BYTES * 3 // 4
WIDE_TILE = 512
NARROW_TILE = 256
HALO = 16
COL_CHUNK = 512
HY_NB = 16
SCAN_T = 128
SSD_BLOCK_TOKENS = 2048
MLSTM_BLOCK_TOKENS = 1024


def _cparams(*sem):
    return pltpu.CompilerParams(dimension_semantics=sem, vmem_limit_bytes=VMEM_LIMIT)


def _dot(a, b):
    return jnp.dot(a, b, preferred_element_type=F32)


def _dot_nt(a, b):
    return lax.dot_general(a, b, (((1,), (1,)), ((), ())), preferred_element_type=F32)


def _split2(x):
    h = x.astype(BF16)
    return h, (x - h.astype(F32)).astype(BF16)


def _dot3(th, tl, x):
    xh, xl = _split2(x)
    return _dot(th, xh) + _dot(th, xl) + _dot(tl, xh)


def _dot2(x, t):
    xh, xl = _split2(x)
    return _dot(xh, t) + _dot(xl, t)


def _sigmoid(x):
    return 1.0 / (1.0 + jnp.exp(-x))


def _silu(x):
    return x * _sigmoid(x)


def _log_sigmoid(x):
    return jnp.minimum(x, 0.0) - jnp.log(1.0 + jnp.exp(-jnp.abs(x)))


def _softplus(x):
    return jnp.maximum(x, 0.0) + jnp.log(1.0 + jnp.exp(-jnp.abs(x)))


def _rms(x):
    return x * lax.rsqrt(jnp.mean(x * x, axis=-1, keepdims=True) + EPS)


def _mod_kernel(c_ref, w_ref, b_ref, o_ref):
    a = _silu(c_ref[...]).astype(BF16)
    o_ref[0] = _dot(a, w_ref[0].astype(BF16)) + b_ref[0]


def _modulation(cond, mod_w, mod_b):
    nc = cond.shape[0]
    tn = 1536
    return pl.pallas_call(
        _mod_kernel,
        grid=(DEPTH, 6 * D_MODEL // tn),
        in_specs=[pl.BlockSpec((nc, D_MODEL), lambda l, j: (0, 0)),
                  pl.BlockSpec((1, D_MODEL, tn), lambda l, j: (l, 0, j)),
                  pl.BlockSpec((1, 1, tn), lambda l, j: (l, 0, j))],
        out_specs=pl.BlockSpec((1, nc, tn), lambda l, j: (l, 0, j)),
        out_shape=jax.ShapeDtypeStruct((DEPTH, nc, 6 * D_MODEL), F32),
        compiler_params=_cparams("arbitrary", "arbitrary"),
        name="modulation",
    )(cond, mod_w, mod_b.reshape(DEPTH, 1, 6 * D_MODEL))


def _inproj_kernel(x_ref, xp_ref, xn_ref, mod_ref, g_ref, w_ref, wt_ref, cw_ref, cb_ref, *rest,
                   plan, nt_plan, tiles_per_seq, conv0):
    nseg = len(plan)
    outs = rest[:nseg]
    nt_outs = rest[nseg:nseg + len(nt_plan)]
    hbuf = rest[nseg + len(nt_plan)]
    tm = x_ref.shape[0]
    i = pl.program_id(0)
    j = i % tiles_per_seq
    pmask = (j != 0).astype(F32)
    nmask = (j != tiles_per_seq - 1).astype(F32)
    mod = mod_ref[0]
    sh, sc = mod[0:1], mod[1:2]
    g = g_ref[...]

    def normmod(xv):
        return (_rms(xv) * g * (1.0 + sc) + sh).astype(BF16)

    hbuf[0:tm] = normmod(x_ref[...])
    hbuf[tm:tm + HALO] = normmod(xp_ref[...])
    hbuf[tm + HALO:tm + 2 * HALO] = normmod(xn_ref[...])
    rowid = lax.broadcasted_iota(jnp.int32, (tm, COL_CHUNK), 0)

    for (c0, width, kind, scale, transposed), o_ref in zip(plan, outs):
        for cc in range(c0, c0 + width, COL_CHUNK):
            w = w_ref[0, :, cc:cc + COL_CHUNK]
            if kind == "raw":
                y = _dot(hbuf[0:tm], w)
            else:
                full = _dot(hbuf[...], w)
                acc = full[0:tm]
                ap = full[tm + HALO - 1:tm + HALO] * pmask
                an = full[tm + HALO:tm + HALO + 1] * nmask
                prev = jnp.where(rowid == 0, ap, pltpu.roll(acc, 1, 0))
                nxt = jnp.where(rowid == tm - 1, an, pltpu.roll(acc, tm - 1, 0))
                cw = cw_ref[0, :, cc - conv0:cc - conv0 + COL_CHUNK]
                cb = cb_ref[0, :, cc - conv0:cc - conv0 + COL_CHUNK]
                y = cb + cw[0:1] * prev + cw[1:2] * acc + cw[2:3] * nxt
                if kind == "conv_silu":
                    y = _silu(y)
                if scale != 1.0:
                    y = y * scale
            if transposed:
                o_ref[cc - c0:cc - c0 + COL_CHUNK, :] = y.T.astype(o_ref.dtype)
            else:
                o_ref[:, cc - c0:cc - c0 + COL_CHUNK] = y.astype(o_ref.dtype)

    for (r0, rows), o_ref in zip(nt_plan, nt_outs):
        o_ref[...] = _dot_nt(wt_ref[0, r0:r0 + rows, :], hbuf[0:tm]).astype(o_ref.dtype)


def _inproj(x, mods_l, g_l, w_all, wt_all, cw_all, cb_all, layer_idx, *, plan, nt_plan, conv0, seq_len,
            per_batch_cond, row_tile):
    M = x.shape[0]
    tm = min(row_tile, seq_len)
    tps = seq_len // tm
    nt = M // tm
    hb = tm // HALO
    ntot = w_all.shape[2]
    tw = wt_all.shape[1]
    cdim = cw_all.shape[2]
    e = layer_idx
    cond_map = (lambda i: (1 + i // tps, 0, 0)) if per_batch_cond else (lambda i: (0, 0, 0))
    in_specs = [
        pl.BlockSpec((tm, D_MODEL), lambda i: (i, 0)),
        pl.BlockSpec((HALO, D_MODEL), lambda i: (jnp.maximum(i * hb - 1, 0), 0)),
        pl.BlockSpec((HALO, D_MODEL), lambda i: (jnp.minimum((i + 1) * hb, nt * hb - 1), 0)),
        pl.BlockSpec((1, 6, D_MODEL), cond_map),
        pl.BlockSpec((1, D_MODEL), lambda i: (0, 0)),
        pl.BlockSpec((1, D_MODEL, ntot), lambda i: (e, 0, 0), pipeline_mode=pl.Buffered(1)),
        pl.BlockSpec((1, tw, D_MODEL), lambda i: (e, 0, 0)),
        pl.BlockSpec((1, 3, cdim), lambda i: (e, 0, 0)),
        pl.BlockSpec((1, 1, cdim), lambda i: (e, 0, 0)),
    ]
    out_specs, out_shape = [], []
    for (c0, width, kind, scale, transposed, dt) in plan:
        if transposed:
            out_specs.append(pl.BlockSpec((width, tm), lambda i: (0, i)))
            out_shape.append(jax.ShapeDtypeStruct((width, M), dt))
        else:
            out_specs.append(pl.BlockSpec((tm, width), lambda i: (i, 0)))
            out_shape.append(jax.ShapeDtypeStruct((M, width), dt))
    for (r0, rows, dt) in nt_plan:
        out_specs.append(pl.BlockSpec((rows, tm), lambda i: (0, i)))
        out_shape.append(jax.ShapeDtypeStruct((rows, M), dt))
    kplan = tuple(p[:5] for p in plan)
    knt = tuple(p[:2] for p in nt_plan)
    return pl.pallas_call(
        functools.partial(_inproj_kernel, plan=kplan, nt_plan=knt, tiles_per_seq=tps, conv0=conv0),
        grid=(nt,),
        in_specs=in_specs,
        out_specs=out_specs,
        out_shape=out_shape,
        scratch_shapes=[pltpu.VMEM((tm + 2 * HALO, D_MODEL), BF16)],
        compiler_params=_cparams("arbitrary"),
        name="inproj",
    )(x, x, x, mods_l, g_l, w_all, wt_all, cw_all, cb_all.reshape(cb_all.shape[0], 1, cdim))


def _outproj_even_kernel(yh_ref, ym_ref, x_ref, mod_ref, w_ref, o_ref):
    out = _dot(yh_ref[...], w_ref[0, 0:HY_W]) + _dot(ym_ref[...], w_ref[0, HY_W:])
    o_ref[...] = x_ref[...] + mod_ref[0][2:3] * out


def _outproj_odd_kernel(u_ref, x_ref, mod_ref, ng_ref, w_ref, o_ref):
    un = (_rms(u_ref[...]) * ng_ref[0]).astype(BF16)
    o_ref[...] = x_ref[...] + mod_ref[0][2:3] * _dot(un, w_ref[0])


def _cond_map(seq_len, per_batch_cond):
    tps = seq_len // WIDE_TILE
    return (lambda i: (1 + i // tps, 0, 0)) if per_batch_cond else (lambda i: (0, 0, 0))


def _outproj_even(yh, ym, x, mods_l, w_all, e, *, seq_len, per_batch_cond):
    M = x.shape[0]
    tm = WIDE_TILE
    return pl.pallas_call(
        _outproj_even_kernel,
        grid=(M // tm,),
        in_specs=[pl.BlockSpec((tm, HY_W), lambda i: (i, 0)),
                  pl.BlockSpec((tm, M_W), lambda i: (i, 0)),
                  pl.BlockSpec((tm, D_MODEL), lambda i: (i, 0)),
                  pl.BlockSpec((1, 6, D_MODEL), _cond_map(seq_len, per_batch_cond)),
                  pl.BlockSpec((1, HY_W + M_W, D_MODEL), lambda i: (e, 0, 0), pipeline_mode=pl.Buffered(1))],
        out_specs=pl.BlockSpec((tm, D_MODEL), lambda i: (i, 0)),
        out_shape=jax.ShapeDtypeStruct((M, D_MODEL), F32),
        compiler_params=_cparams("arbitrary"),
        name="outproj_even",
    )(yh, ym, x, mods_l, w_all)


def _outproj_odd(u, x, mods_l, ng_all, w_all, o, *, seq_len, per_batch_cond):
    M = x.shape[0]
    tm = WIDE_TILE
    return pl.pallas_call(
        _outproj_odd_kernel,
        grid=(M // tm,),
        in_specs=[pl.BlockSpec((tm, SSD_INNER), lambda i: (i, 0)),
                  pl.BlockSpec((tm, D_MODEL), lambda i: (i, 0)),
                  pl.BlockSpec((1, 6, D_MODEL), _cond_map(seq_len, per_batch_cond)),
                  pl.BlockSpec((1, 1, SSD_INNER), lambda i: (o, 0, 0)),
                  pl.BlockSpec((1, SSD_INNER, D_MODEL), lambda i: (o, 0, 0), pipeline_mode=pl.Buffered(1))],
        out_specs=pl.BlockSpec((tm, D_MODEL), lambda i: (i, 0)),
        out_shape=jax.ShapeDtypeStruct((M, D_MODEL), F32),
        compiler_params=_cparams("arbitrary"),
        name="outproj_odd",
    )(u, x, mods_l, ng_all, w_all)


FFN_CHUNK = FFN_HIDDEN // 2


def _ffn_kernel(x_ref, mod_ref, g_ref, w1_ref, w3_ref, w2_ref, fg_ref, o_ref, *, final):
    x = x_ref[...]
    mod = mod_ref[0]
    h = (_rms(x) * g_ref[0] * (1.0 + mod[4:5]) + mod[3:4]).astype(BF16)
    ff = jnp.zeros(x.shape, F32)
    for c0 in range(0, FFN_HIDDEN, FFN_CHUNK):
        a = _dot(h, w1_ref[0, :, c0:c0 + FFN_CHUNK])
        b = _dot(h, w3_ref[0, :, c0:c0 + FFN_CHUNK])
        u = (_silu(a) * b).astype(BF16)
        ff = ff + _dot(u, w2_ref[0, c0:c0 + FFN_CHUNK, :])
    xo = x + mod[5:6] * ff
    if final:
        xo = _rms(xo) * fg_ref[...]
    o_ref[...] = xo


def _ffn(x, mods_l, g_all, w1, w3, w2, fg, l, *, seq_len, per_batch_cond, final):
    M = x.shape[0]
    tm = WIDE_TILE
    wspec = lambda shape: pl.BlockSpec(shape, lambda i: (l, 0, 0), pipeline_mode=pl.Buffered(1))
    return pl.pallas_call(
        functools.partial(_ffn_kernel, final=final),
        grid=(M // tm,),
        in_specs=[pl.BlockSpec((tm, D_MODEL), lambda i: (i, 0)),
                  pl.BlockSpec((1, 6, D_MODEL), _cond_map(seq_len, per_batch_cond)),
                  pl.BlockSpec((1, 1, D_MODEL), lambda i: (l, 0, 0)),
                  wspec((1, D_MODEL, FFN_HIDDEN)), wspec((1, D_MODEL, FFN_HIDDEN)),
                  wspec((1, FFN_HIDDEN, D_MODEL)),
                  pl.BlockSpec((1, D_MODEL), lambda i: (0, 0))],
        out_specs=pl.BlockSpec((tm, D_MODEL), lambda i: (i, 0)),
        out_shape=jax.ShapeDtypeStruct((M, D_MODEL), F32),
        compiler_params=_cparams("arbitrary"),
        name="ffn",
    )(x, mods_l, g_all, w1, w3, w2, fg)


def _scan_consts(d):
    T = SCAN_T
    row = lax.broadcasted_iota(jnp.int32, (T, T), 0)
    col = lax.broadcasted_iota(jnp.int32, (T, T), 1)
    mask, last = (row <= col, T - 1) if d == 0 else (row >= col, 0)
    v_mat = jnp.where(mask, 1.0, 0.0).astype(BF16)
    return mask, v_mat, last


def _mlstm_kernel(qt_ref, k_ref, vt_ref, o_ref, gt_ref, gb_ref, ng_ref, c0_ref, m0_ref,
                  y_ref, cst_ref, mst_ref, h_scr, c_scr, m_scr, sb_scr, vw_scr, rows_scr, *, nc, ncs):
    T = SCAN_T
    H = M_HEADS
    nseq = nc // ncs
    ones_blk = jnp.where(lax.broadcasted_iota(jnp.int32, (M_DH, T), 0) == 0, 1.0, 0.0)
    consts = (_scan_consts(0), _scan_consts(1))
    gb = gb_ref[...]

    def gates(ci, d, slot):
        mask, v_mat, last = consts[d]
        c0 = pl.multiple_of(ci * T, T)
        lanes = pl.ds(c0, T)
        graw = gt_ref[:, lanes] + gb
        lf = _log_sigmoid(graw)
        b_all = _dot2(lf, v_mat)
        for h in range(H):
            fi = (2 * d + 1) * H + h
            ii = (2 * d) * H + h
            hl = slice(h * M_DH, (h + 1) * M_DH)
            tr = slice(h * T, (h + 1) * T)
            b_row, i_row = b_all[fi:fi + 1], graw[ii:ii + 1]
            dlog = jnp.where(mask, jnp.broadcast_to(b_row, (T, T)) - jnp.broadcast_to(b_row - i_row, (T, T)).T,
                             -jnp.inf)
            mloc = jnp.max(dlog, axis=0, keepdims=True)
            sb_scr[slot, d, tr, :] = (_dot(k_ref[pl.ds(c0, T), hl], qt_ref[hl, lanes])
                                      * jnp.exp(dlog - mloc)).astype(BF16)
            mloc_last = jnp.broadcast_to(mloc[:, last:last + 1], (1, T))
            b_last = jnp.broadcast_to(b_row[:, last:last + 1], (1, T))
            w_end = jnp.exp(b_last - b_row + i_row - mloc_last)
            vw_scr[slot, d, h, 0:M_DH, :] = (vt_ref[hl, lanes].astype(F32) * w_end).astype(BF16)
            vw_scr[slot, d, h, M_DH:, :] = (ones_blk * w_end).astype(BF16)
            rows_scr[slot, d, 4 * h:4 * h + 4, :] = jnp.concatenate([b_row, mloc, mloc_last, b_last], axis=0)

    def update(ci, d, slot, accumulate):
        c0 = pl.multiple_of(ci * T, T)
        lanes = pl.ds(c0, T)
        for h in range(H):
            sidx = d * H + h
            hl = slice(h * M_DH, (h + 1) * M_DH)
            tr = slice(h * T, (h + 1) * T)
            rows = rows_scr[slot, d, 4 * h:4 * h + 4, :]
            b_row, mloc, mloc_last, b_last = rows[0:1], rows[1:2], rows[2:3], rows[3:4]
            m_row = m_scr[sidx]
            inter = b_row + m_row
            mt = jnp.maximum(inter, mloc)
            kh = k_ref[pl.ds(c0, T), hl]
            qt = qt_ref[hl, lanes]
            vext = jnp.concatenate([vt_ref[hl, lanes], ones_blk.astype(BF16)], axis=0)
            ct = c_scr[sidx]
            tot = (jnp.exp(mloc - mt) * _dot(vext, sb_scr[slot, d, tr, :])
                   + jnp.exp(inter - mt) * _dot(ct.astype(BF16), qt))
            den = tot[M_DH:M_DH + 1]
            ht = tot[0:M_DH] / jnp.maximum(jnp.abs(den), jnp.exp(-mt))
            if accumulate:
                h_scr[hl, lanes] = h_scr[hl, lanes] + ht
            else:
                h_scr[hl, lanes] = ht
            m_new = jnp.maximum(b_last + m_row, mloc_last)
            c_scr[sidx] = (jnp.exp(b_last + m_row - m_new) * ct
                           + jnp.exp(mloc_last - m_new) * _dot(vw_scr[slot, d, h], kh))
            m_scr[sidx] = m_new

    def chunks_at(j):
        if nseq == 1:
            return j, nc - 1 - j
        seq, loc = divmod(j, ncs)
        return j, seq * ncs + ncs - 1 - loc

    def load_state(seq):
        c_scr[...] = c0_ref[seq]
        m_scr[...] = m0_ref[seq]

    def store_state(seq):
        cst_ref[seq] = c_scr[...]
        mst_ref[seq] = m_scr[...]

    def pair(p, slots, accumulate=None):
        s0, s1 = slots
        for j, slot in ((2 * p, s0), (2 * p + 1, s1)):
            if nseq > 1 and j % ncs == 0:
                load_state(j // ncs)
            acc = (j % ncs >= ncs // 2) if accumulate is None else accumulate
            cf, cb = chunks_at(j)
            update(cf, 0, slot, acc)
            update(cb, 1, slot, acc)
            if nseq > 1 and j % ncs == ncs - 1:
                store_state(j // ncs)
        for j, slot in ((2 * p + 2, 2 - s0), (2 * p + 3, 4 - s1)):
            if nseq == 1:
                gates(jnp.minimum(j, nc - 1), 0, slot)
                gates(jnp.maximum(nc - 1 - j, 0), 1, slot)
            elif j < nc:
                cf, cb = chunks_at(j)
                gates(cf, 0, slot)
                gates(cb, 1, slot)

    for j in (0, 1):
        cf, cb = chunks_at(j)
        gates(cf, 0, j)
        gates(cb, 1, j)
    if nseq == 1:
        load_state(0)
    if nseq == 1 and nc % 8 == 0:
        def body(accumulate):
            def run(i, carry):
                pair(2 * i, (0, 1), accumulate)
                pair(2 * i + 1, (2, 3), accumulate)
                return carry
            return run
        lax.fori_loop(0, nc // 8, body(False), 0)
        lax.fori_loop(nc // 8, nc // 4, body(True), 0)
    else:
        for p in range(nc // 2):
            pair(p, ((0, 1), (2, 3))[p % 2])
    if nseq == 1:
        store_state(0)

    def epilogue(i, carry):
        c0 = pl.multiple_of(i * T, T)
        for h in range(H):
            hl = slice(h * M_DH, (h + 1) * M_DH)
            hh = h_scr[hl, pl.ds(c0, T)]
            hn = hh * lax.rsqrt(jnp.mean(hh * hh, axis=0, keepdims=True) + EPS)
            y = hn.T * ng_ref[:, hl] * _sigmoid(o_ref[pl.ds(c0, T), hl])
            y_ref[pl.ds(c0, T), hl] = y.astype(y_ref.dtype)
        return carry

    lax.fori_loop(0, nc, epilogue, 0)


def _mlstm(qt, k, vt, o, gt, gate_b, norm_g, c0, m0, B, L):
    assert L % (2 * SCAN_T) == 0 and M_DH == SCAN_T
    bb = max(1, min(B, MLSTM_BLOCK_TOKENS // L))
    assert B % bb == 0
    ncs = L // SCAN_T
    nc = bb * ncs
    Lb = bb * L
    ng = 4 * M_HEADS
    M = B * L
    one = pl.Buffered(1)
    st_spec = pl.BlockSpec((bb, 2 * M_HEADS, 2 * M_DH, M_DH), lambda b: (b, 0, 0, 0))
    m_spec = pl.BlockSpec((bb, 2 * M_HEADS, 1, M_DH), lambda b: (b, 0, 0, 0))
    return pl.pallas_call(
        functools.partial(_mlstm_kernel, nc=nc, ncs=ncs),
        grid=(B // bb,),
        in_specs=[pl.BlockSpec((M_W, Lb), lambda b: (0, b), pipeline_mode=one),
                  pl.BlockSpec((Lb, M_W), lambda b: (b, 0), pipeline_mode=one),
                  pl.BlockSpec((M_W, Lb), lambda b: (0, b), pipeline_mode=one),
                  pl.BlockSpec((Lb, M_W), lambda b: (b, 0), pipeline_mode=one),
                  pl.BlockSpec((ng, Lb), lambda b: (0, b)),
                  pl.BlockSpec((ng, SCAN_T), lambda b: (0, 0)),
                  pl.BlockSpec((1, M_W), lambda b: (0, 0)),
                  st_spec, m_spec],
        out_specs=[pl.BlockSpec((Lb, M_W), lambda b: (b, 0)), st_spec, m_spec],
        out_shape=[jax.ShapeDtypeStruct((M, M_W), BF16),
                   jax.ShapeDtypeStruct((B, 2 * M_HEADS, 2 * M_DH, M_DH), F32),
                   jax.ShapeDtypeStruct((B, 2 * M_HEADS, 1, M_DH), F32)],
        scratch_shapes=[pltpu.VMEM((M_W, Lb), F32),
                        pltpu.VMEM((2 * M_HEADS, 2 * M_DH, M_DH), F32),
                        pltpu.VMEM((2 * M_HEADS, 1, M_DH), F32),
                        pltpu.VMEM((4, 2, M_HEADS * SCAN_T, SCAN_T), BF16),
                        pltpu.VMEM((4, 2, M_HEADS, 2 * M_DH, SCAN_T), BF16),
                        pltpu.VMEM((4, 2, 4 * M_HEADS, SCAN_T), F32)],
        compiler_params=_cparams("arbitrary"),
        name="mlstm",
    )(qt, k, vt, o, gt, jnp.broadcast_to(gate_b.reshape(ng, 1), (ng, SCAN_T)), norm_g.reshape(1, M_W), c0, m0)


def _ssd_kernel(xt_ref, b_ref, ct_ref, dtt_ref, z_ref, db_ref, al_ref, dsk_ref, s0_ref,
                y_ref, sout_ref, yf_scr, yb_scr, s_scr, cbt_scr, p_scr, xd_scr, xdec_scr, rows_scr, *, nc, ncs):
    T = SCAN_T
    H = SSD_HPG
    nseq = nc // ncs
    consts = (_scan_consts(0), _scan_consts(1))
    db = db_ref[0]
    a_neg = -jnp.exp(al_ref[0])

    def gates(ci, d, slot):
        mask, v_mat, last = consts[d]
        c0 = pl.multiple_of(ci * T, T)
        lanes = pl.ds(c0, T)
        hr = slice(d * H, (d + 1) * H)
        dt = _softplus(dtt_ref[hr, lanes] + db[hr])
        a = dt * a_neg[hr]
        acs = _dot2(a, v_mat)
        a_last = jnp.broadcast_to(acs[:, last:last + 1], (H, T))
        rows_scr[slot, d, 0:H, :] = jnp.exp(acs)
        rows_scr[slot, d, H:2 * H, :] = jnp.exp(a_last)
        cbt_scr[d] = _dot(b_ref[pl.ds(c0, T), :], ct_ref[:, lanes])
        for r in range(H):
            tr = slice(r * T, (r + 1) * T)
            pr = slice(r * SSD_HEADDIM, (r + 1) * SSD_HEADDIM)
            al = jnp.broadcast_to(acs[r:r + 1], (T, T))
            lm = jnp.exp(jnp.where(mask, al - al.T, -jnp.inf))
            p_scr[slot, d, tr, :] = (cbt_scr[d] * lm).astype(BF16)
            xd = xt_ref[pr, lanes] * dt[r:r + 1]
            xd_scr[slot, d, pr, :] = xd.astype(BF16)
            xdec_scr[slot, d, pr, :] = (xd * jnp.exp(a_last[r:r + 1] - acs[r:r + 1])).astype(BF16)

    def update(ci, d, slot):
        c0 = pl.multiple_of(ci * T, T)
        lanes = pl.ds(c0, T)
        rows = rows_scr[slot, d]
        sc = _dot(s_scr[d].astype(BF16), ct_ref[:, lanes])
        upd = _dot(xdec_scr[slot, d], b_ref[pl.ds(c0, T), :])
        y_scr = yf_scr if d == 0 else yb_scr
        for r in range(H):
            pr = slice(r * SSD_HEADDIM, (r + 1) * SSD_HEADDIM)
            tr = slice(r * T, (r + 1) * T)
            y_scr[pr, lanes] = _dot(xd_scr[slot, d, pr, :], p_scr[slot, d, tr, :]) + rows[r:r + 1] * sc[pr]
            s_scr[d, pr, :] = rows[H + r:H + r + 1] * s_scr[d, pr, :] + upd[pr]

    def chunks_at(j):
        if nseq == 1:
            return j, nc - 1 - j
        seq, loc = divmod(j, ncs)
        return j, seq * ncs + ncs - 1 - loc

    def pair(p, slots):
        s0, s1 = slots
        for j, slot in ((2 * p, s0), (2 * p + 1, s1)):
            if nseq > 1 and j % ncs == 0:
                s_scr[...] = s0_ref[j // ncs, :, 0]
            cf, cb = chunks_at(j)
            update(cf, 0, slot)
            update(cb, 1, slot)
            if nseq > 1 and j % ncs == ncs - 1:
                sout_ref[j // ncs, :, 0] = s_scr[...]
        for j, slot in ((2 * p + 2, 2 - s0), (2 * p + 3, 4 - s1)):
            if nseq == 1:
                gates(jnp.minimum(j, nc - 1), 0, slot)
                gates(jnp.maximum(nc - 1 - j, 0), 1, slot)
            elif j < nc:
                cf, cb = chunks_at(j)
                gates(cf, 0, slot)
                gates(cb, 1, slot)

    for j in (0, 1):
        cf, cb = chunks_at(j)
        gates(cf, 0, j)
        gates(cb, 1, j)
    if nseq == 1:
        s_scr[...] = s0_ref[0, :, 0]
    if nseq == 1 and nc % 4 == 0:
        def body(i, carry):
            pair(2 * i, (0, 1))
            pair(2 * i + 1, (2, 3))
            return carry
        lax.fori_loop(0, nc // 4, body, 0)
    else:
        for p in range(nc // 2):
            pair(p, ((0, 1), (2, 3))[p % 2])
    if nseq == 1:
        sout_ref[0, :, 0] = s_scr[...]

    dsk = dsk_ref[0]

    def epilogue(i, carry):
        c0 = pl.multiple_of(i * T, T)
        lanes = pl.ds(c0, T)
        yt = yf_scr[:, lanes] + yb_scr[:, lanes] + dsk * xt_ref[:, lanes]
        y_ref[pl.ds(c0, T), :] = yt.T * _silu(z_ref[pl.ds(c0, T), :])
        return carry

    lax.fori_loop(0, nc, epilogue, 0)


def _ssd(xt, bm, ct, dtt, z, dt_bias, a_log, dskip, s0, B, L):
    assert L % (2 * SCAN_T) == 0 and SSD_STATE == SCAN_T
    bb = max(1, min(B, SSD_BLOCK_TOKENS // L))
    assert B % bb == 0
    ncs = L // SCAN_T
    nc = bb * ncs
    Lb = bb * L
    G = SSD_GROUPS
    nd = 2 * SSD_HPG
    M = B * L
    rep = lambda p, rows: jnp.broadcast_to(p.reshape(G, rows, 1), (G, rows, SCAN_T))
    small = lambda rows: pl.BlockSpec((1, rows, SCAN_T), lambda b, g: (g, 0, 0))
    st_spec = pl.BlockSpec((bb, 2, 1, SSD_GW, SSD_STATE), lambda b, g: (b, 0, g, 0, 0))
    return pl.pallas_call(
        functools.partial(_ssd_kernel, nc=nc, ncs=ncs),
        grid=(B // bb, G),
        in_specs=[pl.BlockSpec((SSD_GW, Lb), lambda b, g: (g, b)),
                  pl.BlockSpec((Lb, SSD_STATE), lambda b, g: (b, g)),
                  pl.BlockSpec((SSD_STATE, Lb), lambda b, g: (g, b)),
                  pl.BlockSpec((nd, Lb), lambda b, g: (g, b)),
                  pl.BlockSpec((Lb, SSD_GW), lambda b, g: (b, g)),
                  small(nd), small(nd), small(SSD_GW), st_spec],
        out_specs=[pl.BlockSpec((Lb, SSD_GW), lambda b, g: (b, g)), st_spec],
        out_shape=[jax.ShapeDtypeStruct((M, SSD_INNER), F32),
                   jax.ShapeDtypeStruct((B, 2, G, SSD_GW, SSD_STATE), F32)],
        scratch_shapes=[pltpu.VMEM((SSD_GW, Lb), F32), pltpu.VMEM((SSD_GW, Lb), F32),
                        pltpu.VMEM((2, SSD_GW, SSD_STATE), F32),
                        pltpu.VMEM((2, SCAN_T, SCAN_T), F32),
                        pltpu.VMEM((4, 2, SSD_HPG * SCAN_T, SCAN_T), BF16),
                        pltpu.VMEM((4, 2, SSD_GW, SCAN_T), BF16),
                        pltpu.VMEM((4, 2, SSD_GW, SCAN_T), BF16),
                        pltpu.VMEM((4, 2, 2 * SSD_HPG, SCAN_T), F32)],
        compiler_params=_cparams("arbitrary", "arbitrary"),
        name="ssd",
    )(xt, bm, ct, dtt, z, rep(dt_bias, nd), rep(a_log, nd), rep(dskip, SSD_GW), s0)


def _hy_factors(L):
    return (128, 64) if L == 4096 else (32, (2 * L) // 32)


def _hilo(t):
    return _split2(t.astype(F32))


def _cblock(cr, ci):
    return jnp.concatenate([jnp.concatenate([cr, -ci], axis=-1), jnp.concatenate([ci, cr], axis=-1)], axis=-2)


def _dft_tables(L):
    n1c, n2c = _hy_factors(L)
    N = n1c * n2c
    h1 = n1c // 2
    i32 = jnp.int32
    k1 = jnp.arange(n1c, dtype=i32)
    n2 = jnp.arange(n2c, dtype=i32)

    def stage1(n1_count):
        n1 = jnp.arange(n1_count, dtype=i32)
        idx = (k1[None, :, None] * (n2[:, None, None] + n2c * n1[None, None, :])) % N
        ang = idx.astype(F32) * (2.0 * math.pi / N)
        return jnp.cos(ang), jnp.sin(ang)

    c, s = stage1(h1)
    g_fwd = _cblock(c, -s)
    g_inv = _cblock(jnp.swapaxes(c, 1, 2), jnp.swapaxes(s, 1, 2)) / n1c
    cf, sf = stage1(n1c)
    g_real = jnp.concatenate([cf, -sf], axis=1)
    idx2 = (n2[:, None] * n2[None, :]) % n2c
    ang2 = idx2.astype(F32) * (2.0 * math.pi / n2c)
    c2, s2 = jnp.cos(ang2), jnp.sin(ang2)
    f2 = _cblock(c2, -s2)
    f2_inv = _cblock(c2, s2) / n2c
    return dict(g_fwd=g_fwd.astype(BF16), g_inv=g_inv.astype(BF16), f2=f2.astype(BF16),
                f2_inv=f2_inv.astype(BF16), g_real=_hilo(g_real), f2_exact=_hilo(f2))


def _swap01(x):
    return jnp.swapaxes(x, 0, 1)


def _hy_s1_kernel(x_ref, *rest, nb, exact):
    if exact:
        gh_ref, gl_ref, o_ref = rest
    else:
        g_ref, o_ref = rest
    pb, rows, _, C = x_ref.shape
    xs = _swap01(x_ref[...].reshape(pb * rows, nb, C))
    if exact:
        res = [_dot3(gh_ref[j], gl_ref[j], xs[j]) for j in range(nb)]
    else:
        res = [_dot(g_ref[j], xs[j].astype(BF16)) for j in range(nb)]
    out = _swap01(jnp.stack(res, axis=0))
    o_ref[...] = out.reshape(o_ref.shape).astype(o_ref.dtype)


def _hy_stage1(x, tab, out_dtype):
    exact = isinstance(tab, tuple)
    tabs = tab if exact else (tab,)
    n2c, rows_out, R = tabs[0].shape
    Bx, rows, _, C = x.shape
    pb = R // rows
    nb = HY_NB
    cb = min(C, HY_W)
    tspec = pl.BlockSpec((nb, rows_out, R), lambda p, t, c: (t, 0, 0))
    return pl.pallas_call(
        functools.partial(_hy_s1_kernel, nb=nb, exact=exact),
        grid=(Bx // pb, n2c // nb, C // cb),
        in_specs=[pl.BlockSpec((pb, rows, nb, cb), lambda p, t, c: (p, 0, t, c))] + [tspec] * len(tabs),
        out_specs=pl.BlockSpec((None, 2, rows_out // 2, nb, cb), lambda p, t, c: (p, 0, 0, t, c)),
        out_shape=jax.ShapeDtypeStruct((Bx // pb, 2, rows_out // 2, n2c, C), out_dtype),
        compiler_params=_cparams("arbitrary", "arbitrary", "arbitrary"),
        name="hyena_stage1",
    )(x, *tabs)


def _hy_spec_kernel(a_ref, inv_ref, fh_ref, fl_ref, o_ref, *, kb):
    n2c = a_ref.shape[2]
    for kk in range(kb):
        rhs = jnp.concatenate([a_ref[0, kk], a_ref[1, kk]], axis=0)
        X = _dot3(fh_ref[...], fl_ref[...], rhs) * inv_ref[...]
        o_ref[0, kk] = X[0:n2c]
        o_ref[1, kk] = X[n2c:]


def _hy_filter_spectrum(a, inv_norm, f2, kb):
    _, n1c, n2c, W = a.shape
    fh, fl = f2
    blk = pl.BlockSpec((2, kb, n2c, W), lambda t: (0, t, 0, 0))
    fspec = pl.BlockSpec((2 * n2c, 2 * n2c), lambda t: (0, 0))
    return pl.pallas_call(
        functools.partial(_hy_spec_kernel, kb=kb),
        grid=(n1c // kb,),
        in_specs=[blk, pl.BlockSpec((1, W), lambda t: (0, 0)), fspec, fspec],
        out_specs=blk,
        out_shape=jax.ShapeDtypeStruct(a.shape, F32),
        compiler_params=_cparams("arbitrary"),
        name="hyena_filter_spectrum",
    )(a, inv_norm, fh, fl)


def _hy_mid_kernel(a_ref, k_ref, f_ref, i_ref, o_ref, *, kb):
    n2c = a_ref.shape[2]
    res = []
    for kk in range(kb):
        rhs = jnp.concatenate([a_ref[0, kk], a_ref[1, kk]], axis=0)
        X = _dot(f_ref[...], rhs)
        xr, xi = X[0:n2c], X[n2c:]
        kr, ki = k_ref[0, kk], k_ref[1, kk]
        Y = jnp.concatenate([xr * kr - xi * ki, xr * ki + xi * kr], axis=0)
        res.append(_dot(i_ref[...], Y.astype(BF16)))
    o_ref[...] = _swap01(jnp.stack(res, axis=0)).reshape(o_ref.shape).astype(o_ref.dtype)


def _hy_mid(a, kspec, order, f2, f2_inv):
    P, _, n1c, n2c, C = a.shape
    kb = HY_NB
    fspec = pl.BlockSpec((2 * n2c, 2 * n2c), lambda t, p: (0, 0))
    return pl.pallas_call(
        functools.partial(_hy_mid_kernel, kb=kb),
        grid=(n1c // kb, P),
        in_specs=[pl.BlockSpec((None, 2, kb, n2c, C), lambda t, p: (p, 0, t, 0, 0)),
                  pl.BlockSpec((2, kb, n2c, C), lambda t, p: (0, t, 0, order)),
                  fspec, fspec],
        out_specs=pl.BlockSpec((None, 2, n2c, kb, C), lambda t, p: (p, 0, 0, t, 0)),
        out_shape=jax.ShapeDtypeStruct((P, 2, n2c, n1c, C), BF16),
        compiler_params=_cparams("arbitrary", "arbitrary"),
        name="hyena_mid",
    )(a, kspec, f2, f2_inv)


def _hy_last_kernel(b_ref, z_ref, gate_ref, bias_ref, gi_ref, *rest, fuse, nb):
    if fuse:
        gf_ref, zo_ref, ao_ref = rest
    else:
        (zo_ref,) = rest
    _, h1, _, C = gate_ref.shape
    gs = _swap01(gate_ref[...].reshape(2 * h1, nb, C))
    zs = _swap01(z_ref[...].reshape(2 * h1, nb, C)) if fuse else z_ref[...]
    bias = bias_ref[...]
    zn, an = [], []
    for j in range(nb):
        bv = jnp.concatenate([b_ref[0, j], b_ref[1, j]], axis=0)
        y = _dot(gi_ref[j], bv)
        znew = gs[j] * (y + zs[j] * bias)
        zn.append(znew)
        if fuse:
            an.append(_dot(gf_ref[j], znew.astype(BF16)))
    if fuse:
        zo_ref[...] = jnp.stack(zn, axis=0)
        ao_ref[...] = _swap01(jnp.stack(an, axis=0)).reshape(ao_ref.shape).astype(ao_ref.dtype)
    else:
        zo_ref[...] = _swap01(jnp.stack(zn, axis=0)).reshape(zo_ref.shape).astype(zo_ref.dtype)


def _hy_last(bf, z, gate, bias, g_inv, g_fwd, out_dtype):
    P, _, n2c, n1c, C = bf.shape
    B, h1, _, _ = gate.shape
    nb = HY_NB
    cb = C // 2
    fuse = g_fwd is not None
    nat = pl.BlockSpec((2, h1, nb, cb), lambda p, t, c: (p, 0, t, c))
    swp = pl.BlockSpec((None, nb, 2 * h1, cb), lambda p, t, c: (p, t, 0, c))
    in_specs = [pl.BlockSpec((None, 2, nb, n1c, cb), lambda p, t, c: (p, 0, t, 0, c)),
                nat if fuse else swp, nat, pl.BlockSpec((1, cb), lambda p, t, c: (0, c)),
                pl.BlockSpec((nb, 2 * h1, 2 * n1c), lambda p, t, c: (t, 0, 0))]
    args = [bf, z, gate, bias, g_inv]
    if fuse:
        in_specs.append(pl.BlockSpec((nb, 2 * n1c, 2 * h1), lambda p, t, c: (t, 0, 0)))
        args.append(g_fwd)
        out_specs = [swp, pl.BlockSpec((None, 2, n1c, nb, cb), lambda p, t, c: (p, 0, 0, t, c))]
        out_shape = [jax.ShapeDtypeStruct((P, n2c, 2 * h1, C), F32),
                     jax.ShapeDtypeStruct((P, 2, n1c, n2c, C), BF16)]
    else:
        out_specs = [nat]
        out_shape = [jax.ShapeDtypeStruct(gate.shape, out_dtype)]
    return pl.pallas_call(
        functools.partial(_hy_last_kernel, fuse=fuse, nb=nb),
        grid=(P, n2c // nb, C // cb),
        in_specs=in_specs, out_specs=out_specs, out_shape=out_shape,
        compiler_params=_cparams("arbitrary", "arbitrary", "arbitrary"),
        name="hyena_last",
    )(*args)


def _hy_filter_kernel(f_ref, w1_ref, b1_ref, w2_ref, b2_ref, w3_ref, fr_ref, dl_ref, hk_ref, s_ref, *, L):
    i = pl.program_id(0)
    tl = f_ref.shape[0]
    f = f_ref[...]
    fr = fr_ref[...]
    h = jnp.sin(fr * (_dot(f.astype(BF16), w1_ref[...].astype(BF16)) + b1_ref[...]))
    h = jnp.sin(fr * (_dot(h.astype(BF16), w2_ref[...].astype(BF16)) + b2_ref[...]))
    hk = _dot(h.astype(BF16), w3_ref[...].astype(BF16)) * jnp.exp(-f[:, 0:1] * dl_ref[...])
    n = i * tl + lax.broadcasted_iota(jnp.int32, hk.shape, 0)
    hk = jnp.where(n == L, 0.0, hk)
    hk_ref[...] = hk
    tot = jnp.sum(jnp.abs(hk), axis=0, keepdims=True)

    @pl.when(i == 0)
    def _():
        s_ref[...] = tot

    @pl.when(i != 0)
    def _():
        s_ref[...] = s_ref[...] + tot


def _hy_filter(L, w1, b1, w2, b2, w3, freq):
    f32 = F32
    t = jnp.linspace(0.0, 1.0, L, dtype=f32)[:, None]
    ang = (2.0 * math.pi / L) * jnp.arange(L, dtype=f32)[:, None] * \
        jnp.linspace(1e-4, HY_BANDS - 1, HY_BANDS, dtype=f32)[None]
    feats = jnp.concatenate([t, jnp.cos(ang), -jnp.sin(ang)], axis=-1)
    feats = jnp.concatenate([feats, feats[0:1], feats[:0:-1]], axis=0)
    fpad = 128
    feats = jnp.pad(feats, ((0, 0), (0, fpad - HY_EMB)))
    w1p = jnp.pad(w1, ((0, fpad - HY_EMB), (0, 0)))
    deltas = jnp.abs(jnp.linspace(math.log(HY_TARGET) / HY_SLOW, math.log(HY_TARGET) / HY_FAST, HY_W, dtype=f32))
    OC = HY_ORDER * HY_W
    dl = jnp.tile(deltas, HY_ORDER).reshape(1, OC)
    tl = min(L, 512)
    nblk = L // tl
    full = lambda shp: pl.BlockSpec(shp, lambda i: (0, 0))
    return pl.pallas_call(
        functools.partial(_hy_filter_kernel, L=L),
        grid=(2 * nblk,),
        in_specs=[pl.BlockSpec((tl, fpad), lambda i: (i, 0)),
                  full((fpad, HY_HID)), full((1, HY_HID)), full((HY_HID, HY_HID)), full((1, HY_HID)),
                  pl.BlockSpec((HY_HID, OC), lambda i: (0, i // nblk)), full((1, HY_HID)), full((1, OC))],
        out_specs=[pl.BlockSpec((tl, OC), lambda i: (i, 0)), full((1, OC))],
        out_shape=[jax.ShapeDtypeStruct((2 * L, OC), F32), jax.ShapeDtypeStruct((1, OC), F32)],
        compiler_params=_cparams("arbitrary"),
        name="hyena_filter",
    )(feats, w1p, b1.reshape(1, HY_HID), w2, b2.reshape(1, HY_HID), w3, freq.reshape(1, HY_HID), dl)


def _hy_spectrum(L, tabs, w1, b1, w2, b2, w3, freq):
    n1c, n2c = _hy_factors(L)
    OC = HY_ORDER * HY_W
    ktime, s = _hy_filter(L, w1, b1, w2, b2, w3, freq)
    a = _hy_stage1(ktime.reshape(1, n1c, n2c, OC), tabs["g_real"], F32)[0]
    return _hy_filter_spectrum(a, 1.0 / s, tabs["f2_exact"], 8)


def _hyena(v, x1, x2, kspec, bias, tabs, B, L):
    n1c, n2c = _hy_factors(L)
    C = HY_W
    view = lambda t: t.reshape(B, n1c // 2, n2c, C)
    z0, g1, g2 = view(v), view(x1), view(x2)
    a = _hy_stage1(z0, tabs["g_fwd"], BF16)
    bf = _hy_mid(a, kspec, 0, tabs["f2"], tabs["f2_inv"])
    z1, a = _hy_last(bf, z0, g1, bias[0:1], tabs["g_inv"], tabs["g_fwd"], F32)
    bf = _hy_mid(a, kspec, 1, tabs["f2"], tabs["f2_inv"])
    (z2,) = _hy_last(bf, z1, g2, bias[1:2], tabs["g_inv"], None, BF16)
    return z2.reshape(B * L, C)


def _direct_tables(L):
    N = 2 * L
    k = jnp.arange(N, dtype=jnp.int32)
    ang = ((k[:, None] * k[None, :]) % N).astype(F32) * (2.0 * math.pi / N)
    c, s = jnp.cos(ang), jnp.sin(ang)
    fwd = _cblock(c[:, :L], -s[:, :L])
    inv = _cblock(c[:L, :], s[:L, :]) / N
    real = jnp.concatenate([c, -s], axis=0)
    return dict(fwd=fwd.astype(BF16), inv=inv.astype(BF16), real=_hilo(real))


def _hy_direct_spec_kernel(kt_ref, inv_ref, fh_ref, fl_ref, o_ref):
    N = kt_ref.shape[0]
    X = _dot3(fh_ref[...], fl_ref[...], kt_ref[...]) * inv_ref[...]
    o_ref[0] = X[0:N]
    o_ref[1] = X[N:]


def _hy_direct_spectrum(L, tabs, w1, b1, w2, b2, w3, freq):
    N = 2 * L
    OC = HY_ORDER * HY_W
    cb = HY_W // 2
    ktime, s = _hy_filter(L, w1, b1, w2, b2, w3, freq)
    fh, fl = tabs["real"]
    fspec = pl.BlockSpec((2 * N, N), lambda c: (0, 0))
    return pl.pallas_call(
        _hy_direct_spec_kernel,
        grid=(OC // cb,),
        in_specs=[pl.BlockSpec((N, cb), lambda c: (0, c)), pl.BlockSpec((1, cb), lambda c: (0, c)), fspec, fspec],
        out_specs=pl.BlockSpec((2, N, cb), lambda c: (0, 0, c)),
        out_shape=jax.ShapeDtypeStruct((2, N, OC), F32),
        compiler_params=_cparams("arbitrary"),
        name="hyena_direct_spectrum",
    )(ktime, 1.0 / s, fh, fl)


def _hy_direct_kernel(v_ref, g1_ref, g2_ref, k0_ref, k1_ref, bias_ref, f_ref, fi_ref, o_ref):
    L = v_ref.shape[1]
    N = 2 * L
    z = jnp.concatenate([v_ref[0], v_ref[1]], axis=0)
    for o, (k_ref, g_ref) in enumerate(((k0_ref, g1_ref), (k1_ref, g2_ref))):
        X = _dot(f_ref[...], z.astype(BF16))
        xr, xi = X[0:N], X[N:]
        kr, ki = k_ref[0], k_ref[1]
        Y = jnp.concatenate([xr * kr - xi * ki, xr * ki + xi * kr], axis=0)
        y = _dot(fi_ref[...], Y.astype(BF16))
        gate = jnp.concatenate([g_ref[0], g_ref[1]], axis=0)
        z = gate * (y + z * bias_ref[o:o + 1])
    o_ref[0] = z[0:L].astype(o_ref.dtype)
    o_ref[1] = z[L:].astype(o_ref.dtype)


def _hyena_direct(v, x1, x2, kspec, bias, tabs, B, L):
    C = HY_W
    N = 2 * L
    cb = C // 2
    ncb = C // cb
    seq = pl.BlockSpec((2, L, cb), lambda p, c: (p, 0, c))
    view = lambda t: t.reshape(B, L, C)
    out = pl.pallas_call(
        _hy_direct_kernel,
        grid=(B // 2, ncb),
        in_specs=[seq, seq, seq,
                  pl.BlockSpec((2, N, cb), lambda p, c: (0, 0, c)),
                  pl.BlockSpec((2, N, cb), lambda p, c: (0, 0, ncb + c)),
                  pl.BlockSpec((HY_ORDER, cb), lambda p, c: (0, c)),
                  pl.BlockSpec((2 * N, 2 * L), lambda p, c: (0, 0)),
                  pl.BlockSpec((2 * L, 2 * N), lambda p, c: (0, 0))],
        out_specs=seq,
        out_shape=jax.ShapeDtypeStruct((B, L, C), BF16),
        compiler_params=_cparams("arbitrary", "arbitrary"),
        name="hyena_direct",
    )(view(v), view(x1), view(x2), kspec, kspec, bias, tabs["fwd"], tabs["inv"])
    return out.reshape(B * L, C)


HY_DIRECT_MAX_L = 512


_EVEN_PLAN = (
    (0, HY_W, "conv", 1.0, False, F32),
    (HY_W, HY_W, "conv", 1.0, False, F32),
    (2 * HY_W, HY_W, "conv", 1.0, False, F32),
    (3 * HY_W, M_W, "conv_silu", 1.0, True, BF16),
    (3 * HY_W + M_W, M_W, "conv_silu", M_DH ** -0.5, False, BF16),
    (EV_CONV + M_W, M_W, "raw", 1.0, False, F32),
)
_EVEN_NT_PLAN = ((0, M_W, BF16), (M_W, 4 * M_HEADS, F32))
_ODD_NT_PLAN = ((0, 2 * SSD_HEADS, F32),)
_GN = SSD_GROUPS * SSD_STATE
_ODD_PLAN = (
    (0, SSD_INNER, "raw", 1.0, False, F32),
    (SSD_INNER, SSD_INNER, "conv_silu", 1.0, True, F32),
    (2 * SSD_INNER, _GN, "conv_silu", 1.0, False, BF16),
    (2 * SSD_INNER + _GN, _GN, "conv_silu", 1.0, True, BF16),
)


def _even_layer(x, mods_l, P, e, l, B, L, st, per_batch_cond, kspec, tabs):
    hv, hx1, hx2, qt, k, o, vt, gt = _inproj(
        x, mods_l, P["norm1_g"][l:l + 1], P["ev_in_w"], P["ev_in_wt"], P["ev_conv_w"], P["ev_conv_b"], e,
        plan=_EVEN_PLAN, nt_plan=_EVEN_NT_PLAN, conv0=0, seq_len=L, per_batch_cond=per_batch_cond, row_tile=WIDE_TILE)
    hyena = _hyena_direct if L <= HY_DIRECT_MAX_L else _hyena
    y_hy = hyena(hv, hx1, hx2, kspec, P["hy_bias"][e], tabs, B, L)
    c0, m0 = st
    y_m, c_new, m_new = _mlstm(qt, k, vt, o, gt, P["m_gate_b"][e], P["m_norm_g"][e], c0, m0, B, L)
    x = _outproj_even(y_hy, y_m, x, mods_l, P["ev_out_w"], e, seq_len=L, per_batch_cond=per_batch_cond)
    return x, (c_new, m_new)


def _by_group(p):
    return p.reshape(2, SSD_GROUPS, SSD_HPG).transpose(1, 0, 2).reshape(SSD_GROUPS, 2 * SSD_HPG)


def _odd_layer(x, mods_l, P, o, l, B, L, s0, per_batch_cond):
    z, xt, bm, ct, dtt = _inproj(
        x, mods_l, P["norm1_g"][l:l + 1], P["od_in_w"], P["od_in_wt"], P["od_conv_w"], P["od_conv_b"], o,
        plan=_ODD_PLAN, nt_plan=_ODD_NT_PLAN, conv0=SSD_INNER, seq_len=L, per_batch_cond=per_batch_cond,
        row_tile=NARROW_TILE)
    dskip = jnp.repeat(P["ssd_D"][o], SSD_HEADDIM).reshape(SSD_GROUPS, SSD_GW)
    u, s_new = _ssd(xt, bm, ct, dtt, z, _by_group(P["ssd_dt_bias"][o]), _by_group(P["ssd_A_log"][o]), dskip, s0, B, L)
    x = _outproj_odd(u, x, mods_l, P["ssd_norm_g"], P["od_out_w"], o, seq_len=L, per_batch_cond=per_batch_cond)
    return x, s_new


def _to_colmajor(x, B, L):
    rows = L // GRID_W
    return x.reshape(B, rows, GRID_W, D_MODEL).swapaxes(1, 2).reshape(B * L, D_MODEL)


def _from_colmajor(x, B, L):
    rows = L // GRID_W
    return x.reshape(B, GRID_W, rows, D_MODEL).swapaxes(1, 2).reshape(B * L, D_MODEL)


def _trunk(x, mods, P, B, L, init_states, per_batch_cond, grid, kspecs, tabs):
    states = []
    for l in range(DEPTH):
        if grid and l == 2:
            x = _to_colmajor(x, B, L)
        mods_l = mods[l]
        if l % 2 == 0:
            x, st = _even_layer(x, mods_l, P, l // 2, l, B, L, init_states[l], per_batch_cond, kspecs[l // 2], tabs)
        else:
            x, st = _odd_layer(x, mods_l, P, l // 2, l, B, L, init_states[l], per_batch_cond)
        states.append(st)
        x = _ffn(x, mods_l, P["norm2_g"].reshape(DEPTH, 1, D_MODEL), P["ffn_w1"], P["ffn_w3"], P["ffn_w2"],
                 P["final_g"].reshape(1, D_MODEL), l, seq_len=L, per_batch_cond=per_batch_cond,
                 final=(l == DEPTH - 1))
    if grid:
        x = _from_colmajor(x, B, L)
    return x, states


def _pack_mlstm_state(C, n, m):
    B = C.shape[0]
    ext = jnp.concatenate([jnp.swapaxes(C, -1, -2), n[..., None, :],
                           jnp.zeros(C.shape[:-2] + (M_DH - 1, M_DH), F32)], axis=-2)
    m_rep = jnp.broadcast_to(m.reshape(B, 2 * M_HEADS, 1, 1), (B, 2 * M_HEADS, 1, M_DH))
    return ext.reshape(B, 2 * M_HEADS, 2 * M_DH, M_DH), m_rep


def kernel(x_prompt, x_sample, state_mlstm_C, state_mlstm_n, state_mlstm_m, state_ssd, c, c_ctx,
           mod_w, mod_b, norm1_g, norm2_g, ffn_w1, ffn_w3, ffn_w2, final_g,
           ev_in_w, ev_conv_w, ev_conv_b, hy_w1, hy_b1, hy_w2, hy_b2, hy_w3, hy_freq, hy_bias,
           m_gate_b, m_norm_g, ev_out_w,
           od_in_w, od_conv_w, od_conv_b, ssd_dt_bias, ssd_A_log, ssd_D, ssd_norm_g, od_out_w):
    Bp, Lp, _ = x_prompt.shape
    Bs, Ls, _ = x_sample.shape
    bf = lambda w: w.astype(BF16)
    n_odd = od_in_w.shape[0]
    od_dt_w = od_in_w[:, :, OD_PROJ - 2 * SSD_HEADS:].reshape(n_odd, D_MODEL, 2, SSD_GROUPS, SSD_HPG)
    od_dt_w = od_dt_w.transpose(0, 3, 2, 4, 1).reshape(n_odd, 2 * SSD_HEADS, D_MODEL)
    P = dict(norm1_g=norm1_g, norm2_g=norm2_g, ffn_w1=bf(ffn_w1), ffn_w3=bf(ffn_w3), ffn_w2=bf(ffn_w2),
             final_g=final_g, ev_in_w=bf(ev_in_w),
             ev_in_wt=bf(jnp.concatenate([ev_in_w[:, :, EV_CONV:EV_CONV + M_W],
                                          ev_in_w[:, :, EV_PROJ - 4 * M_HEADS:]], axis=2)).swapaxes(1, 2),
             ev_conv_w=ev_conv_w, ev_conv_b=ev_conv_b, hy_bias=hy_bias, m_gate_b=m_gate_b, m_norm_g=m_norm_g,
             ev_out_w=bf(ev_out_w), od_in_w=bf(od_in_w), od_in_wt=bf(od_dt_w),
             od_conv_w=od_conv_w, od_conv_b=od_conv_b, ssd_dt_bias=ssd_dt_bias, ssd_A_log=ssd_A_log,
             ssd_D=ssd_D, ssd_norm_g=ssd_norm_g.reshape(-1, 1, SSD_INNER), od_out_w=bf(od_out_w))

    ncond = 16
    cond = jnp.concatenate([c_ctx[None, :], c, jnp.zeros((ncond - 1 - Bs, D_MODEL), F32)], axis=0)
    mods = _modulation(cond, mod_w, mod_b).reshape(DEPTH, ncond, 6, D_MODEL)

    tabs, kspecs = {}, {}
    for L in (Lp, Ls):
        direct = L <= HY_DIRECT_MAX_L
        tabs[L] = _direct_tables(L) if direct else _dft_tables(L)
        spectrum = _hy_direct_spectrum if direct else _hy_spectrum
        kspecs[L] = [spectrum(L, tabs[L], hy_w1[e], hy_b1[e], hy_w2[e], hy_b2[e], hy_w3[e], hy_freq[e])
                     for e in range((DEPTH + 1) // 2)]

    G = SSD_GROUPS
    zero_m = _pack_mlstm_state(jnp.zeros((Bp, 2, M_HEADS, M_DH, M_DH), F32), jnp.zeros((Bp, 2, M_HEADS, M_DH), F32),
                               jnp.zeros((Bp, 2, M_HEADS), F32))
    zero_s = jnp.zeros((Bp, 2, G, SSD_GW, SSD_STATE), F32)
    ctx_init = [zero_m if l % 2 == 0 else zero_s for l in range(DEPTH)]
    y_prompt, ctx_states = _trunk(x_prompt.reshape(Bp * Lp, D_MODEL), mods, P, Bp, Lp, ctx_init, False, False,
                                  kspecs[Lp], tabs[Lp])

    lat_init = []
    for l in range(DEPTH):
        if l % 2 == 0:
            e = l // 2
            lat_init.append(_pack_mlstm_state(state_mlstm_C[:, e], state_mlstm_n[:, e], state_mlstm_m[:, e]))
        else:
            lat_init.append(state_ssd[:, l // 2].reshape(Bs, 2, G, SSD_GW, SSD_STATE))
    y_sample, _ = _trunk(x_sample.reshape(Bs * Ls, D_MODEL), mods, P, Bs, Ls, lat_init, True, True,
                         kspecs[Ls], tabs[Ls])

    cs = [ctx_states[l][0].reshape(Bp, 2, M_HEADS, 2 * M_DH, M_DH) for l in range(0, DEPTH, 2)]
    ms = [ctx_states[l][1][:, :, 0, 0].reshape(Bp, 2, M_HEADS) for l in range(0, DEPTH, 2)]
    new_C = jnp.stack([jnp.swapaxes(t[..., :M_DH, :], -1, -2) for t in cs], axis=1)
    new_n = jnp.stack([t[..., M_DH, :] for t in cs], axis=1)
    new_m = jnp.stack(ms, axis=1)
    new_ssd = jnp.stack([ctx_states[l].reshape(Bp, 2, SSD_HEADS, SSD_HEADDIM, SSD_STATE)
                         for l in range(1, DEPTH, 2)], axis=1)
    return (y_prompt.reshape(Bp, Lp, D_MODEL), y_sample.reshape(Bs, Ls, D_MODEL), new_C, new_n, new_m, new_ssd)
```

```python
import functools
import math

import jax
import jax.numpy as jnp
from jax import lax
from jax.experimental import pallas as pl
from jax.experimental.pallas import tpu as pltpu

F32 = jnp.float32
BF16 = jnp.bfloat16

D_MODEL = 1024
DEPTH = 4
GRID_W = 64
EPS = 1e-6
HY_W = D_MODEL // 2
HY_ORDER = 2
HY_EMB = 33
HY_BANDS = (HY_EMB - 1) // 2
HY_HID = 64
HY_TARGET = 1e-2
HY_FAST = 0.3
HY_SLOW = 1.5
M_HEADS = 4
M_DH = 128
M_W = M_HEADS * M_DH
EV_CONV = 3 * HY_W + 2 * M_W
EV_PROJ = EV_CONV + 2 * M_W + 4 * M_HEADS
SSD_INNER = 2 * D_MODEL
SSD_HEADDIM = 64
SSD_HEADS = SSD_INNER // SSD_HEADDIM
SSD_GROUPS = 8
SSD_HPG = SSD_HEADS // SSD_GROUPS
SSD_STATE = 128
SSD_GW = SSD_HPG * SSD_HEADDIM
SSD_CONV_DIM = SSD_INNER + 2 * SSD_GROUPS * SSD_STATE
OD_PROJ = SSD_INNER + SSD_CONV_DIM + 2 * SSD_HEADS
FFN_HIDDEN = -(-8 * D_MODEL // (3 * 256)) * 256

V7X_VMEM_BYTES = 64 * 1024 * 1024
VMEM_LIMIT = V7X_VMEM_BYTES * 3 // 4
WIDE_TILE = 512
NARROW_TILE = 256
HALO = 16
COL_CHUNK = 512
HY_NB = 16
SCAN_T = 128
SSD_BLOCK_TOKENS = 2048
MLSTM_BLOCK_TOKENS = 1024


def _cparams(*sem):
    return pltpu.CompilerParams(dimension_semantics=sem, vmem_limit_bytes=VMEM_LIMIT)


def _dot(a, b):
    return jnp.dot(a, b, preferred_element_type=F32)


def _dot_nt(a, b):
    return lax.dot_general(a, b, (((1,), (1,)), ((), ())), preferred_element_type=F32)


def _split2(x):
    h = x.astype(BF16)
    return h, (x - h.astype(F32)).astype(BF16)


def _dot3(th, tl, x):
    xh, xl = _split2(x)
    return _dot(th, xh) + _dot(th, xl) + _dot(tl, xh)


def _dot2(x, t):
    xh, xl = _split2(x)
    return _dot(xh, t) + _dot(xl, t)


def _sigmoid(x):
    return 1.0 / (1.0 + jnp.exp(-x))


def _silu(x):
    return x * _sigmoid(x)


def _log_sigmoid(x):
    return jnp.minimum(x, 0.0) - jnp.log(1.0 + jnp.exp(-jnp.abs(x)))


def _softplus(x):
    return jnp.maximum(x, 0.0) + jnp.log(1.0 + jnp.exp(-jnp.abs(x)))


def _rms(x):
    return x * lax.rsqrt(jnp.mean(x * x, axis=-1, keepdims=True) + EPS)


def _mod_kernel(c_ref, w_ref, b_ref, o_ref):
    a = _silu(c_ref[...]).astype(BF16)
    o_ref[0] = _dot(a, w_ref[0].astype(BF16)) + b_ref[0]


def _modulation(cond, mod_w, mod_b):
    nc = cond.shape[0]
    tn = 1536
    return pl.pallas_call(
        _mod_kernel,
        grid=(DEPTH, 6 * D_MODEL // tn),
        in_specs=[pl.BlockSpec((nc, D_MODEL), lambda l, j: (0, 0)),
                  pl.BlockSpec((1, D_MODEL, tn), lambda l, j: (l, 0, j)),
                  pl.BlockSpec((1, 1, tn), lambda l, j: (l, 0, j))],
        out_specs=pl.BlockSpec((1, nc, tn), lambda l, j: (l, 0, j)),
        out_shape=jax.ShapeDtypeStruct((DEPTH, nc, 6 * D_MODEL), F32),
        compiler_params=_cparams("arbitrary", "arbitrary"),
        name="modulation",
    )(cond, mod_w, mod_b.reshape(DEPTH, 1, 6 * D_MODEL))


def _inproj_kernel(x_ref, xp_ref, xn_ref, mod_ref, g_ref, w_ref, wt_ref, cw_ref, cb_ref, *rest,
                   plan, nt_plan, tiles_per_seq, conv0):
    nseg = len(plan)
    outs = rest[:nseg]
    nt_outs = rest[nseg:nseg + len(nt_plan)]
    hbuf = rest[nseg + len(nt_plan)]
    tm = x_ref.shape[0]
    i = pl.program_id(0)
    j = i % tiles_per_seq
    pmask = (j != 0).astype(F32)
    nmask = (j != tiles_per_seq - 1).astype(F32)
    mod = mod_ref[0]
    sh, sc = mod[0:1], mod[1:2]
    g = g_ref[...]

    def normmod(xv):
        return (_rms(xv) * g * (1.0 + sc) + sh).astype(BF16)

    hbuf[0:tm] = normmod(x_ref[...])
    hbuf[tm:tm + HALO] = normmod(xp_ref[...])
    hbuf[tm + HALO:tm + 2 * HALO] = normmod(xn_ref[...])
    rowid = lax.broadcasted_iota(jnp.int32, (tm, COL_CHUNK), 0)

    for (c0, width, kind, scale, transposed), o_ref in zip(plan, outs):
        for cc in range(c0, c0 + width, COL_CHUNK):
            w = w_ref[0, :, cc:cc + COL_CHUNK]
            if kind == "raw":
                y = _dot(hbuf[0:tm], w)
            else:
                full = _dot(hbuf[...], w)
                acc = full[0:tm]
                ap = full[tm + HALO - 1:tm + HALO] * pmask
                an = full[tm + HALO:tm + HALO + 1] * nmask
                prev = jnp.where(rowid == 0, ap, pltpu.roll(acc, 1, 0))
                nxt = jnp.where(rowid == tm - 1, an, pltpu.roll(acc, tm - 1, 0))
                cw = cw_ref[0, :, cc - conv0:cc - conv0 + COL_CHUNK]
                cb = cb_ref[0, :, cc - conv0:cc - conv0 + COL_CHUNK]
                y = cb + cw[0:1] * prev + cw[1:2] * acc + cw[2:3] * nxt
                if kind == "conv_silu":
                    y = _silu(y)
                if scale != 1.0:
                    y = y * scale
            if transposed:
                o_ref[cc - c0:cc - c0 + COL_CHUNK, :] = y.T.astype(o_ref.dtype)
            else:
                o_ref[:, cc - c0:cc - c0 + COL_CHUNK] = y.astype(o_ref.dtype)

    for (r0, rows), o_ref in zip(nt_plan, nt_outs):
        o_ref[...] = _dot_nt(wt_ref[0, r0:r0 + rows, :], hbuf[0:tm]).astype(o_ref.dtype)


def _inproj(x, mods_l, g_l, w_all, wt_all, cw_all, cb_all, layer_idx, *, plan, nt_plan, conv0, seq_len,
            per_batch_cond, row_tile):
    M = x.shape[0]
    tm = min(row_tile, seq_len)
    tps = seq_len // tm
    nt = M // tm
    hb = tm // HALO
    ntot = w_all.shape[2]
    tw = wt_all.shape[1]
    cdim = cw_all.shape[2]
    e = layer_idx
    cond_map = (lambda i: (1 + i // tps, 0, 0)) if per_batch_cond else (lambda i: (0, 0, 0))
    in_specs = [
        pl.BlockSpec((tm, D_MODEL), lambda i: (i, 0)),
        pl.BlockSpec((HALO, D_MODEL), lambda i: (jnp.maximum(i * hb - 1, 0), 0)),
        pl.BlockSpec((HALO, D_MODEL), lambda i: (jnp.minimum((i + 1) * hb, nt * hb - 1), 0)),
        pl.BlockSpec((1, 6, D_MODEL), cond_map),
        pl.BlockSpec((1, D_MODEL), lambda i: (0, 0)),
        pl.BlockSpec((1, D_MODEL, ntot), lambda i: (e, 0, 0), pipeline_mode=pl.Buffered(1)),
        pl.BlockSpec((1, tw, D_MODEL), lambda i: (e, 0, 0)),
        pl.BlockSpec((1, 3, cdim), lambda i: (e, 0, 0)),
        pl.BlockSpec((1, 1, cdim), lambda i: (e, 0, 0)),
    ]
    out_specs, out_shape = [], []
    for (c0, width, kind, scale, transposed, dt) in plan:
        if transposed:
            out_specs.append(pl.BlockSpec((width, tm), lambda i: (0, i)))
            out_shape.append(jax.ShapeDtypeStruct((width, M), dt))
        else:
            out_specs.append(pl.BlockSpec((tm, width), lambda i: (i, 0)))
            out_shape.append(jax.ShapeDtypeStruct((M, width), dt))
    for (r0, rows, dt) in nt_plan:
        out_specs.append(pl.BlockSpec((rows, tm), lambda i: (0, i)))
        out_shape.append(jax.ShapeDtypeStruct((rows, M), dt))
    kplan = tuple(p[:5] for p in plan)
    knt = tuple(p[:2] for p in nt_plan)
    return pl.pallas_call(
        functools.partial(_inproj_kernel, plan=kplan, nt_plan=knt, tiles_per_seq=tps, conv0=conv0),
        grid=(nt,),
        in_specs=in_specs,
        out_specs=out_specs,
        out_shape=out_shape,
        scratch_shapes=[pltpu.VMEM((tm + 2 * HALO, D_MODEL), BF16)],
        compiler_params=_cparams("arbitrary"),
        name="inproj",
    )(x, x, x, mods_l, g_l, w_all, wt_all, cw_all, cb_all.reshape(cb_all.shape[0], 1, cdim))


def _outproj_even_kernel(yh_ref, ym_ref, x_ref, mod_ref, w_ref, o_ref):
    out = _dot(yh_ref[...], w_ref[0, 0:HY_W]) + _dot(ym_ref[...], w_ref[0, HY_W:])
    o_ref[...] = x_ref[...] + mod_ref[0][2:3] * out


def _outproj_odd_kernel(u_ref, x_ref, mod_ref, ng_ref, w_ref, o_ref):
    un = (_rms(u_ref[...]) * ng_ref[0]).astype(BF16)
    o_ref[...] = x_ref[...] + mod_ref[0][2:3] * _dot(un, w_ref[0])


def _cond_map(seq_len, per_batch_cond):
    tps = seq_len // WIDE_TILE
    return (lambda i: (1 + i // tps, 0, 0)) if per_batch_cond else (lambda i: (0, 0, 0))


def _outproj_even(yh, ym, x, mods_l, w_all, e, *, seq_len, per_batch_cond):
    M = x.shape[0]
    tm = WIDE_TILE
    return pl.pallas_call(
        _outproj_even_kernel,
        grid=(M // tm,),
        in_specs=[pl.BlockSpec((tm, HY_W), lambda i: (i, 0)),
                  pl.BlockSpec((tm, M_W), lambda i: (i, 0)),
                  pl.BlockSpec((tm, D_MODEL), lambda i: (i, 0)),
                  pl.BlockSpec((1, 6, D_MODEL), _cond_map(seq_len, per_batch_cond)),
                  pl.BlockSpec((1, HY_W + M_W, D_MODEL), lambda i: (e, 0, 0), pipeline_mode=pl.Buffered(1))],
        out_specs=pl.BlockSpec((tm, D_MODEL), lambda i: (i, 0)),
        out_shape=jax.ShapeDtypeStruct((M, D_MODEL), F32),
        compiler_params=_cparams("arbitrary"),
        name="outproj_even",
    )(yh, ym, x, mods_l, w_all)


def _outproj_odd(u, x, mods_l, ng_all, w_all, o, *, seq_len, per_batch_cond):
    M = x.shape[0]
    tm = WIDE_TILE
    return pl.pallas_call(
        _outproj_odd_kernel,
        grid=(M // tm,),
        in_specs=[pl.BlockSpec((tm, SSD_INNER), lambda i: (i, 0)),
                  pl.BlockSpec((tm, D_MODEL), lambda i: (i, 0)),
                  pl.BlockSpec((1, 6, D_MODEL), _cond_map(seq_len, per_batch_cond)),
                  pl.BlockSpec((1, 1, SSD_INNER), lambda i: (o, 0, 0)),
                  pl.BlockSpec((1, SSD_INNER, D_MODEL), lambda i: (o, 0, 0), pipeline_mode=pl.Buffered(1))],
        out_specs=pl.BlockSpec((tm, D_MODEL), lambda i: (i, 0)),
        out_shape=jax.ShapeDtypeStruct((M, D_MODEL), F32),
        compiler_params=_cparams("arbitrary"),
        name="outproj_odd",
    )(u, x, mods_l, ng_all, w_all)


FFN_CHUNK = FFN_HIDDEN // 2


def _ffn_kernel(x_ref, mod_ref, g_ref, w1_ref, w3_ref, w2_ref, fg_ref, o_ref, *, final):
    x = x_ref[...]
    mod = mod_ref[0]
    h = (_rms(x) * g_ref[0] * (1.0 + mod[4:5]) + mod[3:4]).astype(BF16)
    ff = jnp.zeros(x.shape, F32)
    for c0 in range(0, FFN_HIDDEN, FFN_CHUNK):
        a = _dot(h, w1_ref[0, :, c0:c0 + FFN_CHUNK])
        b = _dot(h, w3_ref[0, :, c0:c0 + FFN_CHUNK])
        u = (_silu(a) * b).astype(BF16)
        ff = ff + _dot(u, w2_ref[0, c0:c0 + FFN_CHUNK, :])
    xo = x + mod[5:6] * ff
    if final:
        xo = _rms(xo) * fg_ref[...]
    o_ref[...] = xo


def _ffn(x, mods_l, g_all, w1, w3, w2, fg, l, *, seq_len, per_batch_cond, final):
    M = x.shape[0]
    tm = WIDE_TILE
    wspec = lambda shape: pl.BlockSpec(shape, lambda i: (l, 0, 0), pipeline_mode=pl.Buffered(1))
    return pl.pallas_call(
        functools.partial(_ffn_kernel, final=final),
        grid=(M // tm,),
        in_specs=[pl.BlockSpec((tm, D_MODEL), lambda i: (i, 0)),
                  pl.BlockSpec((1, 6, D_MODEL), _cond_map(seq_len, per_batch_cond)),
                  pl.BlockSpec((1, 1, D_MODEL), lambda i: (l, 0, 0)),
                  wspec((1, D_MODEL, FFN_HIDDEN)), wspec((1, D_MODEL, FFN_HIDDEN)),
                  wspec((1, FFN_HIDDEN, D_MODEL)),
                  pl.BlockSpec((1, D_MODEL), lambda i: (0, 0))],
        out_specs=pl.BlockSpec((tm, D_MODEL), lambda i: (i, 0)),
        out_shape=jax.ShapeDtypeStruct((M, D_MODEL), F32),
        compiler_params=_cparams("arbitrary"),
        name="ffn",
    )(x, mods_l, g_all, w1, w3, w2, fg)


def _scan_consts(d):
    T = SCAN_T
    row = lax.broadcasted_iota(jnp.int32, (T, T), 0)
    col = lax.broadcasted_iota(jnp.int32, (T, T), 1)
    mask, last = (row <= col, T - 1) if d == 0 else (row >= col, 0)
    v_mat = jnp.where(mask, 1.0, 0.0).astype(BF16)
    return mask, v_mat, last


def _mlstm_kernel(qt_ref, k_ref, vt_ref, o_ref, gt_ref, gb_ref, ng_ref, c0_ref, m0_ref,
                  y_ref, cst_ref, mst_ref, h_scr, c_scr, m_scr, sb_scr, vw_scr, rows_scr, *, nc, ncs):
    T = SCAN_T
    H = M_HEADS
    nseq = nc // ncs
    ones_blk = jnp.where(lax.broadcasted_iota(jnp.int32, (M_DH, T), 0) == 0, 1.0, 0.0)
    consts = (_scan_consts(0), _scan_consts(1))
    gb = gb_ref[...]

    def gates(ci, d, slot):
        mask, v_mat, last = consts[d]
        c0 = pl.multiple_of(ci * T, T)
        lanes = pl.ds(c0, T)
        graw = gt_ref[:, lanes] + gb
        lf = _log_sigmoid(graw)
        b_all = _dot2(lf, v_mat)
        for h in range(H):
            fi = (2 * d + 1) * H + h
            ii = (2 * d) * H + h
            hl = slice(h * M_DH, (h + 1) * M_DH)
            tr = slice(h * T, (h + 1) * T)
            b_row, i_row = b_all[fi:fi + 1], graw[ii:ii + 1]
            dlog = jnp.where(mask, jnp.broadcast_to(b_row, (T, T)) - jnp.broadcast_to(b_row - i_row, (T, T)).T,
                             -jnp.inf)
            mloc = jnp.max(dlog, axis=0, keepdims=True)
            sb_scr[slot, d, tr, :] = (_dot(k_ref[pl.ds(c0, T), hl], qt_ref[hl, lanes])
                                      * jnp.exp(dlog - mloc)).astype(BF16)
            mloc_last = jnp.broadcast_to(mloc[:, last:last + 1], (1, T))
            b_last = jnp.broadcast_to(b_row[:, last:last + 1], (1, T))
            w_end = jnp.exp(b_last - b_row + i_row - mloc_last)
            vw_scr[slot, d, h, 0:M_DH, :] = (vt_ref[hl, lanes].astype(F32) * w_end).astype(BF16)
            vw_scr[slot, d, h, M_DH:, :] = (ones_blk * w_end).astype(BF16)
            rows_scr[slot, d, 4 * h:4 * h + 4, :] = jnp.concatenate([b_row, mloc, mloc_last, b_last], axis=0)

    def update(ci, d, slot, accumulate):
        c0 = pl.multiple_of(ci * T, T)
        lanes = pl.ds(c0, T)
        for h in range(H):
            sidx = d * H + h
            hl = slice(h * M_DH, (h + 1) * M_DH)
            tr = slice(h * T, (h + 1) * T)
            rows = rows_scr[slot, d, 4 * h:4 * h + 4, :]
            b_row, mloc, mloc_last, b_last = rows[0:1], rows[1:2], rows[2:3], rows[3:4]
            m_row = m_scr[sidx]
            inter = b_row + m_row
            mt = jnp.maximum(inter, mloc)
            kh = k_ref[pl.ds(c0, T), hl]
            qt = qt_ref[hl, lanes]
            vext = jnp.concatenate([vt_ref[hl, lanes], ones_blk.astype(BF16)], axis=0)
            ct = c_scr[sidx]
            tot = (jnp.exp(mloc - mt) * _dot(vext, sb_scr[slot, d, tr, :])
                   + jnp.exp(inter - mt) * _dot(ct.astype(BF16), qt))
            den = tot[M_DH:M_DH + 1]
            ht = tot[0:M_DH] / jnp.maximum(jnp.abs(den), jnp.exp(-mt))
            if accumulate:
                h_scr[hl, lanes] = h_scr[hl, lanes] + ht
            else:
                h_scr[hl, lanes] = ht
            m_new = jnp.maximum(b_last + m_row, mloc_last)
            c_scr[sidx] = (jnp.exp(b_last + m_row - m_new) * ct
                           + jnp.exp(mloc_last - m_new) * _dot(vw_scr[slot, d, h], kh))
            m_scr[sidx] = m_new

    def chunks_at(j):
        if nseq == 1:
            return j, nc - 1 - j
        seq, loc = divmod(j, ncs)
        return j, seq * ncs + ncs - 1 - loc

    def load_state(seq):
        c_scr[...] = c0_ref[seq]
        m_scr[...] = m0_ref[seq]

    def store_state(seq):
        cst_ref[seq] = c_scr[...]
        mst_ref[seq] = m_scr[...]

    def pair(p, slots, accumulate=None):
        s0, s1 = slots
        for j, slot in ((2 * p, s0), (2 * p + 1, s1)):
            if nseq > 1 and j % ncs == 0:
                load_state(j // ncs)
            acc = (j % ncs >= ncs // 2) if accumulate is None else accumulate
            cf, cb = chunks_at(j)
            update(cf, 0, slot, acc)
            update(cb, 1, slot, acc)
            if nseq > 1 and j % ncs == ncs - 1:
                store_state(j // ncs)
        for j, slot in ((2 * p + 2, 2 - s0), (2 * p + 3, 4 - s1)):
            if nseq == 1:
                gates(jnp.minimum(j, nc - 1), 0, slot)
                gates(jnp.maximum(nc - 1 - j, 0), 1, slot)
            elif j < nc:
                cf, cb = chunks_at(j)
                gates(cf, 0, slot)
                gates(cb, 1, slot)

    for j in (0, 1):
        cf, cb = chunks_at(j)
        gates(cf, 0, j)
        gates(cb, 1, j)
    if nseq == 1:
        load_state(0)
    if nseq == 1 and nc % 8 == 0:
        def body(accumulate):
            def run(i, carry):
                pair(2 * i, (0, 1), accumulate)
                pair(2 * i + 1, (2, 3), accumulate)
                return carry
            return run
        lax.fori_loop(0, nc // 8, body(False), 0)
        lax.fori_loop(nc // 8, nc // 4, body(True), 0)
    else:
        for p in range(nc // 2):
            pair(p, ((0, 1), (2, 3))[p % 2])
    if nseq == 1:
        store_state(0)

    def epilogue(i, carry):
        c0 = pl.multiple_of(i * T, T)
        for h in range(H):
            hl = slice(h * M_DH, (h + 1) * M_DH)
            hh = h_scr[hl, pl.ds(c0, T)]
            hn = hh * lax.rsqrt(jnp.mean(hh * hh, axis=0, keepdims=True) + EPS)
            y = hn.T * ng_ref[:, hl] * _sigmoid(o_ref[pl.ds(c0, T), hl])
            y_ref[pl.ds(c0, T), hl] = y.astype(y_ref.dtype)
        return carry

    lax.fori_loop(0, nc, epilogue, 0)


def _mlstm(qt, k, vt, o, gt, gate_b, norm_g, c0, m0, B, L):
    assert L % (2 * SCAN_T) == 0 and M_DH == SCAN_T
    bb = max(1, min(B, MLSTM_BLOCK_TOKENS // L))
    assert B % bb == 0
    ncs = L // SCAN_T
    nc = bb * ncs
    Lb = bb * L
    ng = 4 * M_HEADS
    M = B * L
    one = pl.Buffered(1)
    st_spec = pl.BlockSpec((bb, 2 * M_HEADS, 2 * M_DH, M_DH), lambda b: (b, 0, 0, 0))
    m_spec = pl.BlockSpec((bb, 2 * M_HEADS, 1, M_DH), lambda b: (b, 0, 0, 0))
    return pl.pallas_call(
        functools.partial(_mlstm_kernel, nc=nc, ncs=ncs),
        grid=(B // bb,),
        in_specs=[pl.BlockSpec((M_W, Lb), lambda b: (0, b), pipeline_mode=one),
                  pl.BlockSpec((Lb, M_W), lambda b: (b, 0), pipeline_mode=one),
                  pl.BlockSpec((M_W, Lb), lambda b: (0, b), pipeline_mode=one),
                  pl.BlockSpec((Lb, M_W), lambda b: (b, 0), pipeline_mode=one),
                  pl.BlockSpec((ng, Lb), lambda b: (0, b)),
                  pl.BlockSpec((ng, SCAN_T), lambda b: (0, 0)),
                  pl.BlockSpec((1, M_W), lambda b: (0, 0)),
                  st_spec, m_spec],
        out_specs=[pl.BlockSpec((Lb, M_W), lambda b: (b, 0)), st_spec, m_spec],
        out_shape=[jax.ShapeDtypeStruct((M, M_W), BF16),
                   jax.ShapeDtypeStruct((B, 2 * M_HEADS, 2 * M_DH, M_DH), F32),
                   jax.ShapeDtypeStruct((B, 2 * M_HEADS, 1, M_DH), F32)],
        scratch_shapes=[pltpu.VMEM((M_W, Lb), F32),
                        pltpu.VMEM((2 * M_HEADS, 2 * M_DH, M_DH), F32),
                        pltpu.VMEM((2 * M_HEADS, 1, M_DH), F32),
                        pltpu.VMEM((4, 2, M_HEADS * SCAN_T, SCAN_T), BF16),
                        pltpu.VMEM((4, 2, M_HEADS, 2 * M_DH, SCAN_T), BF16),
                        pltpu.VMEM((4, 2, 4 * M_HEADS, SCAN_T), F32)],
        compiler_params=_cparams("arbitrary"),
        name="mlstm",
    )(qt, k, vt, o, gt, jnp.broadcast_to(gate_b.reshape(ng, 1), (ng, SCAN_T)), norm_g.reshape(1, M_W), c0, m0)


def _ssd_kernel(xt_ref, b_ref, ct_ref, dtt_ref, z_ref, db_ref, al_ref, dsk_ref, s0_ref,
                y_ref, sout_ref, yf_scr, yb_scr, s_scr, cbt_scr, p_scr, xd_scr, xdec_scr, rows_scr, *, nc, ncs):
    T = SCAN_T
    H = SSD_HPG
    nseq = nc // ncs
    consts = (_scan_consts(0), _scan_consts(1))
    db = db_ref[0]
    a_neg = -jnp.exp(al_ref[0])

    def gates(ci, d, slot):
        mask, v_mat, last = consts[d]
        c0 = pl.multiple_of(ci * T, T)
        lanes = pl.ds(c0, T)
        hr = slice(d * H, (d + 1) * H)
        dt = _softplus(dtt_ref[hr, lanes] + db[hr])
        a = dt * a_neg[hr]
        acs = _dot2(a, v_mat)
        a_last = jnp.broadcast_to(acs[:, last:last + 1], (H, T))
        rows_scr[slot, d, 0:H, :] = jnp.exp(acs)
        rows_scr[slot, d, H:2 * H, :] = jnp.exp(a_last)
        cbt_scr[d] = _dot(b_ref[pl.ds(c0, T), :], ct_ref[:, lanes])
        for r in range(H):
            tr = slice(r * T, (r + 1) * T)
            pr = slice(r * SSD_HEADDIM, (r + 1) * SSD_HEADDIM)
            al = jnp.broadcast_to(acs[r:r + 1], (T, T))
            lm = jnp.exp(jnp.where(mask, al - al.T, -jnp.inf))
            p_scr[slot, d, tr, :] = (cbt_scr[d] * lm).astype(BF16)
            xd = xt_ref[pr, lanes] * dt[r:r + 1]
            xd_scr[slot, d, pr, :] = xd.astype(BF16)
            xdec_scr[slot, d, pr, :] = (xd * jnp.exp(a_last[r:r + 1] - acs[r:r + 1])).astype(BF16)

    def update(ci, d, slot):
        c0 = pl.multiple_of(ci * T, T)
        lanes = pl.ds(c0, T)
        rows = rows_scr[slot, d]
        sc = _dot(s_scr[d].astype(BF16), ct_ref[:, lanes])
        upd = _dot(xdec_scr[slot, d], b_ref[pl.ds(c0, T), :])
        y_scr = yf_scr if d == 0 else yb_scr
        for r in range(H):
            pr = slice(r * SSD_HEADDIM, (r + 1) * SSD_HEADDIM)
            tr = slice(r * T, (r + 1) * T)
            y_scr[pr, lanes] = _dot(xd_scr[slot, d, pr, :], p_scr[slot, d, tr, :]) + rows[r:r + 1] * sc[pr]
            s_scr[d, pr, :] = rows[H + r:H + r + 1] * s_scr[d, pr, :] + upd[pr]

    def chunks_at(j):
        if nseq == 1:
            return j, nc - 1 - j
        seq, loc = divmod(j, ncs)
        return j, seq * ncs + ncs - 1 - loc

    def pair(p, slots):
        s0, s1 = slots
        for j, slot in ((2 * p, s0), (2 * p + 1, s1)):
            if nseq > 1 and j % ncs == 0:
                s_scr[...] = s0_ref[j // ncs, :, 0]
            cf, cb = chunks_at(j)
            update(cf, 0, slot)
            update(cb, 1, slot)
            if nseq > 1 and j % ncs == ncs - 1:
                sout_ref[j // ncs, :, 0] = s_scr[...]
        for j, slot in ((2 * p + 2, 2 - s0), (2 * p + 3, 4 - s1)):
            if nseq == 1:
                gates(jnp.minimum(j, nc - 1), 0, slot)
                gates(jnp.maximum(nc - 1 - j, 0), 1, slot)
            elif j < nc:
                cf, cb = chunks_at(j)
                gates(cf, 0, slot)
                gates(cb, 1, slot)

    for j in (0, 1):
        cf, cb = chunks_at(j)
        gates(cf, 0, j)
        gates(cb, 1, j)
    if nseq == 1:
        s_scr[...] = s0_ref[0, :, 0]
    if nseq == 1 and nc % 4 == 0:
        def body(i, carry):
            pair(2 * i, (0, 1))
            pair(2 * i + 1, (2, 3))
            return carry
        lax.fori_loop(0, nc // 4, body, 0)
    else:
        for p in range(nc // 2):
            pair(p, ((0, 1), (2, 3))[p % 2])
    if nseq == 1:
        sout_ref[0, :, 0] = s_scr[...]

    dsk = dsk_ref[0]

    def epilogue(i, carry):
        c0 = pl.multiple_of(i * T, T)
        lanes = pl.ds(c0, T)
        yt = yf_scr[:, lanes] + yb_scr[:, lanes] + dsk * xt_ref[:, lanes]
        y_ref[pl.ds(c0, T), :] = yt.T * _silu(z_ref[pl.ds(c0, T), :])
        return carry

    lax.fori_loop(0, nc, epilogue, 0)


def _ssd(xt, bm, ct, dtt, z, dt_bias, a_log, dskip, s0, B, L):
    assert L % (2 * SCAN_T) == 0 and SSD_STATE == SCAN_T
    bb = max(1, min(B, SSD_BLOCK_TOKENS // L))
    assert B % bb == 0
    ncs = L // SCAN_T
    nc = bb * ncs
    Lb = bb * L
    G = SSD_GROUPS
    nd = 2 * SSD_HPG
    M = B * L
    rep = lambda p, rows: jnp.broadcast_to(p.reshape(G, rows, 1), (G, rows, SCAN_T))
    small = lambda rows: pl.BlockSpec((1, rows, SCAN_T), lambda b, g: (g, 0, 0))
    st_spec = pl.BlockSpec((bb, 2, 1, SSD_GW, SSD_STATE), lambda b, g: (b, 0, g, 0, 0))
    return pl.pallas_call(
        functools.partial(_ssd_kernel, nc=nc, ncs=ncs),
        grid=(B // bb, G),
        in_specs=[pl.BlockSpec((SSD_GW, Lb), lambda b, g: (g, b)),
                  pl.BlockSpec((Lb, SSD_STATE), lambda b, g: (b, g)),
                  pl.BlockSpec((SSD_STATE, Lb), lambda b, g: (g, b)),
                  pl.BlockSpec((nd, Lb), lambda b, g: (g, b)),
                  pl.BlockSpec((Lb, SSD_GW), lambda b, g: (b, g)),
                  small(nd), small(nd), small(SSD_GW), st_spec],
        out_specs=[pl.BlockSpec((Lb, SSD_GW), lambda b, g: (b, g)), st_spec],
        out_shape=[jax.ShapeDtypeStruct((M, SSD_INNER), F32),
                   jax.ShapeDtypeStruct((B, 2, G, SSD_GW, SSD_STATE), F32)],
        scratch_shapes=[pltpu.VMEM((SSD_GW, Lb), F32), pltpu.VMEM((SSD_GW, Lb), F32),
                        pltpu.VMEM((2, SSD_GW, SSD_STATE), F32),
                        pltpu.VMEM((2, SCAN_T, SCAN_T), F32),
                        pltpu.VMEM((4, 2, SSD_HPG * SCAN_T, SCAN_T), BF16),
                        pltpu.VMEM((4, 2, SSD_GW, SCAN_T), BF16),
                        pltpu.VMEM((4, 2, SSD_GW, SCAN_T), BF16),
                        pltpu.VMEM((4, 2, 2 * SSD_HPG, SCAN_T), F32)],
        compiler_params=_cparams("arbitrary", "arbitrary"),
        name="ssd",
    )(xt, bm, ct, dtt, z, rep(dt_bias, nd), rep(a_log, nd), rep(dskip, SSD_GW), s0)


def _hy_factors(L):
    return (128, 64) if L == 4096 else (32, (2 * L) // 32)


def _hilo(t):
    return _split2(t.astype(F32))


def _cblock(cr, ci):
    return jnp.concatenate([jnp.concatenate([cr, -ci], axis=-1), jnp.concatenate([ci, cr], axis=-1)], axis=-2)


def _dft_tables(L):
    n1c, n2c = _hy_factors(L)
    N = n1c * n2c
    h1 = n1c // 2
    i32 = jnp.int32
    k1 = jnp.arange(n1c, dtype=i32)
    n2 = jnp.arange(n2c, dtype=i32)

    def stage1(n1_count):
        n1 = jnp.arange(n1_count, dtype=i32)
        idx = (k1[None, :, None] * (n2[:, None, None] + n2c * n1[None, None, :])) % N
        ang = idx.astype(F32) * (2.0 * math.pi / N)
        return jnp.cos(ang), jnp.sin(ang)

    c, s = stage1(h1)
    g_fwd = _cblock(c, -s)
    g_inv = _cblock(jnp.swapaxes(c, 1, 2), jnp.swapaxes(s, 1, 2)) / n1c
    cf, sf = stage1(n1c)
    g_real = jnp.concatenate([cf, -sf], axis=1)
    idx2 = (n2[:, None] * n2[None, :]) % n2c
    ang2 = idx2.astype(F32) * (2.0 * math.pi / n2c)
    c2, s2 = jnp.cos(ang2), jnp.sin(ang2)
    f2 = _cblock(c2, -s2)
    f2_inv = _cblock(c2, s2) / n2c
    return dict(g_fwd=g_fwd.astype(BF16), g_inv=g_inv.astype(BF16), f2=f2.astype(BF16),
                f2_inv=f2_inv.astype(BF16), g_real=_hilo(g_real), f2_exact=_hilo(f2))


def _swap01(x):
    return jnp.swapaxes(x, 0, 1)


def _hy_s1_kernel(x_ref, *rest, nb, exact):
    if exact:
        gh_ref, gl_ref, o_ref = rest
    else:
        g_ref, o_ref = rest
    pb, rows, _, C = x_ref.shape
    xs = _swap01(x_ref[...].reshape(pb * rows, nb, C))
    if exact:
        res = [_dot3(gh_ref[j], gl_ref[j], xs[j]) for j in range(nb)]
    else:
        res = [_dot(g_ref[j], xs[j].astype(BF16)) for j in range(nb)]
    out = _swap01(jnp.stack(res, axis=0).astype(o_ref.dtype))
    o_ref[...] = out.reshape(o_ref.shape)


def _hy_stage1(x, tab, out_dtype):
    exact = isinstance(tab, tuple)
    tabs = tab if exact else (tab,)
    n2c, rows_out, R = tabs[0].shape
    Bx, rows, _, C = x.shape
    pb = R // rows
    nb = HY_NB
    cb = min(C, HY_W)
    tspec = pl.BlockSpec((nb, rows_out, R), lambda p, t, c: (t, 0, 0))
    return pl.pallas_call(
        functools.partial(_hy_s1_kernel, nb=nb, exact=exact),
        grid=(Bx // pb, n2c // nb, C // cb),
        in_specs=[pl.BlockSpec((pb, rows, nb, cb), lambda p, t, c: (p, 0, t, c))] + [tspec] * len(tabs),
        out_specs=pl.BlockSpec((None, 2, rows_out // 2, nb, cb), lambda p, t, c: (p, 0, 0, t, c)),
        out_shape=jax.ShapeDtypeStruct((Bx // pb, 2, rows_out // 2, n2c, C), out_dtype),
        compiler_params=_cparams("arbitrary", "arbitrary", "arbitrary"),
        name="hyena_stage1",
    )(x, *tabs)


def _hy_spec_kernel(a_ref, inv_ref, fh_ref, fl_ref, o_ref, *, kb):
    n2c = a_ref.shape[2]
    for kk in range(kb):
        rhs = jnp.concatenate([a_ref[0, kk], a_ref[1, kk]], axis=0)
        X = _dot3(fh_ref[...], fl_ref[...], rhs) * inv_ref[...]
        o_ref[0, kk] = X[0:n2c]
        o_ref[1, kk] = X[n2c:]


def _hy_filter_spectrum(a, inv_norm, f2, kb):
    _, n1c, n2c, W = a.shape
    fh, fl = f2
    blk = pl.BlockSpec((2, kb, n2c, W), lambda t: (0, t, 0, 0))
    fspec = pl.BlockSpec((2 * n2c, 2 * n2c), lambda t: (0, 0))
    return pl.pallas_call(
        functools.partial(_hy_spec_kernel, kb=kb),
        grid=(n1c // kb,),
        in_specs=[blk, pl.BlockSpec((1, W), lambda t: (0, 0)), fspec, fspec],
        out_specs=blk,
        out_shape=jax.ShapeDtypeStruct(a.shape, F32),
        compiler_params=_cparams("arbitrary"),
        name="hyena_filter_spectrum",
    )(a, inv_norm, fh, fl)


def _hy_mid_kernel(a_ref, k_ref, f_ref, i_ref, o_ref, *, kb):
    n2c = a_ref.shape[2]
    res = []
    for kk in range(kb):
        rhs = jnp.concatenate([a_ref[0, kk], a_ref[1, kk]], axis=0)
        X = _dot(f_ref[...], rhs)
        xr, xi = X[0:n2c], X[n2c:]
        kr, ki = k_ref[0, kk], k_ref[1, kk]
        Y = jnp.concatenate([xr * kr - xi * ki, xr * ki + xi * kr], axis=0)
        res.append(_dot(i_ref[...], Y.astype(BF16)))
    o_ref[...] = _swap01(jnp.stack(res, axis=0).astype(o_ref.dtype)).reshape(o_ref.shape)


def _hy_mid(a, kspec, order, f2, f2_inv):
    P, _, n1c, n2c, C = a.shape
    kb = HY_NB
    fspec = pl.BlockSpec((2 * n2c, 2 * n2c), lambda t, p: (0, 0))
    return pl.pallas_call(
        functools.partial(_hy_mid_kernel, kb=kb),
        grid=(n1c // kb, P),
        in_specs=[pl.BlockSpec((None, 2, kb, n2c, C), lambda t, p: (p, 0, t, 0, 0)),
                  pl.BlockSpec((2, kb, n2c, C), lambda t, p: (0, t, 0, order)),
                  fspec, fspec],
        out_specs=pl.BlockSpec((None, 2, n2c, kb, C), lambda t, p: (p, 0, 0, t, 0)),
        out_shape=jax.ShapeDtypeStruct((P, 2, n2c, n1c, C), BF16),
        compiler_params=_cparams("arbitrary", "arbitrary"),
        name="hyena_mid",
    )(a, kspec, f2, f2_inv)


def _hy_last_kernel(b_ref, z_ref, gate_ref, bias_ref, gi_ref, *rest, fuse, nb):
    if fuse:
        gf_ref, zo_ref, ao_ref = rest
    else:
        (zo_ref,) = rest
    _, h1, _, C = gate_ref.shape
    gs = _swap01(gate_ref[...].reshape(2 * h1, nb, C))
    zs = _swap01(z_ref[...].reshape(2 * h1, nb, C)) if fuse else z_ref[...]
    bias = bias_ref[...]
    zn, an = [], []
    for j in range(nb):
        bv = jnp.concatenate([b_ref[0, j], b_ref[1, j]], axis=0)
        y = _dot(gi_ref[j], bv)
        znew = gs[j] * (y + zs[j] * bias)
        zn.append(znew)
        if fuse:
            an.append(_dot(gf_ref[j], znew.astype(BF16)))
    if fuse:
        zo_ref[...] = jnp.stack(zn, axis=0)
        ao_ref[...] = _swap01(jnp.stack(an, axis=0).astype(ao_ref.dtype)).reshape(ao_ref.shape)
    else:
        zo_ref[...] = _swap01(jnp.stack(zn, axis=0).astype(zo_ref.dtype)).reshape(zo_ref.shape)


def _hy_last(bf, z, gate, bias, g_inv, g_fwd, out_dtype):
    P, _, n2c, n1c, C = bf.shape
    B, h1, _, _ = gate.shape
    nb = HY_NB
    cb = C // 2
    fuse = g_fwd is not None
    nat = pl.BlockSpec((2, h1, nb, cb), lambda p, t, c: (p, 0, t, c))
    swp = pl.BlockSpec((None, nb, 2 * h1, cb), lambda p, t, c: (p, t, 0, c))
    in_specs = [pl.BlockSpec((None, 2, nb, n1c, cb), lambda p, t, c: (p, 0, t, 0, c)),
                nat if fuse else swp, nat, pl.BlockSpec((1, cb), lambda p, t, c: (0, c)),
                pl.BlockSpec((nb, 2 * h1, 2 * n1c), lambda p, t, c: (t, 0, 0))]
    args = [bf, z, gate, bias, g_inv]
    if fuse:
        in_specs.append(pl.BlockSpec((nb, 2 * n1c, 2 * h1), lambda p, t, c: (t, 0, 0)))
        args.append(g_fwd)
        out_specs = [swp, pl.BlockSpec((None, 2, n1c, nb, cb), lambda p, t, c: (p, 0, 0, t, c))]
        out_shape = [jax.ShapeDtypeStruct((P, n2c, 2 * h1, C), F32),
                     jax.ShapeDtypeStruct((P, 2, n1c, n2c, C), BF16)]
    else:
        out_specs = [nat]
        out_shape = [jax.ShapeDtypeStruct(gate.shape, out_dtype)]
    return pl.pallas_call(
        functools.partial(_hy_last_kernel, fuse=fuse, nb=nb),
        grid=(P, n2c // nb, C // cb),
        in_specs=in_specs, out_specs=out_specs, out_shape=out_shape,
        compiler_params=_cparams("arbitrary", "arbitrary", "arbitrary"),
        name="hyena_last",
    )(*args)


def _hy_filter_kernel(f_ref, w1_ref, b1_ref, w2_ref, b2_ref, w3_ref, fr_ref, dl_ref, hk_ref, s_ref, *, L):
    i = pl.program_id(0)
    tl = f_ref.shape[0]
    f = f_ref[...]
    fr = fr_ref[...]
    h = jnp.sin(fr * (_dot(f.astype(BF16), w1_ref[...].astype(BF16)) + b1_ref[...]))
    h = jnp.sin(fr * (_dot(h.astype(BF16), w2_ref[...].astype(BF16)) + b2_ref[...]))
    hk = _dot(h.astype(BF16), w3_ref[...].astype(BF16)) * jnp.exp(-f[:, 0:1] * dl_ref[...])
    n = i * tl + lax.broadcasted_iota(jnp.int32, hk.shape, 0)
    hk = jnp.where(n == L, 0.0, hk)
    hk_ref[...] = hk
    tot = jnp.sum(jnp.abs(hk), axis=0, keepdims=True)

    @pl.when(i == 0)
    def _():
        s_ref[...] = tot

    @pl.when(i != 0)
    def _():
        s_ref[...] = s_ref[...] + tot


def _hy_filter(L, w1, b1, w2, b2, w3, freq):
    f32 = F32
    t = jnp.linspace(0.0, 1.0, L, dtype=f32)[:, None]
    ang = (2.0 * math.pi / L) * jnp.arange(L, dtype=f32)[:, None] * \
        jnp.linspace(1e-4, HY_BANDS - 1, HY_BANDS, dtype=f32)[None]
    feats = jnp.concatenate([t, jnp.cos(ang), -jnp.sin(ang)], axis=-1)
    feats = jnp.concatenate([feats, feats[0:1], feats[:0:-1]], axis=0)
    fpad = 128
    feats = jnp.pad(feats, ((0, 0), (0, fpad - HY_EMB)))
    w1p = jnp.pad(w1, ((0, fpad - HY_EMB), (0, 0)))
    deltas = jnp.abs(jnp.linspace(math.log(HY_TARGET) / HY_SLOW, math.log(HY_TARGET) / HY_FAST, HY_W, dtype=f32))
    OC = HY_ORDER * HY_W
    dl = jnp.tile(deltas, HY_ORDER).reshape(1, OC)
    tl = min(L, 512)
    nblk = L // tl
    full = lambda shp: pl.BlockSpec(shp, lambda i: (0, 0))
    return pl.pallas_call(
        functools.partial(_hy_filter_kernel, L=L),
        grid=(2 * nblk,),
        in_specs=[pl.BlockSpec((tl, fpad), lambda i: (i, 0)),
                  full((fpad, HY_HID)), full((1, HY_HID)), full((HY_HID, HY_HID)), full((1, HY_HID)),
                  pl.BlockSpec((HY_HID, OC), lambda i: (0, i // nblk)), full((1, HY_HID)), full((1, OC))],
        out_specs=[pl.BlockSpec((tl, OC), lambda i: (i, 0)), full((1, OC))],
        out_shape=[jax.ShapeDtypeStruct((2 * L, OC), F32), jax.ShapeDtypeStruct((1, OC), F32)],
        compiler_params=_cparams("arbitrary"),
        name="hyena_filter",
    )(feats, w1p, b1.reshape(1, HY_HID), w2, b2.reshape(1, HY_HID), w3, freq.reshape(1, HY_HID), dl)


def _hy_spectrum(L, tabs, w1, b1, w2, b2, w3, freq):
    n1c, n2c = _hy_factors(L)
    OC = HY_ORDER * HY_W
    ktime, s = _hy_filter(L, w1, b1, w2, b2, w3, freq)
    a = _hy_stage1(ktime.reshape(1, n1c, n2c, OC), tabs["g_real"], F32)[0]
    return _hy_filter_spectrum(a, 1.0 / s, tabs["f2_exact"], 8)


def _hyena(v, x1, x2, kspec, bias, tabs, B, L):
    n1c, n2c = _hy_factors(L)
    C = HY_W
    view = lambda t: t.reshape(B, n1c // 2, n2c, C)
    z0, g1, g2 = view(v), view(x1), view(x2)
    a = _hy_stage1(z0, tabs["g_fwd"], BF16)
    bf = _hy_mid(a, kspec, 0, tabs["f2"], tabs["f2_inv"])
    z1, a = _hy_last(bf, z0, g1, bias[0:1], tabs["g_inv"], tabs["g_fwd"], F32)
    bf = _hy_mid(a, kspec, 1, tabs["f2"], tabs["f2_inv"])
    (z2,) = _hy_last(bf, z1, g2, bias[1:2], tabs["g_inv"], None, BF16)
    return z2.reshape(B * L, C)


def _direct_tables(L):
    N = 2 * L
    k = jnp.arange(N, dtype=jnp.int32)
    ang = ((k[:, None] * k[None, :]) % N).astype(F32) * (2.0 * math.pi / N)
    c, s = jnp.cos(ang), jnp.sin(ang)
    fwd = _cblock(c[:, :L], -s[:, :L])
    inv = _cblock(c[:L, :], s[:L, :]) / N
    real = jnp.concatenate([c, -s], axis=0)
    return dict(fwd=fwd.astype(BF16), inv=inv.astype(BF16), real=_hilo(real))


def _hy_direct_spec_kernel(kt_ref, inv_ref, fh_ref, fl_ref, o_ref):
    N = kt_ref.shape[0]
    X = _dot3(fh_ref[...], fl_ref[...], kt_ref[...]) * inv_ref[...]
    o_ref[0] = X[0:N]
    o_ref[1] = X[N:]


def _hy_direct_spectrum(L, tabs, w1, b1, w2, b2, w3, freq):
    N = 2 * L
    OC = HY_ORDER * HY_W
    cb = HY_W // 2
    ktime, s = _hy_filter(L, w1, b1, w2, b2, w3, freq)
    fh, fl = tabs["real"]
    fspec = pl.BlockSpec((2 * N, N), lambda c: (0, 0))
    return pl.pallas_call(
        _hy_direct_spec_kernel,
        grid=(OC // cb,),
        in_specs=[pl.BlockSpec((N, cb), lambda c: (0, c)), pl.BlockSpec((1, cb), lambda c: (0, c)), fspec, fspec],
        out_specs=pl.BlockSpec((2, N, cb), lambda c: (0, 0, c)),
        out_shape=jax.ShapeDtypeStruct((2, N, OC), F32),
        compiler_params=_cparams("arbitrary"),
        name="hyena_direct_spectrum",
    )(ktime, 1.0 / s, fh, fl)


def _hy_direct_kernel(v_ref, g1_ref, g2_ref, k0_ref, k1_ref, bias_ref, f_ref, fi_ref, o_ref):
    L = v_ref.shape[1]
    N = 2 * L
    z = jnp.concatenate([v_ref[0], v_ref[1]], axis=0)
    for o, (k_ref, g_ref) in enumerate(((k0_ref, g1_ref), (k1_ref, g2_ref))):
        X = _dot(f_ref[...], z.astype(BF16))
        xr, xi = X[0:N], X[N:]
        kr, ki = k_ref[0], k_ref[1]
        Y = jnp.concatenate([xr * kr - xi * ki, xr * ki + xi * kr], axis=0)
        y = _dot(fi_ref[...], Y.astype(BF16))
        gate = jnp.concatenate([g_ref[0], g_ref[1]], axis=0)
        z = gate * (y + z * bias_ref[o:o + 1])
    o_ref[0] = z[0:L].astype(o_ref.dtype)
    o_ref[1] = z[L:].astype(o_ref.dtype)


def _hyena_direct(v, x1, x2, kspec, bias, tabs, B, L):
    C = HY_W
    N = 2 * L
    cb = C // 2
    ncb = C // cb
    seq = pl.BlockSpec((2, L, cb), lambda p, c: (p, 0, c))
    view = lambda t: t.reshape(B, L, C)
    out = pl.pallas_call(
        _hy_direct_kernel,
        grid=(B // 2, ncb),
        in_specs=[seq, seq, seq,
                  pl.BlockSpec((2, N, cb), lambda p, c: (0, 0, c)),
                  pl.BlockSpec((2, N, cb), lambda p, c: (0, 0, ncb + c)),
                  pl.BlockSpec((HY_ORDER, cb), lambda p, c: (0, c)),
                  pl.BlockSpec((2 * N, 2 * L), lambda p, c: (0, 0)),
                  pl.BlockSpec((2 * L, 2 * N), lambda p, c: (0, 0))],
        out_specs=seq,
        out_shape=jax.ShapeDtypeStruct((B, L, C), BF16),
        compiler_params=_cparams("arbitrary", "arbitrary"),
        name="hyena_direct",
    )(view(v), view(x1), view(x2), kspec, kspec, bias, tabs["fwd"], tabs["inv"])
    return out.reshape(B * L, C)


HY_DIRECT_MAX_L = 512


_EVEN_PLAN = (
    (0, HY_W, "conv", 1.0, False, F32),
    (HY_W, HY_W, "conv", 1.0, False, F32),
    (2 * HY_W, HY_W, "conv", 1.0, False, F32),
    (3 * HY_W, M_W, "conv_silu", 1.0, True, BF16),
    (3 * HY_W + M_W, M_W, "conv_silu", M_DH ** -0.5, False, BF16),
    (EV_CONV + M_W, M_W, "raw", 1.0, False, F32),
)
_EVEN_NT_PLAN = ((0, M_W, BF16), (M_W, 4 * M_HEADS, F32))
_ODD_NT_PLAN = ((0, 2 * SSD_HEADS, F32),)
_GN = SSD_GROUPS * SSD_STATE
_ODD_PLAN = (
    (0, SSD_INNER, "raw", 1.0, False, F32),
    (SSD_INNER, SSD_INNER, "conv_silu", 1.0, True, F32),
    (2 * SSD_INNER, _GN, "conv_silu", 1.0, False, BF16),
    (2 * SSD_INNER + _GN, _GN, "conv_silu", 1.0, True, BF16),
)


def _even_layer(x, mods_l, P, e, l, B, L, st, per_batch_cond, kspec, tabs):
    hv, hx1, hx2, qt, k, o, vt, gt = _inproj(
        x, mods_l, P["norm1_g"][l:l + 1], P["ev_in_w"], P["ev_in_wt"], P["ev_conv_w"], P["ev_conv_b"], e,
        plan=_EVEN_PLAN, nt_plan=_EVEN_NT_PLAN, conv0=0, seq_len=L, per_batch_cond=per_batch_cond, row_tile=WIDE_TILE)
    hyena = _hyena_direct if L <= HY_DIRECT_MAX_L else _hyena
    y_hy = hyena(hv, hx1, hx2, kspec, P["hy_bias"][e], tabs, B, L)
    c0, m0 = st
    y_m, c_new, m_new = _mlstm(qt, k, vt, o, gt, P["m_gate_b"][e], P["m_norm_g"][e], c0, m0, B, L)
    x = _outproj_even(y_hy, y_m, x, mods_l, P["ev_out_w"], e, seq_len=L, per_batch_cond=per_batch_cond)
    return x, (c_new, m_new)


def _by_group(p):
    return p.reshape(2, SSD_GROUPS, SSD_HPG).transpose(1, 0, 2).reshape(SSD_GROUPS, 2 * SSD_HPG)


def _odd_layer(x, mods_l, P, o, l, B, L, s0, per_batch_cond):
    z, xt, bm, ct, dtt = _inproj(
        x, mods_l, P["norm1_g"][l:l + 1], P["od_in_w"], P["od_in_wt"], P["od_conv_w"], P["od_conv_b"], o,
        plan=_ODD_PLAN, nt_plan=_ODD_NT_PLAN, conv0=SSD_INNER, seq_len=L, per_batch_cond=per_batch_cond,
        row_tile=NARROW_TILE)
    dskip = jnp.repeat(P["ssd_D"][o], SSD_HEADDIM).reshape(SSD_GROUPS, SSD_GW)
    u, s_new = _ssd(xt, bm, ct, dtt, z, _by_group(P["ssd_dt_bias"][o]), _by_group(P["ssd_A_log"][o]), dskip, s0, B, L)
    x = _outproj_odd(u, x, mods_l, P["ssd_norm_g"], P["od_out_w"], o, seq_len=L, per_batch_cond=per_batch_cond)
    return x, s_new


def _to_colmajor(x, B, L):
    rows = L // GRID_W
    return x.reshape(B, rows, GRID_W, D_MODEL).swapaxes(1, 2).reshape(B * L, D_MODEL)


def _from_colmajor(x, B, L):
    rows = L // GRID_W
    return x.reshape(B, GRID_W, rows, D_MODEL).swapaxes(1, 2).reshape(B * L, D_MODEL)


def _trunk(x, mods, P, B, L, init_states, per_batch_cond, grid, kspecs, tabs):
    states = []
    for l in range(DEPTH):
        if grid and l == 2:
            x = _to_colmajor(x, B, L)
        mods_l = mods[l]
        if l % 2 == 0:
            x, st = _even_layer(x, mods_l, P, l // 2, l, B, L, init_states[l], per_batch_cond, kspecs[l // 2], tabs)
        else:
            x, st = _odd_layer(x, mods_l, P, l // 2, l, B, L, init_states[l], per_batch_cond)
        states.append(st)
        x = _ffn(x, mods_l, P["norm2_g"].reshape(DEPTH, 1, D_MODEL), P["ffn_w1"], P["ffn_w3"], P["ffn_w2"],
                 P["final_g"].reshape(1, D_MODEL), l, seq_len=L, per_batch_cond=per_batch_cond,
                 final=(l == DEPTH - 1))
    if grid:
        x = _from_colmajor(x, B, L)
    return x, states


def _pack_mlstm_state(C, n, m):
    B = C.shape[0]
    ext = jnp.concatenate([jnp.swapaxes(C, -1, -2), n[..., None, :],
                           jnp.zeros(C.shape[:-2] + (M_DH - 1, M_DH), F32)], axis=-2)
    m_rep = jnp.broadcast_to(m.reshape(B, 2 * M_HEADS, 1, 1), (B, 2 * M_HEADS, 1, M_DH))
    return ext.reshape(B, 2 * M_HEADS, 2 * M_DH, M_DH), m_rep


def kernel(x_prompt, x_sample, state_mlstm_C, state_mlstm_n, state_mlstm_m, state_ssd, c, c_ctx,
           mod_w, mod_b, norm1_g, norm2_g, ffn_w1, ffn_w3, ffn_w2, final_g,
           ev_in_w, ev_conv_w, ev_conv_b, hy_w1, hy_b1, hy_w2, hy_b2, hy_w3, hy_freq, hy_bias,
           m_gate_b, m_norm_g, ev_out_w,
           od_in_w, od_conv_w, od_conv_b, ssd_dt_bias, ssd_A_log, ssd_D, ssd_norm_g, od_out_w):
    Bp, Lp, _ = x_prompt.shape
    Bs, Ls, _ = x_sample.shape
    bf = lambda w: w.astype(BF16)
    n_odd = od_in_w.shape[0]
    od_dt_w = od_in_w[:, :, OD_PROJ - 2 * SSD_HEADS:].reshape(n_odd, D_MODEL, 2, SSD_GROUPS, SSD_HPG)
    od_dt_w = od_dt_w.transpose(0, 3, 2, 4, 1).reshape(n_odd, 2 * SSD_HEADS, D_MODEL)
    P = dict(norm1_g=norm1_g, norm2_g=norm2_g, ffn_w1=bf(ffn_w1), ffn_w3=bf(ffn_w3), ffn_w2=bf(ffn_w2),
             final_g=final_g, ev_in_w=bf(ev_in_w),
             ev_in_wt=bf(jnp.concatenate([ev_in_w[:, :, EV_CONV:EV_CONV + M_W],
                                          ev_in_w[:, :, EV_PROJ - 4 * M_HEADS:]], axis=2)).swapaxes(1, 2),
             ev_conv_w=ev_conv_w, ev_conv_b=ev_conv_b, hy_bias=hy_bias, m_gate_b=m_gate_b, m_norm_g=m_norm_g,
             ev_out_w=bf(ev_out_w), od_in_w=bf(od_in_w), od_in_wt=bf(od_dt_w),
             od_conv_w=od_conv_w, od_conv_b=od_conv_b, ssd_dt_bias=ssd_dt_bias, ssd_A_log=ssd_A_log,
             ssd_D=ssd_D, ssd_norm_g=ssd_norm_g.reshape(-1, 1, SSD_INNER), od_out_w=bf(od_out_w))

    ncond = 16
    cond = jnp.concatenate([c_ctx[None, :], c, jnp.zeros((ncond - 1 - Bs, D_MODEL), F32)], axis=0)
    mods = _modulation(cond, mod_w, mod_b).reshape(DEPTH, ncond, 6, D_MODEL)

    tabs, kspecs = {}, {}
    for L in (Lp, Ls):
        direct = L <= HY_DIRECT_MAX_L
        tabs[L] = _direct_tables(L) if direct else _dft_tables(L)
        spectrum = _hy_direct_spectrum if direct else _hy_spectrum
        kspecs[L] = [spectrum(L, tabs[L], hy_w1[e], hy_b1[e], hy_w2[e], hy_b2[e], hy_w3[e], hy_freq[e])
                     for e in range((DEPTH + 1) // 2)]

    G = SSD_GROUPS
    zero_m = _pack_mlstm_state(jnp.zeros((Bp, 2, M_HEADS, M_DH, M_DH), F32), jnp.zeros((Bp, 2, M_HEADS, M_DH), F32),
                               jnp.zeros((Bp, 2, M_HEADS), F32))
    zero_s = jnp.zeros((Bp, 2, G, SSD_GW, SSD_STATE), F32)
    ctx_init = [zero_m if l % 2 == 0 else zero_s for l in range(DEPTH)]
    y_prompt, ctx_states = _trunk(x_prompt.reshape(Bp * Lp, D_MODEL), mods, P, Bp, Lp, ctx_init, False, False,
                                  kspecs[Lp], tabs[Lp])

    lat_init = []
    for l in range(DEPTH):
        if l % 2 == 0:
            e = l // 2
            lat_init.append(_pack_mlstm_state(state_mlstm_C[:, e], state_mlstm_n[:, e], state_mlstm_m[:, e]))
        else:
            lat_init.append(state_ssd[:, l // 2].reshape(Bs, 2, G, SSD_GW, SSD_STATE))
    y_sample, _ = _trunk(x_sample.reshape(Bs * Ls, D_MODEL), mods, P, Bs, Ls, lat_init, True, True,
                         kspecs[Ls], tabs[Ls])

    cs = [ctx_states[l][0].reshape(Bp, 2, M_HEADS, 2 * M_DH, M_DH) for l in range(0, DEPTH, 2)]
    ms = [ctx_states[l][1][:, :, 0, 0].reshape(Bp, 2, M_HEADS) for l in range(0, DEPTH, 2)]
    new_C = jnp.stack([jnp.swapaxes(t[..., :M_DH, :], -1, -2) for t in cs], axis=1)
    new_n = jnp.stack([t[..., M_DH, :] for t in cs], axis=1)
    new_m = jnp.stack(ms, axis=1)
    new_ssd = jnp.stack([ctx_states[l].reshape(Bp, 2, SSD_HEADS, SSD_HEADDIM, SSD_STATE)
                         for l in range(1, DEPTH, 2)], axis=1)
    return (y_prompt.reshape(Bp, Lp, D_MODEL), y_sample.reshape(Bs, Ls, D_MODEL), new_C, new_n, new_m, new_ssd)
```

```python
import functools
import math

import jax
import jax.numpy as jnp
from jax import lax
from jax.experimental import pallas as pl
from jax.experimental.pallas import tpu as pltpu

F32 = jnp.float32
BF16 = jnp.bfloat16

D_MODEL = 1024
DEPTH = 4
GRID_W = 64
EPS = 1e-6
HY_W = D_MODEL // 2
HY_ORDER = 2
HY_EMB = 33
HY_BANDS = (HY_EMB - 1) // 2
HY_HID = 64
HY_TARGET = 1e-2
HY_FAST = 0.3
HY_SLOW = 1.5
M_HEADS = 4
M_DH = 128
M_W = M_HEADS * M_DH
EV_CONV = 3 * HY_W + 2 * M_W
EV_PROJ = EV_CONV + 2 * M_W + 4 * M_HEADS
SSD_INNER = 2 * D_MODEL
SSD_HEADDIM = 64
SSD_HEADS = SSD_INNER // SSD_HEADDIM
SSD_GROUPS = 8
SSD_HPG = SSD_HEADS // SSD_GROUPS
SSD_STATE = 128
SSD_GW = SSD_HPG * SSD_HEADDIM
SSD_CONV_DIM = SSD_INNER + 2 * SSD_GROUPS * SSD_STATE
OD_PROJ = SSD_INNER + SSD_CONV_DIM + 2 * SSD_HEADS
FFN_HIDDEN = -(-8 * D_MODEL // (3 * 256)) * 256

V7X_VMEM_BYTES = 64 * 1024 * 1024
VMEM_LIMIT = V7X_VMEM_BYTES * 3 // 4
WIDE_TILE = 512
NARROW_TILE = 256
HALO = 16
COL_CHUNK = 512
HY_NB = 16
SCAN_T = 128
EPILOGUE_UNROLL = 4
SSD_BLOCK_TOKENS = 2048
MLSTM_BLOCK_TOKENS = 1024


def _cparams(*sem):
    return pltpu.CompilerParams(dimension_semantics=sem, vmem_limit_bytes=VMEM_LIMIT)


def _dot(a, b):
    return jnp.dot(a, b, preferred_element_type=F32)


def _dot_nt(a, b):
    return lax.dot_general(a, b, (((1,), (1,)), ((), ())), preferred_element_type=F32)


def _split2(x):
    h = x.astype(BF16)
    return h, (x - h.astype(F32)).astype(BF16)


def _dot3(th, tl, x):
    xh, xl = _split2(x)
    return _dot(th, xh) + _dot(th, xl) + _dot(tl, xh)


def _dot2(x, t):
    xh, xl = _split2(x)
    return _dot(xh, t) + _dot(xl, t)


def _sigmoid(x):
    return 1.0 / (1.0 + jnp.exp(-x))


def _silu(x):
    return x * _sigmoid(x)


def _log_sigmoid(x):
    return jnp.minimum(x, 0.0) - jnp.log(1.0 + jnp.exp(-jnp.abs(x)))


def _softplus(x):
    return jnp.maximum(x, 0.0) + jnp.log(1.0 + jnp.exp(-jnp.abs(x)))


def _rms(x):
    return x * lax.rsqrt(jnp.mean(x * x, axis=-1, keepdims=True) + EPS)


def _mod_kernel(c_ref, w_ref, b_ref, o_ref):
    a = _silu(c_ref[...]).astype(BF16)
    o_ref[0] = _dot(a, w_ref[0].astype(BF16)) + b_ref[0]


def _modulation(cond, mod_w, mod_b):
    nc = cond.shape[0]
    tn = 1536
    return pl.pallas_call(
        _mod_kernel,
        grid=(DEPTH, 6 * D_MODEL // tn),
        in_specs=[pl.BlockSpec((nc, D_MODEL), lambda l, j: (0, 0)),
                  pl.BlockSpec((1, D_MODEL, tn), lambda l, j: (l, 0, j)),
                  pl.BlockSpec((1, 1, tn), lambda l, j: (l, 0, j))],
        out_specs=pl.BlockSpec((1, nc, tn), lambda l, j: (l, 0, j)),
        out_shape=jax.ShapeDtypeStruct((DEPTH, nc, 6 * D_MODEL), F32),
        compiler_params=_cparams("arbitrary", "arbitrary"),
        name="modulation",
    )(cond, mod_w, mod_b.reshape(DEPTH, 1, 6 * D_MODEL))


def _inproj_kernel(x_ref, xp_ref, xn_ref, mod_ref, g_ref, w_ref, wt_ref, cw_ref, cb_ref, *rest,
                   plan, nt_plan, tiles_per_seq, conv0):
    nseg = len(plan)
    outs = rest[:nseg]
    nt_outs = rest[nseg:nseg + len(nt_plan)]
    hbuf = rest[nseg + len(nt_plan)]
    tm = x_ref.shape[0]
    i = pl.program_id(0)
    j = i % tiles_per_seq
    pmask = (j != 0).astype(F32)
    nmask = (j != tiles_per_seq - 1).astype(F32)
    mod = mod_ref[0]
    sh, sc = mod[0:1], mod[1:2]
    g = g_ref[...]

    def normmod(xv):
        return (_rms(xv) * g * (1.0 + sc) + sh).astype(BF16)

    hbuf[0:tm] = normmod(x_ref[...])
    hbuf[tm:tm + HALO] = normmod(xp_ref[...])
    hbuf[tm + HALO:tm + 2 * HALO] = normmod(xn_ref[...])
    rowid = lax.broadcasted_iota(jnp.int32, (tm, COL_CHUNK), 0)

    for (c0, width, kind, scale, transposed), o_ref in zip(plan, outs):
        for cc in range(c0, c0 + width, COL_CHUNK):
            w = w_ref[0, :, cc:cc + COL_CHUNK]
            if kind == "raw":
                y = _dot(hbuf[0:tm], w)
            else:
                full = _dot(hbuf[...], w)
                acc = full[0:tm]
                ap = full[tm + HALO - 1:tm + HALO] * pmask
                an = full[tm + HALO:tm + HALO + 1] * nmask
                prev = jnp.where(rowid == 0, ap, pltpu.roll(acc, 1, 0))
                nxt = jnp.where(rowid == tm - 1, an, pltpu.roll(acc, tm - 1, 0))
                cw = cw_ref[0, :, cc - conv0:cc - conv0 + COL_CHUNK]
                cb = cb_ref[0, :, cc - conv0:cc - conv0 + COL_CHUNK]
                y = cb + cw[0:1] * prev + cw[1:2] * acc + cw[2:3] * nxt
                if kind == "conv_silu":
                    y = _silu(y)
                if scale != 1.0:
                    y = y * scale
            if transposed:
                o_ref[cc - c0:cc - c0 + COL_CHUNK, :] = y.T.astype(o_ref.dtype)
            else:
                o_ref[:, cc - c0:cc - c0 + COL_CHUNK] = y.astype(o_ref.dtype)

    for (r0, rows), o_ref in zip(nt_plan, nt_outs):
        o_ref[...] = _dot_nt(wt_ref[0, r0:r0 + rows, :], hbuf[0:tm]).astype(o_ref.dtype)


def _inproj(x, mods_l, g_l, w_all, wt_all, cw_all, cb_all, layer_idx, *, plan, nt_plan, conv0, seq_len,
            per_batch_cond, row_tile):
    M = x.shape[0]
    tm = min(row_tile, seq_len)
    tps = seq_len // tm
    nt = M // tm
    hb = tm // HALO
    ntot = w_all.shape[2]
    tw = wt_all.shape[1]
    cdim = cw_all.shape[2]
    e = layer_idx
    cond_map = (lambda i: (1 + i // tps, 0, 0)) if per_batch_cond else (lambda i: (0, 0, 0))
    in_specs = [
        pl.BlockSpec((tm, D_MODEL), lambda i: (i, 0)),
        pl.BlockSpec((HALO, D_MODEL), lambda i: (jnp.maximum(i * hb - 1, 0), 0)),
        pl.BlockSpec((HALO, D_MODEL), lambda i: (jnp.minimum((i + 1) * hb, nt * hb - 1), 0)),
        pl.BlockSpec((1, 6, D_MODEL), cond_map),
        pl.BlockSpec((1, D_MODEL), lambda i: (0, 0)),
        pl.BlockSpec((1, D_MODEL, ntot), lambda i: (e, 0, 0), pipeline_mode=pl.Buffered(1)),
        pl.BlockSpec((1, tw, D_MODEL), lambda i: (e, 0, 0)),
        pl.BlockSpec((1, 3, cdim), lambda i: (e, 0, 0)),
        pl.BlockSpec((1, 1, cdim), lambda i: (e, 0, 0)),
    ]
    out_specs, out_shape = [], []
    for (c0, width, kind, scale, transposed, dt) in plan:
        if transposed:
            out_specs.append(pl.BlockSpec((width, tm), lambda i: (0, i)))
            out_shape.append(jax.ShapeDtypeStruct((width, M), dt))
        else:
            out_specs.append(pl.BlockSpec((tm, width), lambda i: (i, 0)))
            out_shape.append(jax.ShapeDtypeStruct((M, width), dt))
    for (r0, rows, dt) in nt_plan:
        out_specs.append(pl.BlockSpec((rows, tm), lambda i: (0, i)))
        out_shape.append(jax.ShapeDtypeStruct((rows, M), dt))
    kplan = tuple(p[:5] for p in plan)
    knt = tuple(p[:2] for p in nt_plan)
    return pl.pallas_call(
        functools.partial(_inproj_kernel, plan=kplan, nt_plan=knt, tiles_per_seq=tps, conv0=conv0),
        grid=(nt,),
        in_specs=in_specs,
        out_specs=out_specs,
        out_shape=out_shape,
        scratch_shapes=[pltpu.VMEM((tm + 2 * HALO, D_MODEL), BF16)],
        compiler_params=_cparams("arbitrary"),
        name="inproj",
    )(x, x, x, mods_l, g_l, w_all, wt_all, cw_all, cb_all.reshape(cb_all.shape[0], 1, cdim))


def _outproj_even_kernel(yh_ref, ym_ref, x_ref, mod_ref, w_ref, o_ref):
    out = _dot(yh_ref[...], w_ref[0, 0:HY_W]) + _dot(ym_ref[...], w_ref[0, HY_W:])
    o_ref[...] = x_ref[...] + mod_ref[0][2:3] * out


def _outproj_odd_kernel(u_ref, x_ref, mod_ref, ng_ref, w_ref, o_ref):
    un = (_rms(u_ref[...]) * ng_ref[0]).astype(BF16)
    o_ref[...] = x_ref[...] + mod_ref[0][2:3] * _dot(un, w_ref[0])


def _cond_map(seq_len, per_batch_cond):
    tps = seq_len // WIDE_TILE
    return (lambda i: (1 + i // tps, 0, 0)) if per_batch_cond else (lambda i: (0, 0, 0))


def _outproj_even(yh, ym, x, mods_l, w_all, e, *, seq_len, per_batch_cond):
    M = x.shape[0]
    tm = WIDE_TILE
    return pl.pallas_call(
        _outproj_even_kernel,
        grid=(M // tm,),
        in_specs=[pl.BlockSpec((tm, HY_W), lambda i: (i, 0)),
                  pl.BlockSpec((tm, M_W), lambda i: (i, 0)),
                  pl.BlockSpec((tm, D_MODEL), lambda i: (i, 0)),
                  pl.BlockSpec((1, 6, D_MODEL), _cond_map(seq_len, per_batch_cond)),
                  pl.BlockSpec((1, HY_W + M_W, D_MODEL), lambda i: (e, 0, 0), pipeline_mode=pl.Buffered(1))],
        out_specs=pl.BlockSpec((tm, D_MODEL), lambda i: (i, 0)),
        out_shape=jax.ShapeDtypeStruct((M, D_MODEL), F32),
        compiler_params=_cparams("arbitrary"),
        name="outproj_even",
    )(yh, ym, x, mods_l, w_all)


def _outproj_odd(u, x, mods_l, ng_all, w_all, o, *, seq_len, per_batch_cond):
    M = x.shape[0]
    tm = WIDE_TILE
    return pl.pallas_call(
        _outproj_odd_kernel,
        grid=(M // tm,),
        in_specs=[pl.BlockSpec((tm, SSD_INNER), lambda i: (i, 0)),
                  pl.BlockSpec((tm, D_MODEL), lambda i: (i, 0)),
                  pl.BlockSpec((1, 6, D_MODEL), _cond_map(seq_len, per_batch_cond)),
                  pl.BlockSpec((1, 1, SSD_INNER), lambda i: (o, 0, 0)),
                  pl.BlockSpec((1, SSD_INNER, D_MODEL), lambda i: (o, 0, 0), pipeline_mode=pl.Buffered(1))],
        out_specs=pl.BlockSpec((tm, D_MODEL), lambda i: (i, 0)),
        out_shape=jax.ShapeDtypeStruct((M, D_MODEL), F32),
        compiler_params=_cparams("arbitrary"),
        name="outproj_odd",
    )(u, x, mods_l, ng_all, w_all)


FFN_CHUNK = FFN_HIDDEN // 2


def _ffn_kernel(x_ref, mod_ref, g_ref, w1_ref, w3_ref, w2_ref, fg_ref, o_ref, *, final):
    x = x_ref[...]
    mod = mod_ref[0]
    h = (_rms(x) * g_ref[0] * (1.0 + mod[4:5]) + mod[3:4]).astype(BF16)
    ff = jnp.zeros(x.shape, F32)
    for c0 in range(0, FFN_HIDDEN, FFN_CHUNK):
        a = _dot(h, w1_ref[0, :, c0:c0 + FFN_CHUNK])
        b = _dot(h, w3_ref[0, :, c0:c0 + FFN_CHUNK])
        u = (_silu(a) * b).astype(BF16)
        ff = ff + _dot(u, w2_ref[0, c0:c0 + FFN_CHUNK, :])
    xo = x + mod[5:6] * ff
    if final:
        xo = _rms(xo) * fg_ref[...]
    o_ref[...] = xo


def _ffn(x, mods_l, g_all, w1, w3, w2, fg, l, *, seq_len, per_batch_cond, final):
    M = x.shape[0]
    tm = WIDE_TILE
    wspec = lambda shape: pl.BlockSpec(shape, lambda i: (l, 0, 0), pipeline_mode=pl.Buffered(1))
    return pl.pallas_call(
        functools.partial(_ffn_kernel, final=final),
        grid=(M // tm,),
        in_specs=[pl.BlockSpec((tm, D_MODEL), lambda i: (i, 0)),
                  pl.BlockSpec((1, 6, D_MODEL), _cond_map(seq_len, per_batch_cond)),
                  pl.BlockSpec((1, 1, D_MODEL), lambda i: (l, 0, 0)),
                  wspec((1, D_MODEL, FFN_HIDDEN)), wspec((1, D_MODEL, FFN_HIDDEN)),
                  wspec((1, FFN_HIDDEN, D_MODEL)),
                  pl.BlockSpec((1, D_MODEL), lambda i: (0, 0))],
        out_specs=pl.BlockSpec((tm, D_MODEL), lambda i: (i, 0)),
        out_shape=jax.ShapeDtypeStruct((M, D_MODEL), F32),
        compiler_params=_cparams("arbitrary"),
        name="ffn",
    )(x, mods_l, g_all, w1, w3, w2, fg)


def _scan_consts(d):
    T = SCAN_T
    row = lax.broadcasted_iota(jnp.int32, (T, T), 0)
    col = lax.broadcasted_iota(jnp.int32, (T, T), 1)
    mask, last = (row <= col, T - 1) if d == 0 else (row >= col, 0)
    v_mat = jnp.where(mask, 1.0, 0.0).astype(BF16)
    return mask, v_mat, last


def _mlstm_kernel(qt_ref, k_ref, vt_ref, o_ref, gt_ref, gb_ref, ng_ref, c0_ref, m0_ref,
                  y_ref, cst_ref, mst_ref, h_scr, c_scr, m_scr, sb_scr, vw_scr, rows_scr, *, nc, ncs):
    T = SCAN_T
    H = M_HEADS
    nseq = nc // ncs
    ones_blk = jnp.where(lax.broadcasted_iota(jnp.int32, (M_DH, T), 0) == 0, 1.0, 0.0)
    consts = (_scan_consts(0), _scan_consts(1))
    gb = gb_ref[...]

    def gates(ci, d, slot):
        mask, v_mat, last = consts[d]
        c0 = pl.multiple_of(ci * T, T)
        lanes = pl.ds(c0, T)
        graw = gt_ref[:, lanes] + gb
        lf = _log_sigmoid(graw)
        b_all = _dot2(lf, v_mat)
        for h in range(H):
            fi = (2 * d + 1) * H + h
            ii = (2 * d) * H + h
            hl = slice(h * M_DH, (h + 1) * M_DH)
            tr = slice(h * T, (h + 1) * T)
            b_row, i_row = b_all[fi:fi + 1], graw[ii:ii + 1]
            dlog = jnp.where(mask, jnp.broadcast_to(b_row, (T, T)) - jnp.broadcast_to(b_row - i_row, (T, T)).T,
                             -jnp.inf)
            mloc = jnp.max(dlog, axis=0, keepdims=True)
            sb_scr[slot, d, tr, :] = (_dot(k_ref[pl.ds(c0, T), hl], qt_ref[hl, lanes])
                                      * jnp.exp(dlog - mloc)).astype(BF16)
            mloc_last = jnp.broadcast_to(mloc[:, last:last + 1], (1, T))
            b_last = jnp.broadcast_to(b_row[:, last:last + 1], (1, T))
            w_end = jnp.exp(b_last - b_row + i_row - mloc_last)
            vw_scr[slot, d, h, 0:M_DH, :] = (vt_ref[hl, lanes].astype(F32) * w_end).astype(BF16)
            vw_scr[slot, d, h, M_DH:, :] = (ones_blk * w_end).astype(BF16)
            rows_scr[slot, d, 4 * h:4 * h + 4, :] = jnp.concatenate([b_row, mloc, mloc_last, b_last], axis=0)

    def update(ci, d, slot, accumulate):
        c0 = pl.multiple_of(ci * T, T)
        lanes = pl.ds(c0, T)
        for h in range(H):
            sidx = d * H + h
            hl = slice(h * M_DH, (h + 1) * M_DH)
            tr = slice(h * T, (h + 1) * T)
            rows = rows_scr[slot, d, 4 * h:4 * h + 4, :]
            b_row, mloc, mloc_last, b_last = rows[0:1], rows[1:2], rows[2:3], rows[3:4]
            m_row = m_scr[sidx]
            inter = b_row + m_row
            mt = jnp.maximum(inter, mloc)
            kh = k_ref[pl.ds(c0, T), hl]
            qt = qt_ref[hl, lanes]
            vext = jnp.concatenate([vt_ref[hl, lanes], ones_blk.astype(BF16)], axis=0)
            ct = c_scr[sidx]
            tot = (jnp.exp(mloc - mt) * _dot(vext, sb_scr[slot, d, tr, :])
                   + jnp.exp(inter - mt) * _dot(ct.astype(BF16), qt))
            den = tot[M_DH:M_DH + 1]
            ht = tot[0:M_DH] / jnp.maximum(jnp.abs(den), jnp.exp(-mt))
            if accumulate:
                h_scr[hl, lanes] = h_scr[hl, lanes] + ht
            else:
                h_scr[hl, lanes] = ht
            m_new = jnp.maximum(b_last + m_row, mloc_last)
            c_scr[sidx] = (jnp.exp(b_last + m_row - m_new) * ct
                           + jnp.exp(mloc_last - m_new) * _dot(vw_scr[slot, d, h], kh))
            m_scr[sidx] = m_new

    def chunks_at(j):
        if nseq == 1:
            return j, nc - 1 - j
        seq, loc = divmod(j, ncs)
        return j, seq * ncs + ncs - 1 - loc

    def load_state(seq):
        c_scr[...] = c0_ref[seq]
        m_scr[...] = m0_ref[seq]

    def store_state(seq):
        cst_ref[seq] = c_scr[...]
        mst_ref[seq] = m_scr[...]

    def pair(p, slots, accumulate=None):
        s0, s1 = slots
        for j, slot in ((2 * p, s0), (2 * p + 1, s1)):
            if nseq > 1 and j % ncs == 0:
                load_state(j // ncs)
            acc = (j % ncs >= ncs // 2) if accumulate is None else accumulate
            cf, cb = chunks_at(j)
            update(cf, 0, slot, acc)
            update(cb, 1, slot, acc)
            if nseq > 1 and j % ncs == ncs - 1:
                store_state(j // ncs)
        for j, slot in ((2 * p + 2, 2 - s0), (2 * p + 3, 4 - s1)):
            if nseq == 1:
                gates(jnp.minimum(j, nc - 1), 0, slot)
                gates(jnp.maximum(nc - 1 - j, 0), 1, slot)
            elif j < nc:
                cf, cb = chunks_at(j)
                gates(cf, 0, slot)
                gates(cb, 1, slot)

    for j in (0, 1):
        cf, cb = chunks_at(j)
        gates(cf, 0, j)
        gates(cb, 1, j)
    if nseq == 1:
        load_state(0)
    if nseq == 1 and nc % 8 == 0:
        def body(accumulate):
            def run(i, carry):
                pair(2 * i, (0, 1), accumulate)
                pair(2 * i + 1, (2, 3), accumulate)
                return carry
            return run
        lax.fori_loop(0, nc // 8, body(False), 0)
        lax.fori_loop(nc // 8, nc // 4, body(True), 0)
    else:
        for p in range(nc // 2):
            pair(p, ((0, 1), (2, 3))[p % 2])
    if nseq == 1:
        store_state(0)

    def epilogue(i, carry):
        c0 = pl.multiple_of(i * T, T)
        for h in range(H):
            hl = slice(h * M_DH, (h + 1) * M_DH)
            hh = h_scr[hl, pl.ds(c0, T)]
            hn = hh * lax.rsqrt(jnp.mean(hh * hh, axis=0, keepdims=True) + EPS)
            y = hn.T * ng_ref[:, hl] * _sigmoid(o_ref[pl.ds(c0, T), hl])
            y_ref[pl.ds(c0, T), hl] = y.astype(y_ref.dtype)
        return carry

    lax.fori_loop(0, nc, epilogue, 0, unroll=EPILOGUE_UNROLL)


def _mlstm(qt, k, vt, o, gt, gate_b, norm_g, c0, m0, B, L):
    assert L % (2 * SCAN_T) == 0 and M_DH == SCAN_T
    bb = max(1, min(B, MLSTM_BLOCK_TOKENS // L))
    assert B % bb == 0
    ncs = L // SCAN_T
    nc = bb * ncs
    Lb = bb * L
    ng = 4 * M_HEADS
    M = B * L
    one = pl.Buffered(1)
    st_spec = pl.BlockSpec((bb, 2 * M_HEADS, 2 * M_DH, M_DH), lambda b: (b, 0, 0, 0))
    m_spec = pl.BlockSpec((bb, 2 * M_HEADS, 1, M_DH), lambda b: (b, 0, 0, 0))
    return pl.pallas_call(
        functools.partial(_mlstm_kernel, nc=nc, ncs=ncs),
        grid=(B // bb,),
        in_specs=[pl.BlockSpec((M_W, Lb), lambda b: (0, b), pipeline_mode=one),
                  pl.BlockSpec((Lb, M_W), lambda b: (b, 0), pipeline_mode=one),
                  pl.BlockSpec((M_W, Lb), lambda b: (0, b), pipeline_mode=one),
                  pl.BlockSpec((Lb, M_W), lambda b: (b, 0), pipeline_mode=one),
                  pl.BlockSpec((ng, Lb), lambda b: (0, b)),
                  pl.BlockSpec((ng, SCAN_T), lambda b: (0, 0)),
                  pl.BlockSpec((1, M_W), lambda b: (0, 0)),
                  st_spec, m_spec],
        out_specs=[pl.BlockSpec((Lb, M_W), lambda b: (b, 0)), st_spec, m_spec],
        out_shape=[jax.ShapeDtypeStruct((M, M_W), BF16),
                   jax.ShapeDtypeStruct((B, 2 * M_HEADS, 2 * M_DH, M_DH), F32),
                   jax.ShapeDtypeStruct((B, 2 * M_HEADS, 1, M_DH), F32)],
        scratch_shapes=[pltpu.VMEM((M_W, Lb), F32),
                        pltpu.VMEM((2 * M_HEADS, 2 * M_DH, M_DH), F32),
                        pltpu.VMEM((2 * M_HEADS, 1, M_DH), F32),
                        pltpu.VMEM((4, 2, M_HEADS * SCAN_T, SCAN_T), BF16),
                        pltpu.VMEM((4, 2, M_HEADS, 2 * M_DH, SCAN_T), BF16),
                        pltpu.VMEM((4, 2, 4 * M_HEADS, SCAN_T), F32)],
        compiler_params=_cparams("arbitrary"),
        name="mlstm",
    )(qt, k, vt, o, gt, jnp.broadcast_to(gate_b.reshape(ng, 1), (ng, SCAN_T)), norm_g.reshape(1, M_W), c0, m0)


def _ssd_kernel(xt_ref, b_ref, ct_ref, dtt_ref, z_ref, db_ref, al_ref, dsk_ref, s0_ref,
                y_ref, sout_ref, yf_scr, yb_scr, s_scr, cbt_scr, p_scr, xd_scr, xdec_scr, rows_scr, *, nc, ncs):
    T = SCAN_T
    H = SSD_HPG
    nseq = nc // ncs
    consts = (_scan_consts(0), _scan_consts(1))
    db = db_ref[0]
    a_neg = -jnp.exp(al_ref[0])

    def gates(ci, d, slot):
        mask, v_mat, last = consts[d]
        c0 = pl.multiple_of(ci * T, T)
        lanes = pl.ds(c0, T)
        hr = slice(d * H, (d + 1) * H)
        dt = _softplus(dtt_ref[hr, lanes] + db[hr])
        a = dt * a_neg[hr]
        acs = _dot2(a, v_mat)
        a_last = jnp.broadcast_to(acs[:, last:last + 1], (H, T))
        rows_scr[slot, d, 0:H, :] = jnp.exp(acs)
        rows_scr[slot, d, H:2 * H, :] = jnp.exp(a_last)
        cbt_scr[d] = _dot(b_ref[pl.ds(c0, T), :], ct_ref[:, lanes])
        for r in range(H):
            tr = slice(r * T, (r + 1) * T)
            pr = slice(r * SSD_HEADDIM, (r + 1) * SSD_HEADDIM)
            al = jnp.broadcast_to(acs[r:r + 1], (T, T))
            lm = jnp.exp(jnp.where(mask, al - al.T, -jnp.inf))
            p_scr[slot, d, tr, :] = (cbt_scr[d] * lm).astype(BF16)
            xd = xt_ref[pr, lanes] * dt[r:r + 1]
            xd_scr[slot, d, pr, :] = xd.astype(BF16)
            xdec_scr[slot, d, pr, :] = (xd * jnp.exp(a_last[r:r + 1] - acs[r:r + 1])).astype(BF16)

    def update(ci, d, slot):
        c0 = pl.multiple_of(ci * T, T)
        lanes = pl.ds(c0, T)
        rows = rows_scr[slot, d]
        sc = _dot(s_scr[d].astype(BF16), ct_ref[:, lanes])
        upd = _dot(xdec_scr[slot, d], b_ref[pl.ds(c0, T), :])
        y_scr = yf_scr if d == 0 else yb_scr
        for r in range(H):
            pr = slice(r * SSD_HEADDIM, (r + 1) * SSD_HEADDIM)
            tr = slice(r * T, (r + 1) * T)
            y_scr[pr, lanes] = _dot(xd_scr[slot, d, pr, :], p_scr[slot, d, tr, :]) + rows[r:r + 1] * sc[pr]
            s_scr[d, pr, :] = rows[H + r:H + r + 1] * s_scr[d, pr, :] + upd[pr]

    def chunks_at(j):
        if nseq == 1:
            return j, nc - 1 - j
        seq, loc = divmod(j, ncs)
        return j, seq * ncs + ncs - 1 - loc

    def pair(p, slots):
        s0, s1 = slots
        for j, slot in ((2 * p, s0), (2 * p + 1, s1)):
            if nseq > 1 and j % ncs == 0:
                s_scr[...] = s0_ref[j // ncs, :, 0]
            cf, cb = chunks_at(j)
            update(cf, 0, slot)
            update(cb, 1, slot)
            if nseq > 1 and j % ncs == ncs - 1:
                sout_ref[j // ncs, :, 0] = s_scr[...]
        for j, slot in ((2 * p + 2, 2 - s0), (2 * p + 3, 4 - s1)):
            if nseq == 1:
                gates(jnp.minimum(j, nc - 1), 0, slot)
                gates(jnp.maximum(nc - 1 - j, 0), 1, slot)
            elif j < nc:
                cf, cb = chunks_at(j)
                gates(cf, 0, slot)
                gates(cb, 1, slot)

    for j in (0, 1):
        cf, cb = chunks_at(j)
        gates(cf, 0, j)
        gates(cb, 1, j)
    if nseq == 1:
        s_scr[...] = s0_ref[0, :, 0]
    if nseq == 1 and nc % 4 == 0:
        def body(i, carry):
            pair(2 * i, (0, 1))
            pair(2 * i + 1, (2, 3))
            return carry
        lax.fori_loop(0, nc // 4, body, 0)
    else:
        for p in range(nc // 2):
            pair(p, ((0, 1), (2, 3))[p % 2])
    if nseq == 1:
        sout_ref[0, :, 0] = s_scr[...]

    dsk = dsk_ref[0]

    def epilogue(i, carry):
        c0 = pl.multiple_of(i * T, T)
        lanes = pl.ds(c0, T)
        yt = yf_scr[:, lanes] + yb_scr[:, lanes] + dsk * xt_ref[:, lanes]
        y_ref[pl.ds(c0, T), :] = yt.T * _silu(z_ref[pl.ds(c0, T), :])
        return carry

    lax.fori_loop(0, nc, epilogue, 0, unroll=EPILOGUE_UNROLL)


def _ssd(xt, bm, ct, dtt, z, dt_bias, a_log, dskip, s0, B, L):
    assert L % (2 * SCAN_T) == 0 and SSD_STATE == SCAN_T
    bb = max(1, min(B, SSD_BLOCK_TOKENS // L))
    assert B % bb == 0
    ncs = L // SCAN_T
    nc = bb * ncs
    Lb = bb * L
    G = SSD_GROUPS
    nd = 2 * SSD_HPG
    M = B * L
    rep = lambda p, rows: jnp.broadcast_to(p.reshape(G, rows, 1), (G, rows, SCAN_T))
    small = lambda rows: pl.BlockSpec((1, rows, SCAN_T), lambda b, g: (g, 0, 0))
    st_spec = pl.BlockSpec((bb, 2, 1, SSD_GW, SSD_STATE), lambda b, g: (b, 0, g, 0, 0))
    return pl.pallas_call(
        functools.partial(_ssd_kernel, nc=nc, ncs=ncs),
        grid=(B // bb, G),
        in_specs=[pl.BlockSpec((SSD_GW, Lb), lambda b, g: (g, b)),
                  pl.BlockSpec((Lb, SSD_STATE), lambda b, g: (b, g)),
                  pl.BlockSpec((SSD_STATE, Lb), lambda b, g: (g, b)),
                  pl.BlockSpec((nd, Lb), lambda b, g: (g, b)),
                  pl.BlockSpec((Lb, SSD_GW), lambda b, g: (b, g)),
                  small(nd), small(nd), small(SSD_GW), st_spec],
        out_specs=[pl.BlockSpec((Lb, SSD_GW), lambda b, g: (b, g)), st_spec],
        out_shape=[jax.ShapeDtypeStruct((M, SSD_INNER), F32),
                   jax.ShapeDtypeStruct((B, 2, G, SSD_GW, SSD_STATE), F32)],
        scratch_shapes=[pltpu.VMEM((SSD_GW, Lb), F32), pltpu.VMEM((SSD_GW, Lb), F32),
                        pltpu.VMEM((2, SSD_GW, SSD_STATE), F32),
                        pltpu.VMEM((2, SCAN_T, SCAN_T), F32),
                        pltpu.VMEM((4, 2, SSD_HPG * SCAN_T, SCAN_T), BF16),
                        pltpu.VMEM((4, 2, SSD_GW, SCAN_T), BF16),
                        pltpu.VMEM((4, 2, SSD_GW, SCAN_T), BF16),
                        pltpu.VMEM((4, 2, 2 * SSD_HPG, SCAN_T), F32)],
        compiler_params=_cparams("arbitrary", "arbitrary"),
        name="ssd",
    )(xt, bm, ct, dtt, z, rep(dt_bias, nd), rep(a_log, nd), rep(dskip, SSD_GW), s0)


def _hy_factors(L):
    return (128, 64) if L == 4096 else (32, (2 * L) // 32)


def _hilo(t):
    return _split2(t.astype(F32))


def _cblock(cr, ci):
    return jnp.concatenate([jnp.concatenate([cr, -ci], axis=-1), jnp.concatenate([ci, cr], axis=-1)], axis=-2)


def _dft_tables(L):
    n1c, n2c = _hy_factors(L)
    N = n1c * n2c
    h1 = n1c // 2
    i32 = jnp.int32
    k1 = jnp.arange(n1c, dtype=i32)
    n2 = jnp.arange(n2c, dtype=i32)

    def stage1(n1_count):
        n1 = jnp.arange(n1_count, dtype=i32)
        idx = (k1[None, :, None] * (n2[:, None, None] + n2c * n1[None, None, :])) % N
        ang = idx.astype(F32) * (2.0 * math.pi / N)
        return jnp.cos(ang), jnp.sin(ang)

    c, s = stage1(h1)
    g_fwd = _cblock(c, -s)
    g_inv = _cblock(jnp.swapaxes(c, 1, 2), jnp.swapaxes(s, 1, 2)) / n1c
    cf, sf = stage1(n1c)
    g_real = jnp.concatenate([cf, -sf], axis=1)
    idx2 = (n2[:, None] * n2[None, :]) % n2c
    ang2 = idx2.astype(F32) * (2.0 * math.pi / n2c)
    c2, s2 = jnp.cos(ang2), jnp.sin(ang2)
    f2 = _cblock(c2, -s2)
    f2_inv = _cblock(c2, s2) / n2c
    return dict(g_fwd=g_fwd.astype(BF16), g_inv=g_inv.astype(BF16), f2=f2.astype(BF16),
                f2_inv=f2_inv.astype(BF16), g_real=_hilo(g_real), f2_exact=_hilo(f2))


def _swap01(x):
    return jnp.swapaxes(x, 0, 1)


def _hy_s1_kernel(x_ref, *rest, nb, exact):
    if exact:
        gh_ref, gl_ref, o_ref = rest
    else:
        g_ref, o_ref = rest
    pb, rows, _, C = x_ref.shape
    xs = _swap01(x_ref[...].reshape(pb * rows, nb, C))
    if exact:
        res = [_dot3(gh_ref[j], gl_ref[j], xs[j]) for j in range(nb)]
    else:
        res = [_dot(g_ref[j], xs[j].astype(BF16)) for j in range(nb)]
    out = _swap01(jnp.stack(res, axis=0).astype(o_ref.dtype))
    o_ref[...] = out.reshape(o_ref.shape)


def _hy_stage1(x, tab, out_dtype):
    exact = isinstance(tab, tuple)
    tabs = tab if exact else (tab,)
    n2c, rows_out, R = tabs[0].shape
    Bx, rows, _, C = x.shape
    pb = R // rows
    nb = HY_NB
    cb = min(C, HY_W)
    tspec = pl.BlockSpec((nb, rows_out, R), lambda p, t, c: (t, 0, 0))
    return pl.pallas_call(
        functools.partial(_hy_s1_kernel, nb=nb, exact=exact),
        grid=(Bx // pb, n2c // nb, C // cb),
        in_specs=[pl.BlockSpec((pb, rows, nb, cb), lambda p, t, c: (p, 0, t, c))] + [tspec] * len(tabs),
        out_specs=pl.BlockSpec((None, 2, rows_out // 2, nb, cb), lambda p, t, c: (p, 0, 0, t, c)),
        out_shape=jax.ShapeDtypeStruct((Bx // pb, 2, rows_out // 2, n2c, C), out_dtype),
        compiler_params=_cparams("arbitrary", "arbitrary", "arbitrary"),
        name="hyena_stage1",
    )(x, *tabs)


def _hy_spec_kernel(a_ref, inv_ref, fh_ref, fl_ref, o_ref, *, kb):
    n2c = a_ref.shape[2]
    for kk in range(kb):
        rhs = jnp.concatenate([a_ref[0, kk], a_ref[1, kk]], axis=0)
        X = _dot3(fh_ref[...], fl_ref[...], rhs) * inv_ref[...]
        o_ref[0, kk] = X[0:n2c]
        o_ref[1, kk] = X[n2c:]


def _hy_filter_spectrum(a, inv_norm, f2, kb):
    _, n1c, n2c, W = a.shape
    fh, fl = f2
    blk = pl.BlockSpec((2, kb, n2c, W), lambda t: (0, t, 0, 0))
    fspec = pl.BlockSpec((2 * n2c, 2 * n2c), lambda t: (0, 0))
    return pl.pallas_call(
        functools.partial(_hy_spec_kernel, kb=kb),
        grid=(n1c // kb,),
        in_specs=[blk, pl.BlockSpec((1, W), lambda t: (0, 0)), fspec, fspec],
        out_specs=blk,
        out_shape=jax.ShapeDtypeStruct(a.shape, F32),
        compiler_params=_cparams("arbitrary"),
        name="hyena_filter_spectrum",
    )(a, inv_norm, fh, fl)


def _hy_mid_kernel(a_ref, k_ref, f_ref, i_ref, o_ref, *, kb):
    n2c = a_ref.shape[2]
    res = []
    for kk in range(kb):
        rhs = jnp.concatenate([a_ref[0, kk], a_ref[1, kk]], axis=0)
        X = _dot(f_ref[...], rhs)
        xr, xi = X[0:n2c], X[n2c:]
        kr, ki = k_ref[0, kk], k_ref[1, kk]
        Y = jnp.concatenate([xr * kr - xi * ki, xr * ki + xi * kr], axis=0)
        res.append(_dot(i_ref[...], Y.astype(BF16)))
    o_ref[...] = _swap01(jnp.stack(res, axis=0).astype(o_ref.dtype)).reshape(o_ref.shape)


def _hy_mid(a, kspec, order, f2, f2_inv):
    P, _, n1c, n2c, C = a.shape
    kb = HY_NB
    fspec = pl.BlockSpec((2 * n2c, 2 * n2c), lambda t, p: (0, 0))
    return pl.pallas_call(
        functools.partial(_hy_mid_kernel, kb=kb),
        grid=(n1c // kb, P),
        in_specs=[pl.BlockSpec((None, 2, kb, n2c, C), lambda t, p: (p, 0, t, 0, 0)),
                  pl.BlockSpec((2, kb, n2c, C), lambda t, p: (0, t, 0, order)),
                  fspec, fspec],
        out_specs=pl.BlockSpec((None, 2, n2c, kb, C), lambda t, p: (p, 0, 0, t, 0)),
        out_shape=jax.ShapeDtypeStruct((P, 2, n2c, n1c, C), BF16),
        compiler_params=_cparams("arbitrary", "arbitrary"),
        name="hyena_mid",
    )(a, kspec, f2, f2_inv)


def _hy_last_kernel(b_ref, z_ref, gate_ref, bias_ref, gi_ref, *rest, fuse, nb):
    if fuse:
        gf_ref, zo_ref, ao_ref = rest
    else:
        (zo_ref,) = rest
    _, h1, _, C = gate_ref.shape
    gs = _swap01(gate_ref[...].reshape(2 * h1, nb, C))
    zs = _swap01(z_ref[...].reshape(2 * h1, nb, C)) if fuse else z_ref[...]
    bias = bias_ref[...]
    zn, an = [], []
    for j in range(nb):
        bv = jnp.concatenate([b_ref[0, j], b_ref[1, j]], axis=0)
        y = _dot(gi_ref[j], bv)
        znew = gs[j] * (y + zs[j] * bias)
        zn.append(znew)
        if fuse:
            an.append(_dot(gf_ref[j], znew.astype(BF16)))
    if fuse:
        zo_ref[...] = jnp.stack(zn, axis=0)
        ao_ref[...] = _swap01(jnp.stack(an, axis=0).astype(ao_ref.dtype)).reshape(ao_ref.shape)
    else:
        zo_ref[...] = _swap01(jnp.stack(zn, axis=0).astype(zo_ref.dtype)).reshape(zo_ref.shape)


def _hy_last(bf, z, gate, bias, g_inv, g_fwd, out_dtype):
    P, _, n2c, n1c, C = bf.shape
    B, h1, _, _ = gate.shape
    nb = HY_NB
    cb = C // 2
    fuse = g_fwd is not None
    nat = pl.BlockSpec((2, h1, nb, cb), lambda p, t, c: (p, 0, t, c))
    swp = pl.BlockSpec((None, nb, 2 * h1, cb), lambda p, t, c: (p, t, 0, c))
    in_specs = [pl.BlockSpec((None, 2, nb, n1c, cb), lambda p, t, c: (p, 0, t, 0, c)),
                nat if fuse else swp, nat, pl.BlockSpec((1, cb), lambda p, t, c: (0, c)),
                pl.BlockSpec((nb, 2 * h1, 2 * n1c), lambda p, t, c: (t, 0, 0))]
    args = [bf, z, gate, bias, g_inv]
    if fuse:
        in_specs.append(pl.BlockSpec((nb, 2 * n1c, 2 * h1), lambda p, t, c: (t, 0, 0)))
        args.append(g_fwd)
        out_specs = [swp, pl.BlockSpec((None, 2, n1c, nb, cb), lambda p, t, c: (p, 0, 0, t, c))]
        out_shape = [jax.ShapeDtypeStruct((P, n2c, 2 * h1, C), F32),
                     jax.ShapeDtypeStruct((P, 2, n1c, n2c, C), BF16)]
    else:
        out_specs = [nat]
        out_shape = [jax.ShapeDtypeStruct(gate.shape, out_dtype)]
    return pl.pallas_call(
        functools.partial(_hy_last_kernel, fuse=fuse, nb=nb),
        grid=(P, n2c // nb, C // cb),
        in_specs=in_specs, out_specs=out_specs, out_shape=out_shape,
        compiler_params=_cparams("arbitrary", "arbitrary", "arbitrary"),
        name="hyena_last",
    )(*args)


def _hy_filter_kernel(f_ref, w1_ref, b1_ref, w2_ref, b2_ref, w3_ref, fr_ref, dl_ref, hk_ref, s_ref, *, L):
    i = pl.program_id(0)
    tl = f_ref.shape[0]
    f = f_ref[...]
    fr = fr_ref[...]
    h = jnp.sin(fr * (_dot(f.astype(BF16), w1_ref[...].astype(BF16)) + b1_ref[...]))
    h = jnp.sin(fr * (_dot(h.astype(BF16), w2_ref[...].astype(BF16)) + b2_ref[...]))
    hk = _dot(h.astype(BF16), w3_ref[...].astype(BF16)) * jnp.exp(-f[:, 0:1] * dl_ref[...])
    n = i * tl + lax.broadcasted_iota(jnp.int32, hk.shape, 0)
    hk = jnp.where(n == L, 0.0, hk)
    hk_ref[...] = hk
    tot = jnp.sum(jnp.abs(hk), axis=0, keepdims=True)

    @pl.when(i == 0)
    def _():
        s_ref[...] = tot

    @pl.when(i != 0)
    def _():
        s_ref[...] = s_ref[...] + tot


def _hy_filter(L, w1, b1, w2, b2, w3, freq):
    f32 = F32
    t = jnp.linspace(0.0, 1.0, L, dtype=f32)[:, None]
    ang = (2.0 * math.pi / L) * jnp.arange(L, dtype=f32)[:, None] * \
        jnp.linspace(1e-4, HY_BANDS - 1, HY_BANDS, dtype=f32)[None]
    feats = jnp.concatenate([t, jnp.cos(ang), -jnp.sin(ang)], axis=-1)
    feats = jnp.concatenate([feats, feats[0:1], feats[:0:-1]], axis=0)
    fpad = 128
    feats = jnp.pad(feats, ((0, 0), (0, fpad - HY_EMB)))
    w1p = jnp.pad(w1, ((0, fpad - HY_EMB), (0, 0)))
    deltas = jnp.abs(jnp.linspace(math.log(HY_TARGET) / HY_SLOW, math.log(HY_TARGET) / HY_FAST, HY_W, dtype=f32))
    OC = HY_ORDER * HY_W
    dl = jnp.tile(deltas, HY_ORDER).reshape(1, OC)
    tl = min(L, 512)
    nblk = L // tl
    full = lambda shp: pl.BlockSpec(shp, lambda i: (0, 0))
    return pl.pallas_call(
        functools.partial(_hy_filter_kernel, L=L),
        grid=(2 * nblk,),
        in_specs=[pl.BlockSpec((tl, fpad), lambda i: (i, 0)),
                  full((fpad, HY_HID)), full((1, HY_HID)), full((HY_HID, HY_HID)), full((1, HY_HID)),
                  pl.BlockSpec((HY_HID, OC), lambda i: (0, i // nblk)), full((1, HY_HID)), full((1, OC))],
        out_specs=[pl.BlockSpec((tl, OC), lambda i: (i, 0)), full((1, OC))],
        out_shape=[jax.ShapeDtypeStruct((2 * L, OC), F32), jax.ShapeDtypeStruct((1, OC), F32)],
        compiler_params=_cparams("arbitrary"),
        name="hyena_filter",
    )(feats, w1p, b1.reshape(1, HY_HID), w2, b2.reshape(1, HY_HID), w3, freq.reshape(1, HY_HID), dl)


def _hy_spectrum(L, tabs, w1, b1, w2, b2, w3, freq):
    n1c, n2c = _hy_factors(L)
    OC = HY_ORDER * HY_W
    ktime, s = _hy_filter(L, w1, b1, w2, b2, w3, freq)
    a = _hy_stage1(ktime.reshape(1, n1c, n2c, OC), tabs["g_real"], F32)[0]
    return _hy_filter_spectrum(a, 1.0 / s, tabs["f2_exact"], 8)


def _hyena(v, x1, x2, kspec, bias, tabs, B, L):
    n1c, n2c = _hy_factors(L)
    C = HY_W
    view = lambda t: t.reshape(B, n1c // 2, n2c, C)
    z0, g1, g2 = view(v), view(x1), view(x2)
    a = _hy_stage1(z0, tabs["g_fwd"], BF16)
    bf = _hy_mid(a, kspec, 0, tabs["f2"], tabs["f2_inv"])
    z1, a = _hy_last(bf, z0, g1, bias[0:1], tabs["g_inv"], tabs["g_fwd"], F32)
    bf = _hy_mid(a, kspec, 1, tabs["f2"], tabs["f2_inv"])
    (z2,) = _hy_last(bf, z1, g2, bias[1:2], tabs["g_inv"], None, BF16)
    return z2.reshape(B * L, C)


def _direct_tables(L):
    N = 2 * L
    k = jnp.arange(N, dtype=jnp.int32)
    ang = ((k[:, None] * k[None, :]) % N).astype(F32) * (2.0 * math.pi / N)
    c, s = jnp.cos(ang), jnp.sin(ang)
    fwd = _cblock(c[:, :L], -s[:, :L])
    inv = _cblock(c[:L, :], s[:L, :]) / N
    real = jnp.concatenate([c, -s], axis=0)
    return dict(fwd=fwd.astype(BF16), inv=inv.astype(BF16), real=_hilo(real))


def _hy_direct_spec_kernel(kt_ref, inv_ref, fh_ref, fl_ref, o_ref):
    N = kt_ref.shape[0]
    X = _dot3(fh_ref[...], fl_ref[...], kt_ref[...]) * inv_ref[...]
    o_ref[0] = X[0:N]
    o_ref[1] = X[N:]


def _hy_direct_spectrum(L, tabs, w1, b1, w2, b2, w3, freq):
    N = 2 * L
    OC = HY_ORDER * HY_W
    cb = HY_W // 2
    ktime, s = _hy_filter(L, w1, b1, w2, b2, w3, freq)
    fh, fl = tabs["real"]
    fspec = pl.BlockSpec((2 * N, N), lambda c: (0, 0))
    return pl.pallas_call(
        _hy_direct_spec_kernel,
        grid=(OC // cb,),
        in_specs=[pl.BlockSpec((N, cb), lambda c: (0, c)), pl.BlockSpec((1, cb), lambda c: (0, c)), fspec, fspec],
        out_specs=pl.BlockSpec((2, N, cb), lambda c: (0, 0, c)),
        out_shape=jax.ShapeDtypeStruct((2, N, OC), F32),
        compiler_params=_cparams("arbitrary"),
        name="hyena_direct_spectrum",
    )(ktime, 1.0 / s, fh, fl)


def _hy_direct_kernel(v_ref, g1_ref, g2_ref, k0_ref, k1_ref, bias_ref, f_ref, fi_ref, o_ref):
    L = v_ref.shape[1]
    N = 2 * L
    z = jnp.concatenate([v_ref[0], v_ref[1]], axis=0)
    for o, (k_ref, g_ref) in enumerate(((k0_ref, g1_ref), (k1_ref, g2_ref))):
        X = _dot(f_ref[...], z.astype(BF16))
        xr, xi = X[0:N], X[N:]
        kr, ki = k_ref[0], k_ref[1]
        Y = jnp.concatenate([xr * kr - xi * ki, xr * ki + xi * kr], axis=0)
        y = _dot(fi_ref[...], Y.astype(BF16))
        gate = jnp.concatenate([g_ref[0], g_ref[1]], axis=0)
        z = gate * (y + z * bias_ref[o:o + 1])
    o_ref[0] = z[0:L].astype(o_ref.dtype)
    o_ref[1] = z[L:].astype(o_ref.dtype)


def _hyena_direct(v, x1, x2, kspec, bias, tabs, B, L):
    C = HY_W
    N = 2 * L
    cb = C // 2
    ncb = C // cb
    seq = pl.BlockSpec((2, L, cb), lambda p, c: (p, 0, c))
    view = lambda t: t.reshape(B, L, C)
    out = pl.pallas_call(
        _hy_direct_kernel,
        grid=(B // 2, ncb),
        in_specs=[seq, seq, seq,
                  pl.BlockSpec((2, N, cb), lambda p, c: (0, 0, c)),
                  pl.BlockSpec((2, N, cb), lambda p, c: (0, 0, ncb + c)),
                  pl.BlockSpec((HY_ORDER, cb), lambda p, c: (0, c)),
                  pl.BlockSpec((2 * N, 2 * L), lambda p, c: (0, 0)),
                  pl.BlockSpec((2 * L, 2 * N), lambda p, c: (0, 0))],
        out_specs=seq,
        out_shape=jax.ShapeDtypeStruct((B, L, C), BF16),
        compiler_params=_cparams("arbitrary", "arbitrary"),
        name="hyena_direct",
    )(view(v), view(x1), view(x2), kspec, kspec, bias, tabs["fwd"], tabs["inv"])
    return out.reshape(B * L, C)


HY_DIRECT_MAX_L = 512


_EVEN_PLAN = (
    (0, HY_W, "conv", 1.0, False, F32),
    (HY_W, HY_W, "conv", 1.0, False, F32),
    (2 * HY_W, HY_W, "conv", 1.0, False, F32),
    (3 * HY_W, M_W, "conv_silu", 1.0, True, BF16),
    (3 * HY_W + M_W, M_W, "conv_silu", M_DH ** -0.5, False, BF16),
    (EV_CONV + M_W, M_W, "raw", 1.0, False, F32),
)
_EVEN_NT_PLAN = ((0, M_W, BF16), (M_W, 4 * M_HEADS, F32))
_ODD_NT_PLAN = ((0, 2 * SSD_HEADS, F32),)
_GN = SSD_GROUPS * SSD_STATE
_ODD_PLAN = (
    (0, SSD_INNER, "raw", 1.0, False, F32),
    (SSD_INNER, SSD_INNER, "conv_silu", 1.0, True, F32),
    (2 * SSD_INNER, _GN, "conv_silu", 1.0, False, BF16),
    (2 * SSD_INNER + _GN, _GN, "conv_silu", 1.0, True, BF16),
)


def _even_layer(x, mods_l, P, e, l, B, L, st, per_batch_cond, kspec, tabs):
    hv, hx1, hx2, qt, k, o, vt, gt = _inproj(
        x, mods_l, P["norm1_g"][l:l + 1], P["ev_in_w"], P["ev_in_wt"], P["ev_conv_w"], P["ev_conv_b"], e,
        plan=_EVEN_PLAN, nt_plan=_EVEN_NT_PLAN, conv0=0, seq_len=L, per_batch_cond=per_batch_cond, row_tile=WIDE_TILE)
    hyena = _hyena_direct if L <= HY_DIRECT_MAX_L else _hyena
    y_hy = hyena(hv, hx1, hx2, kspec, P["hy_bias"][e], tabs, B, L)
    c0, m0 = st
    y_m, c_new, m_new = _mlstm(qt, k, vt, o, gt, P["m_gate_b"][e], P["m_norm_g"][e], c0, m0, B, L)
    x = _outproj_even(y_hy, y_m, x, mods_l, P["ev_out_w"], e, seq_len=L, per_batch_cond=per_batch_cond)
    return x, (c_new, m_new)


def _by_group(p):
    return p.reshape(2, SSD_GROUPS, SSD_HPG).transpose(1, 0, 2).reshape(SSD_GROUPS, 2 * SSD_HPG)


def _odd_layer(x, mods_l, P, o, l, B, L, s0, per_batch_cond):
    z, xt, bm, ct, dtt = _inproj(
        x, mods_l, P["norm1_g"][l:l + 1], P["od_in_w"], P["od_in_wt"], P["od_conv_w"], P["od_conv_b"], o,
        plan=_ODD_PLAN, nt_plan=_ODD_NT_PLAN, conv0=SSD_INNER, seq_len=L, per_batch_cond=per_batch_cond,
        row_tile=NARROW_TILE)
    dskip = jnp.repeat(P["ssd_D"][o], SSD_HEADDIM).reshape(SSD_GROUPS, SSD_GW)
    u, s_new = _ssd(xt, bm, ct, dtt, z, _by_group(P["ssd_dt_bias"][o]), _by_group(P["ssd_A_log"][o]), dskip, s0, B, L)
    x = _outproj_odd(u, x, mods_l, P["ssd_norm_g"], P["od_out_w"], o, seq_len=L, per_batch_cond=per_batch_cond)
    return x, s_new


def _to_colmajor(x, B, L):
    rows = L // GRID_W
    return x.reshape(B, rows, GRID_W, D_MODEL).swapaxes(1, 2).reshape(B * L, D_MODEL)


def _from_colmajor(x, B, L):
    rows = L // GRID_W
    return x.reshape(B, GRID_W, rows, D_MODEL).swapaxes(1, 2).reshape(B * L, D_MODEL)


def _trunk(x, mods, P, B, L, init_states, per_batch_cond, grid, kspecs, tabs):
    states = []
    for l in range(DEPTH):
        if grid and l == 2:
            x = _to_colmajor(x, B, L)
        mods_l = mods[l]
        if l % 2 == 0:
            x, st = _even_layer(x, mods_l, P, l // 2, l, B, L, init_states[l], per_batch_cond, kspecs[l // 2], tabs)
        else:
            x, st = _odd_layer(x, mods_l, P, l // 2, l, B, L, init_states[l], per_batch_cond)
        states.append(st)
        x = _ffn(x, mods_l, P["norm2_g"].reshape(DEPTH, 1, D_MODEL), P["ffn_w1"], P["ffn_w3"], P["ffn_w2"],
                 P["final_g"].reshape(1, D_MODEL), l, seq_len=L, per_batch_cond=per_batch_cond,
                 final=(l == DEPTH - 1))
    if grid:
        x = _from_colmajor(x, B, L)
    return x, states


def _pack_mlstm_state(C, n, m):
    B = C.shape[0]
    ext = jnp.concatenate([jnp.swapaxes(C, -1, -2), n[..., None, :],
                           jnp.zeros(C.shape[:-2] + (M_DH - 1, M_DH), F32)], axis=-2)
    m_rep = jnp.broadcast_to(m.reshape(B, 2 * M_HEADS, 1, 1), (B, 2 * M_HEADS, 1, M_DH))
    return ext.reshape(B, 2 * M_HEADS, 2 * M_DH, M_DH), m_rep


def kernel(x_prompt, x_sample, state_mlstm_C, state_mlstm_n, state_mlstm_m, state_ssd, c, c_ctx,
           mod_w, mod_b, norm1_g, norm2_g, ffn_w1, ffn_w3, ffn_w2, final_g,
           ev_in_w, ev_conv_w, ev_conv_b, hy_w1, hy_b1, hy_w2, hy_b2, hy_w3, hy_freq, hy_bias,
           m_gate_b, m_norm_g, ev_out_w,
           od_in_w, od_conv_w, od_conv_b, ssd_dt_bias, ssd_A_log, ssd_D, ssd_norm_g, od_out_w):
    Bp, Lp, _ = x_prompt.shape
    Bs, Ls, _ = x_sample.shape
    bf = lambda w: w.astype(BF16)
    n_odd = od_in_w.shape[0]
    od_dt_w = od_in_w[:, :, OD_PROJ - 2 * SSD_HEADS:].reshape(n_odd, D_MODEL, 2, SSD_GROUPS, SSD_HPG)
    od_dt_w = od_dt_w.transpose(0, 3, 2, 4, 1).reshape(n_odd, 2 * SSD_HEADS, D_MODEL)
    P = dict(norm1_g=norm1_g, norm2_g=norm2_g, ffn_w1=bf(ffn_w1), ffn_w3=bf(ffn_w3), ffn_w2=bf(ffn_w2),
             final_g=final_g, ev_in_w=bf(ev_in_w),
             ev_in_wt=bf(jnp.concatenate([ev_in_w[:, :, EV_CONV:EV_CONV + M_W],
                                          ev_in_w[:, :, EV_PROJ - 4 * M_HEADS:]], axis=2)).swapaxes(1, 2),
             ev_conv_w=ev_conv_w, ev_conv_b=ev_conv_b, hy_bias=hy_bias, m_gate_b=m_gate_b, m_norm_g=m_norm_g,
             ev_out_w=bf(ev_out_w), od_in_w=bf(od_in_w), od_in_wt=bf(od_dt_w),
             od_conv_w=od_conv_w, od_conv_b=od_conv_b, ssd_dt_bias=ssd_dt_bias, ssd_A_log=ssd_A_log,
             ssd_D=ssd_D, ssd_norm_g=ssd_norm_g.reshape(-1, 1, SSD_INNER), od_out_w=bf(od_out_w))

    ncond = 16
    cond = jnp.concatenate([c_ctx[None, :], c, jnp.zeros((ncond - 1 - Bs, D_MODEL), F32)], axis=0)
    mods = _modulation(cond, mod_w, mod_b).reshape(DEPTH, ncond, 6, D_MODEL)

    tabs, kspecs = {}, {}
    for L in (Lp, Ls):
        direct = L <= HY_DIRECT_MAX_L
        tabs[L] = _direct_tables(L) if direct else _dft_tables(L)
        spectrum = _hy_direct_spectrum if direct else _hy_spectrum
        kspecs[L] = [spectrum(L, tabs[L], hy_w1[e], hy_b1[e], hy_w2[e], hy_b2[e], hy_w3[e], hy_freq[e])
                     for e in range((DEPTH + 1) // 2)]

    G = SSD_GROUPS
    zero_m = _pack_mlstm_state(jnp.zeros((Bp, 2, M_HEADS, M_DH, M_DH), F32), jnp.zeros((Bp, 2, M_HEADS, M_DH), F32),
                               jnp.zeros((Bp, 2, M_HEADS), F32))
    zero_s = jnp.zeros((Bp, 2, G, SSD_GW, SSD_STATE), F32)
    ctx_init = [zero_m if l % 2 == 0 else zero_s for l in range(DEPTH)]
    y_prompt, ctx_states = _trunk(x_prompt.reshape(Bp * Lp, D_MODEL), mods, P, Bp, Lp, ctx_init, False, False,
                                  kspecs[Lp], tabs[Lp])

    lat_init = []
    for l in range(DEPTH):
        if l % 2 == 0:
            e = l // 2
            lat_init.append(_pack_mlstm_state(state_mlstm_C[:, e], state_mlstm_n[:, e], state_mlstm_m[:, e]))
        else:
            lat_init.append(state_ssd[:, l // 2].reshape(Bs, 2, G, SSD_GW, SSD_STATE))
    y_sample, _ = _trunk(x_sample.reshape(Bs * Ls, D_MODEL), mods, P, Bs, Ls, lat_init, True, True,
                         kspecs[Ls], tabs[Ls])

    cs = [ctx_states[l][0].reshape(Bp, 2, M_HEADS, 2 * M_DH, M_DH) for l in range(0, DEPTH, 2)]
    ms = [ctx_states[l][1][:, :, 0, 0].reshape(Bp, 2, M_HEADS) for l in range(0, DEPTH, 2)]
    new_C = jnp.stack([jnp.swapaxes(t[..., :M_DH, :], -1, -2) for t in cs], axis=1)
    new_n = jnp.stack([t[..., M_DH, :] for t in cs], axis=1)
    new_m = jnp.stack(ms, axis=1)
    new_ssd = jnp.stack([ctx_states[l].reshape(Bp, 2, SSD_HEADS, SSD_HEADDIM, SSD_STATE)
                         for l in range(1, DEPTH, 2)], axis=1)
    return (y_prompt.reshape(Bp, Lp, D_MODEL), y_sample.reshape(Bs, Ls, D_MODEL), new_C, new_n, new_m, new_ssd)
```

```python
import functools
import math

import jax
import jax.numpy as jnp
from jax import lax
from jax.experimental import pallas as pl
from jax.experimental.pallas import tpu as pltpu

F32 = jnp.float32
BF16 = jnp.bfloat16

D_MODEL = 1024
DEPTH = 4
GRID_W = 64
EPS = 1e-6
HY_W = D_MODEL // 2
HY_ORDER = 2
HY_EMB = 33
HY_BANDS = (HY_EMB - 1) // 2
HY_HID = 64
HY_TARGET = 1e-2
HY_FAST = 0.3
HY_SLOW = 1.5
M_HEADS = 4
M_DH = 128
M_W = M_HEADS * M_DH
EV_CONV = 3 * HY_W + 2 * M_W
EV_PROJ = EV_CONV + 2 * M_W + 4 * M_HEADS
SSD_INNER = 2 * D_MODEL
SSD_HEADDIM = 64
SSD_HEADS = SSD_INNER // SSD_HEADDIM
SSD_GROUPS = 8
SSD_HPG = SSD_HEADS // SSD_GROUPS
SSD_STATE = 128
SSD_GW = SSD_HPG * SSD_HEADDIM
SSD_CONV_DIM = SSD_INNER + 2 * SSD_GROUPS * SSD_STATE
OD_PROJ = SSD_INNER + SSD_CONV_DIM + 2 * SSD_HEADS
FFN_HIDDEN = -(-8 * D_MODEL // (3 * 256)) * 256

V7X_VMEM_BYTES = 64 * 1024 * 1024
VMEM_LIMIT = V7X_VMEM_BYTES * 3 // 4
WIDE_TILE = 512
NARROW_TILE = 256
HALO = 16
COL_CHUNK = 512
HY_NB = 16
SCAN_T = 128
EPILOGUE_UNROLL = 8
SSD_BLOCK_TOKENS = 2048
MLSTM_BLOCK_TOKENS = 1024


def _cparams(*sem):
    return pltpu.CompilerParams(dimension_semantics=sem, vmem_limit_bytes=VMEM_LIMIT)


def _dot(a, b):
    return jnp.dot(a, b, preferred_element_type=F32)


def _dot_nt(a, b):
    return lax.dot_general(a, b, (((1,), (1,)), ((), ())), preferred_element_type=F32)


def _split2(x):
    h = x.astype(BF16)
    return h, (x - h.astype(F32)).astype(BF16)


def _dot3(th, tl, x):
    xh, xl = _split2(x)
    return _dot(th, xh) + _dot(th, xl) + _dot(tl, xh)


def _dot2(x, t):
    xh, xl = _split2(x)
    return _dot(xh, t) + _dot(xl, t)


def _sigmoid(x):
    return 1.0 / (1.0 + jnp.exp(-x))


def _silu(x):
    return x * _sigmoid(x)


def _log_sigmoid(x):
    return jnp.minimum(x, 0.0) - jnp.log(1.0 + jnp.exp(-jnp.abs(x)))


def _softplus(x):
    return jnp.maximum(x, 0.0) + jnp.log(1.0 + jnp.exp(-jnp.abs(x)))


def _rms(x):
    return x * lax.rsqrt(jnp.mean(x * x, axis=-1, keepdims=True) + EPS)


def _mod_kernel(c_ref, w_ref, b_ref, o_ref):
    a = _silu(c_ref[...]).astype(BF16)
    o_ref[0] = _dot(a, w_ref[0].astype(BF16)) + b_ref[0]


def _modulation(cond, mod_w, mod_b):
    nc = cond.shape[0]
    tn = 1536
    return pl.pallas_call(
        _mod_kernel,
        grid=(DEPTH, 6 * D_MODEL // tn),
        in_specs=[pl.BlockSpec((nc, D_MODEL), lambda l, j: (0, 0)),
                  pl.BlockSpec((1, D_MODEL, tn), lambda l, j: (l, 0, j)),
                  pl.BlockSpec((1, 1, tn), lambda l, j: (l, 0, j))],
        out_specs=pl.BlockSpec((1, nc, tn), lambda l, j: (l, 0, j)),
        out_shape=jax.ShapeDtypeStruct((DEPTH, nc, 6 * D_MODEL), F32),
        compiler_params=_cparams("arbitrary", "arbitrary"),
        name="modulation",
    )(cond, mod_w, mod_b.reshape(DEPTH, 1, 6 * D_MODEL))


def _inproj_kernel(x_ref, xp_ref, xn_ref, mod_ref, g_ref, w_ref, wt_ref, cw_ref, cb_ref, *rest,
                   plan, nt_plan, tiles_per_seq, conv0):
    nseg = len(plan)
    outs = rest[:nseg]
    nt_outs = rest[nseg:nseg + len(nt_plan)]
    hbuf = rest[nseg + len(nt_plan)]
    tm = x_ref.shape[0]
    i = pl.program_id(0)
    j = i % tiles_per_seq
    pmask = (j != 0).astype(F32)
    nmask = (j != tiles_per_seq - 1).astype(F32)
    mod = mod_ref[0]
    sh, sc = mod[0:1], mod[1:2]
    g = g_ref[...]

    def normmod(xv):
        return (_rms(xv) * g * (1.0 + sc) + sh).astype(BF16)

    hbuf[0:tm] = normmod(x_ref[...])
    hbuf[tm:tm + HALO] = normmod(xp_ref[...])
    hbuf[tm + HALO:tm + 2 * HALO] = normmod(xn_ref[...])
    rowid = lax.broadcasted_iota(jnp.int32, (tm, COL_CHUNK), 0)

    for (c0, width, kind, scale, transposed), o_ref in zip(plan, outs):
        for cc in range(c0, c0 + width, COL_CHUNK):
            w = w_ref[0, :, cc:cc + COL_CHUNK]
            if kind == "raw":
                y = _dot(hbuf[0:tm], w)
            else:
                full = _dot(hbuf[...], w)
                acc = full[0:tm]
                ap = full[tm + HALO - 1:tm + HALO] * pmask
                an = full[tm + HALO:tm + HALO + 1] * nmask
                prev = jnp.where(rowid == 0, ap, pltpu.roll(acc, 1, 0))
                nxt = jnp.where(rowid == tm - 1, an, pltpu.roll(acc, tm - 1, 0))
                cw = cw_ref[0, :, cc - conv0:cc - conv0 + COL_CHUNK]
                cb = cb_ref[0, :, cc - conv0:cc - conv0 + COL_CHUNK]
                y = cb + cw[0:1] * prev + cw[1:2] * acc + cw[2:3] * nxt
                if kind == "conv_silu":
                    y = _silu(y)
                if scale != 1.0:
                    y = y * scale
            if transposed:
                o_ref[cc - c0:cc - c0 + COL_CHUNK, :] = y.T.astype(o_ref.dtype)
            else:
                o_ref[:, cc - c0:cc - c0 + COL_CHUNK] = y.astype(o_ref.dtype)

    for (r0, rows), o_ref in zip(nt_plan, nt_outs):
        o_ref[...] = _dot_nt(wt_ref[0, r0:r0 + rows, :], hbuf[0:tm]).astype(o_ref.dtype)


def _inproj(x, mods_l, g_l, w_all, wt_all, cw_all, cb_all, layer_idx, *, plan, nt_plan, conv0, seq_len,
            per_batch_cond, row_tile):
    M = x.shape[0]
    tm = min(row_tile, seq_len)
    tps = seq_len // tm
    nt = M // tm
    hb = tm // HALO
    ntot = w_all.shape[2]
    tw = wt_all.shape[1]
    cdim = cw_all.shape[2]
    e = layer_idx
    cond_map = (lambda i: (1 + i // tps, 0, 0)) if per_batch_cond else (lambda i: (0, 0, 0))
    in_specs = [
        pl.BlockSpec((tm, D_MODEL), lambda i: (i, 0)),
        pl.BlockSpec((HALO, D_MODEL), lambda i: (jnp.maximum(i * hb - 1, 0), 0)),
        pl.BlockSpec((HALO, D_MODEL), lambda i: (jnp.minimum((i + 1) * hb, nt * hb - 1), 0)),
        pl.BlockSpec((1, 6, D_MODEL), cond_map),
        pl.BlockSpec((1, D_MODEL), lambda i: (0, 0)),
        pl.BlockSpec((1, D_MODEL, ntot), lambda i: (e, 0, 0), pipeline_mode=pl.Buffered(1)),
        pl.BlockSpec((1, tw, D_MODEL), lambda i: (e, 0, 0)),
        pl.BlockSpec((1, 3, cdim), lambda i: (e, 0, 0)),
        pl.BlockSpec((1, 1, cdim), lambda i: (e, 0, 0)),
    ]
    out_specs, out_shape = [], []
    for (c0, width, kind, scale, transposed, dt) in plan:
        if transposed:
            out_specs.append(pl.BlockSpec((width, tm), lambda i: (0, i)))
            out_shape.append(jax.ShapeDtypeStruct((width, M), dt))
        else:
            out_specs.append(pl.BlockSpec((tm, width), lambda i: (i, 0)))
            out_shape.append(jax.ShapeDtypeStruct((M, width), dt))
    for (r0, rows, dt) in nt_plan:
        out_specs.append(pl.BlockSpec((rows, tm), lambda i: (0, i)))
        out_shape.append(jax.ShapeDtypeStruct((rows, M), dt))
    kplan = tuple(p[:5] for p in plan)
    knt = tuple(p[:2] for p in nt_plan)
    return pl.pallas_call(
        functools.partial(_inproj_kernel, plan=kplan, nt_plan=knt, tiles_per_seq=tps, conv0=conv0),
        grid=(nt,),
        in_specs=in_specs,
        out_specs=out_specs,
        out_shape=out_shape,
        scratch_shapes=[pltpu.VMEM((tm + 2 * HALO, D_MODEL), BF16)],
        compiler_params=_cparams("arbitrary"),
        name="inproj",
    )(x, x, x, mods_l, g_l, w_all, wt_all, cw_all, cb_all.reshape(cb_all.shape[0], 1, cdim))


def _outproj_even_kernel(yh_ref, ym_ref, x_ref, mod_ref, w_ref, o_ref):
    out = _dot(yh_ref[...], w_ref[0, 0:HY_W]) + _dot(ym_ref[...], w_ref[0, HY_W:])
    o_ref[...] = x_ref[...] + mod_ref[0][2:3] * out


def _outproj_odd_kernel(u_ref, x_ref, mod_ref, ng_ref, w_ref, o_ref):
    un = (_rms(u_ref[...]) * ng_ref[0]).astype(BF16)
    o_ref[...] = x_ref[...] + mod_ref[0][2:3] * _dot(un, w_ref[0])


def _cond_map(seq_len, per_batch_cond):
    tps = seq_len // WIDE_TILE
    return (lambda i: (1 + i // tps, 0, 0)) if per_batch_cond else (lambda i: (0, 0, 0))


def _outproj_even(yh, ym, x, mods_l, w_all, e, *, seq_len, per_batch_cond):
    M = x.shape[0]
    tm = WIDE_TILE
    return pl.pallas_call(
        _outproj_even_kernel,
        grid=(M // tm,),
        in_specs=[pl.BlockSpec((tm, HY_W), lambda i: (i, 0)),
                  pl.BlockSpec((tm, M_W), lambda i: (i, 0)),
                  pl.BlockSpec((tm, D_MODEL), lambda i: (i, 0)),
                  pl.BlockSpec((1, 6, D_MODEL), _cond_map(seq_len, per_batch_cond)),
                  pl.BlockSpec((1, HY_W + M_W, D_MODEL), lambda i: (e, 0, 0), pipeline_mode=pl.Buffered(1))],
        out_specs=pl.BlockSpec((tm, D_MODEL), lambda i: (i, 0)),
        out_shape=jax.ShapeDtypeStruct((M, D_MODEL), F32),
        compiler_params=_cparams("arbitrary"),
        name="outproj_even",
    )(yh, ym, x, mods_l, w_all)


def _outproj_odd(u, x, mods_l, ng_all, w_all, o, *, seq_len, per_batch_cond):
    M = x.shape[0]
    tm = WIDE_TILE
    return pl.pallas_call(
        _outproj_odd_kernel,
        grid=(M // tm,),
        in_specs=[pl.BlockSpec((tm, SSD_INNER), lambda i: (i, 0)),
                  pl.BlockSpec((tm, D_MODEL), lambda i: (i, 0)),
                  pl.BlockSpec((1, 6, D_MODEL), _cond_map(seq_len, per_batch_cond)),
                  pl.BlockSpec((1, 1, SSD_INNER), lambda i: (o, 0, 0)),
                  pl.BlockSpec((1, SSD_INNER, D_MODEL), lambda i: (o, 0, 0), pipeline_mode=pl.Buffered(1))],
        out_specs=pl.BlockSpec((tm, D_MODEL), lambda i: (i, 0)),
        out_shape=jax.ShapeDtypeStruct((M, D_MODEL), F32),
        compiler_params=_cparams("arbitrary"),
        name="outproj_odd",
    )(u, x, mods_l, ng_all, w_all)


FFN_CHUNK = FFN_HIDDEN // 2


def _ffn_kernel(x_ref, mod_ref, g_ref, w1_ref, w3_ref, w2_ref, fg_ref, o_ref, *, final):
    x = x_ref[...]
    mod = mod_ref[0]
    h = (_rms(x) * g_ref[0] * (1.0 + mod[4:5]) + mod[3:4]).astype(BF16)
    ff = jnp.zeros(x.shape, F32)
    for c0 in range(0, FFN_HIDDEN, FFN_CHUNK):
        a = _dot(h, w1_ref[0, :, c0:c0 + FFN_CHUNK])
        b = _dot(h, w3_ref[0, :, c0:c0 + FFN_CHUNK])
        u = (_silu(a) * b).astype(BF16)
        ff = ff + _dot(u, w2_ref[0, c0:c0 + FFN_CHUNK, :])
    xo = x + mod[5:6] * ff
    if final:
        xo = _rms(xo) * fg_ref[...]
    o_ref[...] = xo


def _ffn(x, mods_l, g_all, w1, w3, w2, fg, l, *, seq_len, per_batch_cond, final):
    M = x.shape[0]
    tm = WIDE_TILE
    wspec = lambda shape: pl.BlockSpec(shape, lambda i: (l, 0, 0), pipeline_mode=pl.Buffered(1))
    return pl.pallas_call(
        functools.partial(_ffn_kernel, final=final),
        grid=(M // tm,),
        in_specs=[pl.BlockSpec((tm, D_MODEL), lambda i: (i, 0)),
                  pl.BlockSpec((1, 6, D_MODEL), _cond_map(seq_len, per_batch_cond)),
                  pl.BlockSpec((1, 1, D_MODEL), lambda i: (l, 0, 0)),
                  wspec((1, D_MODEL, FFN_HIDDEN)), wspec((1, D_MODEL, FFN_HIDDEN)),
                  wspec((1, FFN_HIDDEN, D_MODEL)),
                  pl.BlockSpec((1, D_MODEL), lambda i: (0, 0))],
        out_specs=pl.BlockSpec((tm, D_MODEL), lambda i: (i, 0)),
        out_shape=jax.ShapeDtypeStruct((M, D_MODEL), F32),
        compiler_params=_cparams("arbitrary"),
        name="ffn",
    )(x, mods_l, g_all, w1, w3, w2, fg)


def _scan_consts(d):
    T = SCAN_T
    row = lax.broadcasted_iota(jnp.int32, (T, T), 0)
    col = lax.broadcasted_iota(jnp.int32, (T, T), 1)
    mask, last = (row <= col, T - 1) if d == 0 else (row >= col, 0)
    v_mat = jnp.where(mask, 1.0, 0.0).astype(BF16)
    return mask, v_mat, last


def _mlstm_kernel(qt_ref, k_ref, vt_ref, o_ref, gt_ref, gb_ref, ng_ref, c0_ref, m0_ref,
                  y_ref, cst_ref, mst_ref, h_scr, c_scr, m_scr, sb_scr, vw_scr, rows_scr, *, nc, ncs):
    T = SCAN_T
    H = M_HEADS
    nseq = nc // ncs
    ones_blk = jnp.where(lax.broadcasted_iota(jnp.int32, (M_DH, T), 0) == 0, 1.0, 0.0)
    consts = (_scan_consts(0), _scan_consts(1))
    gb = gb_ref[...]

    def gates(ci, d, slot):
        mask, v_mat, last = consts[d]
        c0 = pl.multiple_of(ci * T, T)
        lanes = pl.ds(c0, T)
        graw = gt_ref[:, lanes] + gb
        lf = _log_sigmoid(graw)
        b_all = _dot2(lf, v_mat)
        for h in range(H):
            fi = (2 * d + 1) * H + h
            ii = (2 * d) * H + h
            hl = slice(h * M_DH, (h + 1) * M_DH)
            tr = slice(h * T, (h + 1) * T)
            b_row, i_row = b_all[fi:fi + 1], graw[ii:ii + 1]
            dlog = jnp.where(mask, jnp.broadcast_to(b_row, (T, T)) - jnp.broadcast_to(b_row - i_row, (T, T)).T,
                             -jnp.inf)
            mloc = jnp.max(dlog, axis=0, keepdims=True)
            sb_scr[slot, d, tr, :] = (_dot(k_ref[pl.ds(c0, T), hl], qt_ref[hl, lanes])
                                      * jnp.exp(dlog - mloc)).astype(BF16)
            mloc_last = jnp.broadcast_to(mloc[:, last:last + 1], (1, T))
            b_last = jnp.broadcast_to(b_row[:, last:last + 1], (1, T))
            w_end = jnp.exp(b_last - b_row + i_row - mloc_last)
            vw_scr[slot, d, h, 0:M_DH, :] = (vt_ref[hl, lanes].astype(F32) * w_end).astype(BF16)
            vw_scr[slot, d, h, M_DH:, :] = (ones_blk * w_end).astype(BF16)
            rows_scr[slot, d, 4 * h:4 * h + 4, :] = jnp.concatenate([b_row, mloc, mloc_last, b_last], axis=0)

    def update(ci, d, slot, accumulate):
        c0 = pl.multiple_of(ci * T, T)
        lanes = pl.ds(c0, T)
        for h in range(H):
            sidx = d * H + h
            hl = slice(h * M_DH, (h + 1) * M_DH)
            tr = slice(h * T, (h + 1) * T)
            rows = rows_scr[slot, d, 4 * h:4 * h + 4, :]
            b_row, mloc, mloc_last, b_last = rows[0:1], rows[1:2], rows[2:3], rows[3:4]
            m_row = m_scr[sidx]
            inter = b_row + m_row
            mt = jnp.maximum(inter, mloc)
            kh = k_ref[pl.ds(c0, T), hl]
            qt = qt_ref[hl, lanes]
            vext = jnp.concatenate([vt_ref[hl, lanes], ones_blk.astype(BF16)], axis=0)
            ct = c_scr[sidx]
            tot = (jnp.exp(mloc - mt) * _dot(vext, sb_scr[slot, d, tr, :])
                   + jnp.exp(inter - mt) * _dot(ct.astype(BF16), qt))
            den = tot[M_DH:M_DH + 1]
            ht = tot[0:M_DH] / jnp.maximum(jnp.abs(den), jnp.exp(-mt))
            if accumulate:
                h_scr[hl, lanes] = h_scr[hl, lanes] + ht
            else:
                h_scr[hl, lanes] = ht
            m_new = jnp.maximum(b_last + m_row, mloc_last)
            c_scr[sidx] = (jnp.exp(b_last + m_row - m_new) * ct
                           + jnp.exp(mloc_last - m_new) * _dot(vw_scr[slot, d, h], kh))
            m_scr[sidx] = m_new

    def chunks_at(j):
        if nseq == 1:
            return j, nc - 1 - j
        seq, loc = divmod(j, ncs)
        return j, seq * ncs + ncs - 1 - loc

    def load_state(seq):
        c_scr[...] = c0_ref[seq]
        m_scr[...] = m0_ref[seq]

    def store_state(seq):
        cst_ref[seq] = c_scr[...]
        mst_ref[seq] = m_scr[...]

    def pair(p, slots, accumulate=None):
        s0, s1 = slots
        for j, slot in ((2 * p, s0), (2 * p + 1, s1)):
            if nseq > 1 and j % ncs == 0:
                load_state(j // ncs)
            acc = (j % ncs >= ncs // 2) if accumulate is None else accumulate
            cf, cb = chunks_at(j)
            update(cf, 0, slot, acc)
            update(cb, 1, slot, acc)
            if nseq > 1 and j % ncs == ncs - 1:
                store_state(j // ncs)
        for j, slot in ((2 * p + 2, 2 - s0), (2 * p + 3, 4 - s1)):
            if nseq == 1:
                gates(jnp.minimum(j, nc - 1), 0, slot)
                gates(jnp.maximum(nc - 1 - j, 0), 1, slot)
            elif j < nc:
                cf, cb = chunks_at(j)
                gates(cf, 0, slot)
                gates(cb, 1, slot)

    for j in (0, 1):
        cf, cb = chunks_at(j)
        gates(cf, 0, j)
        gates(cb, 1, j)
    if nseq == 1:
        load_state(0)
    if nseq == 1 and nc % 8 == 0:
        def body(accumulate):
            def run(i, carry):
                pair(2 * i, (0, 1), accumulate)
                pair(2 * i + 1, (2, 3), accumulate)
                return carry
            return run
        lax.fori_loop(0, nc // 8, body(False), 0)
        lax.fori_loop(nc // 8, nc // 4, body(True), 0)
    else:
        for p in range(nc // 2):
            pair(p, ((0, 1), (2, 3))[p % 2])
    if nseq == 1:
        store_state(0)

    def epilogue(i, carry):
        c0 = pl.multiple_of(i * T, T)
        for h in range(H):
            hl = slice(h * M_DH, (h + 1) * M_DH)
            hh = h_scr[hl, pl.ds(c0, T)]
            hn = hh * lax.rsqrt(jnp.mean(hh * hh, axis=0, keepdims=True) + EPS)
            y = hn.T * ng_ref[:, hl] * _sigmoid(o_ref[pl.ds(c0, T), hl])
            y_ref[pl.ds(c0, T), hl] = y.astype(y_ref.dtype)
        return carry

    lax.fori_loop(0, nc, epilogue, 0, unroll=EPILOGUE_UNROLL)


def _mlstm(qt, k, vt, o, gt, gate_b, norm_g, c0, m0, B, L):
    assert L % (2 * SCAN_T) == 0 and M_DH == SCAN_T
    bb = max(1, min(B, MLSTM_BLOCK_TOKENS // L))
    assert B % bb == 0
    ncs = L // SCAN_T
    nc = bb * ncs
    Lb = bb * L
    ng = 4 * M_HEADS
    M = B * L
    one = pl.Buffered(1)
    st_spec = pl.BlockSpec((bb, 2 * M_HEADS, 2 * M_DH, M_DH), lambda b: (b, 0, 0, 0))
    m_spec = pl.BlockSpec((bb, 2 * M_HEADS, 1, M_DH), lambda b: (b, 0, 0, 0))
    return pl.pallas_call(
        functools.partial(_mlstm_kernel, nc=nc, ncs=ncs),
        grid=(B // bb,),
        in_specs=[pl.BlockSpec((M_W, Lb), lambda b: (0, b), pipeline_mode=one),
                  pl.BlockSpec((Lb, M_W), lambda b: (b, 0), pipeline_mode=one),
                  pl.BlockSpec((M_W, Lb), lambda b: (0, b), pipeline_mode=one),
                  pl.BlockSpec((Lb, M_W), lambda b: (b, 0), pipeline_mode=one),
                  pl.BlockSpec((ng, Lb), lambda b: (0, b)),
                  pl.BlockSpec((ng, SCAN_T), lambda b: (0, 0)),
                  pl.BlockSpec((1, M_W), lambda b: (0, 0)),
                  st_spec, m_spec],
        out_specs=[pl.BlockSpec((Lb, M_W), lambda b: (b, 0)), st_spec, m_spec],
        out_shape=[jax.ShapeDtypeStruct((M, M_W), BF16),
                   jax.ShapeDtypeStruct((B, 2 * M_HEADS, 2 * M_DH, M_DH), F32),
                   jax.ShapeDtypeStruct((B, 2 * M_HEADS, 1, M_DH), F32)],
        scratch_shapes=[pltpu.VMEM((M_W, Lb), F32),
                        pltpu.VMEM((2 * M_HEADS, 2 * M_DH, M_DH), F32),
                        pltpu.VMEM((2 * M_HEADS, 1, M_DH), F32),
                        pltpu.VMEM((4, 2, M_HEADS * SCAN_T, SCAN_T), BF16),
                        pltpu.VMEM((4, 2, M_HEADS, 2 * M_DH, SCAN_T), BF16),
                        pltpu.VMEM((4, 2, 4 * M_HEADS, SCAN_T), F32)],
        compiler_params=_cparams("arbitrary"),
        name="mlstm",
    )(qt, k, vt, o, gt, jnp.broadcast_to(gate_b.reshape(ng, 1), (ng, SCAN_T)), norm_g.reshape(1, M_W), c0, m0)


def _ssd_kernel(xt_ref, b_ref, ct_ref, dtt_ref, z_ref, db_ref, al_ref, dsk_ref, s0_ref,
                y_ref, sout_ref, yf_scr, yb_scr, s_scr, cbt_scr, p_scr, xd_scr, xdec_scr, rows_scr, *, nc, ncs):
    T = SCAN_T
    H = SSD_HPG
    nseq = nc // ncs
    consts = (_scan_consts(0), _scan_consts(1))
    db = db_ref[0]
    a_neg = -jnp.exp(al_ref[0])

    def gates(ci, d, slot):
        mask, v_mat, last = consts[d]
        c0 = pl.multiple_of(ci * T, T)
        lanes = pl.ds(c0, T)
        hr = slice(d * H, (d + 1) * H)
        dt = _softplus(dtt_ref[hr, lanes] + db[hr])
        a = dt * a_neg[hr]
        acs = _dot2(a, v_mat)
        a_last = jnp.broadcast_to(acs[:, last:last + 1], (H, T))
        rows_scr[slot, d, 0:H, :] = jnp.exp(acs)
        rows_scr[slot, d, H:2 * H, :] = jnp.exp(a_last)
        cbt_scr[d] = _dot(b_ref[pl.ds(c0, T), :], ct_ref[:, lanes])
        for r in range(H):
            tr = slice(r * T, (r + 1) * T)
            pr = slice(r * SSD_HEADDIM, (r + 1) * SSD_HEADDIM)
            al = jnp.broadcast_to(acs[r:r + 1], (T, T))
            lm = jnp.exp(jnp.where(mask, al - al.T, -jnp.inf))
            p_scr[slot, d, tr, :] = (cbt_scr[d] * lm).astype(BF16)
            xd = xt_ref[pr, lanes] * dt[r:r + 1]
            xd_scr[slot, d, pr, :] = xd.astype(BF16)
            xdec_scr[slot, d, pr, :] = (xd * jnp.exp(a_last[r:r + 1] - acs[r:r + 1])).astype(BF16)

    def update(ci, d, slot):
        c0 = pl.multiple_of(ci * T, T)
        lanes = pl.ds(c0, T)
        rows = rows_scr[slot, d]
        sc = _dot(s_scr[d].astype(BF16), ct_ref[:, lanes])
        upd = _dot(xdec_scr[slot, d], b_ref[pl.ds(c0, T), :])
        y_scr = yf_scr if d == 0 else yb_scr
        for r in range(H):
            pr = slice(r * SSD_HEADDIM, (r + 1) * SSD_HEADDIM)
            tr = slice(r * T, (r + 1) * T)
            y_scr[pr, lanes] = _dot(xd_scr[slot, d, pr, :], p_scr[slot, d, tr, :]) + rows[r:r + 1] * sc[pr]
            s_scr[d, pr, :] = rows[H + r:H + r + 1] * s_scr[d, pr, :] + upd[pr]

    def chunks_at(j):
        if nseq == 1:
            return j, nc - 1 - j
        seq, loc = divmod(j, ncs)
        return j, seq * ncs + ncs - 1 - loc

    def pair(p, slots):
        s0, s1 = slots
        for j, slot in ((2 * p, s0), (2 * p + 1, s1)):
            if nseq > 1 and j % ncs == 0:
                s_scr[...] = s0_ref[j // ncs, :, 0]
            cf, cb = chunks_at(j)
            update(cf, 0, slot)
            update(cb, 1, slot)
            if nseq > 1 and j % ncs == ncs - 1:
                sout_ref[j // ncs, :, 0] = s_scr[...]
        for j, slot in ((2 * p + 2, 2 - s0), (2 * p + 3, 4 - s1)):
            if nseq == 1:
                gates(jnp.minimum(j, nc - 1), 0, slot)
                gates(jnp.maximum(nc - 1 - j, 0), 1, slot)
            elif j < nc:
                cf, cb = chunks_at(j)
                gates(cf, 0, slot)
                gates(cb, 1, slot)

    for j in (0, 1):
        cf, cb = chunks_at(j)
        gates(cf, 0, j)
        gates(cb, 1, j)
    if nseq == 1:
        s_scr[...] = s0_ref[0, :, 0]
    if nseq == 1 and nc % 4 == 0:
        def body(i, carry):
            pair(2 * i, (0, 1))
            pair(2 * i + 1, (2, 3))
            return carry
        lax.fori_loop(0, nc // 4, body, 0)
    else:
        for p in range(nc // 2):
            pair(p, ((0, 1), (2, 3))[p % 2])
    if nseq == 1:
        sout_ref[0, :, 0] = s_scr[...]

    dsk = dsk_ref[0]

    def epilogue(i, carry):
        c0 = pl.multiple_of(i * T, T)
        lanes = pl.ds(c0, T)
        yt = yf_scr[:, lanes] + yb_scr[:, lanes] + dsk * xt_ref[:, lanes]
        y_ref[pl.ds(c0, T), :] = yt.T * _silu(z_ref[pl.ds(c0, T), :])
        return carry

    lax.fori_loop(0, nc, epilogue, 0, unroll=EPILOGUE_UNROLL)


def _ssd(xt, bm, ct, dtt, z, dt_bias, a_log, dskip, s0, B, L):
    assert L % (2 * SCAN_T) == 0 and SSD_STATE == SCAN_T
    bb = max(1, min(B, SSD_BLOCK_TOKENS // L))
    assert B % bb == 0
    ncs = L // SCAN_T
    nc = bb * ncs
    Lb = bb * L
    G = SSD_GROUPS
    nd = 2 * SSD_HPG
    M = B * L
    rep = lambda p, rows: jnp.broadcast_to(p.reshape(G, rows, 1), (G, rows, SCAN_T))
    small = lambda rows: pl.BlockSpec((1, rows, SCAN_T), lambda b, g: (g, 0, 0))
    st_spec = pl.BlockSpec((bb, 2, 1, SSD_GW, SSD_STATE), lambda b, g: (b, 0, g, 0, 0))
    return pl.pallas_call(
        functools.partial(_ssd_kernel, nc=nc, ncs=ncs),
        grid=(B // bb, G),
        in_specs=[pl.BlockSpec((SSD_GW, Lb), lambda b, g: (g, b)),
                  pl.BlockSpec((Lb, SSD_STATE), lambda b, g: (b, g)),
                  pl.BlockSpec((SSD_STATE, Lb), lambda b, g: (g, b)),
                  pl.BlockSpec((nd, Lb), lambda b, g: (g, b)),
                  pl.BlockSpec((Lb, SSD_GW), lambda b, g: (b, g)),
                  small(nd), small(nd), small(SSD_GW), st_spec],
        out_specs=[pl.BlockSpec((Lb, SSD_GW), lambda b, g: (b, g)), st_spec],
        out_shape=[jax.ShapeDtypeStruct((M, SSD_INNER), F32),
                   jax.ShapeDtypeStruct((B, 2, G, SSD_GW, SSD_STATE), F32)],
        scratch_shapes=[pltpu.VMEM((SSD_GW, Lb), F32), pltpu.VMEM((SSD_GW, Lb), F32),
                        pltpu.VMEM((2, SSD_GW, SSD_STATE), F32),
                        pltpu.VMEM((2, SCAN_T, SCAN_T), F32),
                        pltpu.VMEM((4, 2, SSD_HPG * SCAN_T, SCAN_T), BF16),
                        pltpu.VMEM((4, 2, SSD_GW, SCAN_T), BF16),
                        pltpu.VMEM((4, 2, SSD_GW, SCAN_T), BF16),
                        pltpu.VMEM((4, 2, 2 * SSD_HPG, SCAN_T), F32)],
        compiler_params=_cparams("arbitrary", "arbitrary"),
        name="ssd",
    )(xt, bm, ct, dtt, z, rep(dt_bias, nd), rep(a_log, nd), rep(dskip, SSD_GW), s0)


def _hy_factors(L):
    return (128, 64) if L == 4096 else (32, (2 * L) // 32)


def _hilo(t):
    return _split2(t.astype(F32))


def _cblock(cr, ci):
    return jnp.concatenate([jnp.concatenate([cr, -ci], axis=-1), jnp.concatenate([ci, cr], axis=-1)], axis=-2)


def _dft_tables(L):
    n1c, n2c = _hy_factors(L)
    N = n1c * n2c
    h1 = n1c // 2
    i32 = jnp.int32
    k1 = jnp.arange(n1c, dtype=i32)
    n2 = jnp.arange(n2c, dtype=i32)

    def stage1(n1_count):
        n1 = jnp.arange(n1_count, dtype=i32)
        idx = (k1[None, :, None] * (n2[:, None, None] + n2c * n1[None, None, :])) % N
        ang = idx.astype(F32) * (2.0 * math.pi / N)
        return jnp.cos(ang), jnp.sin(ang)

    c, s = stage1(h1)
    g_fwd = _cblock(c, -s)
    g_inv = _cblock(jnp.swapaxes(c, 1, 2), jnp.swapaxes(s, 1, 2)) / n1c
    cf, sf = stage1(n1c)
    g_real = jnp.concatenate([cf, -sf], axis=1)
    idx2 = (n2[:, None] * n2[None, :]) % n2c
    ang2 = idx2.astype(F32) * (2.0 * math.pi / n2c)
    c2, s2 = jnp.cos(ang2), jnp.sin(ang2)
    f2 = _cblock(c2, -s2)
    f2_inv = _cblock(c2, s2) / n2c
    return dict(g_fwd=g_fwd.astype(BF16), g_inv=g_inv.astype(BF16), f2=f2.astype(BF16),
                f2_inv=f2_inv.astype(BF16), g_real=_hilo(g_real), f2_exact=_hilo(f2))


def _swap01(x):
    return jnp.swapaxes(x, 0, 1)


def _hy_s1_kernel(x_ref, *rest, nb, exact):
    if exact:
        gh_ref, gl_ref, o_ref = rest
    else:
        g_ref, o_ref = rest
    pb, rows, _, C = x_ref.shape
    xs = _swap01(x_ref[...].reshape(pb * rows, nb, C))
    if exact:
        res = [_dot3(gh_ref[j], gl_ref[j], xs[j]) for j in range(nb)]
    else:
        res = [_dot(g_ref[j], xs[j].astype(BF16)) for j in range(nb)]
    out = _swap01(jnp.stack(res, axis=0).astype(o_ref.dtype))
    o_ref[...] = out.reshape(o_ref.shape)


def _hy_stage1(x, tab, out_dtype):
    exact = isinstance(tab, tuple)
    tabs = tab if exact else (tab,)
    n2c, rows_out, R = tabs[0].shape
    Bx, rows, _, C = x.shape
    pb = R // rows
    nb = HY_NB
    cb = min(C, HY_W)
    tspec = pl.BlockSpec((nb, rows_out, R), lambda p, t, c: (t, 0, 0))
    return pl.pallas_call(
        functools.partial(_hy_s1_kernel, nb=nb, exact=exact),
        grid=(Bx // pb, n2c // nb, C // cb),
        in_specs=[pl.BlockSpec((pb, rows, nb, cb), lambda p, t, c: (p, 0, t, c))] + [tspec] * len(tabs),
        out_specs=pl.BlockSpec((None, 2, rows_out // 2, nb, cb), lambda p, t, c: (p, 0, 0, t, c)),
        out_shape=jax.ShapeDtypeStruct((Bx // pb, 2, rows_out // 2, n2c, C), out_dtype),
        compiler_params=_cparams("arbitrary", "arbitrary", "arbitrary"),
        name="hyena_stage1",
    )(x, *tabs)


def _hy_spec_kernel(a_ref, inv_ref, fh_ref, fl_ref, o_ref, *, kb):
    n2c = a_ref.shape[2]
    for kk in range(kb):
        rhs = jnp.concatenate([a_ref[0, kk], a_ref[1, kk]], axis=0)
        X = _dot3(fh_ref[...], fl_ref[...], rhs) * inv_ref[...]
        o_ref[0, kk] = X[0:n2c]
        o_ref[1, kk] = X[n2c:]


def _hy_filter_spectrum(a, inv_norm, f2, kb):
    _, n1c, n2c, W = a.shape
    fh, fl = f2
    blk = pl.BlockSpec((2, kb, n2c, W), lambda t: (0, t, 0, 0))
    fspec = pl.BlockSpec((2 * n2c, 2 * n2c), lambda t: (0, 0))
    return pl.pallas_call(
        functools.partial(_hy_spec_kernel, kb=kb),
        grid=(n1c // kb,),
        in_specs=[blk, pl.BlockSpec((1, W), lambda t: (0, 0)), fspec, fspec],
        out_specs=blk,
        out_shape=jax.ShapeDtypeStruct(a.shape, F32),
        compiler_params=_cparams("arbitrary"),
        name="hyena_filter_spectrum",
    )(a, inv_norm, fh, fl)


def _hy_mid_kernel(a_ref, k_ref, f_ref, i_ref, o_ref, *, kb):
    n2c = a_ref.shape[2]
    res = []
    for kk in range(kb):
        rhs = jnp.concatenate([a_ref[0, kk], a_ref[1, kk]], axis=0)
        X = _dot(f_ref[...], rhs)
        xr, xi = X[0:n2c], X[n2c:]
        kr, ki = k_ref[0, kk], k_ref[1, kk]
        Y = jnp.concatenate([xr * kr - xi * ki, xr * ki + xi * kr], axis=0)
        res.append(_dot(i_ref[...], Y.astype(BF16)))
    o_ref[...] = _swap01(jnp.stack(res, axis=0).astype(o_ref.dtype)).reshape(o_ref.shape)


def _hy_mid(a, kspec, order, f2, f2_inv):
    P, _, n1c, n2c, C = a.shape
    kb = HY_NB
    fspec = pl.BlockSpec((2 * n2c, 2 * n2c), lambda t, p: (0, 0))
    return pl.pallas_call(
        functools.partial(_hy_mid_kernel, kb=kb),
        grid=(n1c // kb, P),
        in_specs=[pl.BlockSpec((None, 2, kb, n2c, C), lambda t, p: (p, 0, t, 0, 0)),
                  pl.BlockSpec((2, kb, n2c, C), lambda t, p: (0, t, 0, order)),
                  fspec, fspec],
        out_specs=pl.BlockSpec((None, 2, n2c, kb, C), lambda t, p: (p, 0, 0, t, 0)),
        out_shape=jax.ShapeDtypeStruct((P, 2, n2c, n1c, C), BF16),
        compiler_params=_cparams("arbitrary", "arbitrary"),
        name="hyena_mid",
    )(a, kspec, f2, f2_inv)


def _hy_last_kernel(b_ref, z_ref, gate_ref, bias_ref, gi_ref, *rest, fuse, nb):
    if fuse:
        gf_ref, zo_ref, ao_ref = rest
    else:
        (zo_ref,) = rest
    _, h1, _, C = gate_ref.shape
    gs = _swap01(gate_ref[...].reshape(2 * h1, nb, C))
    zs = _swap01(z_ref[...].reshape(2 * h1, nb, C)) if fuse else z_ref[...]
    bias = bias_ref[...]
    zn, an = [], []
    for j in range(nb):
        bv = jnp.concatenate([b_ref[0, j], b_ref[1, j]], axis=0)
        y = _dot(gi_ref[j], bv)
        znew = gs[j] * (y + zs[j] * bias)
        zn.append(znew)
        if fuse:
            an.append(_dot(gf_ref[j], znew.astype(BF16)))
    if fuse:
        zo_ref[...] = jnp.stack(zn, axis=0)
        ao_ref[...] = _swap01(jnp.stack(an, axis=0).astype(ao_ref.dtype)).reshape(ao_ref.shape)
    else:
        zo_ref[...] = _swap01(jnp.stack(zn, axis=0).astype(zo_ref.dtype)).reshape(zo_ref.shape)


def _hy_last(bf, z, gate, bias, g_inv, g_fwd, out_dtype):
    P, _, n2c, n1c, C = bf.shape
    B, h1, _, _ = gate.shape
    nb = HY_NB
    cb = C // 2
    fuse = g_fwd is not None
    nat = pl.BlockSpec((2, h1, nb, cb), lambda p, t, c: (p, 0, t, c))
    swp = pl.BlockSpec((None, nb, 2 * h1, cb), lambda p, t, c: (p, t, 0, c))
    in_specs = [pl.BlockSpec((None, 2, nb, n1c, cb), lambda p, t, c: (p, 0, t, 0, c)),
                nat if fuse else swp, nat, pl.BlockSpec((1, cb), lambda p, t, c: (0, c)),
                pl.BlockSpec((nb, 2 * h1, 2 * n1c), lambda p, t, c: (t, 0, 0))]
    args = [bf, z, gate, bias, g_inv]
    if fuse:
        in_specs.append(pl.BlockSpec((nb, 2 * n1c, 2 * h1), lambda p, t, c: (t, 0, 0)))
        args.append(g_fwd)
        out_specs = [swp, pl.BlockSpec((None, 2, n1c, nb, cb), lambda p, t, c: (p, 0, 0, t, c))]
        out_shape = [jax.ShapeDtypeStruct((P, n2c, 2 * h1, C), F32),
                     jax.ShapeDtypeStruct((P, 2, n1c, n2c, C), BF16)]
    else:
        out_specs = [nat]
        out_shape = [jax.ShapeDtypeStruct(gate.shape, out_dtype)]
    return pl.pallas_call(
        functools.partial(_hy_last_kernel, fuse=fuse, nb=nb),
        grid=(P, n2c // nb, C // cb),
        in_specs=in_specs, out_specs=out_specs, out_shape=out_shape,
        compiler_params=_cparams("arbitrary", "arbitrary", "arbitrary"),
        name="hyena_last",
    )(*args)


def _hy_filter_kernel(f_ref, w1_ref, b1_ref, w2_ref, b2_ref, w3_ref, fr_ref, dl_ref, hk_ref, s_ref, *, L):
    i = pl.program_id(0)
    tl = f_ref.shape[0]
    f = f_ref[...]
    fr = fr_ref[...]
    h = jnp.sin(fr * (_dot(f.astype(BF16), w1_ref[...].astype(BF16)) + b1_ref[...]))
    h = jnp.sin(fr * (_dot(h.astype(BF16), w2_ref[...].astype(BF16)) + b2_ref[...]))
    hk = _dot(h.astype(BF16), w3_ref[...].astype(BF16)) * jnp.exp(-f[:, 0:1] * dl_ref[...])
    n = i * tl + lax.broadcasted_iota(jnp.int32, hk.shape, 0)
    hk = jnp.where(n == L, 0.0, hk)
    hk_ref[...] = hk
    tot = jnp.sum(jnp.abs(hk), axis=0, keepdims=True)

    @pl.when(i == 0)
    def _():
        s_ref[...] = tot

    @pl.when(i != 0)
    def _():
        s_ref[...] = s_ref[...] + tot


def _hy_filter(L, w1, b1, w2, b2, w3, freq):
    f32 = F32
    t = jnp.linspace(0.0, 1.0, L, dtype=f32)[:, None]
    ang = (2.0 * math.pi / L) * jnp.arange(L, dtype=f32)[:, None] * \
        jnp.linspace(1e-4, HY_BANDS - 1, HY_BANDS, dtype=f32)[None]
    feats = jnp.concatenate([t, jnp.cos(ang), -jnp.sin(ang)], axis=-1)
    feats = jnp.concatenate([feats, feats[0:1], feats[:0:-1]], axis=0)
    fpad = 128
    feats = jnp.pad(feats, ((0, 0), (0, fpad - HY_EMB)))
    w1p = jnp.pad(w1, ((0, fpad - HY_EMB), (0, 0)))
    deltas = jnp.abs(jnp.linspace(math.log(HY_TARGET) / HY_SLOW, math.log(HY_TARGET) / HY_FAST, HY_W, dtype=f32))
    OC = HY_ORDER * HY_W
    dl = jnp.tile(deltas, HY_ORDER).reshape(1, OC)
    tl = min(L, 512)
    nblk = L // tl
    full = lambda shp: pl.BlockSpec(shp, lambda i: (0, 0))
    return pl.pallas_call(
        functools.partial(_hy_filter_kernel, L=L),
        grid=(2 * nblk,),
        in_specs=[pl.BlockSpec((tl, fpad), lambda i: (i, 0)),
                  full((fpad, HY_HID)), full((1, HY_HID)), full((HY_HID, HY_HID)), full((1, HY_HID)),
                  pl.BlockSpec((HY_HID, OC), lambda i: (0, i // nblk)), full((1, HY_HID)), full((1, OC))],
        out_specs=[pl.BlockSpec((tl, OC), lambda i: (i, 0)), full((1, OC))],
        out_shape=[jax.ShapeDtypeStruct((2 * L, OC), F32), jax.ShapeDtypeStruct((1, OC), F32)],
        compiler_params=_cparams("arbitrary"),
        name="hyena_filter",
    )(feats, w1p, b1.reshape(1, HY_HID), w2, b2.reshape(1, HY_HID), w3, freq.reshape(1, HY_HID), dl)


def _hy_spectrum(L, tabs, w1, b1, w2, b2, w3, freq):
    n1c, n2c = _hy_factors(L)
    OC = HY_ORDER * HY_W
    ktime, s = _hy_filter(L, w1, b1, w2, b2, w3, freq)
    a = _hy_stage1(ktime.reshape(1, n1c, n2c, OC), tabs["g_real"], F32)[0]
    return _hy_filter_spectrum(a, 1.0 / s, tabs["f2_exact"], 8)


def _hyena(v, x1, x2, kspec, bias, tabs, B, L):
    n1c, n2c = _hy_factors(L)
    C = HY_W
    view = lambda t: t.reshape(B, n1c // 2, n2c, C)
    z0, g1, g2 = view(v), view(x1), view(x2)
    a = _hy_stage1(z0, tabs["g_fwd"], BF16)
    bf = _hy_mid(a, kspec, 0, tabs["f2"], tabs["f2_inv"])
    z1, a = _hy_last(bf, z0, g1, bias[0:1], tabs["g_inv"], tabs["g_fwd"], F32)
    bf = _hy_mid(a, kspec, 1, tabs["f2"], tabs["f2_inv"])
    (z2,) = _hy_last(bf, z1, g2, bias[1:2], tabs["g_inv"], None, BF16)
    return z2.reshape(B * L, C)


def _direct_tables(L):
    N = 2 * L
    k = jnp.arange(N, dtype=jnp.int32)
    ang = ((k[:, None] * k[None, :]) % N).astype(F32) * (2.0 * math.pi / N)
    c, s = jnp.cos(ang), jnp.sin(ang)
    fwd = _cblock(c[:, :L], -s[:, :L])
    inv = _cblock(c[:L, :], s[:L, :]) / N
    real = jnp.concatenate([c, -s], axis=0)
    return dict(fwd=fwd.astype(BF16), inv=inv.astype(BF16), real=_hilo(real))


def _hy_direct_spec_kernel(kt_ref, inv_ref, fh_ref, fl_ref, o_ref):
    N = kt_ref.shape[0]
    X = _dot3(fh_ref[...], fl_ref[...], kt_ref[...]) * inv_ref[...]
    o_ref[0] = X[0:N]
    o_ref[1] = X[N:]


def _hy_direct_spectrum(L, tabs, w1, b1, w2, b2, w3, freq):
    N = 2 * L
    OC = HY_ORDER * HY_W
    cb = HY_W // 2
    ktime, s = _hy_filter(L, w1, b1, w2, b2, w3, freq)
    fh, fl = tabs["real"]
    fspec = pl.BlockSpec((2 * N, N), lambda c: (0, 0))
    return pl.pallas_call(
        _hy_direct_spec_kernel,
        grid=(OC // cb,),
        in_specs=[pl.BlockSpec((N, cb), lambda c: (0, c)), pl.BlockSpec((1, cb), lambda c: (0, c)), fspec, fspec],
        out_specs=pl.BlockSpec((2, N, cb), lambda c: (0, 0, c)),
        out_shape=jax.ShapeDtypeStruct((2, N, OC), F32),
        compiler_params=_cparams("arbitrary"),
        name="hyena_direct_spectrum",
    )(ktime, 1.0 / s, fh, fl)


def _hy_direct_kernel(v_ref, g1_ref, g2_ref, k0_ref, k1_ref, bias_ref, f_ref, fi_ref, o_ref):
    L = v_ref.shape[1]
    N = 2 * L
    z = jnp.concatenate([v_ref[0], v_ref[1]], axis=0)
    for o, (k_ref, g_ref) in enumerate(((k0_ref, g1_ref), (k1_ref, g2_ref))):
        X = _dot(f_ref[...], z.astype(BF16))
        xr, xi = X[0:N], X[N:]
        kr, ki = k_ref[0], k_ref[1]
        Y = jnp.concatenate([xr * kr - xi * ki, xr * ki + xi * kr], axis=0)
        y = _dot(fi_ref[...], Y.astype(BF16))
        gate = jnp.concatenate([g_ref[0], g_ref[1]], axis=0)
        z = gate * (y + z * bias_ref[o:o + 1])
    o_ref[0] = z[0:L].astype(o_ref.dtype)
    o_ref[1] = z[L:].astype(o_ref.dtype)


def _hyena_direct(v, x1, x2, kspec, bias, tabs, B, L):
    C = HY_W
    N = 2 * L
    cb = C // 2
    ncb = C // cb
    seq = pl.BlockSpec((2, L, cb), lambda p, c: (p, 0, c))
    view = lambda t: t.reshape(B, L, C)
    out = pl.pallas_call(
        _hy_direct_kernel,
        grid=(B // 2, ncb),
        in_specs=[seq, seq, seq,
                  pl.BlockSpec((2, N, cb), lambda p, c: (0, 0, c)),
                  pl.BlockSpec((2, N, cb), lambda p, c: (0, 0, ncb + c)),
                  pl.BlockSpec((HY_ORDER, cb), lambda p, c: (0, c)),
                  pl.BlockSpec((2 * N, 2 * L), lambda p, c: (0, 0)),
                  pl.BlockSpec((2 * L, 2 * N), lambda p, c: (0, 0))],
        out_specs=seq,
        out_shape=jax.ShapeDtypeStruct((B, L, C), BF16),
        compiler_params=_cparams("arbitrary", "arbitrary"),
        name="hyena_direct",
    )(view(v), view(x1), view(x2), kspec, kspec, bias, tabs["fwd"], tabs["inv"])
    return out.reshape(B * L, C)


HY_DIRECT_MAX_L = 512


_EVEN_PLAN = (
    (0, HY_W, "conv", 1.0, False, F32),
    (HY_W, HY_W, "conv", 1.0, False, F32),
    (2 * HY_W, HY_W, "conv", 1.0, False, F32),
    (3 * HY_W, M_W, "conv_silu", 1.0, True, BF16),
    (3 * HY_W + M_W, M_W, "conv_silu", M_DH ** -0.5, False, BF16),
    (EV_CONV + M_W, M_W, "raw", 1.0, False, F32),
)
_EVEN_NT_PLAN = ((0, M_W, BF16), (M_W, 4 * M_HEADS, F32))
_ODD_NT_PLAN = ((0, 2 * SSD_HEADS, F32),)
_GN = SSD_GROUPS * SSD_STATE
_ODD_PLAN = (
    (0, SSD_INNER, "raw", 1.0, False, F32),
    (SSD_INNER, SSD_INNER, "conv_silu", 1.0, True, F32),
    (2 * SSD_INNER, _GN, "conv_silu", 1.0, False, BF16),
    (2 * SSD_INNER + _GN, _GN, "conv_silu", 1.0, True, BF16),
)


def _even_layer(x, mods_l, P, e, l, B, L, st, per_batch_cond, kspec, tabs):
    hv, hx1, hx2, qt, k, o, vt, gt = _inproj(
        x, mods_l, P["norm1_g"][l:l + 1], P["ev_in_w"], P["ev_in_wt"], P["ev_conv_w"], P["ev_conv_b"], e,
        plan=_EVEN_PLAN, nt_plan=_EVEN_NT_PLAN, conv0=0, seq_len=L, per_batch_cond=per_batch_cond, row_tile=WIDE_TILE)
    hyena = _hyena_direct if L <= HY_DIRECT_MAX_L else _hyena
    y_hy = hyena(hv, hx1, hx2, kspec, P["hy_bias"][e], tabs, B, L)
    c0, m0 = st
    y_m, c_new, m_new = _mlstm(qt, k, vt, o, gt, P["m_gate_b"][e], P["m_norm_g"][e], c0, m0, B, L)
    x = _outproj_even(y_hy, y_m, x, mods_l, P["ev_out_w"], e, seq_len=L, per_batch_cond=per_batch_cond)
    return x, (c_new, m_new)


def _by_group(p):
    return p.reshape(2, SSD_GROUPS, SSD_HPG).transpose(1, 0, 2).reshape(SSD_GROUPS, 2 * SSD_HPG)


def _odd_layer(x, mods_l, P, o, l, B, L, s0, per_batch_cond):
    z, xt, bm, ct, dtt = _inproj(
        x, mods_l, P["norm1_g"][l:l + 1], P["od_in_w"], P["od_in_wt"], P["od_conv_w"], P["od_conv_b"], o,
        plan=_ODD_PLAN, nt_plan=_ODD_NT_PLAN, conv0=SSD_INNER, seq_len=L, per_batch_cond=per_batch_cond,
        row_tile=NARROW_TILE)
    dskip = jnp.repeat(P["ssd_D"][o], SSD_HEADDIM).reshape(SSD_GROUPS, SSD_GW)
    u, s_new = _ssd(xt, bm, ct, dtt, z, _by_group(P["ssd_dt_bias"][o]), _by_group(P["ssd_A_log"][o]), dskip, s0, B, L)
    x = _outproj_odd(u, x, mods_l, P["ssd_norm_g"], P["od_out_w"], o, seq_len=L, per_batch_cond=per_batch_cond)
    return x, s_new


def _to_colmajor(x, B, L):
    rows = L // GRID_W
    return x.reshape(B, rows, GRID_W, D_MODEL).swapaxes(1, 2).reshape(B * L, D_MODEL)


def _from_colmajor(x, B, L):
    rows = L // GRID_W
    return x.reshape(B, GRID_W, rows, D_MODEL).swapaxes(1, 2).reshape(B * L, D_MODEL)


def _trunk(x, mods, P, B, L, init_states, per_batch_cond, grid, kspecs, tabs):
    states = []
    for l in range(DEPTH):
        if grid and l == 2:
            x = _to_colmajor(x, B, L)
        mods_l = mods[l]
        if l % 2 == 0:
            x, st = _even_layer(x, mods_l, P, l // 2, l, B, L, init_states[l], per_batch_cond, kspecs[l // 2], tabs)
        else:
            x, st = _odd_layer(x, mods_l, P, l // 2, l, B, L, init_states[l], per_batch_cond)
        states.append(st)
        x = _ffn(x, mods_l, P["norm2_g"].reshape(DEPTH, 1, D_MODEL), P["ffn_w1"], P["ffn_w3"], P["ffn_w2"],
                 P["final_g"].reshape(1, D_MODEL), l, seq_len=L, per_batch_cond=per_batch_cond,
                 final=(l == DEPTH - 1))
    if grid:
        x = _from_colmajor(x, B, L)
    return x, states


def _pack_mlstm_state(C, n, m):
    B = C.shape[0]
    ext = jnp.concatenate([jnp.swapaxes(C, -1, -2), n[..., None, :],
                           jnp.zeros(C.shape[:-2] + (M_DH - 1, M_DH), F32)], axis=-2)
    m_rep = jnp.broadcast_to(m.reshape(B, 2 * M_HEADS, 1, 1), (B, 2 * M_HEADS, 1, M_DH))
    return ext.reshape(B, 2 * M_HEADS, 2 * M_DH, M_DH), m_rep


def kernel(x_prompt, x_sample, state_mlstm_C, state_mlstm_n, state_mlstm_m, state_ssd, c, c_ctx,
           mod_w, mod_b, norm1_g, norm2_g, ffn_w1, ffn_w3, ffn_w2, final_g,
           ev_in_w, ev_conv_w, ev_conv_b, hy_w1, hy_b1, hy_w2, hy_b2, hy_w3, hy_freq, hy_bias,
           m_gate_b, m_norm_g, ev_out_w,
           od_in_w, od_conv_w, od_conv_b, ssd_dt_bias, ssd_A_log, ssd_D, ssd_norm_g, od_out_w):
    Bp, Lp, _ = x_prompt.shape
    Bs, Ls, _ = x_sample.shape
    bf = lambda w: w.astype(BF16)
    n_odd = od_in_w.shape[0]
    od_dt_w = od_in_w[:, :, OD_PROJ - 2 * SSD_HEADS:].reshape(n_odd, D_MODEL, 2, SSD_GROUPS, SSD_HPG)
    od_dt_w = od_dt_w.transpose(0, 3, 2, 4, 1).reshape(n_odd, 2 * SSD_HEADS, D_MODEL)
    P = dict(norm1_g=norm1_g, norm2_g=norm2_g, ffn_w1=bf(ffn_w1), ffn_w3=bf(ffn_w3), ffn_w2=bf(ffn_w2),
             final_g=final_g, ev_in_w=bf(ev_in_w),
             ev_in_wt=bf(jnp.concatenate([ev_in_w[:, :, EV_CONV:EV_CONV + M_W],
                                          ev_in_w[:, :, EV_PROJ - 4 * M_HEADS:]], axis=2)).swapaxes(1, 2),
             ev_conv_w=ev_conv_w, ev_conv_b=ev_conv_b, hy_bias=hy_bias, m_gate_b=m_gate_b, m_norm_g=m_norm_g,
             ev_out_w=bf(ev_out_w), od_in_w=bf(od_in_w), od_in_wt=bf(od_dt_w),
             od_conv_w=od_conv_w, od_conv_b=od_conv_b, ssd_dt_bias=ssd_dt_bias, ssd_A_log=ssd_A_log,
             ssd_D=ssd_D, ssd_norm_g=ssd_norm_g.reshape(-1, 1, SSD_INNER), od_out_w=bf(od_out_w))

    ncond = 16
    cond = jnp.concatenate([c_ctx[None, :], c, jnp.zeros((ncond - 1 - Bs, D_MODEL), F32)], axis=0)
    mods = _modulation(cond, mod_w, mod_b).reshape(DEPTH, ncond, 6, D_MODEL)

    tabs, kspecs = {}, {}
    for L in (Lp, Ls):
        direct = L <= HY_DIRECT_MAX_L
        tabs[L] = _direct_tables(L) if direct else _dft_tables(L)
        spectrum = _hy_direct_spectrum if direct else _hy_spectrum
        kspecs[L] = [spectrum(L, tabs[L], hy_w1[e], hy_b1[e], hy_w2[e], hy_b2[e], hy_w3[e], hy_freq[e])
                     for e in range((DEPTH + 1) // 2)]

    G = SSD_GROUPS
    zero_m = _pack_mlstm_state(jnp.zeros((Bp, 2, M_HEADS, M_DH, M_DH), F32), jnp.zeros((Bp, 2, M_HEADS, M_DH), F32),
                               jnp.zeros((Bp, 2, M_HEADS), F32))
    zero_s = jnp.zeros((Bp, 2, G, SSD_GW, SSD_STATE), F32)
    ctx_init = [zero_m if l % 2 == 0 else zero_s for l in range(DEPTH)]
    y_prompt, ctx_states = _trunk(x_prompt.reshape(Bp * Lp, D_MODEL), mods, P, Bp, Lp, ctx_init, False, False,
                                  kspecs[Lp], tabs[Lp])

    lat_init = []
    for l in range(DEPTH):
        if l % 2 == 0:
            e = l // 2
            lat_init.append(_pack_mlstm_state(state_mlstm_C[:, e], state_mlstm_n[:, e], state_mlstm_m[:, e]))
        else:
            lat_init.append(state_ssd[:, l // 2].reshape(Bs, 2, G, SSD_GW, SSD_STATE))
    y_sample, _ = _trunk(x_sample.reshape(Bs * Ls, D_MODEL), mods, P, Bs, Ls, lat_init, True, True,
                         kspecs[Ls], tabs[Ls])

    cs = [ctx_states[l][0].reshape(Bp, 2, M_HEADS, 2 * M_DH, M_DH) for l in range(0, DEPTH, 2)]
    ms = [ctx_states[l][1][:, :, 0, 0].reshape(Bp, 2, M_HEADS) for l in range(0, DEPTH, 2)]
    new_C = jnp.stack([jnp.swapaxes(t[..., :M_DH, :], -1, -2) for t in cs], axis=1)
    new_n = jnp.stack([t[..., M_DH, :] for t in cs], axis=1)
    new_m = jnp.stack(ms, axis=1)
    new_ssd = jnp.stack([ctx_states[l].reshape(Bp, 2, SSD_HEADS, SSD_HEADDIM, SSD_STATE)
                         for l in range(1, DEPTH, 2)], axis=1)
    return (y_prompt.reshape(Bp, Lp, D_MODEL), y_sample.reshape(Bs, Ls, D_MODEL), new_C, new_n, new_m, new_ssd)
```

```python
import functools
import math

import jax
import jax.numpy as jnp
from jax import lax
from jax.experimental import pallas as pl
from jax.experimental.pallas import tpu as pltpu

F32 = jnp.float32
BF16 = jnp.bfloat16

D_MODEL = 1024
DEPTH = 4
GRID_W = 64
EPS = 1e-6
HY_W = D_MODEL // 2
HY_ORDER = 2
HY_EMB = 33
HY_BANDS = (HY_EMB - 1) // 2
HY_HID = 64
HY_TARGET = 1e-2
HY_FAST = 0.3
HY_SLOW = 1.5
M_HEADS = 4
M_DH = 128
M_W = M_HEADS * M_DH
EV_CONV = 3 * HY_W + 2 * M_W
EV_PROJ = EV_CONV + 2 * M_W + 4 * M_HEADS
SSD_INNER = 2 * D_MODEL
SSD_HEADDIM = 64
SSD_HEADS = SSD_INNER // SSD_HEADDIM
SSD_GROUPS = 8
SSD_HPG = SSD_HEADS // SSD_GROUPS
SSD_STATE = 128
SSD_GW = SSD_HPG * SSD_HEADDIM
SSD_CONV_DIM = SSD_INNER + 2 * SSD_GROUPS * SSD_STATE
OD_PROJ = SSD_INNER + SSD_CONV_DIM + 2 * SSD_HEADS
FFN_HIDDEN = -(-8 * D_MODEL // (3 * 256)) * 256

V7X_VMEM_BYTES = 64 * 1024 * 1024
VMEM_LIMIT = V7X_VMEM_BYTES * 3 // 4
WIDE_TILE = 512
NARROW_TILE = 256
HALO = 16
COL_CHUNK = 512
HY_NB = 16
SCAN_T = 128
EPILOGUE_UNROLL = 16
SSD_BLOCK_TOKENS = 2048
MLSTM_BLOCK_TOKENS = 1024


def _cparams(*sem):
    return pltpu.CompilerParams(dimension_semantics=sem, vmem_limit_bytes=VMEM_LIMIT)


def _dot(a, b):
    return jnp.dot(a, b, preferred_element_type=F32)


def _dot_nt(a, b):
    return lax.dot_general(a, b, (((1,), (1,)), ((), ())), preferred_element_type=F32)


def _split2(x):
    h = x.astype(BF16)
    return h, (x - h.astype(F32)).astype(BF16)


def _dot3(th, tl, x):
    xh, xl = _split2(x)
    return _dot(th, xh) + _dot(th, xl) + _dot(tl, xh)


def _dot2(x, t):
    xh, xl = _split2(x)
    return _dot(xh, t) + _dot(xl, t)


def _sigmoid(x):
    return 1.0 / (1.0 + jnp.exp(-x))


def _silu(x):
    return x * _sigmoid(x)


def _log_sigmoid(x):
    return jnp.minimum(x, 0.0) - jnp.log(1.0 + jnp.exp(-jnp.abs(x)))


def _softplus(x):
    return jnp.maximum(x, 0.0) + jnp.log(1.0 + jnp.exp(-jnp.abs(x)))


def _rms(x):
    return x * lax.rsqrt(jnp.mean(x * x, axis=-1, keepdims=True) + EPS)


def _mod_kernel(c_ref, w_ref, b_ref, o_ref):
    a = _silu(c_ref[...]).astype(BF16)
    o_ref[0] = _dot(a, w_ref[0].astype(BF16)) + b_ref[0]


def _modulation(cond, mod_w, mod_b):
    nc = cond.shape[0]
    tn = 1536
    return pl.pallas_call(
        _mod_kernel,
        grid=(DEPTH, 6 * D_MODEL // tn),
        in_specs=[pl.BlockSpec((nc, D_MODEL), lambda l, j: (0, 0)),
                  pl.BlockSpec((1, D_MODEL, tn), lambda l, j: (l, 0, j)),
                  pl.BlockSpec((1, 1, tn), lambda l, j: (l, 0, j))],
        out_specs=pl.BlockSpec((1, nc, tn), lambda l, j: (l, 0, j)),
        out_shape=jax.ShapeDtypeStruct((DEPTH, nc, 6 * D_MODEL), F32),
        compiler_params=_cparams("arbitrary", "arbitrary"),
        name="modulation",
    )(cond, mod_w, mod_b.reshape(DEPTH, 1, 6 * D_MODEL))


def _inproj_kernel(x_ref, xp_ref, xn_ref, mod_ref, g_ref, w_ref, wt_ref, cw_ref, cb_ref, *rest,
                   plan, nt_plan, tiles_per_seq, conv0):
    nseg = len(plan)
    outs = rest[:nseg]
    nt_outs = rest[nseg:nseg + len(nt_plan)]
    hbuf = rest[nseg + len(nt_plan)]
    tm = x_ref.shape[0]
    i = pl.program_id(0)
    j = i % tiles_per_seq
    pmask = (j != 0).astype(F32)
    nmask = (j != tiles_per_seq - 1).astype(F32)
    mod = mod_ref[0]
    sh, sc = mod[0:1], mod[1:2]
    g = g_ref[...]

    def normmod(xv):
        return (_rms(xv) * g * (1.0 + sc) + sh).astype(BF16)

    hbuf[0:tm] = normmod(x_ref[...])
    hbuf[tm:tm + HALO] = normmod(xp_ref[...])
    hbuf[tm + HALO:tm + 2 * HALO] = normmod(xn_ref[...])
    rowid = lax.broadcasted_iota(jnp.int32, (tm, COL_CHUNK), 0)

    for (c0, width, kind, scale, transposed), o_ref in zip(plan, outs):
        for cc in range(c0, c0 + width, COL_CHUNK):
            w = w_ref[0, :, cc:cc + COL_CHUNK]
            if kind == "raw":
                y = _dot(hbuf[0:tm], w)
            else:
                full = _dot(hbuf[...], w)
                acc = full[0:tm]
                ap = full[tm + HALO - 1:tm + HALO] * pmask
                an = full[tm + HALO:tm + HALO + 1] * nmask
                prev = jnp.where(rowid == 0, ap, pltpu.roll(acc, 1, 0))
                nxt = jnp.where(rowid == tm - 1, an, pltpu.roll(acc, tm - 1, 0))
                cw = cw_ref[0, :, cc - conv0:cc - conv0 + COL_CHUNK]
                cb = cb_ref[0, :, cc - conv0:cc - conv0 + COL_CHUNK]
                y = cb + cw[0:1] * prev + cw[1:2] * acc + cw[2:3] * nxt
                if kind == "conv_silu":
                    y = _silu(y)
                if scale != 1.0:
                    y = y * scale
            if transposed:
                o_ref[cc - c0:cc - c0 + COL_CHUNK, :] = y.T.astype(o_ref.dtype)
            else:
                o_ref[:, cc - c0:cc - c0 + COL_CHUNK] = y.astype(o_ref.dtype)

    for (r0, rows), o_ref in zip(nt_plan, nt_outs):
        o_ref[...] = _dot_nt(wt_ref[0, r0:r0 + rows, :], hbuf[0:tm]).astype(o_ref.dtype)


def _inproj(x, mods_l, g_l, w_all, wt_all, cw_all, cb_all, layer_idx, *, plan, nt_plan, conv0, seq_len,
            per_batch_cond, row_tile):
    M = x.shape[0]
    tm = min(row_tile, seq_len)
    tps = seq_len // tm
    nt = M // tm
    hb = tm // HALO
    ntot = w_all.shape[2]
    tw = wt_all.shape[1]
    cdim = cw_all.shape[2]
    e = layer_idx
    cond_map = (lambda i: (1 + i // tps, 0, 0)) if per_batch_cond else (lambda i: (0, 0, 0))
    in_specs = [
        pl.BlockSpec((tm, D_MODEL), lambda i: (i, 0)),
        pl.BlockSpec((HALO, D_MODEL), lambda i: (jnp.maximum(i * hb - 1, 0), 0)),
        pl.BlockSpec((HALO, D_MODEL), lambda i: (jnp.minimum((i + 1) * hb, nt * hb - 1), 0)),
        pl.BlockSpec((1, 6, D_MODEL), cond_map),
        pl.BlockSpec((1, D_MODEL), lambda i: (0, 0)),
        pl.BlockSpec((1, D_MODEL, ntot), lambda i: (e, 0, 0), pipeline_mode=pl.Buffered(1)),
        pl.BlockSpec((1, tw, D_MODEL), lambda i: (e, 0, 0)),
        pl.BlockSpec((1, 3, cdim), lambda i: (e, 0, 0)),
        pl.BlockSpec((1, 1, cdim), lambda i: (e, 0, 0)),
    ]
    out_specs, out_shape = [], []
    for (c0, width, kind, scale, transposed, dt) in plan:
        if transposed:
            out_specs.append(pl.BlockSpec((width, tm), lambda i: (0, i)))
            out_shape.append(jax.ShapeDtypeStruct((width, M), dt))
        else:
            out_specs.append(pl.BlockSpec((tm, width), lambda i: (i, 0)))
            out_shape.append(jax.ShapeDtypeStruct((M, width), dt))
    for (r0, rows, dt) in nt_plan:
        out_specs.append(pl.BlockSpec((rows, tm), lambda i: (0, i)))
        out_shape.append(jax.ShapeDtypeStruct((rows, M), dt))
    kplan = tuple(p[:5] for p in plan)
    knt = tuple(p[:2] for p in nt_plan)
    return pl.pallas_call(
        functools.partial(_inproj_kernel, plan=kplan, nt_plan=knt, tiles_per_seq=tps, conv0=conv0),
        grid=(nt,),
        in_specs=in_specs,
        out_specs=out_specs,
        out_shape=out_shape,
        scratch_shapes=[pltpu.VMEM((tm + 2 * HALO, D_MODEL), BF16)],
        compiler_params=_cparams("arbitrary"),
        name="inproj",
    )(x, x, x, mods_l, g_l, w_all, wt_all, cw_all, cb_all.reshape(cb_all.shape[0], 1, cdim))


def _outproj_even_kernel(yh_ref, ym_ref, x_ref, mod_ref, w_ref, o_ref):
    out = _dot(yh_ref[...], w_ref[0, 0:HY_W]) + _dot(ym_ref[...], w_ref[0, HY_W:])
    o_ref[...] = x_ref[...] + mod_ref[0][2:3] * out


def _outproj_odd_kernel(u_ref, x_ref, mod_ref, ng_ref, w_ref, o_ref):
    un = (_rms(u_ref[...]) * ng_ref[0]).astype(BF16)
    o_ref[...] = x_ref[...] + mod_ref[0][2:3] * _dot(un, w_ref[0])


def _cond_map(seq_len, per_batch_cond):
    tps = seq_len // WIDE_TILE
    return (lambda i: (1 + i // tps, 0, 0)) if per_batch_cond else (lambda i: (0, 0, 0))


def _outproj_even(yh, ym, x, mods_l, w_all, e, *, seq_len, per_batch_cond):
    M = x.shape[0]
    tm = WIDE_TILE
    return pl.pallas_call(
        _outproj_even_kernel,
        grid=(M // tm,),
        in_specs=[pl.BlockSpec((tm, HY_W), lambda i: (i, 0)),
                  pl.BlockSpec((tm, M_W), lambda i: (i, 0)),
                  pl.BlockSpec((tm, D_MODEL), lambda i: (i, 0)),
                  pl.BlockSpec((1, 6, D_MODEL), _cond_map(seq_len, per_batch_cond)),
                  pl.BlockSpec((1, HY_W + M_W, D_MODEL), lambda i: (e, 0, 0), pipeline_mode=pl.Buffered(1))],
        out_specs=pl.BlockSpec((tm, D_MODEL), lambda i: (i, 0)),
        out_shape=jax.ShapeDtypeStruct((M, D_MODEL), F32),
        compiler_params=_cparams("arbitrary"),
        name="outproj_even",
    )(yh, ym, x, mods_l, w_all)


def _outproj_odd(u, x, mods_l, ng_all, w_all, o, *, seq_len, per_batch_cond):
    M = x.shape[0]
    tm = WIDE_TILE
    return pl.pallas_call(
        _outproj_odd_kernel,
        grid=(M // tm,),
        in_specs=[pl.BlockSpec((tm, SSD_INNER), lambda i: (i, 0)),
                  pl.BlockSpec((tm, D_MODEL), lambda i: (i, 0)),
                  pl.BlockSpec((1, 6, D_MODEL), _cond_map(seq_len, per_batch_cond)),
                  pl.BlockSpec((1, 1, SSD_INNER), lambda i: (o, 0, 0)),
                  pl.BlockSpec((1, SSD_INNER, D_MODEL), lambda i: (o, 0, 0), pipeline_mode=pl.Buffered(1))],
        out_specs=pl.BlockSpec((tm, D_MODEL), lambda i: (i, 0)),
        out_shape=jax.ShapeDtypeStruct((M, D_MODEL), F32),
        compiler_params=_cparams("arbitrary"),
        name="outproj_odd",
    )(u, x, mods_l, ng_all, w_all)


FFN_CHUNK = FFN_HIDDEN // 2


def _ffn_kernel(x_ref, mod_ref, g_ref, w1_ref, w3_ref, w2_ref, fg_ref, o_ref, *, final):
    x = x_ref[...]
    mod = mod_ref[0]
    h = (_rms(x) * g_ref[0] * (1.0 + mod[4:5]) + mod[3:4]).astype(BF16)
    ff = jnp.zeros(x.shape, F32)
    for c0 in range(0, FFN_HIDDEN, FFN_CHUNK):
        a = _dot(h, w1_ref[0, :, c0:c0 + FFN_CHUNK])
        b = _dot(h, w3_ref[0, :, c0:c0 + FFN_CHUNK])
        u = (_silu(a) * b).astype(BF16)
        ff = ff + _dot(u, w2_ref[0, c0:c0 + FFN_CHUNK, :])
    xo = x + mod[5:6] * ff
    if final:
        xo = _rms(xo) * fg_ref[...]
    o_ref[...] = xo


def _ffn(x, mods_l, g_all, w1, w3, w2, fg, l, *, seq_len, per_batch_cond, final):
    M = x.shape[0]
    tm = WIDE_TILE
    wspec = lambda shape: pl.BlockSpec(shape, lambda i: (l, 0, 0), pipeline_mode=pl.Buffered(1))
    return pl.pallas_call(
        functools.partial(_ffn_kernel, final=final),
        grid=(M // tm,),
        in_specs=[pl.BlockSpec((tm, D_MODEL), lambda i: (i, 0)),
                  pl.BlockSpec((1, 6, D_MODEL), _cond_map(seq_len, per_batch_cond)),
                  pl.BlockSpec((1, 1, D_MODEL), lambda i: (l, 0, 0)),
                  wspec((1, D_MODEL, FFN_HIDDEN)), wspec((1, D_MODEL, FFN_HIDDEN)),
                  wspec((1, FFN_HIDDEN, D_MODEL)),
                  pl.BlockSpec((1, D_MODEL), lambda i: (0, 0))],
        out_specs=pl.BlockSpec((tm, D_MODEL), lambda i: (i, 0)),
        out_shape=jax.ShapeDtypeStruct((M, D_MODEL), F32),
        compiler_params=_cparams("arbitrary"),
        name="ffn",
    )(x, mods_l, g_all, w1, w3, w2, fg)


def _scan_consts(d):
    T = SCAN_T
    row = lax.broadcasted_iota(jnp.int32, (T, T), 0)
    col = lax.broadcasted_iota(jnp.int32, (T, T), 1)
    mask, last = (row <= col, T - 1) if d == 0 else (row >= col, 0)
    v_mat = jnp.where(mask, 1.0, 0.0).astype(BF16)
    return mask, v_mat, last


def _mlstm_kernel(qt_ref, k_ref, vt_ref, o_ref, gt_ref, gb_ref, ng_ref, c0_ref, m0_ref,
                  y_ref, cst_ref, mst_ref, h_scr, c_scr, m_scr, sb_scr, vw_scr, rows_scr, *, nc, ncs):
    T = SCAN_T
    H = M_HEADS
    nseq = nc // ncs
    ones_blk = jnp.where(lax.broadcasted_iota(jnp.int32, (M_DH, T), 0) == 0, 1.0, 0.0)
    consts = (_scan_consts(0), _scan_consts(1))
    gb = gb_ref[...]

    def gates(ci, d, slot):
        mask, v_mat, last = consts[d]
        c0 = pl.multiple_of(ci * T, T)
        lanes = pl.ds(c0, T)
        graw = gt_ref[:, lanes] + gb
        lf = _log_sigmoid(graw)
        b_all = _dot2(lf, v_mat)
        for h in range(H):
            fi = (2 * d + 1) * H + h
            ii = (2 * d) * H + h
            hl = slice(h * M_DH, (h + 1) * M_DH)
            tr = slice(h * T, (h + 1) * T)
            b_row, i_row = b_all[fi:fi + 1], graw[ii:ii + 1]
            dlog = jnp.where(mask, jnp.broadcast_to(b_row, (T, T)) - jnp.broadcast_to(b_row - i_row, (T, T)).T,
                             -jnp.inf)
            mloc = jnp.max(dlog, axis=0, keepdims=True)
            sb_scr[slot, d, tr, :] = (_dot(k_ref[pl.ds(c0, T), hl], qt_ref[hl, lanes])
                                      * jnp.exp(dlog - mloc)).astype(BF16)
            mloc_last = jnp.broadcast_to(mloc[:, last:last + 1], (1, T))
            b_last = jnp.broadcast_to(b_row[:, last:last + 1], (1, T))
            w_end = jnp.exp(b_last - b_row + i_row - mloc_last)
            vw_scr[slot, d, h, 0:M_DH, :] = (vt_ref[hl, lanes].astype(F32) * w_end).astype(BF16)
            vw_scr[slot, d, h, M_DH:, :] = (ones_blk * w_end).astype(BF16)
            rows_scr[slot, d, 4 * h:4 * h + 4, :] = jnp.concatenate([b_row, mloc, mloc_last, b_last], axis=0)

    def update(ci, d, slot, accumulate):
        c0 = pl.multiple_of(ci * T, T)
        lanes = pl.ds(c0, T)
        for h in range(H):
            sidx = d * H + h
            hl = slice(h * M_DH, (h + 1) * M_DH)
            tr = slice(h * T, (h + 1) * T)
            rows = rows_scr[slot, d, 4 * h:4 * h + 4, :]
            b_row, mloc, mloc_last, b_last = rows[0:1], rows[1:2], rows[2:3], rows[3:4]
            m_row = m_scr[sidx]
            inter = b_row + m_row
            mt = jnp.maximum(inter, mloc)
            kh = k_ref[pl.ds(c0, T), hl]
            qt = qt_ref[hl, lanes]
            vext = jnp.concatenate([vt_ref[hl, lanes], ones_blk.astype(BF16)], axis=0)
            ct = c_scr[sidx]
            tot = (jnp.exp(mloc - mt) * _dot(vext, sb_scr[slot, d, tr, :])
                   + jnp.exp(inter - mt) * _dot(ct.astype(BF16), qt))
            den = tot[M_DH:M_DH + 1]
            ht = tot[0:M_DH] / jnp.maximum(jnp.abs(den), jnp.exp(-mt))
            if accumulate:
                h_scr[hl, lanes] = h_scr[hl, lanes] + ht
            else:
                h_scr[hl, lanes] = ht
            m_new = jnp.maximum(b_last + m_row, mloc_last)
            c_scr[sidx] = (jnp.exp(b_last + m_row - m_new) * ct
                           + jnp.exp(mloc_last - m_new) * _dot(vw_scr[slot, d, h], kh))
            m_scr[sidx] = m_new

    def chunks_at(j):
        if nseq == 1:
            return j, nc - 1 - j
        seq, loc = divmod(j, ncs)
        return j, seq * ncs + ncs - 1 - loc

    def load_state(seq):
        c_scr[...] = c0_ref[seq]
        m_scr[...] = m0_ref[seq]

    def store_state(seq):
        cst_ref[seq] = c_scr[...]
        mst_ref[seq] = m_scr[...]

    def pair(p, slots, accumulate=None):
        s0, s1 = slots
        for j, slot in ((2 * p, s0), (2 * p + 1, s1)):
            if nseq > 1 and j % ncs == 0:
                load_state(j // ncs)
            acc = (j % ncs >= ncs // 2) if accumulate is None else accumulate
            cf, cb = chunks_at(j)
            update(cf, 0, slot, acc)
            update(cb, 1, slot, acc)
            if nseq > 1 and j % ncs == ncs - 1:
                store_state(j // ncs)
        for j, slot in ((2 * p + 2, 2 - s0), (2 * p + 3, 4 - s1)):
            if nseq == 1:
                gates(jnp.minimum(j, nc - 1), 0, slot)
                gates(jnp.maximum(nc - 1 - j, 0), 1, slot)
            elif j < nc:
                cf, cb = chunks_at(j)
                gates(cf, 0, slot)
                gates(cb, 1, slot)

    for j in (0, 1):
        cf, cb = chunks_at(j)
        gates(cf, 0, j)
        gates(cb, 1, j)
    if nseq == 1:
        load_state(0)
    if nseq == 1 and nc % 8 == 0:
        def body(accumulate):
            def run(i, carry):
                pair(2 * i, (0, 1), accumulate)
                pair(2 * i + 1, (2, 3), accumulate)
                return carry
            return run
        lax.fori_loop(0, nc // 8, body(False), 0)
        lax.fori_loop(nc // 8, nc // 4, body(True), 0)
    else:
        for p in range(nc // 2):
            pair(p, ((0, 1), (2, 3))[p % 2])
    if nseq == 1:
        store_state(0)

    def epilogue(i, carry):
        c0 = pl.multiple_of(i * T, T)
        for h in range(H):
            hl = slice(h * M_DH, (h + 1) * M_DH)
            hh = h_scr[hl, pl.ds(c0, T)]
            hn = hh * lax.rsqrt(jnp.mean(hh * hh, axis=0, keepdims=True) + EPS)
            y = hn.T * ng_ref[:, hl] * _sigmoid(o_ref[pl.ds(c0, T), hl])
            y_ref[pl.ds(c0, T), hl] = y.astype(y_ref.dtype)
        return carry

    lax.fori_loop(0, nc, epilogue, 0, unroll=min(nc, EPILOGUE_UNROLL))


def _mlstm(qt, k, vt, o, gt, gate_b, norm_g, c0, m0, B, L):
    assert L % (2 * SCAN_T) == 0 and M_DH == SCAN_T
    bb = max(1, min(B, MLSTM_BLOCK_TOKENS // L))
    assert B % bb == 0
    ncs = L // SCAN_T
    nc = bb * ncs
    Lb = bb * L
    ng = 4 * M_HEADS
    M = B * L
    one = pl.Buffered(1)
    st_spec = pl.BlockSpec((bb, 2 * M_HEADS, 2 * M_DH, M_DH), lambda b: (b, 0, 0, 0))
    m_spec = pl.BlockSpec((bb, 2 * M_HEADS, 1, M_DH), lambda b: (b, 0, 0, 0))
    return pl.pallas_call(
        functools.partial(_mlstm_kernel, nc=nc, ncs=ncs),
        grid=(B // bb,),
        in_specs=[pl.BlockSpec((M_W, Lb), lambda b: (0, b), pipeline_mode=one),
                  pl.BlockSpec((Lb, M_W), lambda b: (b, 0), pipeline_mode=one),
                  pl.BlockSpec((M_W, Lb), lambda b: (0, b), pipeline_mode=one),
                  pl.BlockSpec((Lb, M_W), lambda b: (b, 0), pipeline_mode=one),
                  pl.BlockSpec((ng, Lb), lambda b: (0, b)),
                  pl.BlockSpec((ng, SCAN_T), lambda b: (0, 0)),
                  pl.BlockSpec((1, M_W), lambda b: (0, 0)),
                  st_spec, m_spec],
        out_specs=[pl.BlockSpec((Lb, M_W), lambda b: (b, 0)), st_spec, m_spec],
        out_shape=[jax.ShapeDtypeStruct((M, M_W), BF16),
                   jax.ShapeDtypeStruct((B, 2 * M_HEADS, 2 * M_DH, M_DH), F32),
                   jax.ShapeDtypeStruct((B, 2 * M_HEADS, 1, M_DH), F32)],
        scratch_shapes=[pltpu.VMEM((M_W, Lb), F32),
                        pltpu.VMEM((2 * M_HEADS, 2 * M_DH, M_DH), F32),
                        pltpu.VMEM((2 * M_HEADS, 1, M_DH), F32),
                        pltpu.VMEM((4, 2, M_HEADS * SCAN_T, SCAN_T), BF16),
                        pltpu.VMEM((4, 2, M_HEADS, 2 * M_DH, SCAN_T), BF16),
                        pltpu.VMEM((4, 2, 4 * M_HEADS, SCAN_T), F32)],
        compiler_params=_cparams("arbitrary"),
        name="mlstm",
    )(qt, k, vt, o, gt, jnp.broadcast_to(gate_b.reshape(ng, 1), (ng, SCAN_T)), norm_g.reshape(1, M_W), c0, m0)


def _ssd_kernel(xt_ref, b_ref, ct_ref, dtt_ref, z_ref, db_ref, al_ref, dsk_ref, s0_ref,
                y_ref, sout_ref, yf_scr, yb_scr, s_scr, cbt_scr, p_scr, xd_scr, xdec_scr, rows_scr, *, nc, ncs):
    T = SCAN_T
    H = SSD_HPG
    nseq = nc // ncs
    consts = (_scan_consts(0), _scan_consts(1))
    db = db_ref[0]
    a_neg = -jnp.exp(al_ref[0])

    def gates(ci, d, slot):
        mask, v_mat, last = consts[d]
        c0 = pl.multiple_of(ci * T, T)
        lanes = pl.ds(c0, T)
        hr = slice(d * H, (d + 1) * H)
        dt = _softplus(dtt_ref[hr, lanes] + db[hr])
        a = dt * a_neg[hr]
        acs = _dot2(a, v_mat)
        a_last = jnp.broadcast_to(acs[:, last:last + 1], (H, T))
        rows_scr[slot, d, 0:H, :] = jnp.exp(acs)
        rows_scr[slot, d, H:2 * H, :] = jnp.exp(a_last)
        cbt_scr[d] = _dot(b_ref[pl.ds(c0, T), :], ct_ref[:, lanes])
        for r in range(H):
            tr = slice(r * T, (r + 1) * T)
            pr = slice(r * SSD_HEADDIM, (r + 1) * SSD_HEADDIM)
            al = jnp.broadcast_to(acs[r:r + 1], (T, T))
            lm = jnp.exp(jnp.where(mask, al - al.T, -jnp.inf))
            p_scr[slot, d, tr, :] = (cbt_scr[d] * lm).astype(BF16)
            xd = xt_ref[pr, lanes] * dt[r:r + 1]
            xd_scr[slot, d, pr, :] = xd.astype(BF16)
            xdec_scr[slot, d, pr, :] = (xd * jnp.exp(a_last[r:r + 1] - acs[r:r + 1])).astype(BF16)

    def update(ci, d, slot):
        c0 = pl.multiple_of(ci * T, T)
        lanes = pl.ds(c0, T)
        rows = rows_scr[slot, d]
        sc = _dot(s_scr[d].astype(BF16), ct_ref[:, lanes])
        upd = _dot(xdec_scr[slot, d], b_ref[pl.ds(c0, T), :])
        y_scr = yf_scr if d == 0 else yb_scr
        for r in range(H):
            pr = slice(r * SSD_HEADDIM, (r + 1) * SSD_HEADDIM)
            tr = slice(r * T, (r + 1) * T)
            y_scr[pr, lanes] = _dot(xd_scr[slot, d, pr, :], p_scr[slot, d, tr, :]) + rows[r:r + 1] * sc[pr]
            s_scr[d, pr, :] = rows[H + r:H + r + 1] * s_scr[d, pr, :] + upd[pr]

    def chunks_at(j):
        if nseq == 1:
            return j, nc - 1 - j
        seq, loc = divmod(j, ncs)
        return j, seq * ncs + ncs - 1 - loc

    def pair(p, slots):
        s0, s1 = slots
        for j, slot in ((2 * p, s0), (2 * p + 1, s1)):
            if nseq > 1 and j % ncs == 0:
                s_scr[...] = s0_ref[j // ncs, :, 0]
            cf, cb = chunks_at(j)
            update(cf, 0, slot)
            update(cb, 1, slot)
            if nseq > 1 and j % ncs == ncs - 1:
                sout_ref[j // ncs, :, 0] = s_scr[...]
        for j, slot in ((2 * p + 2, 2 - s0), (2 * p + 3, 4 - s1)):
            if nseq == 1:
                gates(jnp.minimum(j, nc - 1), 0, slot)
                gates(jnp.maximum(nc - 1 - j, 0), 1, slot)
            elif j < nc:
                cf, cb = chunks_at(j)
                gates(cf, 0, slot)
                gates(cb, 1, slot)

    for j in (0, 1):
        cf, cb = chunks_at(j)
        gates(cf, 0, j)
        gates(cb, 1, j)
    if nseq == 1:
        s_scr[...] = s0_ref[0, :, 0]
    if nseq == 1 and nc % 4 == 0:
        def body(i, carry):
            pair(2 * i, (0, 1))
            pair(2 * i + 1, (2, 3))
            return carry
        lax.fori_loop(0, nc // 4, body, 0)
    else:
        for p in range(nc // 2):
            pair(p, ((0, 1), (2, 3))[p % 2])
    if nseq == 1:
        sout_ref[0, :, 0] = s_scr[...]

    dsk = dsk_ref[0]

    def epilogue(i, carry):
        c0 = pl.multiple_of(i * T, T)
        lanes = pl.ds(c0, T)
        yt = yf_scr[:, lanes] + yb_scr[:, lanes] + dsk * xt_ref[:, lanes]
        y_ref[pl.ds(c0, T), :] = yt.T * _silu(z_ref[pl.ds(c0, T), :])
        return carry

    lax.fori_loop(0, nc, epilogue, 0, unroll=min(nc, EPILOGUE_UNROLL))


def _ssd(xt, bm, ct, dtt, z, dt_bias, a_log, dskip, s0, B, L):
    assert L % (2 * SCAN_T) == 0 and SSD_STATE == SCAN_T
    bb = max(1, min(B, SSD_BLOCK_TOKENS // L))
    assert B % bb == 0
    ncs = L // SCAN_T
    nc = bb * ncs
    Lb = bb * L
    G = SSD_GROUPS
    nd = 2 * SSD_HPG
    M = B * L
    rep = lambda p, rows: jnp.broadcast_to(p.reshape(G, rows, 1), (G, rows, SCAN_T))
    small = lambda rows: pl.BlockSpec((1, rows, SCAN_T), lambda b, g: (g, 0, 0))
    st_spec = pl.BlockSpec((bb, 2, 1, SSD_GW, SSD_STATE), lambda b, g: (b, 0, g, 0, 0))
    return pl.pallas_call(
        functools.partial(_ssd_kernel, nc=nc, ncs=ncs),
        grid=(B // bb, G),
        in_specs=[pl.BlockSpec((SSD_GW, Lb), lambda b, g: (g, b)),
                  pl.BlockSpec((Lb, SSD_STATE), lambda b, g: (b, g)),
                  pl.BlockSpec((SSD_STATE, Lb), lambda b, g: (g, b)),
                  pl.BlockSpec((nd, Lb), lambda b, g: (g, b)),
                  pl.BlockSpec((Lb, SSD_GW), lambda b, g: (b, g)),
                  small(nd), small(nd), small(SSD_GW), st_spec],
        out_specs=[pl.BlockSpec((Lb, SSD_GW), lambda b, g: (b, g)), st_spec],
        out_shape=[jax.ShapeDtypeStruct((M, SSD_INNER), F32),
                   jax.ShapeDtypeStruct((B, 2, G, SSD_GW, SSD_STATE), F32)],
        scratch_shapes=[pltpu.VMEM((SSD_GW, Lb), F32), pltpu.VMEM((SSD_GW, Lb), F32),
                        pltpu.VMEM((2, SSD_GW, SSD_STATE), F32),
                        pltpu.VMEM((2, SCAN_T, SCAN_T), F32),
                        pltpu.VMEM((4, 2, SSD_HPG * SCAN_T, SCAN_T), BF16),
                        pltpu.VMEM((4, 2, SSD_GW, SCAN_T), BF16),
                        pltpu.VMEM((4, 2, SSD_GW, SCAN_T), BF16),
                        pltpu.VMEM((4, 2, 2 * SSD_HPG, SCAN_T), F32)],
        compiler_params=_cparams("arbitrary", "arbitrary"),
        name="ssd",
    )(xt, bm, ct, dtt, z, rep(dt_bias, nd), rep(a_log, nd), rep(dskip, SSD_GW), s0)


def _hy_factors(L):
    return (128, 64) if L == 4096 else (32, (2 * L) // 32)


def _hilo(t):
    return _split2(t.astype(F32))


def _cblock(cr, ci):
    return jnp.concatenate([jnp.concatenate([cr, -ci], axis=-1), jnp.concatenate([ci, cr], axis=-1)], axis=-2)


def _dft_tables(L):
    n1c, n2c = _hy_factors(L)
    N = n1c * n2c
    h1 = n1c // 2
    i32 = jnp.int32
    k1 = jnp.arange(n1c, dtype=i32)
    n2 = jnp.arange(n2c, dtype=i32)

    def stage1(n1_count):
        n1 = jnp.arange(n1_count, dtype=i32)
        idx = (k1[None, :, None] * (n2[:, None, None] + n2c * n1[None, None, :])) % N
        ang = idx.astype(F32) * (2.0 * math.pi / N)
        return jnp.cos(ang), jnp.sin(ang)

    c, s = stage1(h1)
    g_fwd = _cblock(c, -s)
    g_inv = _cblock(jnp.swapaxes(c, 1, 2), jnp.swapaxes(s, 1, 2)) / n1c
    cf, sf = stage1(n1c)
    g_real = jnp.concatenate([cf, -sf], axis=1)
    idx2 = (n2[:, None] * n2[None, :]) % n2c
    ang2 = idx2.astype(F32) * (2.0 * math.pi / n2c)
    c2, s2 = jnp.cos(ang2), jnp.sin(ang2)
    f2 = _cblock(c2, -s2)
    f2_inv = _cblock(c2, s2) / n2c
    return dict(g_fwd=g_fwd.astype(BF16), g_inv=g_inv.astype(BF16), f2=f2.astype(BF16),
                f2_inv=f2_inv.astype(BF16), g_real=_hilo(g_real), f2_exact=_hilo(f2))


def _swap01(x):
    return jnp.swapaxes(x, 0, 1)


def _hy_s1_kernel(x_ref, *rest, nb, exact):
    if exact:
        gh_ref, gl_ref, o_ref = rest
    else:
        g_ref, o_ref = rest
    pb, rows, _, C = x_ref.shape
    xs = _swap01(x_ref[...].reshape(pb * rows, nb, C))
    if exact:
        res = [_dot3(gh_ref[j], gl_ref[j], xs[j]) for j in range(nb)]
    else:
        res = [_dot(g_ref[j], xs[j].astype(BF16)) for j in range(nb)]
    out = _swap01(jnp.stack(res, axis=0).astype(o_ref.dtype))
    o_ref[...] = out.reshape(o_ref.shape)


def _hy_stage1(x, tab, out_dtype):
    exact = isinstance(tab, tuple)
    tabs = tab if exact else (tab,)
    n2c, rows_out, R = tabs[0].shape
    Bx, rows, _, C = x.shape
    pb = R // rows
    nb = HY_NB
    cb = min(C, HY_W)
    tspec = pl.BlockSpec((nb, rows_out, R), lambda p, t, c: (t, 0, 0))
    return pl.pallas_call(
        functools.partial(_hy_s1_kernel, nb=nb, exact=exact),
        grid=(Bx // pb, n2c // nb, C // cb),
        in_specs=[pl.BlockSpec((pb, rows, nb, cb), lambda p, t, c: (p, 0, t, c))] + [tspec] * len(tabs),
        out_specs=pl.BlockSpec((None, 2, rows_out // 2, nb, cb), lambda p, t, c: (p, 0, 0, t, c)),
        out_shape=jax.ShapeDtypeStruct((Bx // pb, 2, rows_out // 2, n2c, C), out_dtype),
        compiler_params=_cparams("arbitrary", "arbitrary", "arbitrary"),
        name="hyena_stage1",
    )(x, *tabs)


def _hy_spec_kernel(a_ref, inv_ref, fh_ref, fl_ref, o_ref, *, kb):
    n2c = a_ref.shape[2]
    for kk in range(kb):
        rhs = jnp.concatenate([a_ref[0, kk], a_ref[1, kk]], axis=0)
        X = _dot3(fh_ref[...], fl_ref[...], rhs) * inv_ref[...]
        o_ref[0, kk] = X[0:n2c]
        o_ref[1, kk] = X[n2c:]


def _hy_filter_spectrum(a, inv_norm, f2, kb):
    _, n1c, n2c, W = a.shape
    fh, fl = f2
    blk = pl.BlockSpec((2, kb, n2c, W), lambda t: (0, t, 0, 0))
    fspec = pl.BlockSpec((2 * n2c, 2 * n2c), lambda t: (0, 0))
    return pl.pallas_call(
        functools.partial(_hy_spec_kernel, kb=kb),
        grid=(n1c // kb,),
        in_specs=[blk, pl.BlockSpec((1, W), lambda t: (0, 0)), fspec, fspec],
        out_specs=blk,
        out_shape=jax.ShapeDtypeStruct(a.shape, F32),
        compiler_params=_cparams("arbitrary"),
        name="hyena_filter_spectrum",
    )(a, inv_norm, fh, fl)


def _hy_mid_kernel(a_ref, k_ref, f_ref, i_ref, o_ref, *, kb):
    n2c = a_ref.shape[2]
    res = []
    for kk in range(kb):
        rhs = jnp.concatenate([a_ref[0, kk], a_ref[1, kk]], axis=0)
        X = _dot(f_ref[...], rhs)
        xr, xi = X[0:n2c], X[n2c:]
        kr, ki = k_ref[0, kk], k_ref[1, kk]
        Y = jnp.concatenate([xr * kr - xi * ki, xr * ki + xi * kr], axis=0)
        res.append(_dot(i_ref[...], Y.astype(BF16)))
    o_ref[...] = _swap01(jnp.stack(res, axis=0).astype(o_ref.dtype)).reshape(o_ref.shape)


def _hy_mid(a, kspec, order, f2, f2_inv):
    P, _, n1c, n2c, C = a.shape
    kb = HY_NB
    fspec = pl.BlockSpec((2 * n2c, 2 * n2c), lambda t, p: (0, 0))
    return pl.pallas_call(
        functools.partial(_hy_mid_kernel, kb=kb),
        grid=(n1c // kb, P),
        in_specs=[pl.BlockSpec((None, 2, kb, n2c, C), lambda t, p: (p, 0, t, 0, 0)),
                  pl.BlockSpec((2, kb, n2c, C), lambda t, p: (0, t, 0, order)),
                  fspec, fspec],
        out_specs=pl.BlockSpec((None, 2, n2c, kb, C), lambda t, p: (p, 0, 0, t, 0)),
        out_shape=jax.ShapeDtypeStruct((P, 2, n2c, n1c, C), BF16),
        compiler_params=_cparams("arbitrary", "arbitrary"),
        name="hyena_mid",
    )(a, kspec, f2, f2_inv)


def _hy_last_kernel(b_ref, z_ref, gate_ref, bias_ref, gi_ref, *rest, fuse, nb):
    if fuse:
        gf_ref, zo_ref, ao_ref = rest
    else:
        (zo_ref,) = rest
    _, h1, _, C = gate_ref.shape
    gs = _swap01(gate_ref[...].reshape(2 * h1, nb, C))
    zs = _swap01(z_ref[...].reshape(2 * h1, nb, C)) if fuse else z_ref[...]
    bias = bias_ref[...]
    zn, an = [], []
    for j in range(nb):
        bv = jnp.concatenate([b_ref[0, j], b_ref[1, j]], axis=0)
        y = _dot(gi_ref[j], bv)
        znew = gs[j] * (y + zs[j] * bias)
        zn.append(znew)
        if fuse:
            an.append(_dot(gf_ref[j], znew.astype(BF16)))
    if fuse:
        zo_ref[...] = jnp.stack(zn, axis=0)
        ao_ref[...] = _swap01(jnp.stack(an, axis=0).astype(ao_ref.dtype)).reshape(ao_ref.shape)
    else:
        zo_ref[...] = _swap01(jnp.stack(zn, axis=0).astype(zo_ref.dtype)).reshape(zo_ref.shape)


def _hy_last(bf, z, gate, bias, g_inv, g_fwd, out_dtype):
    P, _, n2c, n1c, C = bf.shape
    B, h1, _, _ = gate.shape
    nb = HY_NB
    cb = C // 2
    fuse = g_fwd is not None
    nat = pl.BlockSpec((2, h1, nb, cb), lambda p, t, c: (p, 0, t, c))
    swp = pl.BlockSpec((None, nb, 2 * h1, cb), lambda p, t, c: (p, t, 0, c))
    in_specs = [pl.BlockSpec((None, 2, nb, n1c, cb), lambda p, t, c: (p, 0, t, 0, c)),
                nat if fuse else swp, nat, pl.BlockSpec((1, cb), lambda p, t, c: (0, c)),
                pl.BlockSpec((nb, 2 * h1, 2 * n1c), lambda p, t, c: (t, 0, 0))]
    args = [bf, z, gate, bias, g_inv]
    if fuse:
        in_specs.append(pl.BlockSpec((nb, 2 * n1c, 2 * h1), lambda p, t, c: (t, 0, 0)))
        args.append(g_fwd)
        out_specs = [swp, pl.BlockSpec((None, 2, n1c, nb, cb), lambda p, t, c: (p, 0, 0, t, c))]
        out_shape = [jax.ShapeDtypeStruct((P, n2c, 2 * h1, C), F32),
                     jax.ShapeDtypeStruct((P, 2, n1c, n2c, C), BF16)]
    else:
        out_specs = [nat]
        out_shape = [jax.ShapeDtypeStruct(gate.shape, out_dtype)]
    return pl.pallas_call(
        functools.partial(_hy_last_kernel, fuse=fuse, nb=nb),
        grid=(P, n2c // nb, C // cb),
        in_specs=in_specs, out_specs=out_specs, out_shape=out_shape,
        compiler_params=_cparams("arbitrary", "arbitrary", "arbitrary"),
        name="hyena_last",
    )(*args)


def _hy_filter_kernel(f_ref, w1_ref, b1_ref, w2_ref, b2_ref, w3_ref, fr_ref, dl_ref, hk_ref, s_ref, *, L):
    i = pl.program_id(0)
    tl = f_ref.shape[0]
    f = f_ref[...]
    fr = fr_ref[...]
    h = jnp.sin(fr * (_dot(f.astype(BF16), w1_ref[...].astype(BF16)) + b1_ref[...]))
    h = jnp.sin(fr * (_dot(h.astype(BF16), w2_ref[...].astype(BF16)) + b2_ref[...]))
    hk = _dot(h.astype(BF16), w3_ref[...].astype(BF16)) * jnp.exp(-f[:, 0:1] * dl_ref[...])
    n = i * tl + lax.broadcasted_iota(jnp.int32, hk.shape, 0)
    hk = jnp.where(n == L, 0.0, hk)
    hk_ref[...] = hk
    tot = jnp.sum(jnp.abs(hk), axis=0, keepdims=True)

    @pl.when(i == 0)
    def _():
        s_ref[...] = tot

    @pl.when(i != 0)
    def _():
        s_ref[...] = s_ref[...] + tot


def _hy_filter(L, w1, b1, w2, b2, w3, freq):
    f32 = F32
    t = jnp.linspace(0.0, 1.0, L, dtype=f32)[:, None]
    ang = (2.0 * math.pi / L) * jnp.arange(L, dtype=f32)[:, None] * \
        jnp.linspace(1e-4, HY_BANDS - 1, HY_BANDS, dtype=f32)[None]
    feats = jnp.concatenate([t, jnp.cos(ang), -jnp.sin(ang)], axis=-1)
    feats = jnp.concatenate([feats, feats[0:1], feats[:0:-1]], axis=0)
    fpad = 128
    feats = jnp.pad(feats, ((0, 0), (0, fpad - HY_EMB)))
    w1p = jnp.pad(w1, ((0, fpad - HY_EMB), (0, 0)))
    deltas = jnp.abs(jnp.linspace(math.log(HY_TARGET) / HY_SLOW, math.log(HY_TARGET) / HY_FAST, HY_W, dtype=f32))
    OC = HY_ORDER * HY_W
    dl = jnp.tile(deltas, HY_ORDER).reshape(1, OC)
    tl = min(L, 512)
    nblk = L // tl
    full = lambda shp: pl.BlockSpec(shp, lambda i: (0, 0))
    return pl.pallas_call(
        functools.partial(_hy_filter_kernel, L=L),
        grid=(2 * nblk,),
        in_specs=[pl.BlockSpec((tl, fpad), lambda i: (i, 0)),
                  full((fpad, HY_HID)), full((1, HY_HID)), full((HY_HID, HY_HID)), full((1, HY_HID)),
                  pl.BlockSpec((HY_HID, OC), lambda i: (0, i // nblk)), full((1, HY_HID)), full((1, OC))],
        out_specs=[pl.BlockSpec((tl, OC), lambda i: (i, 0)), full((1, OC))],
        out_shape=[jax.ShapeDtypeStruct((2 * L, OC), F32), jax.ShapeDtypeStruct((1, OC), F32)],
        compiler_params=_cparams("arbitrary"),
        name="hyena_filter",
    )(feats, w1p, b1.reshape(1, HY_HID), w2, b2.reshape(1, HY_HID), w3, freq.reshape(1, HY_HID), dl)


def _hy_spectrum(L, tabs, w1, b1, w2, b2, w3, freq):
    n1c, n2c = _hy_factors(L)
    OC = HY_ORDER * HY_W
    ktime, s = _hy_filter(L, w1, b1, w2, b2, w3, freq)
    a = _hy_stage1(ktime.reshape(1, n1c, n2c, OC), tabs["g_real"], F32)[0]
    return _hy_filter_spectrum(a, 1.0 / s, tabs["f2_exact"], 8)


def _hyena(v, x1, x2, kspec, bias, tabs, B, L):
    n1c, n2c = _hy_factors(L)
    C = HY_W
    view = lambda t: t.reshape(B, n1c // 2, n2c, C)
    z0, g1, g2 = view(v), view(x1), view(x2)
    a = _hy_stage1(z0, tabs["g_fwd"], BF16)
    bf = _hy_mid(a, kspec, 0, tabs["f2"], tabs["f2_inv"])
    z1, a = _hy_last(bf, z0, g1, bias[0:1], tabs["g_inv"], tabs["g_fwd"], F32)
    bf = _hy_mid(a, kspec, 1, tabs["f2"], tabs["f2_inv"])
    (z2,) = _hy_last(bf, z1, g2, bias[1:2], tabs["g_inv"], None, BF16)
    return z2.reshape(B * L, C)


def _direct_tables(L):
    N = 2 * L
    k = jnp.arange(N, dtype=jnp.int32)
    ang = ((k[:, None] * k[None, :]) % N).astype(F32) * (2.0 * math.pi / N)
    c, s = jnp.cos(ang), jnp.sin(ang)
    fwd = _cblock(c[:, :L], -s[:, :L])
    inv = _cblock(c[:L, :], s[:L, :]) / N
    real = jnp.concatenate([c, -s], axis=0)
    return dict(fwd=fwd.astype(BF16), inv=inv.astype(BF16), real=_hilo(real))


def _hy_direct_spec_kernel(kt_ref, inv_ref, fh_ref, fl_ref, o_ref):
    N = kt_ref.shape[0]
    X = _dot3(fh_ref[...], fl_ref[...], kt_ref[...]) * inv_ref[...]
    o_ref[0] = X[0:N]
    o_ref[1] = X[N:]


def _hy_direct_spectrum(L, tabs, w1, b1, w2, b2, w3, freq):
    N = 2 * L
    OC = HY_ORDER * HY_W
    cb = HY_W // 2
    ktime, s = _hy_filter(L, w1, b1, w2, b2, w3, freq)
    fh, fl = tabs["real"]
    fspec = pl.BlockSpec((2 * N, N), lambda c: (0, 0))
    return pl.pallas_call(
        _hy_direct_spec_kernel,
        grid=(OC // cb,),
        in_specs=[pl.BlockSpec((N, cb), lambda c: (0, c)), pl.BlockSpec((1, cb), lambda c: (0, c)), fspec, fspec],
        out_specs=pl.BlockSpec((2, N, cb), lambda c: (0, 0, c)),
        out_shape=jax.ShapeDtypeStruct((2, N, OC), F32),
        compiler_params=_cparams("arbitrary"),
        name="hyena_direct_spectrum",
    )(ktime, 1.0 / s, fh, fl)


def _hy_direct_kernel(v_ref, g1_ref, g2_ref, k0_ref, k1_ref, bias_ref, f_ref, fi_ref, o_ref):
    L = v_ref.shape[1]
    N = 2 * L
    z = jnp.concatenate([v_ref[0], v_ref[1]], axis=0)
    for o, (k_ref, g_ref) in enumerate(((k0_ref, g1_ref), (k1_ref, g2_ref))):
        X = _dot(f_ref[...], z.astype(BF16))
        xr, xi = X[0:N], X[N:]
        kr, ki = k_ref[0], k_ref[1]
        Y = jnp.concatenate([xr * kr - xi * ki, xr * ki + xi * kr], axis=0)
        y = _dot(fi_ref[...], Y.astype(BF16))
        gate = jnp.concatenate([g_ref[0], g_ref[1]], axis=0)
        z = gate * (y + z * bias_ref[o:o + 1])
    o_ref[0] = z[0:L].astype(o_ref.dtype)
    o_ref[1] = z[L:].astype(o_ref.dtype)


def _hyena_direct(v, x1, x2, kspec, bias, tabs, B, L):
    C = HY_W
    N = 2 * L
    cb = C // 2
    ncb = C // cb
    seq = pl.BlockSpec((2, L, cb), lambda p, c: (p, 0, c))
    view = lambda t: t.reshape(B, L, C)
    out = pl.pallas_call(
        _hy_direct_kernel,
        grid=(B // 2, ncb),
        in_specs=[seq, seq, seq,
                  pl.BlockSpec((2, N, cb), lambda p, c: (0, 0, c)),
                  pl.BlockSpec((2, N, cb), lambda p, c: (0, 0, ncb + c)),
                  pl.BlockSpec((HY_ORDER, cb), lambda p, c: (0, c)),
                  pl.BlockSpec((2 * N, 2 * L), lambda p, c: (0, 0)),
                  pl.BlockSpec((2 * L, 2 * N), lambda p, c: (0, 0))],
        out_specs=seq,
        out_shape=jax.ShapeDtypeStruct((B, L, C), BF16),
        compiler_params=_cparams("arbitrary", "arbitrary"),
        name="hyena_direct",
    )(view(v), view(x1), view(x2), kspec, kspec, bias, tabs["fwd"], tabs["inv"])
    return out.reshape(B * L, C)


HY_DIRECT_MAX_L = 512


_EVEN_PLAN = (
    (0, HY_W, "conv", 1.0, False, F32),
    (HY_W, HY_W, "conv", 1.0, False, F32),
    (2 * HY_W, HY_W, "conv", 1.0, False, F32),
    (3 * HY_W, M_W, "conv_silu", 1.0, True, BF16),
    (3 * HY_W + M_W, M_W, "conv_silu", M_DH ** -0.5, False, BF16),
    (EV_CONV + M_W, M_W, "raw", 1.0, False, F32),
)
_EVEN_NT_PLAN = ((0, M_W, BF16), (M_W, 4 * M_HEADS, F32))
_ODD_NT_PLAN = ((0, 2 * SSD_HEADS, F32),)
_GN = SSD_GROUPS * SSD_STATE
_ODD_PLAN = (
    (0, SSD_INNER, "raw", 1.0, False, F32),
    (SSD_INNER, SSD_INNER, "conv_silu", 1.0, True, F32),
    (2 * SSD_INNER, _GN, "conv_silu", 1.0, False, BF16),
    (2 * SSD_INNER + _GN, _GN, "conv_silu", 1.0, True, BF16),
)


def _even_layer(x, mods_l, P, e, l, B, L, st, per_batch_cond, kspec, tabs):
    hv, hx1, hx2, qt, k, o, vt, gt = _inproj(
        x, mods_l, P["norm1_g"][l:l + 1], P["ev_in_w"], P["ev_in_wt"], P["ev_conv_w"], P["ev_conv_b"], e,
        plan=_EVEN_PLAN, nt_plan=_EVEN_NT_PLAN, conv0=0, seq_len=L, per_batch_cond=per_batch_cond, row_tile=WIDE_TILE)
    hyena = _hyena_direct if L <= HY_DIRECT_MAX_L else _hyena
    y_hy = hyena(hv, hx1, hx2, kspec, P["hy_bias"][e], tabs, B, L)
    c0, m0 = st
    y_m, c_new, m_new = _mlstm(qt, k, vt, o, gt, P["m_gate_b"][e], P["m_norm_g"][e], c0, m0, B, L)
    x = _outproj_even(y_hy, y_m, x, mods_l, P["ev_out_w"], e, seq_len=L, per_batch_cond=per_batch_cond)
    return x, (c_new, m_new)


def _by_group(p):
    return p.reshape(2, SSD_GROUPS, SSD_HPG).transpose(1, 0, 2).reshape(SSD_GROUPS, 2 * SSD_HPG)


def _odd_layer(x, mods_l, P, o, l, B, L, s0, per_batch_cond):
    z, xt, bm, ct, dtt = _inproj(
        x, mods_l, P["norm1_g"][l:l + 1], P["od_in_w"], P["od_in_wt"], P["od_conv_w"], P["od_conv_b"], o,
        plan=_ODD_PLAN, nt_plan=_ODD_NT_PLAN, conv0=SSD_INNER, seq_len=L, per_batch_cond=per_batch_cond,
        row_tile=NARROW_TILE)
    dskip = jnp.repeat(P["ssd_D"][o], SSD_HEADDIM).reshape(SSD_GROUPS, SSD_GW)
    u, s_new = _ssd(xt, bm, ct, dtt, z, _by_group(P["ssd_dt_bias"][o]), _by_group(P["ssd_A_log"][o]), dskip, s0, B, L)
    x = _outproj_odd(u, x, mods_l, P["ssd_norm_g"], P["od_out_w"], o, seq_len=L, per_batch_cond=per_batch_cond)
    return x, s_new


def _to_colmajor(x, B, L):
    rows = L // GRID_W
    return x.reshape(B, rows, GRID_W, D_MODEL).swapaxes(1, 2).reshape(B * L, D_MODEL)


def _from_colmajor(x, B, L):
    rows = L // GRID_W
    return x.reshape(B, GRID_W, rows, D_MODEL).swapaxes(1, 2).reshape(B * L, D_MODEL)


def _trunk(x, mods, P, B, L, init_states, per_batch_cond, grid, kspecs, tabs):
    states = []
    for l in range(DEPTH):
        if grid and l == 2:
            x = _to_colmajor(x, B, L)
        mods_l = mods[l]
        if l % 2 == 0:
            x, st = _even_layer(x, mods_l, P, l // 2, l, B, L, init_states[l], per_batch_cond, kspecs[l // 2], tabs)
        else:
            x, st = _odd_layer(x, mods_l, P, l // 2, l, B, L, init_states[l], per_batch_cond)
        states.append(st)
        x = _ffn(x, mods_l, P["norm2_g"].reshape(DEPTH, 1, D_MODEL), P["ffn_w1"], P["ffn_w3"], P["ffn_w2"],
                 P["final_g"].reshape(1, D_MODEL), l, seq_len=L, per_batch_cond=per_batch_cond,
                 final=(l == DEPTH - 1))
    if grid:
        x = _from_colmajor(x, B, L)
    return x, states


def _pack_mlstm_state(C, n, m):
    B = C.shape[0]
    ext = jnp.concatenate([jnp.swapaxes(C, -1, -2), n[..., None, :],
                           jnp.zeros(C.shape[:-2] + (M_DH - 1, M_DH), F32)], axis=-2)
    m_rep = jnp.broadcast_to(m.reshape(B, 2 * M_HEADS, 1, 1), (B, 2 * M_HEADS, 1, M_DH))
    return ext.reshape(B, 2 * M_HEADS, 2 * M_DH, M_DH), m_rep


def kernel(x_prompt, x_sample, state_mlstm_C, state_mlstm_n, state_mlstm_m, state_ssd, c, c_ctx,
           mod_w, mod_b, norm1_g, norm2_g, ffn_w1, ffn_w3, ffn_w2, final_g,
           ev_in_w, ev_conv_w, ev_conv_b, hy_w1, hy_b1, hy_w2, hy_b2, hy_w3, hy_freq, hy_bias,
           m_gate_b, m_norm_g, ev_out_w,
           od_in_w, od_conv_w, od_conv_b, ssd_dt_bias, ssd_A_log, ssd_D, ssd_norm_g, od_out_w):
    Bp, Lp, _ = x_prompt.shape
    Bs, Ls, _ = x_sample.shape
    bf = lambda w: w.astype(BF16)
    n_odd = od_in_w.shape[0]
    od_dt_w = od_in_w[:, :, OD_PROJ - 2 * SSD_HEADS:].reshape(n_odd, D_MODEL, 2, SSD_GROUPS, SSD_HPG)
    od_dt_w = od_dt_w.transpose(0, 3, 2, 4, 1).reshape(n_odd, 2 * SSD_HEADS, D_MODEL)
    P = dict(norm1_g=norm1_g, norm2_g=norm2_g, ffn_w1=bf(ffn_w1), ffn_w3=bf(ffn_w3), ffn_w2=bf(ffn_w2),
             final_g=final_g, ev_in_w=bf(ev_in_w),
             ev_in_wt=bf(jnp.concatenate([ev_in_w[:, :, EV_CONV:EV_CONV + M_W],
                                          ev_in_w[:, :, EV_PROJ - 4 * M_HEADS:]], axis=2)).swapaxes(1, 2),
             ev_conv_w=ev_conv_w, ev_conv_b=ev_conv_b, hy_bias=hy_bias, m_gate_b=m_gate_b, m_norm_g=m_norm_g,
             ev_out_w=bf(ev_out_w), od_in_w=bf(od_in_w), od_in_wt=bf(od_dt_w),
             od_conv_w=od_conv_w, od_conv_b=od_conv_b, ssd_dt_bias=ssd_dt_bias, ssd_A_log=ssd_A_log,
             ssd_D=ssd_D, ssd_norm_g=ssd_norm_g.reshape(-1, 1, SSD_INNER), od_out_w=bf(od_out_w))

    ncond = 16
    cond = jnp.concatenate([c_ctx[None, :], c, jnp.zeros((ncond - 1 - Bs, D_MODEL), F32)], axis=0)
    mods = _modulation(cond, mod_w, mod_b).reshape(DEPTH, ncond, 6, D_MODEL)

    tabs, kspecs = {}, {}
    for L in (Lp, Ls):
        direct = L <= HY_DIRECT_MAX_L
        tabs[L] = _direct_tables(L) if direct else _dft_tables(L)
        spectrum = _hy_direct_spectrum if direct else _hy_spectrum
        kspecs[L] = [spectrum(L, tabs[L], hy_w1[e], hy_b1[e], hy_w2[e], hy_b2[e], hy_w3[e], hy_freq[e])
                     for e in range((DEPTH + 1) // 2)]

    G = SSD_GROUPS
    zero_m = _pack_mlstm_state(jnp.zeros((Bp, 2, M_HEADS, M_DH, M_DH), F32), jnp.zeros((Bp, 2, M_HEADS, M_DH), F32),
                               jnp.zeros((Bp, 2, M_HEADS), F32))
    zero_s = jnp.zeros((Bp, 2, G, SSD_GW, SSD_STATE), F32)
    ctx_init = [zero_m if l % 2 == 0 else zero_s for l in range(DEPTH)]
    y_prompt, ctx_states = _trunk(x_prompt.reshape(Bp * Lp, D_MODEL), mods, P, Bp, Lp, ctx_init, False, False,
                                  kspecs[Lp], tabs[Lp])

    lat_init = []
    for l in range(DEPTH):
        if l % 2 == 0:
            e = l // 2
            lat_init.append(_pack_mlstm_state(state_mlstm_C[:, e], state_mlstm_n[:, e], state_mlstm_m[:, e]))
        else:
            lat_init.append(state_ssd[:, l // 2].reshape(Bs, 2, G, SSD_GW, SSD_STATE))
    y_sample, _ = _trunk(x_sample.reshape(Bs * Ls, D_MODEL), mods, P, Bs, Ls, lat_init, True, True,
                         kspecs[Ls], tabs[Ls])

    cs = [ctx_states[l][0].reshape(Bp, 2, M_HEADS, 2 * M_DH, M_DH) for l in range(0, DEPTH, 2)]
    ms = [ctx_states[l][1][:, :, 0, 0].reshape(Bp, 2, M_HEADS) for l in range(0, DEPTH, 2)]
    new_C = jnp.stack([jnp.swapaxes(t[..., :M_DH, :], -1, -2) for t in cs], axis=1)
    new_n = jnp.stack([t[..., M_DH, :] for t in cs], axis=1)
    new_m = jnp.stack(ms, axis=1)
    new_ssd = jnp.stack([ctx_states[l].reshape(Bp, 2, SSD_HEADS, SSD_HEADDIM, SSD_STATE)
                         for l in range(1, DEPTH, 2)], axis=1)
    return (y_prompt.reshape(Bp, Lp, D_MODEL), y_sample.reshape(Bs, Ls, D_MODEL), new_C, new_n, new_m, new_ssd)
```
